```python
import math
import jax, jax.numpy as jnp
from jax import lax
import numpy as np

D_MODEL = 1024
BATCH = 1
SEQ = 16384
DEPTH = 2

GRID_W = 64
CTX_LEN = 256
RMS_EPS = 1e-6
ROPE_BASE = 10000.0

FNET_GROUPS = 4
FNET_GROUP_DIM = 64
FNET_WIDTH = FNET_GROUPS * FNET_GROUP_DIM
NA_HEADS = 12
NA_HEAD_DIM = 64
NA_WIDTH = NA_HEADS * NA_HEAD_DIM
NA_KH = 8
NA_KW = 16
NA_KEYW = 2 * NA_KW
NA_NCB = GRID_W // NA_KW
AB_IN = FNET_WIDTH + 3 * NA_WIDTH
AB_OUT = FNET_WIDTH + NA_WIDTH

RET_HEADS = 4
RET_DK = D_MODEL // RET_HEADS
RET_DV = 2 * RET_DK
RET_QK = RET_HEADS * RET_DK
RET_V = RET_HEADS * RET_DV
RET_IN = 2 * RET_QK + 2 * RET_V
RET_CHUNK = 128

N_EXPERTS = 16
N_GROUPS = 4
EXPERTS_PER_GROUP = N_EXPERTS // N_GROUPS
TOP_K = 2
D_EXPERT = D_MODEL // 2

kernel_name = 'hybrid_fnet_natten_retnet_moe_dit'


def rmsnorm(x, g):
    xf = x.astype(jnp.float32)
    y = xf * lax.rsqrt(jnp.mean(xf * xf, axis=-1, keepdims=True) + RMS_EPS)
    return y.astype(x.dtype) * g


def modulate(x, shift, scale):
    return x * (1 + scale) + shift


def _rope_half(x, pos):
    n = x.shape[-1] // 2
    inv_freq = ROPE_BASE ** (-jnp.arange(n, dtype=jnp.float32) / n)
    ang = pos.astype(jnp.float32)[:, None] * inv_freq[None, :]
    cos = jnp.cos(ang)[None, :, None, :]
    sin = jnp.sin(ang)[None, :, None, :]
    xf = x.astype(jnp.float32)
    x1, x2 = xf[..., :n], xf[..., n:]
    return jnp.concatenate([x1 * cos - x2 * sin, x2 * cos + x1 * sin], axis=-1)


def axial_rope(x, row, col):
    half = x.shape[-1] // 2
    return jnp.concatenate([_rope_half(x[..., :half], row), _rope_half(x[..., half:], col)], axis=-1).astype(x.dtype)


def fourier_mix(a):
    B, N, _ = a.shape
    ag = a.reshape(B, N, FNET_GROUPS, FNET_GROUP_DIM).astype(jnp.float32)
    return jnp.fft.fft2(ag, axes=(1, 3), norm='ortho').real.astype(a.dtype).reshape(B, N, FNET_WIDTH)


def dense_attention(q, k, v):
    s = jnp.einsum('bqhd,bkhd->bhqk', q, k).astype(jnp.float32) * (q.shape[-1] ** -0.5)
    p = jax.nn.softmax(s, axis=-1).astype(v.dtype)
    return jnp.einsum('bhqk,bkhd->bqhd', p, v)


def _na_static():
    j = np.arange(NA_NCB)
    key_start = np.clip(j * NA_KW - NA_KW // 2, 0, GRID_W - NA_KEYW)
    key_cols = key_start[:, None] + np.arange(NA_KEYW)[None, :]
    q_cols = j[:, None] * NA_KW + np.arange(NA_KW)[None, :]
    win_start = np.clip(q_cols - NA_KW // 2, 0, GRID_W - NA_KW)[:, :, None]
    kc = key_cols[:, None, :]
    col_mask = (kc >= win_start) & (kc < win_start + NA_KW)
    dc_idx = np.clip(kc - q_cols[:, :, None] + NA_KW - 1, 0, 2 * NA_KW - 2)
    return key_cols, col_mask, dc_idx


def neighborhood_attention(q, k, v, k_ctx, v_ctx, rpb):
    B, L, H, Dh = q.shape
    rows = L // GRID_W
    kh = min(NA_KH, rows)
    scale = Dh ** -0.5
    key_cols, col_mask, dc_idx = _na_static()
    qg = q.reshape(B, rows, NA_NCB, NA_KW, H, Dh)
    kg = k.reshape(B, rows, GRID_W, H, Dh)
    vg = v.reshape(B, rows, GRID_W, H, Dh)
    rpb_cols = rpb[:, :, dc_idx]
    mask = jnp.asarray(col_mask)[None, None, :, :, None, :]

    def row_block(r):
        rs = jnp.clip(r - kh // 2, 0, rows - kh)
        k_blk = lax.dynamic_slice_in_dim(kg, rs, kh, axis=1)[:, :, key_cols]
        v_blk = lax.dynamic_slice_in_dim(vg, rs, kh, axis=1)[:, :, key_cols]
        q_r = lax.dynamic_index_in_dim(qg, r, axis=1, keepdims=False)
        s_loc = jnp.einsum('bjqhd,brjkhd->bhjqrk', q_r, k_blk).astype(jnp.float32) * scale
        dr = rs + jnp.arange(kh) - r + (NA_KH - 1)
        bias = jnp.transpose(rpb_cols[:, dr], (0, 2, 3, 1, 4)).astype(jnp.float32)
        s_loc = jnp.where(mask, s_loc + bias[None], -jnp.inf).reshape(B, H, NA_NCB, NA_KW, kh * NA_KEYW)
        s_ctx = jnp.einsum('bjqhd,bkhd->bhjqk', q_r, k_ctx).astype(jnp.float32) * scale
        p = jax.nn.softmax(jnp.concatenate([s_loc, s_ctx], axis=-1), axis=-1).astype(v.dtype)
        p_loc = p[..., :kh * NA_KEYW].reshape(B, H, NA_NCB, NA_KW, kh, NA_KEYW)
        p_ctx = p[..., kh * NA_KEYW:]
        return (jnp.einsum('bhjqrk,brjkhd->bjqhd', p_loc, v_blk)
                + jnp.einsum('bhjqk,bkhd->bjqhd', p_ctx, v_ctx))

    o = lax.map(row_block, jnp.arange(rows))
    return jnp.moveaxis(o, 0, 1).reshape(B, L, H * Dh)


def fnet_na_mixer(xm, cm, w_in, w_out, rpb, need_ctx):
    B, L, _ = xm.shape
    pl = xm @ w_in
    pc = cm @ w_in

    def heads(p):
        qkv = p[..., FNET_WIDTH:].reshape(p.shape[0], p.shape[1], 3, NA_HEADS, NA_HEAD_DIM)
        return qkv[:, :, 0], qkv[:, :, 1], qkv[:, :, 2]

    ql, kl, vl = heads(pl)
    qc, kc, vc = heads(pc)
    a_lat = fourier_mix(pl[..., :FNET_WIDTH])
    o_lat = neighborhood_attention(ql, kl, vl, kc, vc, rpb)
    y_lat = jnp.concatenate([a_lat, o_lat], axis=-1) @ w_out
    y_ctx = None
    if need_ctx:
        a_ctx = fourier_mix(pc[..., :FNET_WIDTH])
        o_ctx = dense_attention(qc, kc, vc).reshape(B, -1, NA_WIDTH)
        y_ctx = jnp.concatenate([a_ctx, o_ctx], axis=-1) @ w_out
    return y_lat, y_ctx


def retention_scan(q, k, v, log_gamma):
    B, H, N, dk = q.shape
    dv = v.shape[-1]
    nc = N // RET_CHUNK

    def chunks(t):
        return jnp.moveaxis(t.reshape(B, H, nc, RET_CHUNK, t.shape[-1]), 2, 0)

    pos = jnp.arange(RET_CHUNK, dtype=jnp.float32)
    diff = pos[:, None] - pos[None, :]
    lower = diff >= 0
    intra_decay = jnp.where(lower, jnp.exp(jnp.where(lower, diff, 0.0)[None] * log_gamma[:, None, None]), 0.0)
    q_decay = jnp.exp((pos + 1.0)[None, :] * log_gamma[:, None])[..., None]
    k_decay = jnp.exp((RET_CHUNK - 1.0 - pos)[None, :] * log_gamma[:, None])[..., None]
    chunk_decay = jnp.exp(RET_CHUNK * log_gamma)[:, None, None]

    def step(state, qkv):
        qc, kc, vc = qkv
        scores = jnp.einsum('bhnd,bhmd->bhnm', qc, kc) * intra_decay
        out = (jnp.einsum('bhnm,bhmv->bhnv', scores, vc)
               + jnp.einsum('bhnd,bhdv->bhnv', qc * q_decay, state))
        state = state * chunk_decay + jnp.einsum('bhmd,bhmv->bhdv', kc * k_decay, vc)
        return state, out

    state0 = jnp.zeros((B, H, dk, dv), jnp.float32)
    _, out = lax.scan(step, state0, (chunks(q), chunks(k), chunks(v)))
    return jnp.moveaxis(out, 0, 2).reshape(B, H, N, dv)


def retention_mixer(xm, cm, w_in, w_out, decay_param, row, col, need_ctx):
    B, L, _ = xm.shape
    n_ctx = cm.shape[1]

    def split(p):
        n = p.shape[1]
        q = p[..., :RET_QK].reshape(B, n, RET_HEADS, RET_DK)
        k = p[..., RET_QK:2 * RET_QK].reshape(B, n, RET_HEADS, RET_DK) * (RET_DK ** -0.5)
        v = p[..., 2 * RET_QK:2 * RET_QK + RET_V].reshape(B, n, RET_HEADS, RET_DV)
        g = p[..., 2 * RET_QK + RET_V:]
        return q, k, v, g

    ql, kl, vl, gl = split(xm @ w_in)
    qc, kc, vc, gc = split(cm @ w_in)
    ql = axial_rope(ql, row, col)
    kl = axial_rope(kl, row, col)

    def bhnd(t_ctx, t_lat):
        return jnp.concatenate([t_ctx, t_lat], axis=1).transpose(0, 2, 1, 3).astype(jnp.float32)

    q, k, v = bhnd(qc, ql), bhnd(kc, kl), bhnd(vc, vl)
    log_gamma = jnp.log1p(-jnp.exp(decay_param.astype(jnp.float32)))

    def flip(t):
        return jnp.concatenate([t[:, :, :n_ctx][:, :, ::-1], t[:, :, n_ctx:][:, :, ::-1]], axis=2)

    o = retention_scan(q, k, v, log_gamma[0]) + flip(retention_scan(flip(q), flip(k), flip(v), log_gamma[1]))

    def gated_out(o_part, g):
        on = o_part * lax.rsqrt(jnp.mean(o_part * o_part, axis=-1, keepdims=True) + RMS_EPS)
        on = on.transpose(0, 2, 1, 3).reshape(B, -1, RET_V).astype(g.dtype)
        return (jax.nn.silu(g) * on) @ w_out

    y_lat = gated_out(o[:, :, n_ctx:], gl)
    y_ctx = gated_out(o[:, :, :n_ctx], gc) if need_ctx else None
    return y_lat, y_ctx


def moe_ffn(tok, router_w, router_b, w1, w3, w2):
    T = tok.shape[0]
    scores = jax.nn.sigmoid((tok @ router_w).astype(jnp.float32))
    biased = (scores + router_b.astype(jnp.float32)).reshape(T, N_GROUPS, EXPERTS_PER_GROUP)
    group_score = lax.top_k(biased, TOP_K)[0].sum(axis=-1)
    g_sel = jnp.argmax(group_score, axis=-1)
    in_group = jnp.take_along_axis(biased, g_sel[:, None, None], axis=1)[:, 0]
    _, local = lax.top_k(in_group, TOP_K)
    expert_idx = g_sel[:, None] * EXPERTS_PER_GROUP + local
    w = jnp.take_along_axis(scores, expert_idx, axis=-1)
    w = w / jnp.sum(w, axis=-1, keepdims=True)
    gate = jnp.sum(jax.nn.one_hot(expert_idx, N_EXPERTS, dtype=jnp.float32) * w[..., None], axis=1).astype(tok.dtype)
    out = jnp.zeros_like(tok)
    for e in range(N_EXPERTS):
        hid = jax.nn.silu(tok @ w1[e]) * (tok @ w3[e])
        out = out + gate[:, e:e + 1] * (hid @ w2[e])
    return out


def setup_inputs(seed: int = 0) -> dict:
    key = jax.random.key(seed)
    ks = jax.random.split(key, 20)
    n_even = (DEPTH + 1) // 2
    n_odd = DEPTH // 2

    def nrm(k, shape, scale):
        return jax.random.normal(k, shape, jnp.float32) * scale

    base_decay = -(5.0 + jnp.arange(RET_HEADS, dtype=jnp.float32)) * math.log(2.0)
    return {
        'x': nrm(ks[0], (BATCH, SEQ, D_MODEL), 1.0),
        'c': nrm(ks[1], (BATCH, D_MODEL), 1.0),
        'ctx': nrm(ks[2], (BATCH, CTX_LEN, D_MODEL), 1.0),
        'c_ctx': nrm(ks[3], (D_MODEL,), 1.0),
        'ada_w': nrm(ks[4], (DEPTH, D_MODEL, 6 * D_MODEL), 0.5 * D_MODEL ** -0.5),
        'ada_b': nrm(ks[5], (DEPTH, 6 * D_MODEL), 0.02),
        'norm_g': 1.0 + nrm(ks[6], (DEPTH, 2, D_MODEL), 0.02),
        'final_norm_g': 1.0 + nrm(ks[7], (D_MODEL,), 0.02),
        'mixab_w_in': nrm(ks[8], (n_even, D_MODEL, AB_IN), D_MODEL ** -0.5),
        'mixab_w_out': nrm(ks[9], (n_even, AB_OUT, D_MODEL), AB_OUT ** -0.5),
        'na_rpb': nrm(ks[10], (n_even, NA_HEADS, 2 * NA_KH - 1, 2 * NA_KW - 1), 0.1),
        'ret_w_in': nrm(ks[11], (n_odd, D_MODEL, RET_IN), D_MODEL ** -0.5),
        'ret_w_out': nrm(ks[12], (n_odd, RET_V, D_MODEL), RET_V ** -0.5),
        'ret_decay': base_decay[None, None, :] + nrm(ks[13], (n_odd, 2, RET_HEADS), 0.05),
        'router_w': nrm(ks[14], (D_MODEL, N_EXPERTS), D_MODEL ** -0.5),
        'router_b': nrm(ks[15], (N_EXPERTS,), 0.01),
        'moe_w1': nrm(ks[16], (DEPTH, N_EXPERTS, D_MODEL, D_EXPERT), D_MODEL ** -0.5),
        'moe_w3': nrm(ks[17], (DEPTH, N_EXPERTS, D_MODEL, D_EXPERT), D_MODEL ** -0.5),
        'moe_w2': nrm(ks[18], (DEPTH, N_EXPERTS, D_EXPERT, D_MODEL), D_EXPERT ** -0.5),
    }


def reference(x, c, ctx, c_ctx, ada_w, ada_b, norm_g, final_norm_g, mixab_w_in, mixab_w_out, na_rpb,
              ret_w_in, ret_w_out, ret_decay, router_w, router_b, moe_w1, moe_w3, moe_w2):
    B, L, D = x.shape
    t = jnp.arange(L)
    row, col = t // GRID_W, t % GRID_W
    h, hc = x, ctx
    for layer in range(DEPTH):
        need_ctx = layer < DEPTH - 1
        mod = jax.nn.silu(c) @ ada_w[layer] + ada_b[layer]
        mod_c = jax.nn.silu(c_ctx) @ ada_w[layer] + ada_b[layer]
        sh1, sc1, g1, sh2, sc2, g2 = [m[:, None, :] for m in jnp.split(mod, 6, axis=-1)]
        sh1c, sc1c, g1c, sh2c, sc2c, g2c = jnp.split(mod_c, 6, axis=-1)
        xm = modulate(rmsnorm(h, norm_g[layer, 0]), sh1, sc1)
        cm = modulate(rmsnorm(hc, norm_g[layer, 0]), sh1c, sc1c)
        idx = layer // 2
        if layer % 2 == 0:
            y, yc = fnet_na_mixer(xm, cm, mixab_w_in[idx], mixab_w_out[idx], na_rpb[idx], need_ctx)
        else:
            y, yc = retention_mixer(xm, cm, ret_w_in[idx], ret_w_out[idx], ret_decay[idx], row, col, need_ctx)
        h = h + g1 * y
        xm2 = modulate(rmsnorm(h, norm_g[layer, 1]), sh2, sc2)
        if need_ctx:
            hc = hc + g1c * yc
            cm2 = modulate(rmsnorm(hc, norm_g[layer, 1]), sh2c, sc2c)
            n_c = cm2.shape[0] * cm2.shape[1]
            tok = jnp.concatenate([cm2.reshape(-1, D), xm2.reshape(-1, D)], axis=0)
            f = moe_ffn(tok, router_w, router_b, moe_w1[layer], moe_w3[layer], moe_w2[layer])
            hc = hc + g2c * f[:n_c].reshape(hc.shape)
            h = h + g2 * f[n_c:].reshape(h.shape)
        else:
            f = moe_ffn(xm2.reshape(-1, D), router_w, router_b, moe_w1[layer], moe_w3[layer], moe_w2[layer])
            h = h + g2 * f.reshape(h.shape)
    return rmsnorm(h, final_norm_g)
```

```python
import functools
import math

import numpy as np
import jax
import jax.numpy as jnp
from jax import lax
from jax.experimental import pallas as pl
from jax.experimental.pallas import tpu as pltpu

F32 = jnp.float32
BF16 = jnp.bfloat16

D = 1024
SEQ = 16384
CTX = 256
T = SEQ + CTX
DEPTH = 2
GRID_W = 64
ROWS = SEQ // GRID_W
RMS_EPS = 1e-6
ROPE_BASE = 10000.0

FNET_W = 256
FNET_GD = 64
NA_HEADS = 12
NA_HD = 64
NA_W = NA_HEADS * NA_HD
NA_KH = 8
NA_KW = 16
AB_IN = FNET_W + 3 * NA_W
QKV_W = 3 * NA_W

RET_HEADS = 4
RET_DK = 256
RET_DV = 512
RET_QK = RET_HEADS * RET_DK
RET_V = RET_HEADS * RET_DV
RET_CHUNK = 128

N_EXPERTS = 16
EPG = 4
D_EXPERT = 512

LANES = 128
TM = 256
NT = T // TM
NT_LAT = SEQ // TM
TMM = 256
N_BUCKET_IDS = 64
N_REAL_BUCKETS = 24
XROW_W = D + LANES
FFT_N1 = 128
FFT_N2 = 128
NEG_BIG = -1e30

VMEM_LIMIT = 56 * 1024 * 1024


def _params(sem, vmem=VMEM_LIMIT):
    return pltpu.CompilerParams(dimension_semantics=sem, vmem_limit_bytes=vmem)


def _dot(a, b):
    return jnp.dot(a, b, preferred_element_type=F32)


def _dot_nt(a, b):
    return lax.dot_general(a, b, (((1,), (1,)), ((), ())), preferred_element_type=F32)


def _dot_tn(a, b):
    return lax.dot_general(a, b, (((0,), (0,)), ((), ())), preferred_element_type=F32)


def _rms_mod(x, g, sh, sc):
    ms = jnp.mean(x * x, axis=-1, keepdims=True)
    y = x * lax.rsqrt(ms + RMS_EPS) * g
    return y * (1.0 + sc) + sh


def _resident(shape):
    nd = len(shape)
    return pl.BlockSpec(shape, lambda *_: (0,) * nd, pipeline_mode=pl.Buffered(1))


MOD_TN = 768


def _mod_kernel(cs_ref, w_ref, b_ref, o_ref):
    cs = cs_ref[...]
    s = cs * jax.nn.sigmoid(cs)
    w = w_ref[0]
    r0 = jnp.sum(s[:, 0:1] * w, axis=0, keepdims=True)
    r1 = jnp.sum(s[:, 1:2] * w, axis=0, keepdims=True)
    o_ref[0] = jnp.concatenate([r0, r1], axis=0) + b_ref[0]


def _mod_vectors(c, c_ctx, ada_w, ada_b):
    cs = jnp.stack([c[0], c_ctx], axis=1)
    n = 6 * D
    return pl.pallas_call(
        _mod_kernel,
        grid=(DEPTH, n // MOD_TN),
        in_specs=[
            pl.BlockSpec((D, 2), lambda l, j: (0, 0)),
            pl.BlockSpec((1, D, MOD_TN), lambda l, j: (l, 0, j)),
            pl.BlockSpec((1, 1, MOD_TN), lambda l, j: (l, 0, j)),
        ],
        out_specs=pl.BlockSpec((1, 2, MOD_TN), lambda l, j: (l, 0, j)),
        out_shape=jax.ShapeDtypeStruct((DEPTH, 2, n), F32),
        compiler_params=_params(("arbitrary", "arbitrary")),
        name="mod_vectors",
    )(cs, ada_w, ada_b.reshape(DEPTH, 1, n))


def _mod_spec():
    return pl.BlockSpec((1, 8, D), lambda i, *_: (i // NT_LAT, 0, 0))


def _inproj0_kernel(h_ref, mod_ref, g_ref, w_ref, d64_ref, qkv_ref, br_ref, bi_ref):
    xm = _rms_mod(h_ref[...], g_ref[...], mod_ref[0, 0:1, :], mod_ref[0, 1:2, :]).astype(BF16)
    p = _dot(xm, w_ref[...])
    qkv_ref[...] = p[:, FNET_W:].astype(BF16)
    b = _dot(p[:, :FNET_W].astype(BF16), d64_ref[...])
    br_ref[...] = b[:, :FNET_W].astype(BF16)
    bi_ref[...] = b[:, FNET_W:].astype(BF16)


def _inproj0(h, modv, g, w_in, d64):
    return pl.pallas_call(
        _inproj0_kernel,
        grid=(NT,),
        in_specs=[
            pl.BlockSpec((TM, D), lambda i: (i, 0)),
            _mod_spec(),
            _resident((1, D)),
            _resident((D, AB_IN)),
            _resident((FNET_W, 2 * FNET_W)),
        ],
        out_specs=[
            pl.BlockSpec((TM, QKV_W), lambda i: (i, 0)),
            pl.BlockSpec((TM, FNET_W), lambda i: (i, 0)),
            pl.BlockSpec((TM, FNET_W), lambda i: (i, 0)),
        ],
        out_shape=[
            jax.ShapeDtypeStruct((T, QKV_W), BF16),
            jax.ShapeDtypeStruct((T, FNET_W), BF16),
            jax.ShapeDtypeStruct((T, FNET_W), BF16),
        ],
        compiler_params=_params(("arbitrary",)),
        name="inproj0",
    )(h, modv, g, w_in, d64)


FFT_A_NB = 8
FFT_B_TN = 4096


def _fft_a_kernel(br_ref, bi_ref, ga_ref, v_ref):
    for u in range(FFT_A_NB):
        cols = slice(u * FNET_W, (u + 1) * FNET_W)
        z = _dot(ga_ref[u, :, 0:FFT_N2], br_ref[:, cols]) + _dot(ga_ref[u, :, FFT_N2:], bi_ref[:, cols])
        v_ref[0, u] = z[:FFT_N2].astype(BF16)
        v_ref[1, u] = z[FFT_N2:].astype(BF16)


def _fft_b_kernel(cs_ref, v_ref, o_ref):
    o_ref[...] = _dot(cs_ref[...], v_ref[...]) * (1.0 / math.sqrt(SEQ * FNET_GD))


def _fft_latent(br, bi, ga, csb):
    width = FFT_N1 * FNET_W
    br2 = br.reshape(T // FFT_N1, width)
    bi2 = bi.reshape(T // FFT_N1, width)
    v = pl.pallas_call(
        _fft_a_kernel,
        grid=(FFT_N1 // FFT_A_NB,),
        in_specs=[
            pl.BlockSpec((FFT_N2, FFT_A_NB * FNET_W), lambda j: (0, j)),
            pl.BlockSpec((FFT_N2, FFT_A_NB * FNET_W), lambda j: (0, j)),
            pl.BlockSpec((FFT_A_NB, 2 * FFT_N2, 2 * FFT_N2), lambda j: (j, 0, 0)),
        ],
        out_specs=pl.BlockSpec((2, FFT_A_NB, FFT_N2, FNET_W), lambda j: (0, j, 0, 0)),
        out_shape=jax.ShapeDtypeStruct((2, FFT_N1, FFT_N2, FNET_W), BF16),
        compiler_params=_params(("arbitrary",)),
        name="fft_stage_a",
    )(br2, bi2, ga)
    v2 = v.reshape(2 * FFT_N1, FFT_N2 * FNET_W)
    x = pl.pallas_call(
        _fft_b_kernel,
        grid=(FFT_N2 * FNET_W // FFT_B_TN,),
        in_specs=[
            _resident((FFT_N1, 2 * FFT_N1)),
            pl.BlockSpec((2 * FFT_N1, FFT_B_TN), lambda j: (0, j)),
        ],
        out_specs=pl.BlockSpec((FFT_N1, FFT_B_TN), lambda j: (0, j)),
        out_shape=jax.ShapeDtypeStruct((FFT_N1, FFT_N2 * FNET_W), F32),
        compiler_params=_params(("arbitrary",)),
        name="fft_stage_b",
    )(csb, v2)
    return x.reshape(SEQ, FNET_W)


def _head_pair_masks():
    lane = lax.broadcasted_iota(jnp.int32, (1, LANES), 1)
    return lane < NA_HD


def _ctx_kernel(qkv_ref, br_ref, bi_ref, cs_ref, o_ref):
    a = _dot(cs_ref[:, 0:CTX], br_ref[...]) + _dot(cs_ref[:, CTX:], bi_ref[...])
    o_ref[:, 0:FNET_W] = (a * (1.0 / math.sqrt(CTX * FNET_GD))).astype(BF16)
    m0 = _head_pair_masks()
    for hp in range(NA_HEADS // 2):
        q = qkv_ref[:, hp * LANES:(hp + 1) * LANES]
        k = qkv_ref[:, NA_W + hp * LANES:NA_W + (hp + 1) * LANES]
        v = qkv_ref[:, 2 * NA_W + hp * LANES:2 * NA_W + (hp + 1) * LANES]
        outs = []
        for a_ in range(2):
            qa = jnp.where(m0 if a_ == 0 else jnp.logical_not(m0), q, jnp.zeros_like(q))
            s = _dot_nt(qa, k) * (NA_HD ** -0.5)
            p = jnp.exp(s - jnp.max(s, axis=-1, keepdims=True))
            l = jnp.sum(p, axis=-1, keepdims=True)
            outs.append(_dot(p.astype(BF16), v) / l)
        o_ref[:, FNET_W + hp * LANES:FNET_W + (hp + 1) * LANES] = jnp.where(m0, outs[0], outs[1]).astype(BF16)


def _ctx_mixer(qkv, br, bi, csc):
    return pl.pallas_call(
        _ctx_kernel,
        grid=(1,),
        in_specs=[
            pl.BlockSpec((CTX, QKV_W), lambda i: (SEQ // CTX, 0)),
            pl.BlockSpec((CTX, FNET_W), lambda i: (SEQ // CTX, 0)),
            pl.BlockSpec((CTX, FNET_W), lambda i: (SEQ // CTX, 0)),
            pl.BlockSpec((CTX, 2 * CTX), lambda i: (0, 0)),
        ],
        out_specs=pl.BlockSpec((CTX, D), lambda i: (0, 0)),
        out_shape=jax.ShapeDtypeStruct((CTX, D), BF16),
        compiler_params=_params(("arbitrary",)),
        name="ctx_mixer",
    )(qkv, br, bi, csc)


NA_RB = 8
NA_WIN = NA_KH * GRID_W


def _na_kernel(q_ref, k_ref, v_ref, kc_ref, vc_ref, bias_ref, o_ref):
    b = pl.program_id(1)
    m0 = _head_pair_masks()
    kc = kc_ref[...]
    vc = vc_ref[...]

    def row(i, carry):
        r = b * NA_RB + i
        rs = jnp.clip(r - NA_KH // 2, 0, ROWS - NA_KH)
        e = rs - r + (NA_KH - 1)
        q = q_ref[pl.ds(pl.multiple_of(i * GRID_W, GRID_W), GRID_W), :]
        start = pl.multiple_of(rs * GRID_W, GRID_W)
        kw = k_ref[pl.ds(start, NA_WIN), :]
        vw = v_ref[pl.ds(start, NA_WIN), :]
        outs = []
        for a in range(2):
            qa = jnp.where(m0 if a == 0 else jnp.logical_not(m0), q, jnp.zeros_like(q))
            s1 = _dot_nt(qa, kw) * (NA_HD ** -0.5) + bias_ref[a, e]
            s2 = _dot_nt(qa, kc) * (NA_HD ** -0.5)
            mx = jnp.maximum(jnp.max(s1, axis=-1, keepdims=True), jnp.max(s2, axis=-1, keepdims=True))
            p1 = jnp.exp(s1 - mx)
            p2 = jnp.exp(s2 - mx)
            l = jnp.sum(p1, axis=-1, keepdims=True) + jnp.sum(p2, axis=-1, keepdims=True)
            outs.append((_dot(p1.astype(BF16), vw) + _dot(p2.astype(BF16), vc)) / l)
        o_ref[pl.ds(pl.multiple_of(i * GRID_W, GRID_W), GRID_W), :] = jnp.where(m0, outs[0], outs[1]).astype(BF16)
        return carry

    lax.fori_loop(0, NA_RB, row, 0)


def _na_attention(qkv, bias):
    nq = NA_RB * GRID_W
    nhp = NA_HEADS // 2
    return pl.pallas_call(
        _na_kernel,
        grid=(nhp, ROWS // NA_RB),
        in_specs=[
            pl.BlockSpec((nq, LANES), lambda hp, b: (b, hp)),
            pl.BlockSpec((SEQ, LANES), lambda hp, b: (0, nhp + hp)),
            pl.BlockSpec((SEQ, LANES), lambda hp, b: (0, 2 * nhp + hp)),
            pl.BlockSpec((CTX, LANES), lambda hp, b: (SEQ // CTX, nhp + hp)),
            pl.BlockSpec((CTX, LANES), lambda hp, b: (SEQ // CTX, 2 * nhp + hp)),
            pl.BlockSpec((2, NA_KH, GRID_W, NA_WIN), lambda hp, b: (hp, 0, 0, 0)),
        ],
        out_specs=pl.BlockSpec((nq, LANES), lambda hp, b: (b, hp)),
        out_shape=jax.ShapeDtypeStruct((SEQ, NA_W), BF16),
        compiler_params=_params(("arbitrary", "arbitrary")),
        name="na_attention",
    )(qkv, qkv, qkv, qkv, qkv, bias)


def _na_bias_rows(rpb):
    c = np.arange(GRID_W)[:, None]
    kc = np.arange(GRID_W)[None, :]
    ws = np.clip(c - NA_KW // 2, 0, GRID_W - NA_KW)
    col_ok = (kc >= ws) & (kc < ws + NA_KW)
    dc = np.clip(kc - c + NA_KW - 1, 0, 2 * NA_KW - 2)
    dr = np.arange(NA_KH)[:, None] + np.arange(NA_KH)[None, :]
    g = rpb[:, dr[:, None, :, None], dc[None, :, None, :]]
    g = jnp.where(jnp.asarray(col_ok)[None, None, :, None, :], g.astype(F32), NEG_BIG)
    return g.reshape(NA_HEADS, NA_KH, GRID_W, NA_WIN)


def _lane_shift(x, d):
    return pltpu.roll(x, (LANES - d) % LANES, axis=1)


def _epilogue(h1, mod_ref, g2_ref, rw_ref, rb_ref, tril_ref, run_ref, xrow_ref, meta_ref, cnt_ref):
    xm2 = _rms_mod(h1, g2_ref[...], mod_ref[0, 3:4, :], mod_ref[0, 4:5, :])
    xrow_ref[:, 0:D] = xm2
    xh = xm2.astype(BF16)
    xl = (xm2 - xh.astype(F32)).astype(BF16)
    w = rw_ref[...]
    wh = w.astype(BF16)
    wl = (w - wh.astype(F32)).astype(BF16)
    logits = _dot(xh, wh) + (_dot(xl, wh) + _dot(xh, wl))
    scores = jax.nn.sigmoid(logits)
    biased = scores + rb_ref[...]
    lane = lax.broadcasted_iota(jnp.int32, (TM, LANES), 1)
    k = lane & (EPG - 1)
    valid = lane < N_EXPERTS
    offs = (-3, -2, -1, 1, 2, 3)
    in_grp = {d: (k + d >= 0) & (k + d < EPG) for d in offs}
    rank = jnp.zeros((TM, LANES), F32)
    for d in offs:
        v = _lane_shift(biased, d)
        beats = (v > biased) | (v == biased) if d < 0 else (v > biased)
        rank = rank + jnp.where(in_grp[d] & beats, 1.0, 0.0)
    top2 = rank < 2.0
    t = jnp.where(top2, biased, 0.0)
    gs = t
    for d in offs:
        gs = gs + jnp.where(in_grp[d], _lane_shift(t, d), 0.0)
    beaten = jnp.zeros((TM, LANES), jnp.bool_)
    for d in (-12, -8, -4, 4, 8, 12):
        v = _lane_shift(gs, d)
        ok = (lane + d >= 0) & (lane + d < N_EXPERTS)
        beats = (v > gs) | (v == gs) if d < 0 else (v > gs)
        beaten = beaten | (ok & beats)
    sel = valid & top2 & jnp.logical_not(beaten)
    wsel = jnp.where(sel, scores, 0.0)
    gate = wsel / jnp.sum(wsel, axis=-1, keepdims=True)
    self_f = jnp.where(sel, 1.0, 0.0)
    before = jnp.zeros((TM, LANES), F32)
    for d in (-3, -2, -1):
        before = before + jnp.where(in_grp[d], _lane_shift(self_f, d), 0.0)
    is_lo = sel & (before == 0.0)
    w_lo = jnp.sum(jnp.where(is_lo, gate, 0.0), axis=-1, keepdims=True)
    w_hi = jnp.sum(jnp.where(sel & jnp.logical_not(is_lo), gate, 0.0), axis=-1, keepdims=True)
    code = ((1 << k) + 8 * (lane >> 2)).astype(F32)
    bid = jnp.sum(jnp.where(sel, code, 0.0), axis=-1, keepdims=True)
    onehot = lane.astype(F32) == bid
    cum = _dot(tril_ref[...], jnp.where(onehot, 1.0, 0.0).astype(BF16))
    run = run_ref[0:1, :]
    rank_g = jnp.sum(jnp.where(onehot, cum + run, 0.0), axis=-1, keepdims=True)
    cnt = jnp.sum(jnp.where(onehot, 1.0, 0.0), axis=0, keepdims=True)
    new_run = jnp.broadcast_to(run + cnt, (8, LANES))
    run_ref[...] = new_run
    cnt_ref[...] = new_run
    meta = jnp.where(lane == 0, w_lo, jnp.where(lane == 1, w_hi, jnp.where(lane == 2, bid, jnp.where(lane == 3, rank_g, 0.0))))
    xrow_ref[:, D:XROW_W] = meta
    meta_ref[...] = meta


def _epilogue_specs(ntiles):
    in_specs = [
        _resident((1, D)),
        _resident((D, LANES)),
        _resident((1, LANES)),
        _resident((TM, TM)),
    ]
    out_specs = [
        pl.BlockSpec((TM, D), lambda i: (i, 0)),
        pl.BlockSpec((TM, XROW_W), lambda i: (i, 0)),
        pl.BlockSpec((TM, LANES), lambda i: (i, 0)),
        pl.BlockSpec((8, LANES), lambda i: (0, 0)),
    ]
    n = ntiles * TM
    out_shape = [
        jax.ShapeDtypeStruct((n, D), F32),
        jax.ShapeDtypeStruct((n, XROW_W), F32),
        jax.ShapeDtypeStruct((n, LANES), F32),
        jax.ShapeDtypeStruct((8, LANES), F32),
    ]
    return in_specs, out_specs, out_shape


def _outproj0_kernel(ao_ref, h_ref, mod_ref, w_ref, g2_ref, rw_ref, rb_ref, tril_ref,
                     h1_ref, xrow_ref, meta_ref, cnt_ref, run_ref):
    @pl.when(pl.program_id(0) == 0)
    def _():
        run_ref[...] = jnp.zeros_like(run_ref)

    y = _dot(ao_ref[...], w_ref[...])
    h1 = h_ref[...] + mod_ref[0, 2:3, :] * y
    h1_ref[...] = h1
    _epilogue(h1, mod_ref, g2_ref, rw_ref, rb_ref, tril_ref, run_ref, xrow_ref, meta_ref, cnt_ref)


def _outproj0(ao, h, modv, w_out, g2, rw, rb, tril):
    e_in, e_out, e_shape = _epilogue_specs(NT)
    return pl.pallas_call(
        _outproj0_kernel,
        grid=(NT,),
        in_specs=[
            pl.BlockSpec((TM, D), lambda i: (i, 0)),
            pl.BlockSpec((TM, D), lambda i: (i, 0)),
            _mod_spec(),
            _resident((D, D)),
        ] + e_in,
        out_specs=e_out,
        out_shape=e_shape,
        scratch_shapes=[pltpu.VMEM((8, LANES), F32)],
        compiler_params=_params(("arbitrary",)),
        name="outproj0",
    )(ao, h, modv, w_out, g2, rw, rb, tril)


def _bucket_tables():
    lo = np.zeros(N_BUCKET_IDS, np.int32)
    hi = np.zeros(N_BUCKET_IDS, np.int32)
    for g in range(N_EXPERTS // EPG):
        for a in range(EPG):
            for b in range(a + 1, EPG):
                i = 16 * g + (1 << a) + (1 << b)
                lo[i] = EPG * g + a
                hi[i] = EPG * g + b
    return lo, hi


def _route_plan(meta, cnt, n_tiles, layer):
    counts = cnt[0, :N_BUCKET_IDS].astype(jnp.int32)
    nt_b = (counts + TMM - 1) // TMM
    ends = jnp.cumsum(nt_b)
    starts = ends - nt_b
    bid = meta[:, 2].astype(jnp.int32)
    rank = meta[:, 3].astype(jnp.int32)
    pos = starts[bid] * TMM + rank
    n_used = ends[-1]
    tile = jnp.minimum(jnp.arange(n_tiles, dtype=jnp.int32), n_used - 1)
    tile_b = jnp.sum((ends[None, :] <= tile[:, None]).astype(jnp.int32), axis=1)
    lo, hi = _bucket_tables()
    e_lo = jnp.asarray(lo)[tile_b] + layer * N_EXPERTS
    e_hi = jnp.asarray(hi)[tile_b] + layer * N_EXPERTS
    return pos.astype(jnp.int32), e_lo.astype(jnp.int32), e_hi.astype(jnp.int32), n_used.reshape(1).astype(jnp.int32)


def _dispatch_kernel(pos_ref, x_ref, xs_in_ref, xs_ref, sem):
    del xs_in_ref
    base = pl.program_id(0) * TM

    def issue(r, carry):
        p = pos_ref[base + r]
        pltpu.make_async_copy(x_ref.at[pl.ds(r, 1), :], xs_ref.at[pl.ds(p, 1), :], sem).start()
        return carry

    lax.fori_loop(0, TM, issue, 0)
    pltpu.make_async_copy(x_ref, xs_ref.at[pl.ds(0, TM), :], sem).wait()


def _dispatch(pos, xrow, n_tiles):
    ntok = xrow.shape[0]
    xs0 = jnp.zeros((n_tiles * TMM, XROW_W), F32)
    return pl.pallas_call(
        _dispatch_kernel,
        grid_spec=pltpu.PrefetchScalarGridSpec(
            num_scalar_prefetch=1,
            grid=(ntok // TM,),
            in_specs=[
                pl.BlockSpec((TM, XROW_W), lambda i, pos: (i, 0)),
                pl.BlockSpec(memory_space=pl.ANY),
            ],
            out_specs=pl.BlockSpec(memory_space=pl.ANY),
            scratch_shapes=[pltpu.SemaphoreType.DMA(())],
        ),
        out_shape=jax.ShapeDtypeStruct((n_tiles * TMM, XROW_W), F32),
        input_output_aliases={2: 0},
        compiler_params=_params(("arbitrary",)),
        name="moe_dispatch",
    )(pos, xrow, xs0)


def _moe_kernel(elo_ref, ehi_ref, nu_ref, xs_ref, w1a, w3a, w2a, w1b, w3b, w2b, ys_ref):
    j = pl.program_id(0)

    @pl.when(j < nu_ref[0])
    def _():
        x = xs_ref[:, 0:D].astype(BF16)
        g_lo = xs_ref[:, D:D + 1]
        g_hi = xs_ref[:, D + 1:D + 2]

        def expert(w1, w3, w2):
            a = _dot(x, w1[0])
            hid = (a * jax.nn.sigmoid(a)) * _dot(x, w3[0])
            return _dot(hid.astype(BF16), w2[0])

        ys_ref[...] = g_lo * expert(w1a, w3a, w2a) + g_hi * expert(w1b, w3b, w2b)

    @pl.when(j >= nu_ref[0])
    def _():
        ys_ref[...] = jnp.zeros_like(ys_ref)


def _moe(e_lo, e_hi, n_used, xs, w1, w3, w2, n_tiles):
    def wspec(shape, which):
        if which == 0:
            return pl.BlockSpec(shape, lambda j, lo, hi, nu: (lo[j], 0, 0))
        return pl.BlockSpec(shape, lambda j, lo, hi, nu: (hi[j], 0, 0))

    s13 = (1, D, D_EXPERT)
    s2 = (1, D_EXPERT, D)
    return pl.pallas_call(
        _moe_kernel,
        grid_spec=pltpu.PrefetchScalarGridSpec(
            num_scalar_prefetch=3,
            grid=(n_tiles,),
            in_specs=[
                pl.BlockSpec((TMM, XROW_W), lambda j, lo, hi, nu: (j, 0)),
                wspec(s13, 0), wspec(s13, 0), wspec(s2, 0),
                wspec(s13, 1), wspec(s13, 1), wspec(s2, 1),
            ],
            out_specs=pl.BlockSpec((TMM, D), lambda j, lo, hi, nu: (j, 0)),
        ),
        out_shape=jax.ShapeDtypeStruct((n_tiles * TMM, D), F32),
        compiler_params=_params(("arbitrary",)),
        name="moe_experts",
    )(e_lo, e_hi, n_used, xs, w1, w3, w2, w1, w3, w2)


def _gather_issue(pos_ref, ys_hbm, buf, sem, tile, slot):
    base = tile * TM

    def issue(r, carry):
        p = pos_ref[base + r]
        pltpu.make_async_copy(ys_hbm.at[pl.ds(p, 1), :], buf.at[slot, pl.ds(r, 1), :], sem.at[slot]).start()
        return carry

    lax.fori_loop(0, TM, issue, 0)


def _gathered_rows(pos_ref, ys_hbm, buf, sem):
    i = pl.program_id(0)
    slot = i % 2

    @pl.when(i == 0)
    def _():
        _gather_issue(pos_ref, ys_hbm, buf, sem, 0, 0)

    @pl.when(i + 1 < pl.num_programs(0))
    def _():
        _gather_issue(pos_ref, ys_hbm, buf, sem, i + 1, 1 - slot)

    pltpu.make_async_copy(ys_hbm.at[pl.ds(0, TM), :], buf.at[slot], sem.at[slot]).wait()
    return buf[slot]


_GATHER_SCRATCH = [pltpu.VMEM((2, TM, D), F32), pltpu.SemaphoreType.DMA((2,))]


def _rope_store(x, tabs, out_ref):
    cr, sr, cc, sc = tabs
    for ch in range(RET_QK // LANES):
        xc = x[:, ch * LANES:(ch + 1) * LANES]
        xr = pltpu.roll(xc, LANES // 2, axis=1)
        y = xc * cr + xr * sr if ch % 2 == 0 else xc * cc + xr * sc
        out_ref[:, ch * LANES:(ch + 1) * LANES] = y.astype(BF16)


def _inproj1_kernel(pos_ref, h_ref, ys_hbm, mod0_ref, mod1_ref, g_ref, w_ref, cr_ref, sr_ref, cc_ref, sc_ref,
                    h2_ref, q_ref, k_ref, v_ref, gg_ref, buf, sem):
    f = _gathered_rows(pos_ref, ys_hbm, buf, sem)
    h2 = h_ref[...] + mod0_ref[0, 5:6, :] * f
    h2_ref[...] = h2
    xm = _rms_mod(h2, g_ref[...], mod1_ref[0, 0:1, :], mod1_ref[0, 1:2, :]).astype(BF16)
    tabs = (cr_ref[...], sr_ref[...], cc_ref[...], sc_ref[...])
    _rope_store(_dot(xm, w_ref[:, 0:RET_QK]), tabs, q_ref)
    _rope_store(_dot(xm, w_ref[:, RET_QK:2 * RET_QK]) * (RET_DK ** -0.5), tabs, k_ref)
    v_ref[...] = _dot(xm, w_ref[:, 2 * RET_QK:2 * RET_QK + RET_V]).astype(BF16)
    gg_ref[...] = _dot(xm, w_ref[:, 2 * RET_QK + RET_V:]).astype(BF16)


def _inproj1(pos, h1, ys, modv0, modv1, g, w_in, tabs):
    tok = lambda w: pl.BlockSpec((TM, w), lambda i, pos: (i, 0))
    res = lambda shape: pl.BlockSpec(shape, lambda i, pos: (0,) * len(shape), pipeline_mode=pl.Buffered(1))
    mod = pl.BlockSpec((1, 8, D), lambda i, pos: (i // NT_LAT, 0, 0))
    return pl.pallas_call(
        _inproj1_kernel,
        grid_spec=pltpu.PrefetchScalarGridSpec(
            num_scalar_prefetch=1,
            grid=(NT,),
            in_specs=[tok(D), pl.BlockSpec(memory_space=pl.ANY), mod, mod, res((1, D)), res((D, 2 * RET_QK + 2 * RET_V)),
                      tok(LANES), tok(LANES), tok(LANES), tok(LANES)],
            out_specs=[tok(D), tok(RET_QK), tok(RET_QK), tok(RET_V), tok(RET_V)],
            scratch_shapes=_GATHER_SCRATCH,
        ),
        out_shape=[
            jax.ShapeDtypeStruct((T, D), F32),
            jax.ShapeDtypeStruct((T, RET_QK), BF16),
            jax.ShapeDtypeStruct((T, RET_QK), BF16),
            jax.ShapeDtypeStruct((T, RET_V), BF16),
            jax.ShapeDtypeStruct((T, RET_V), BF16),
        ],
        compiler_params=_params(("arbitrary",)),
        name="inproj1",
    )(pos, h1, ys, modv0, modv1, g, w_in, *tabs)


def _rope_tables():
    n = LANES // 2
    inv_freq = ROPE_BASE ** (-jnp.arange(n, dtype=F32) / n)
    t = jnp.arange(SEQ)

    def tab(p):
        ang = p.astype(F32)[:, None] * inv_freq[None, :]
        cos, sin = jnp.cos(ang), jnp.sin(ang)
        c = jnp.concatenate([cos, cos], axis=1)
        s = jnp.concatenate([-sin, sin], axis=1)
        c = jnp.concatenate([c, jnp.ones((CTX, LANES), F32)], axis=0)
        s = jnp.concatenate([s, jnp.zeros((CTX, LANES), F32)], axis=0)
        return c, s

    cr, sr = tab(t // GRID_W)
    cc, sc = tab(t % GRID_W)
    return cr, sr, cc, sc


RET_NCHUNK = T // RET_CHUNK
RET_CTX_CHUNKS = CTX // RET_CHUNK
RET_LAT_CHUNKS = SEQ // RET_CHUNK
RET_NHD = 2 * RET_HEADS


def _ret_kernel(dec_ref, qf_ref, kf_ref, vf_ref, qb_ref, kb_ref, vb_ref, of_ref, ob_ref,
                state, dm, qd, kd, cd):
    C = RET_CHUNK

    @pl.when(pl.program_id(0) == 0)
    def _():
        state[...] = jnp.zeros_like(state)
        n = lax.broadcasted_iota(jnp.int32, (C, C), 0).astype(F32)
        m = lax.broadcasted_iota(jnp.int32, (C, C), 1).astype(F32)
        for d in range(2):
            for h in range(RET_HEADS):
                idx = d * RET_HEADS + h
                lg = jnp.log1p(-jnp.exp(jnp.full((C, C), dec_ref[d, h], F32)))
                diff = n - m if d == 0 else m - n
                low = diff >= 0.0
                dm[idx] = jnp.where(low, jnp.exp(jnp.where(low, diff, 0.0) * lg), 0.0)
                qpow = n + 1.0 if d == 0 else C - n
                kpow = (C - 1.0) - n if d == 0 else n
                qcol = jnp.exp(qpow * lg)
                kcol = jnp.exp(kpow * lg)
                qd[idx] = jnp.concatenate([qcol] * (RET_DV // C), axis=1)
                kd[idx] = jnp.concatenate([kcol] * (RET_DK // C), axis=1)
                cd[idx] = jnp.exp(C * lg[0:8, :])

    for d, (q_ref, k_ref, v_ref, o_ref) in enumerate(((qf_ref, kf_ref, vf_ref, of_ref),
                                                       (qb_ref, kb_ref, vb_ref, ob_ref))):
        for h in range(RET_HEADS):
            idx = d * RET_HEADS + h
            q = q_ref[:, h * RET_DK:(h + 1) * RET_DK]
            k = k_ref[:, h * RET_DK:(h + 1) * RET_DK]
            v = v_ref[:, h * RET_DV:(h + 1) * RET_DV]
            st = state[idx]
            s = _dot_nt(q, k) * dm[idx]
            o = _dot(s.astype(BF16), v) + qd[idx] * _dot(q, st.astype(BF16))
            o_ref[:, h * RET_DV:(h + 1) * RET_DV] = o.astype(BF16)
            kdk = (k.astype(F32) * kd[idx]).astype(BF16)
            cdv = jnp.concatenate([cd[idx, 0:1, :]] * (RET_DV // C), axis=1)
            state[idx] = st * cdv + _dot_tn(kdk, v)


def _ret_scan(decay, q, k, v):
    def fwd(s):
        return jnp.where(s < RET_CTX_CHUNKS, RET_LAT_CHUNKS + s, s - RET_CTX_CHUNKS)

    def bwd(s):
        return RET_NCHUNK - 1 - s

    def spec(w, f):
        return pl.BlockSpec((RET_CHUNK, w), lambda s: (f(s), 0))

    return pl.pallas_call(
        _ret_kernel,
        grid=(RET_NCHUNK,),
        in_specs=[
            pl.BlockSpec(memory_space=pltpu.SMEM),
            spec(RET_QK, fwd), spec(RET_QK, fwd), spec(RET_V, fwd),
            spec(RET_QK, bwd), spec(RET_QK, bwd), spec(RET_V, bwd),
        ],
        out_specs=[spec(RET_V, fwd), spec(RET_V, bwd)],
        out_shape=[jax.ShapeDtypeStruct((T, RET_V), BF16), jax.ShapeDtypeStruct((T, RET_V), BF16)],
        scratch_shapes=[
            pltpu.VMEM((RET_NHD, RET_DK, RET_DV), F32),
            pltpu.VMEM((RET_NHD, RET_CHUNK, RET_CHUNK), F32),
            pltpu.VMEM((RET_NHD, RET_CHUNK, RET_DV), F32),
            pltpu.VMEM((RET_NHD, RET_CHUNK, RET_DK), F32),
            pltpu.VMEM((RET_NHD, 8, RET_CHUNK), F32),
        ],
        compiler_params=_params(("arbitrary",)),
        name="retention_scan",
    )(decay, q, k, v, q, k, v)


def _outproj1_kernel(of_ref, ob_ref, gg_ref, h_ref, mod_ref, w_ref, g2_ref, rw_ref, rb_ref, tril_ref,
                     h1_ref, xrow_ref, meta_ref, cnt_ref, run_ref):
    @pl.when(pl.program_id(0) == 0)
    def _():
        run_ref[...] = jnp.zeros_like(run_ref)

    y = jnp.zeros((TM, D), F32)
    for hh in range(RET_HEADS):
        cols = slice(hh * RET_DV, (hh + 1) * RET_DV)
        o = of_ref[:, cols].astype(F32) + ob_ref[:, cols].astype(F32)
        on = o * lax.rsqrt(jnp.mean(o * o, axis=-1, keepdims=True) + RMS_EPS)
        g = gg_ref[:, cols].astype(F32)
        z = (g * jax.nn.sigmoid(g)) * on
        y = y + _dot(z.astype(BF16), w_ref[cols, :])
    h1 = h_ref[...] + mod_ref[0, 2:3, :] * y
    h1_ref[...] = h1
    _epilogue(h1, mod_ref, g2_ref, rw_ref, rb_ref, tril_ref, run_ref, xrow_ref, meta_ref, cnt_ref)


def _outproj1(o_f, o_b, gg, h, modv, w_out, g2, rw, rb, tril):
    e_in, e_out, e_shape = _epilogue_specs(NT_LAT)
    tok = lambda w: pl.BlockSpec((TM, w), lambda i: (i, 0))
    return pl.pallas_call(
        _outproj1_kernel,
        grid=(NT_LAT,),
        in_specs=[tok(RET_V), tok(RET_V), tok(RET_V), tok(D), _mod_spec(), _resident((RET_V, D))] + e_in,
        out_specs=e_out,
        out_shape=e_shape,
        scratch_shapes=[pltpu.VMEM((8, LANES), F32)],
        compiler_params=_params(("arbitrary",)),
        name="outproj1",
    )(o_f, o_b, gg, h, modv, w_out, g2, rw, rb, tril)


def _final_kernel(pos_ref, h_ref, ys_hbm, mod_ref, g_ref, o_ref, buf, sem):
    f = _gathered_rows(pos_ref, ys_hbm, buf, sem)
    h = h_ref[...] + mod_ref[0, 5:6, :] * f
    o_ref[...] = h * lax.rsqrt(jnp.mean(h * h, axis=-1, keepdims=True) + RMS_EPS) * g_ref[...]


def _final(pos, h, ys, modv, g):
    tok = pl.BlockSpec((TM, D), lambda i, pos: (i, 0))
    return pl.pallas_call(
        _final_kernel,
        grid_spec=pltpu.PrefetchScalarGridSpec(
            num_scalar_prefetch=1,
            grid=(NT_LAT,),
            in_specs=[tok, pl.BlockSpec(memory_space=pl.ANY),
                      pl.BlockSpec((1, 8, D), lambda i, pos: (0, 0, 0)),
                      pl.BlockSpec((1, D), lambda i, pos: (0, 0))],
            out_specs=tok,
            scratch_shapes=_GATHER_SCRATCH,
        ),
        out_shape=jax.ShapeDtypeStruct((SEQ, D), F32),
        compiler_params=_params(("arbitrary",)),
        name="final_norm",
    )(pos, h, ys, modv, g)


def _dft_constants():
    c = np.arange(FNET_GD)
    ang = 2.0 * np.pi * np.outer(c, c) / FNET_GD
    d64 = np.zeros((FNET_W, 2 * FNET_W))
    for g in range(FNET_W // FNET_GD):
        sl = slice(g * FNET_GD, (g + 1) * FNET_GD)
        d64[sl, sl] = np.cos(ang)
        d64[sl, FNET_W + g * FNET_GD:FNET_W + (g + 1) * FNET_GD] = -np.sin(ang)
    k2 = np.arange(FFT_N2)[:, None]
    n2 = np.arange(FFT_N2)[None, :]
    ga = np.zeros((FFT_N1, 2 * FFT_N2, 2 * FFT_N2))
    for n1 in range(FFT_N1):
        th = 2.0 * np.pi * ((k2 * (n1 + FFT_N1 * n2)) % SEQ) / SEQ
        cs, sn = np.cos(th), np.sin(th)
        ga[n1] = np.block([[cs, sn], [-sn, cs]])
    k1 = np.arange(FFT_N1)
    ph = 2.0 * np.pi * np.outer(k1, k1) / FFT_N1
    csb = np.concatenate([np.cos(ph), np.sin(ph)], axis=1)
    p = np.arange(CTX)
    pc = 2.0 * np.pi * (np.outer(p, p) % CTX) / CTX
    csc = np.concatenate([np.cos(pc), np.sin(pc)], axis=1)
    as_bf16 = lambda a: jnp.asarray(a, dtype=F32).astype(BF16)
    return as_bf16(d64), as_bf16(ga), as_bf16(csb), as_bf16(csc)


def _mod_table(mod_l):
    t = mod_l.reshape(2, 6, D)
    return jnp.concatenate([t, jnp.zeros((2, 2, D), F32)], axis=1)


def kernel(x, c, ctx, c_ctx, ada_w, ada_b, norm_g, final_norm_g, mixab_w_in, mixab_w_out, na_rpb,
           ret_w_in, ret_w_out, ret_decay, router_w, router_b, moe_w1, moe_w3, moe_w2):
    d64, ga, csb, csc = _dft_constants()
    tril = jnp.asarray(np.tril(np.ones((TM, TM)), -1), dtype=BF16)
    rw = jnp.pad(router_w, ((0, 0), (0, LANES - N_EXPERTS)))
    rb = jnp.pad(router_b, (0, LANES - N_EXPERTS)).reshape(1, LANES)
    w1 = moe_w1.reshape(DEPTH * N_EXPERTS, D, D_EXPERT).astype(BF16)
    w3 = moe_w3.reshape(DEPTH * N_EXPERTS, D, D_EXPERT).astype(BF16)
    w2 = moe_w2.reshape(DEPTH * N_EXPERTS, D_EXPERT, D).astype(BF16)

    mod = _mod_vectors(c, c_ctx, ada_w, ada_b)
    modv0, modv1 = _mod_table(mod[0]), _mod_table(mod[1])
    h0 = jnp.concatenate([x[0], ctx[0]], axis=0)

    qkv, br, bi = _inproj0(h0, modv0, norm_g[0, 0].reshape(1, D), mixab_w_in[0].astype(BF16), d64)
    a_lat = _fft_latent(br, bi, ga, csb)
    ao_ctx = _ctx_mixer(qkv, br, bi, csc)
    o_lat = _na_attention(qkv, _na_bias_rows(na_rpb[0]))
    ao = jnp.concatenate([jnp.concatenate([a_lat.astype(BF16), o_lat], axis=1), ao_ctx], axis=0)
    h1, xrow, meta, cnt = _outproj0(ao, h0, modv0, mixab_w_out[0].astype(BF16), norm_g[0, 1].reshape(1, D),
                                    rw, rb, tril)
    n_tiles0 = NT + N_REAL_BUCKETS
    pos0, e_lo, e_hi, n_used = _route_plan(meta, cnt, n_tiles0, 0)
    xs = _dispatch(pos0, xrow, n_tiles0)
    ys0 = _moe(e_lo, e_hi, n_used, xs, w1, w3, w2, n_tiles0)

    h2, q, k, v, gg = _inproj1(pos0, h1, ys0, modv0, modv1, norm_g[1, 0].reshape(1, D),
                               ret_w_in[0].astype(BF16), _rope_tables())
    o_f, o_b = _ret_scan(ret_decay[0].astype(F32), q, k, v)
    h3, xrow1, meta1, cnt1 = _outproj1(o_f, o_b, gg, h2, modv1, ret_w_out[0].astype(BF16),
                                       norm_g[1, 1].reshape(1, D), rw, rb, tril)
    n_tiles1 = NT_LAT + N_REAL_BUCKETS
    pos1, e_lo1, e_hi1, n_used1 = _route_plan(meta1, cnt1, n_tiles1, 1)
    xs1 = _dispatch(pos1, xrow1, n_tiles1)
    ys1 = _moe(e_lo1, e_hi1, n_used1, xs1, w1, w3, w2, n_tiles1)
    out = _final(pos1, h3, ys1, modv1, final_norm_g.reshape(1, D))
    return out[None]
```

```python
import functools
import math

import numpy as np
import jax
import jax.numpy as jnp
from jax import lax
from jax.experimental import pallas as pl
from jax.experimental.pallas import tpu as pltpu

F32 = jnp.float32
BF16 = jnp.bfloat16

D = 1024
SEQ = 16384
CTX = 256
T = SEQ + CTX
DEPTH = 2
GRID_W = 64
ROWS = SEQ // GRID_W
RMS_EPS = 1e-6
ROPE_BASE = 10000.0

FNET_W = 256
FNET_GD = 64
NA_HEADS = 12
NA_HD = 64
NA_W = NA_HEADS * NA_HD
NA_KH = 8
NA_KW = 16
AB_IN = FNET_W + 3 * NA_W
QKV_W = 3 * NA_W

RET_HEADS = 4
RET_DK = 256
RET_DV = 512
RET_QK = RET_HEADS * RET_DK
RET_V = RET_HEADS * RET_DV
RET_CHUNK = 128

N_EXPERTS = 16
EPG = 4
D_EXPERT = 512

LANES = 128
TM = 256
NT = T // TM
NT_LAT = SEQ // TM
TMM = 256
N_BUCKET_IDS = 64
N_REAL_BUCKETS = 24
XROW_W = D + LANES
FFT_N1 = 128
FFT_N2 = 128
NEG_BIG = -1e30

VMEM_LIMIT = 56 * 1024 * 1024


def _params(sem, vmem=VMEM_LIMIT):
    return pltpu.CompilerParams(dimension_semantics=sem, vmem_limit_bytes=vmem)


def _dot(a, b):
    return jnp.dot(a, b, preferred_element_type=F32)


def _dot_nt(a, b):
    return lax.dot_general(a, b, (((1,), (1,)), ((), ())), preferred_element_type=F32)


def _dot_tn(a, b):
    return lax.dot_general(a, b, (((0,), (0,)), ((), ())), preferred_element_type=F32)


def _rms_mod(x, g, sh, sc):
    ms = jnp.mean(x * x, axis=-1, keepdims=True)
    y = x * lax.rsqrt(ms + RMS_EPS) * g
    return y * (1.0 + sc) + sh


def _resident(shape):
    nd = len(shape)
    return pl.BlockSpec(shape, lambda *_: (0,) * nd, pipeline_mode=pl.Buffered(1))


MOD_TN = 768


def _mod_kernel(cs_ref, w_ref, b_ref, o_ref):
    cs = cs_ref[...]
    s = cs * jax.nn.sigmoid(cs)
    w = w_ref[0]
    r0 = jnp.sum(s[:, 0:1] * w, axis=0, keepdims=True)
    r1 = jnp.sum(s[:, 1:2] * w, axis=0, keepdims=True)
    o_ref[0] = jnp.concatenate([r0, r1], axis=0) + b_ref[0]


def _mod_vectors(c, c_ctx, ada_w, ada_b):
    cs = jnp.stack([c[0], c_ctx], axis=1)
    n = 6 * D
    return pl.pallas_call(
        _mod_kernel,
        grid=(DEPTH, n // MOD_TN),
        in_specs=[
            pl.BlockSpec((D, 2), lambda l, j: (0, 0)),
            pl.BlockSpec((1, D, MOD_TN), lambda l, j: (l, 0, j)),
            pl.BlockSpec((1, 1, MOD_TN), lambda l, j: (l, 0, j)),
        ],
        out_specs=pl.BlockSpec((1, 2, MOD_TN), lambda l, j: (l, 0, j)),
        out_shape=jax.ShapeDtypeStruct((DEPTH, 2, n), F32),
        compiler_params=_params(("arbitrary", "arbitrary")),
        name="mod_vectors",
    )(cs, ada_w, ada_b.reshape(DEPTH, 1, n))


def _mod_spec():
    return pl.BlockSpec((1, 8, D), lambda i, *_: (i // NT_LAT, 0, 0))


def _inproj0_kernel(h_ref, mod_ref, g_ref, w_ref, d64_ref, qkv_ref, br_ref, bi_ref):
    xm = _rms_mod(h_ref[...], g_ref[...], mod_ref[0, 0:1, :], mod_ref[0, 1:2, :]).astype(BF16)
    p = _dot(xm, w_ref[...])
    qkv_ref[...] = p[:, FNET_W:].astype(BF16)
    b = _dot(p[:, :FNET_W].astype(BF16), d64_ref[...])
    br_ref[...] = b[:, :FNET_W].astype(BF16)
    bi_ref[...] = b[:, FNET_W:].astype(BF16)


def _inproj0(h, modv, g, w_in, d64):
    return pl.pallas_call(
        _inproj0_kernel,
        grid=(NT,),
        in_specs=[
            pl.BlockSpec((TM, D), lambda i: (i, 0)),
            _mod_spec(),
            _resident((1, D)),
            _resident((D, AB_IN)),
            _resident((FNET_W, 2 * FNET_W)),
        ],
        out_specs=[
            pl.BlockSpec((TM, QKV_W), lambda i: (i, 0)),
            pl.BlockSpec((TM, FNET_W), lambda i: (i, 0)),
            pl.BlockSpec((TM, FNET_W), lambda i: (i, 0)),
        ],
        out_shape=[
            jax.ShapeDtypeStruct((T, QKV_W), BF16),
            jax.ShapeDtypeStruct((T, FNET_W), BF16),
            jax.ShapeDtypeStruct((T, FNET_W), BF16),
        ],
        compiler_params=_params(("arbitrary",)),
        name="inproj0",
    )(h, modv, g, w_in, d64)


FFT_A_NB = 8
FFT_B_TN = 4096


def _fft_a_kernel(br_ref, bi_ref, ga_ref, v_ref):
    for u in range(FFT_A_NB):
        cols = slice(u * FNET_W, (u + 1) * FNET_W)
        z = _dot(ga_ref[u, :, 0:FFT_N2], br_ref[:, cols]) + _dot(ga_ref[u, :, FFT_N2:], bi_ref[:, cols])
        v_ref[0, u] = z[:FFT_N2].astype(BF16)
        v_ref[1, u] = z[FFT_N2:].astype(BF16)


def _fft_b_kernel(cs_ref, v_ref, o_ref):
    o_ref[...] = _dot(cs_ref[...], v_ref[...]) * (1.0 / math.sqrt(SEQ * FNET_GD))


def _fft_latent(br, bi, ga, csb):
    width = FFT_N1 * FNET_W
    br2 = br.reshape(T // FFT_N1, width)
    bi2 = bi.reshape(T // FFT_N1, width)
    v = pl.pallas_call(
        _fft_a_kernel,
        grid=(FFT_N1 // FFT_A_NB,),
        in_specs=[
            pl.BlockSpec((FFT_N2, FFT_A_NB * FNET_W), lambda j: (0, j)),
            pl.BlockSpec((FFT_N2, FFT_A_NB * FNET_W), lambda j: (0, j)),
            pl.BlockSpec((FFT_A_NB, 2 * FFT_N2, 2 * FFT_N2), lambda j: (j, 0, 0)),
        ],
        out_specs=pl.BlockSpec((2, FFT_A_NB, FFT_N2, FNET_W), lambda j: (0, j, 0, 0)),
        out_shape=jax.ShapeDtypeStruct((2, FFT_N1, FFT_N2, FNET_W), BF16),
        compiler_params=_params(("arbitrary",)),
        name="fft_stage_a",
    )(br2, bi2, ga)
    v2 = v.reshape(2 * FFT_N1, FFT_N2 * FNET_W)
    x = pl.pallas_call(
        _fft_b_kernel,
        grid=(FFT_N2 * FNET_W // FFT_B_TN,),
        in_specs=[
            _resident((FFT_N1, 2 * FFT_N1)),
            pl.BlockSpec((2 * FFT_N1, FFT_B_TN), lambda j: (0, j)),
        ],
        out_specs=pl.BlockSpec((FFT_N1, FFT_B_TN), lambda j: (0, j)),
        out_shape=jax.ShapeDtypeStruct((FFT_N1, FFT_N2 * FNET_W), F32),
        compiler_params=_params(("arbitrary",)),
        name="fft_stage_b",
    )(csb, v2)
    return x.reshape(SEQ, FNET_W)


def _head_pair_masks():
    lane = lax.broadcasted_iota(jnp.int32, (1, LANES), 1)
    return lane < NA_HD


def _ctx_kernel(qkv_ref, br_ref, bi_ref, cs_ref, o_ref):
    a = _dot(cs_ref[:, 0:CTX], br_ref[...]) + _dot(cs_ref[:, CTX:], bi_ref[...])
    o_ref[:, 0:FNET_W] = (a * (1.0 / math.sqrt(CTX * FNET_GD))).astype(BF16)
    m0 = _head_pair_masks()
    for hp in range(NA_HEADS // 2):
        q = qkv_ref[:, hp * LANES:(hp + 1) * LANES]
        k = qkv_ref[:, NA_W + hp * LANES:NA_W + (hp + 1) * LANES]
        v = qkv_ref[:, 2 * NA_W + hp * LANES:2 * NA_W + (hp + 1) * LANES]
        outs = []
        for a_ in range(2):
            qa = jnp.where(m0 if a_ == 0 else jnp.logical_not(m0), q, jnp.zeros_like(q))
            s = _dot_nt(qa, k) * (NA_HD ** -0.5)
            p = jnp.exp(s - jnp.max(s, axis=-1, keepdims=True))
            l = jnp.sum(p, axis=-1, keepdims=True)
            outs.append(_dot(p.astype(BF16), v) / l)
        o_ref[:, FNET_W + hp * LANES:FNET_W + (hp + 1) * LANES] = jnp.where(m0, outs[0], outs[1]).astype(BF16)


def _ctx_mixer(qkv, br, bi, csc):
    return pl.pallas_call(
        _ctx_kernel,
        grid=(1,),
        in_specs=[
            pl.BlockSpec((CTX, QKV_W), lambda i: (SEQ // CTX, 0)),
            pl.BlockSpec((CTX, FNET_W), lambda i: (SEQ // CTX, 0)),
            pl.BlockSpec((CTX, FNET_W), lambda i: (SEQ // CTX, 0)),
            pl.BlockSpec((CTX, 2 * CTX), lambda i: (0, 0)),
        ],
        out_specs=pl.BlockSpec((CTX, D), lambda i: (0, 0)),
        out_shape=jax.ShapeDtypeStruct((CTX, D), BF16),
        compiler_params=_params(("arbitrary",)),
        name="ctx_mixer",
    )(qkv, br, bi, csc)


NA_RB = 8
NA_WIN = NA_KH * GRID_W


def _na_kernel(q_ref, k_ref, v_ref, kc_ref, vc_ref, bias_ref, o_ref):
    b = pl.program_id(1)
    m0 = _head_pair_masks()
    kc = kc_ref[...]
    vc = vc_ref[...]

    def row(i, carry):
        r = b * NA_RB + i
        rs = jnp.clip(r - NA_KH // 2, 0, ROWS - NA_KH)
        e = rs - r + (NA_KH - 1)
        q = q_ref[pl.ds(pl.multiple_of(i * GRID_W, GRID_W), GRID_W), :]
        start = pl.multiple_of(rs * GRID_W, GRID_W)
        kw = k_ref[pl.ds(start, NA_WIN), :]
        vw = v_ref[pl.ds(start, NA_WIN), :]
        outs = []
        for a in range(2):
            qa = jnp.where(m0 if a == 0 else jnp.logical_not(m0), q, jnp.zeros_like(q))
            s1 = _dot_nt(qa, kw) * (NA_HD ** -0.5) + bias_ref[a, e]
            s2 = _dot_nt(qa, kc) * (NA_HD ** -0.5)
            mx = jnp.maximum(jnp.max(s1, axis=-1, keepdims=True), jnp.max(s2, axis=-1, keepdims=True))
            p1 = jnp.exp(s1 - mx)
            p2 = jnp.exp(s2 - mx)
            l = jnp.sum(p1, axis=-1, keepdims=True) + jnp.sum(p2, axis=-1, keepdims=True)
            outs.append((_dot(p1.astype(BF16), vw) + _dot(p2.astype(BF16), vc)) / l)
        o_ref[pl.ds(pl.multiple_of(i * GRID_W, GRID_W), GRID_W), :] = jnp.where(m0, outs[0], outs[1]).astype(BF16)
        return carry

    lax.fori_loop(0, NA_RB, row, 0)


def _na_attention(qkv, bias):
    nq = NA_RB * GRID_W
    nhp = NA_HEADS // 2
    return pl.pallas_call(
        _na_kernel,
        grid=(nhp, ROWS // NA_RB),
        in_specs=[
            pl.BlockSpec((nq, LANES), lambda hp, b: (b, hp)),
            pl.BlockSpec((SEQ, LANES), lambda hp, b: (0, nhp + hp)),
            pl.BlockSpec((SEQ, LANES), lambda hp, b: (0, 2 * nhp + hp)),
            pl.BlockSpec((CTX, LANES), lambda hp, b: (SEQ // CTX, nhp + hp)),
            pl.BlockSpec((CTX, LANES), lambda hp, b: (SEQ // CTX, 2 * nhp + hp)),
            pl.BlockSpec((2, NA_KH, GRID_W, NA_WIN), lambda hp, b: (hp, 0, 0, 0)),
        ],
        out_specs=pl.BlockSpec((nq, LANES), lambda hp, b: (b, hp)),
        out_shape=jax.ShapeDtypeStruct((SEQ, NA_W), BF16),
        compiler_params=_params(("arbitrary", "arbitrary")),
        name="na_attention",
    )(qkv, qkv, qkv, qkv, qkv, bias)


def _na_bias_rows(rpb):
    c = np.arange(GRID_W)[:, None]
    kc = np.arange(GRID_W)[None, :]
    ws = np.clip(c - NA_KW // 2, 0, GRID_W - NA_KW)
    col_ok = (kc >= ws) & (kc < ws + NA_KW)
    dc = np.clip(kc - c + NA_KW - 1, 0, 2 * NA_KW - 2)
    pick = (dc[None] == np.arange(2 * NA_KW - 1)[:, None, None]).astype(np.float32)
    full = jnp.einsum("hrd,dck->hrck", rpb.astype(F32), jnp.asarray(pick), precision=lax.Precision.HIGHEST)
    full = jnp.where(jnp.asarray(col_ok)[None, None], full, NEG_BIG)
    rows = [jnp.transpose(full[:, e:e + NA_KH], (0, 2, 1, 3)).reshape(NA_HEADS, GRID_W, NA_WIN)
            for e in range(NA_KH)]
    return jnp.stack(rows, axis=1)


def _lane_shift(x, d):
    return pltpu.roll(x, (LANES - d) % LANES, axis=1)


def _epilogue(h1, mod_ref, g2_ref, rw_ref, rb_ref, tril_ref, run_ref, xrow_ref, meta_ref, cnt_ref):
    xm2 = _rms_mod(h1, g2_ref[...], mod_ref[0, 3:4, :], mod_ref[0, 4:5, :])
    xrow_ref[:, 0:D] = xm2
    xh = xm2.astype(BF16)
    xl = (xm2 - xh.astype(F32)).astype(BF16)
    w = rw_ref[...]
    wh = w.astype(BF16)
    wl = (w - wh.astype(F32)).astype(BF16)
    logits = _dot(xh, wh) + (_dot(xl, wh) + _dot(xh, wl))
    scores = jax.nn.sigmoid(logits)
    biased = scores + rb_ref[...]
    lane = lax.broadcasted_iota(jnp.int32, (TM, LANES), 1)
    k = lane & (EPG - 1)
    valid = lane < N_EXPERTS
    offs = (-3, -2, -1, 1, 2, 3)
    in_grp = {d: (k + d >= 0) & (k + d < EPG) for d in offs}
    rank = jnp.zeros((TM, LANES), F32)
    for d in offs:
        v = _lane_shift(biased, d)
        beats = (v > biased) | (v == biased) if d < 0 else (v > biased)
        rank = rank + jnp.where(in_grp[d] & beats, 1.0, 0.0)
    top2 = rank < 2.0
    t = jnp.where(top2, biased, 0.0)
    gs = t
    for d in offs:
        gs = gs + jnp.where(in_grp[d], _lane_shift(t, d), 0.0)
    beaten = jnp.zeros((TM, LANES), jnp.bool_)
    for d in (-12, -8, -4, 4, 8, 12):
        v = _lane_shift(gs, d)
        ok = (lane + d >= 0) & (lane + d < N_EXPERTS)
        beats = (v > gs) | (v == gs) if d < 0 else (v > gs)
        beaten = beaten | (ok & beats)
    sel = valid & top2 & jnp.logical_not(beaten)
    wsel = jnp.where(sel, scores, 0.0)
    gate = wsel / jnp.sum(wsel, axis=-1, keepdims=True)
    self_f = jnp.where(sel, 1.0, 0.0)
    before = jnp.zeros((TM, LANES), F32)
    for d in (-3, -2, -1):
        before = before + jnp.where(in_grp[d], _lane_shift(self_f, d), 0.0)
    is_lo = sel & (before == 0.0)
    w_lo = jnp.sum(jnp.where(is_lo, gate, 0.0), axis=-1, keepdims=True)
    w_hi = jnp.sum(jnp.where(sel & jnp.logical_not(is_lo), gate, 0.0), axis=-1, keepdims=True)
    code = ((1 << k) + 8 * (lane >> 2)).astype(F32)
    bid = jnp.sum(jnp.where(sel, code, 0.0), axis=-1, keepdims=True)
    onehot = lane.astype(F32) == bid
    cum = _dot(tril_ref[...], jnp.where(onehot, 1.0, 0.0).astype(BF16))
    run = run_ref[0:1, :]
    rank_g = jnp.sum(jnp.where(onehot, cum + run, 0.0), axis=-1, keepdims=True)
    cnt = jnp.sum(jnp.where(onehot, 1.0, 0.0), axis=0, keepdims=True)
    new_run = jnp.broadcast_to(run + cnt, (8, LANES))
    run_ref[...] = new_run
    cnt_ref[...] = new_run
    meta = jnp.where(lane == 0, w_lo, jnp.where(lane == 1, w_hi, jnp.where(lane == 2, bid, jnp.where(lane == 3, rank_g, 0.0))))
    xrow_ref[:, D:XROW_W] = meta
    meta_ref[...] = meta


def _epilogue_specs(ntiles):
    in_specs = [
        _resident((1, D)),
        _resident((D, LANES)),
        _resident((1, LANES)),
        _resident((TM, TM)),
    ]
    out_specs = [
        pl.BlockSpec((TM, D), lambda i: (i, 0)),
        pl.BlockSpec((TM, XROW_W), lambda i: (i, 0)),
        pl.BlockSpec((TM, LANES), lambda i: (i, 0)),
        pl.BlockSpec((8, LANES), lambda i: (0, 0)),
    ]
    n = ntiles * TM
    out_shape = [
        jax.ShapeDtypeStruct((n, D), F32),
        jax.ShapeDtypeStruct((n, XROW_W), F32),
        jax.ShapeDtypeStruct((n, LANES), F32),
        jax.ShapeDtypeStruct((8, LANES), F32),
    ]
    return in_specs, out_specs, out_shape


def _outproj0_kernel(ao_ref, h_ref, mod_ref, w_ref, g2_ref, rw_ref, rb_ref, tril_ref,
                     h1_ref, xrow_ref, meta_ref, cnt_ref, run_ref):
    @pl.when(pl.program_id(0) == 0)
    def _():
        run_ref[...] = jnp.zeros_like(run_ref)

    y = _dot(ao_ref[...], w_ref[...])
    h1 = h_ref[...] + mod_ref[0, 2:3, :] * y
    h1_ref[...] = h1
    _epilogue(h1, mod_ref, g2_ref, rw_ref, rb_ref, tril_ref, run_ref, xrow_ref, meta_ref, cnt_ref)


def _outproj0(ao, h, modv, w_out, g2, rw, rb, tril):
    e_in, e_out, e_shape = _epilogue_specs(NT)
    return pl.pallas_call(
        _outproj0_kernel,
        grid=(NT,),
        in_specs=[
            pl.BlockSpec((TM, D), lambda i: (i, 0)),
            pl.BlockSpec((TM, D), lambda i: (i, 0)),
            _mod_spec(),
            _resident((D, D)),
        ] + e_in,
        out_specs=e_out,
        out_shape=e_shape,
        scratch_shapes=[pltpu.VMEM((8, LANES), F32)],
        compiler_params=_params(("arbitrary",)),
        name="outproj0",
    )(ao, h, modv, w_out, g2, rw, rb, tril)


def _bucket_tables():
    lo = np.zeros(N_BUCKET_IDS, np.int32)
    hi = np.zeros(N_BUCKET_IDS, np.int32)
    for g in range(N_EXPERTS // EPG):
        for a in range(EPG):
            for b in range(a + 1, EPG):
                i = 16 * g + (1 << a) + (1 << b)
                lo[i] = EPG * g + a
                hi[i] = EPG * g + b
    return lo, hi


def _slot_kernel(meta_ref, starts_ref, o_ref):
    lane = lax.broadcasted_iota(jnp.int32, (TM, LANES), 1).astype(F32)
    first = jnp.sum(jnp.where(lane == meta_ref[:, 2:3], starts_ref[...], 0.0), axis=-1, keepdims=True)
    o_ref[...] = jnp.broadcast_to((first + meta_ref[:, 3:4]).astype(jnp.int32), (TM, LANES))


def _token_slots(meta, first_row):
    n = meta.shape[0]
    out = pl.pallas_call(
        _slot_kernel,
        grid=(n // TM,),
        in_specs=[pl.BlockSpec((TM, LANES), lambda i: (i, 0)), pl.BlockSpec((1, LANES), lambda i: (0, 0))],
        out_specs=pl.BlockSpec((TM, LANES), lambda i: (i, 0)),
        out_shape=jax.ShapeDtypeStruct((n, LANES), jnp.int32),
        compiler_params=_params(("arbitrary",)),
        name="token_slots",
    )(meta, first_row)
    return out[:, 0]


def _route_plan(meta, cnt, n_tiles, layer):
    counts = cnt[0, :N_BUCKET_IDS].astype(jnp.int32)
    nt_b = (counts + TMM - 1) // TMM
    ends = jnp.cumsum(nt_b)
    starts = ends - nt_b
    first_row = jnp.pad((starts * TMM).astype(F32), (0, LANES - N_BUCKET_IDS)).reshape(1, LANES)
    pos = _token_slots(meta, first_row)
    n_used = ends[-1]
    tile = jnp.minimum(jnp.arange(n_tiles, dtype=jnp.int32), n_used - 1)
    tile_b = jnp.sum((ends[None, :] <= tile[:, None]).astype(jnp.int32), axis=1)
    lo, hi = _bucket_tables()
    e_lo = jnp.asarray(lo)[tile_b] + layer * N_EXPERTS
    e_hi = jnp.asarray(hi)[tile_b] + layer * N_EXPERTS
    return pos.astype(jnp.int32), e_lo.astype(jnp.int32), e_hi.astype(jnp.int32), n_used.reshape(1).astype(jnp.int32)


def _dispatch_kernel(pos_ref, x_ref, xs_in_ref, xs_ref, sem):
    del xs_in_ref
    base = pl.program_id(0) * TM

    def issue(r, carry):
        p = pos_ref[base + r]
        pltpu.make_async_copy(x_ref.at[pl.ds(r, 1), :], xs_ref.at[pl.ds(p, 1), :], sem).start()
        return carry

    lax.fori_loop(0, TM, issue, 0)
    pltpu.make_async_copy(x_ref, xs_ref.at[pl.ds(0, TM), :], sem).wait()


def _dispatch(pos, xrow, n_tiles):
    ntok = xrow.shape[0]
    xs0 = jnp.zeros((n_tiles * TMM, XROW_W), F32)
    return pl.pallas_call(
        _dispatch_kernel,
        grid_spec=pltpu.PrefetchScalarGridSpec(
            num_scalar_prefetch=1,
            grid=(ntok // TM,),
            in_specs=[
                pl.BlockSpec((TM, XROW_W), lambda i, pos: (i, 0)),
                pl.BlockSpec(memory_space=pl.ANY),
            ],
            out_specs=pl.BlockSpec(memory_space=pl.ANY),
            scratch_shapes=[pltpu.SemaphoreType.DMA(())],
        ),
        out_shape=jax.ShapeDtypeStruct((n_tiles * TMM, XROW_W), F32),
        input_output_aliases={2: 0},
        compiler_params=_params(("arbitrary",)),
        name="moe_dispatch",
    )(pos, xrow, xs0)


def _moe_kernel(elo_ref, ehi_ref, nu_ref, xs_ref, w1a, w3a, w2a, w1b, w3b, w2b, ys_ref):
    j = pl.program_id(0)

    @pl.when(j < nu_ref[0])
    def _():
        x = xs_ref[:, 0:D].astype(BF16)
        g_lo = xs_ref[:, D:D + 1]
        g_hi = xs_ref[:, D + 1:D + 2]

        def expert(w1, w3, w2):
            a = _dot(x, w1[0])
            hid = (a * jax.nn.sigmoid(a)) * _dot(x, w3[0])
            return _dot(hid.astype(BF16), w2[0])

        ys_ref[...] = g_lo * expert(w1a, w3a, w2a) + g_hi * expert(w1b, w3b, w2b)

    @pl.when(j >= nu_ref[0])
    def _():
        ys_ref[...] = jnp.zeros_like(ys_ref)


def _moe(e_lo, e_hi, n_used, xs, w1, w3, w2, n_tiles):
    def wspec(shape, which):
        if which == 0:
            return pl.BlockSpec(shape, lambda j, lo, hi, nu: (lo[j], 0, 0))
        return pl.BlockSpec(shape, lambda j, lo, hi, nu: (hi[j], 0, 0))

    s13 = (1, D, D_EXPERT)
    s2 = (1, D_EXPERT, D)
    return pl.pallas_call(
        _moe_kernel,
        grid_spec=pltpu.PrefetchScalarGridSpec(
            num_scalar_prefetch=3,
            grid=(n_tiles,),
            in_specs=[
                pl.BlockSpec((TMM, XROW_W), lambda j, lo, hi, nu: (j, 0)),
                wspec(s13, 0), wspec(s13, 0), wspec(s2, 0),
                wspec(s13, 1), wspec(s13, 1), wspec(s2, 1),
            ],
            out_specs=pl.BlockSpec((TMM, D), lambda j, lo, hi, nu: (j, 0)),
        ),
        out_shape=jax.ShapeDtypeStruct((n_tiles * TMM, D), F32),
        compiler_params=_params(("arbitrary",)),
        name="moe_experts",
    )(e_lo, e_hi, n_used, xs, w1, w3, w2, w1, w3, w2)


def _gather_issue(pos_ref, ys_hbm, buf, sem, tile, slot):
    base = tile * TM

    def issue(r, carry):
        p = pos_ref[base + r]
        pltpu.make_async_copy(ys_hbm.at[pl.ds(p, 1), :], buf.at[slot, pl.ds(r, 1), :], sem.at[slot]).start()
        return carry

    lax.fori_loop(0, TM, issue, 0)


def _gathered_rows(pos_ref, ys_hbm, buf, sem):
    i = pl.program_id(0)
    slot = i % 2

    @pl.when(i == 0)
    def _():
        _gather_issue(pos_ref, ys_hbm, buf, sem, 0, 0)

    @pl.when(i + 1 < pl.num_programs(0))
    def _():
        _gather_issue(pos_ref, ys_hbm, buf, sem, i + 1, 1 - slot)

    pltpu.make_async_copy(ys_hbm.at[pl.ds(0, TM), :], buf.at[slot], sem.at[slot]).wait()
    return buf[slot]


_GATHER_SCRATCH = [pltpu.VMEM((2, TM, D), F32), pltpu.SemaphoreType.DMA((2,))]


def _rope_store(x, tabs, out_ref):
    cr, sr, cc, sc = tabs
    for ch in range(RET_QK // LANES):
        xc = x[:, ch * LANES:(ch + 1) * LANES]
        xr = pltpu.roll(xc, LANES // 2, axis=1)
        y = xc * cr + xr * sr if ch % 2 == 0 else xc * cc + xr * sc
        out_ref[:, ch * LANES:(ch + 1) * LANES] = y.astype(BF16)


def _inproj1_kernel(pos_ref, h_ref, ys_hbm, mod0_ref, mod1_ref, g_ref, w_ref, cr_ref, sr_ref, cc_ref, sc_ref,
                    h2_ref, q_ref, k_ref, v_ref, gg_ref, buf, sem):
    f = _gathered_rows(pos_ref, ys_hbm, buf, sem)
    h2 = h_ref[...] + mod0_ref[0, 5:6, :] * f
    h2_ref[...] = h2
    xm = _rms_mod(h2, g_ref[...], mod1_ref[0, 0:1, :], mod1_ref[0, 1:2, :]).astype(BF16)
    tabs = (cr_ref[...], sr_ref[...], cc_ref[...], sc_ref[...])
    _rope_store(_dot(xm, w_ref[:, 0:RET_QK]), tabs, q_ref)
    _rope_store(_dot(xm, w_ref[:, RET_QK:2 * RET_QK]) * (RET_DK ** -0.5), tabs, k_ref)
    v_ref[...] = _dot(xm, w_ref[:, 2 * RET_QK:2 * RET_QK + RET_V]).astype(BF16)
    gg_ref[...] = _dot(xm, w_ref[:, 2 * RET_QK + RET_V:]).astype(BF16)


def _inproj1(pos, h1, ys, modv0, modv1, g, w_in, tabs):
    tok = lambda w: pl.BlockSpec((TM, w), lambda i, pos: (i, 0))
    res = lambda shape: pl.BlockSpec(shape, lambda i, pos: (0,) * len(shape), pipeline_mode=pl.Buffered(1))
    mod = pl.BlockSpec((1, 8, D), lambda i, pos: (i // NT_LAT, 0, 0))
    return pl.pallas_call(
        _inproj1_kernel,
        grid_spec=pltpu.PrefetchScalarGridSpec(
            num_scalar_prefetch=1,
            grid=(NT,),
            in_specs=[tok(D), pl.BlockSpec(memory_space=pl.ANY), mod, mod, res((1, D)), res((D, 2 * RET_QK + 2 * RET_V)),
                      tok(LANES), tok(LANES), tok(LANES), tok(LANES)],
            out_specs=[tok(D), tok(RET_QK), tok(RET_QK), tok(RET_V), tok(RET_V)],
            scratch_shapes=_GATHER_SCRATCH,
        ),
        out_shape=[
            jax.ShapeDtypeStruct((T, D), F32),
            jax.ShapeDtypeStruct((T, RET_QK), BF16),
            jax.ShapeDtypeStruct((T, RET_QK), BF16),
            jax.ShapeDtypeStruct((T, RET_V), BF16),
            jax.ShapeDtypeStruct((T, RET_V), BF16),
        ],
        compiler_params=_params(("arbitrary",)),
        name="inproj1",
    )(pos, h1, ys, modv0, modv1, g, w_in, *tabs)


def _rope_tables():
    n = LANES // 2
    inv_freq = ROPE_BASE ** (-jnp.arange(n, dtype=F32) / n)
    t = jnp.arange(SEQ)

    def tab(p):
        ang = p.astype(F32)[:, None] * inv_freq[None, :]
        cos, sin = jnp.cos(ang), jnp.sin(ang)
        c = jnp.concatenate([cos, cos], axis=1)
        s = jnp.concatenate([-sin, sin], axis=1)
        c = jnp.concatenate([c, jnp.ones((CTX, LANES), F32)], axis=0)
        s = jnp.concatenate([s, jnp.zeros((CTX, LANES), F32)], axis=0)
        return c, s

    cr, sr = tab(t // GRID_W)
    cc, sc = tab(t % GRID_W)
    return cr, sr, cc, sc


RET_NCHUNK = T // RET_CHUNK
RET_CTX_CHUNKS = CTX // RET_CHUNK
RET_LAT_CHUNKS = SEQ // RET_CHUNK
RET_NHD = 2 * RET_HEADS


def _ret_kernel(dec_ref, qf_ref, kf_ref, vf_ref, qb_ref, kb_ref, vb_ref, of_ref, ob_ref,
                state, dm, qd, kd, cd):
    C = RET_CHUNK

    @pl.when(pl.program_id(0) == 0)
    def _():
        state[...] = jnp.zeros_like(state)
        n = lax.broadcasted_iota(jnp.int32, (C, C), 0).astype(F32)
        m = lax.broadcasted_iota(jnp.int32, (C, C), 1).astype(F32)
        for d in range(2):
            for h in range(RET_HEADS):
                idx = d * RET_HEADS + h
                lg = jnp.log1p(-jnp.exp(jnp.full((C, C), dec_ref[d, h], F32)))
                diff = n - m if d == 0 else m - n
                low = diff >= 0.0
                dm[idx] = jnp.where(low, jnp.exp(jnp.where(low, diff, 0.0) * lg), 0.0)
                qpow = n + 1.0 if d == 0 else C - n
                kpow = (C - 1.0) - n if d == 0 else n
                qcol = jnp.exp(qpow * lg)
                kcol = jnp.exp(kpow * lg)
                qd[idx] = jnp.concatenate([qcol] * (RET_DV // C), axis=1)
                kd[idx] = jnp.concatenate([kcol] * (RET_DK // C), axis=1)
                cd[idx] = jnp.exp(C * lg[0:8, :])

    for d, (q_ref, k_ref, v_ref, o_ref) in enumerate(((qf_ref, kf_ref, vf_ref, of_ref),
                                                       (qb_ref, kb_ref, vb_ref, ob_ref))):
        for h in range(RET_HEADS):
            idx = d * RET_HEADS + h
            q = q_ref[:, h * RET_DK:(h + 1) * RET_DK]
            k = k_ref[:, h * RET_DK:(h + 1) * RET_DK]
            v = v_ref[:, h * RET_DV:(h + 1) * RET_DV]
            st = state[idx]
            s = _dot_nt(q, k) * dm[idx]
            o = _dot(s.astype(BF16), v) + qd[idx] * _dot(q, st.astype(BF16))
            o_ref[:, h * RET_DV:(h + 1) * RET_DV] = o.astype(BF16)
            kdk = (k.astype(F32) * kd[idx]).astype(BF16)
            cdv = jnp.concatenate([cd[idx, 0:1, :]] * (RET_DV // C), axis=1)
            state[idx] = st * cdv + _dot_tn(kdk, v)


def _ret_scan(decay, q, k, v):
    def fwd(s):
        return jnp.where(s < RET_CTX_CHUNKS, RET_LAT_CHUNKS + s, s - RET_CTX_CHUNKS)

    def bwd(s):
        return RET_NCHUNK - 1 - s

    def spec(w, f):
        return pl.BlockSpec((RET_CHUNK, w), lambda s: (f(s), 0))

    return pl.pallas_call(
        _ret_kernel,
        grid=(RET_NCHUNK,),
        in_specs=[
            pl.BlockSpec(memory_space=pltpu.SMEM),
            spec(RET_QK, fwd), spec(RET_QK, fwd), spec(RET_V, fwd),
            spec(RET_QK, bwd), spec(RET_QK, bwd), spec(RET_V, bwd),
        ],
        out_specs=[spec(RET_V, fwd), spec(RET_V, bwd)],
        out_shape=[jax.ShapeDtypeStruct((T, RET_V), BF16), jax.ShapeDtypeStruct((T, RET_V), BF16)],
        scratch_shapes=[
            pltpu.VMEM((RET_NHD, RET_DK, RET_DV), F32),
            pltpu.VMEM((RET_NHD, RET_CHUNK, RET_CHUNK), F32),
            pltpu.VMEM((RET_NHD, RET_CHUNK, RET_DV), F32),
            pltpu.VMEM((RET_NHD, RET_CHUNK, RET_DK), F32),
            pltpu.VMEM((RET_NHD, 8, RET_CHUNK), F32),
        ],
        compiler_params=_params(("arbitrary",)),
        name="retention_scan",
    )(decay, q, k, v, q, k, v)


def _outproj1_kernel(of_ref, ob_ref, gg_ref, h_ref, mod_ref, w_ref, g2_ref, rw_ref, rb_ref, tril_ref,
                     h1_ref, xrow_ref, meta_ref, cnt_ref, run_ref):
    @pl.when(pl.program_id(0) == 0)
    def _():
        run_ref[...] = jnp.zeros_like(run_ref)

    y = jnp.zeros((TM, D), F32)
    for hh in range(RET_HEADS):
        cols = slice(hh * RET_DV, (hh + 1) * RET_DV)
        o = of_ref[:, cols].astype(F32) + ob_ref[:, cols].astype(F32)
        on = o * lax.rsqrt(jnp.mean(o * o, axis=-1, keepdims=True) + RMS_EPS)
        g = gg_ref[:, cols].astype(F32)
        z = (g * jax.nn.sigmoid(g)) * on
        y = y + _dot(z.astype(BF16), w_ref[cols, :])
    h1 = h_ref[...] + mod_ref[0, 2:3, :] * y
    h1_ref[...] = h1
    _epilogue(h1, mod_ref, g2_ref, rw_ref, rb_ref, tril_ref, run_ref, xrow_ref, meta_ref, cnt_ref)


def _outproj1(o_f, o_b, gg, h, modv, w_out, g2, rw, rb, tril):
    e_in, e_out, e_shape = _epilogue_specs(NT_LAT)
    tok = lambda w: pl.BlockSpec((TM, w), lambda i: (i, 0))
    return pl.pallas_call(
        _outproj1_kernel,
        grid=(NT_LAT,),
        in_specs=[tok(RET_V), tok(RET_V), tok(RET_V), tok(D), _mod_spec(), _resident((RET_V, D))] + e_in,
        out_specs=e_out,
        out_shape=e_shape,
        scratch_shapes=[pltpu.VMEM((8, LANES), F32)],
        compiler_params=_params(("arbitrary",)),
        name="outproj1",
    )(o_f, o_b, gg, h, modv, w_out, g2, rw, rb, tril)


def _final_kernel(pos_ref, h_ref, ys_hbm, mod_ref, g_ref, o_ref, buf, sem):
    f = _gathered_rows(pos_ref, ys_hbm, buf, sem)
    h = h_ref[...] + mod_ref[0, 5:6, :] * f
    o_ref[...] = h * lax.rsqrt(jnp.mean(h * h, axis=-1, keepdims=True) + RMS_EPS) * g_ref[...]


def _final(pos, h, ys, modv, g):
    tok = pl.BlockSpec((TM, D), lambda i, pos: (i, 0))
    return pl.pallas_call(
        _final_kernel,
        grid_spec=pltpu.PrefetchScalarGridSpec(
            num_scalar_prefetch=1,
            grid=(NT_LAT,),
            in_specs=[tok, pl.BlockSpec(memory_space=pl.ANY),
                      pl.BlockSpec((1, 8, D), lambda i, pos: (0, 0, 0)),
                      pl.BlockSpec((1, D), lambda i, pos: (0, 0))],
            out_specs=tok,
            scratch_shapes=_GATHER_SCRATCH,
        ),
        out_shape=jax.ShapeDtypeStruct((SEQ, D), F32),
        compiler_params=_params(("arbitrary",)),
        name="final_norm",
    )(pos, h, ys, modv, g)


def _dft_constants():
    c = np.arange(FNET_GD)
    ang = 2.0 * np.pi * np.outer(c, c) / FNET_GD
    d64 = np.zeros((FNET_W, 2 * FNET_W))
    for g in range(FNET_W // FNET_GD):
        sl = slice(g * FNET_GD, (g + 1) * FNET_GD)
        d64[sl, sl] = np.cos(ang)
        d64[sl, FNET_W + g * FNET_GD:FNET_W + (g + 1) * FNET_GD] = -np.sin(ang)
    k2 = np.arange(FFT_N2)[:, None]
    n2 = np.arange(FFT_N2)[None, :]
    ga = np.zeros((FFT_N1, 2 * FFT_N2, 2 * FFT_N2))
    for n1 in range(FFT_N1):
        th = 2.0 * np.pi * ((k2 * (n1 + FFT_N1 * n2)) % SEQ) / SEQ
        cs, sn = np.cos(th), np.sin(th)
        ga[n1] = np.block([[cs, sn], [-sn, cs]])
    k1 = np.arange(FFT_N1)
    ph = 2.0 * np.pi * np.outer(k1, k1) / FFT_N1
    csb = np.concatenate([np.cos(ph), np.sin(ph)], axis=1)
    p = np.arange(CTX)
    pc = 2.0 * np.pi * (np.outer(p, p) % CTX) / CTX
    csc = np.concatenate([np.cos(pc), np.sin(pc)], axis=1)
    as_bf16 = lambda a: jnp.asarray(a, dtype=F32).astype(BF16)
    return as_bf16(d64), as_bf16(ga), as_bf16(csb), as_bf16(csc)


def _mod_table(mod_l):
    t = mod_l.reshape(2, 6, D)
    return jnp.concatenate([t, jnp.zeros((2, 2, D), F32)], axis=1)


def kernel(x, c, ctx, c_ctx, ada_w, ada_b, norm_g, final_norm_g, mixab_w_in, mixab_w_out, na_rpb,
           ret_w_in, ret_w_out, ret_decay, router_w, router_b, moe_w1, moe_w3, moe_w2):
    d64, ga, csb, csc = _dft_constants()
    tril = jnp.asarray(np.tril(np.ones((TM, TM)), -1), dtype=BF16)
    rw = jnp.pad(router_w, ((0, 0), (0, LANES - N_EXPERTS)))
    rb = jnp.pad(router_b, (0, LANES - N_EXPERTS)).reshape(1, LANES)
    w1 = moe_w1.reshape(DEPTH * N_EXPERTS, D, D_EXPERT).astype(BF16)
    w3 = moe_w3.reshape(DEPTH * N_EXPERTS, D, D_EXPERT).astype(BF16)
    w2 = moe_w2.reshape(DEPTH * N_EXPERTS, D_EXPERT, D).astype(BF16)

    mod = _mod_vectors(c, c_ctx, ada_w, ada_b)
    modv0, modv1 = _mod_table(mod[0]), _mod_table(mod[1])
    h0 = jnp.concatenate([x[0], ctx[0]], axis=0)

    qkv, br, bi = _inproj0(h0, modv0, norm_g[0, 0].reshape(1, D), mixab_w_in[0].astype(BF16), d64)
    a_lat = _fft_latent(br, bi, ga, csb)
    ao_ctx = _ctx_mixer(qkv, br, bi, csc)
    o_lat = _na_attention(qkv, _na_bias_rows(na_rpb[0]))
    ao = jnp.concatenate([jnp.concatenate([a_lat.astype(BF16), o_lat], axis=1), ao_ctx], axis=0)
    h1, xrow, meta, cnt = _outproj0(ao, h0, modv0, mixab_w_out[0].astype(BF16), norm_g[0, 1].reshape(1, D),
                                    rw, rb, tril)
    n_tiles0 = NT + N_REAL_BUCKETS
    pos0, e_lo, e_hi, n_used = _route_plan(meta, cnt, n_tiles0, 0)
    xs = _dispatch(pos0, xrow, n_tiles0)
    ys0 = _moe(e_lo, e_hi, n_used, xs, w1, w3, w2, n_tiles0)

    h2, q, k, v, gg = _inproj1(pos0, h1, ys0, modv0, modv1, norm_g[1, 0].reshape(1, D),
                               ret_w_in[0].astype(BF16), _rope_tables())
    o_f, o_b = _ret_scan(ret_decay[0].astype(F32), q, k, v)
    h3, xrow1, meta1, cnt1 = _outproj1(o_f, o_b, gg, h2, modv1, ret_w_out[0].astype(BF16),
                                       norm_g[1, 1].reshape(1, D), rw, rb, tril)
    n_tiles1 = NT_LAT + N_REAL_BUCKETS
    pos1, e_lo1, e_hi1, n_used1 = _route_plan(meta1, cnt1, n_tiles1, 1)
    xs1 = _dispatch(pos1, xrow1, n_tiles1)
    ys1 = _moe(e_lo1, e_hi1, n_used1, xs1, w1, w3, w2, n_tiles1)
    out = _final(pos1, h3, ys1, modv1, final_norm_g.reshape(1, D))
    return out[None]
```

```python
import functools
import math

import numpy as np
import jax
import jax.numpy as jnp
from jax import lax
from jax.experimental import pallas as pl
from jax.experimental.pallas import tpu as pltpu

F32 = jnp.float32
BF16 = jnp.bfloat16

D = 1024
SEQ = 16384
CTX = 256
T = SEQ + CTX
DEPTH = 2
GRID_W = 64
ROWS = SEQ // GRID_W
RMS_EPS = 1e-6
ROPE_BASE = 10000.0

FNET_W = 256
FNET_GD = 64
NA_HEADS = 12
NA_HD = 64
NA_W = NA_HEADS * NA_HD
NA_KH = 8
NA_KW = 16
AB_IN = FNET_W + 3 * NA_W
QKV_W = 3 * NA_W

RET_HEADS = 4
RET_DK = 256
RET_DV = 512
RET_QK = RET_HEADS * RET_DK
RET_V = RET_HEADS * RET_DV
RET_CHUNK = 128

N_EXPERTS = 16
EPG = 4
D_EXPERT = 512

LANES = 128
TM = 256
NT = T // TM
NT_LAT = SEQ // TM
TMM = 256
N_BUCKET_IDS = 64
N_REAL_BUCKETS = 24
XROW_W = D + LANES
FFT_N1 = 128
FFT_N2 = 128
NEG_BIG = -1e30

VMEM_LIMIT = 56 * 1024 * 1024


def _params(sem, vmem=VMEM_LIMIT):
    return pltpu.CompilerParams(dimension_semantics=sem, vmem_limit_bytes=vmem)


def _dot(a, b):
    return jnp.dot(a, b, preferred_element_type=F32)


def _dot_nt(a, b):
    return lax.dot_general(a, b, (((1,), (1,)), ((), ())), preferred_element_type=F32)


def _dot_tn(a, b):
    return lax.dot_general(a, b, (((0,), (0,)), ((), ())), preferred_element_type=F32)


def _rms_mod(x, g, sh, sc):
    ms = jnp.mean(x * x, axis=-1, keepdims=True)
    y = x * lax.rsqrt(ms + RMS_EPS) * g
    return y * (1.0 + sc) + sh


def _resident(shape):
    nd = len(shape)
    return pl.BlockSpec(shape, lambda *_: (0,) * nd, pipeline_mode=pl.Buffered(1))


MOD_TN = 768


def _mod_kernel(cs_ref, w_ref, b_ref, o_ref):
    cs = cs_ref[...]
    s = cs * jax.nn.sigmoid(cs)
    w = w_ref[0]
    r0 = jnp.sum(s[:, 0:1] * w, axis=0, keepdims=True)
    r1 = jnp.sum(s[:, 1:2] * w, axis=0, keepdims=True)
    o_ref[0] = jnp.concatenate([r0, r1], axis=0) + b_ref[0]


def _mod_vectors(c, c_ctx, ada_w, ada_b):
    cs = jnp.stack([c[0], c_ctx], axis=1)
    n = 6 * D
    return pl.pallas_call(
        _mod_kernel,
        grid=(DEPTH, n // MOD_TN),
        in_specs=[
            pl.BlockSpec((D, 2), lambda l, j: (0, 0)),
            pl.BlockSpec((1, D, MOD_TN), lambda l, j: (l, 0, j)),
            pl.BlockSpec((1, 1, MOD_TN), lambda l, j: (l, 0, j)),
        ],
        out_specs=pl.BlockSpec((1, 2, MOD_TN), lambda l, j: (l, 0, j)),
        out_shape=jax.ShapeDtypeStruct((DEPTH, 2, n), F32),
        compiler_params=_params(("arbitrary", "arbitrary")),
        name="mod_vectors",
    )(cs, ada_w, ada_b.reshape(DEPTH, 1, n))


def _mod_spec():
    return pl.BlockSpec((1, 8, D), lambda i, *_: (i // NT_LAT, 0, 0))


def _inproj0_kernel(h_ref, mod_ref, g_ref, w_ref, d64_ref, qkv_ref, br_ref, bi_ref):
    xm = _rms_mod(h_ref[...], g_ref[...], mod_ref[0, 0:1, :], mod_ref[0, 1:2, :]).astype(BF16)
    p = _dot(xm, w_ref[...])
    qkv_ref[...] = p[:, FNET_W:].astype(BF16)
    b = _dot(p[:, :FNET_W].astype(BF16), d64_ref[...])
    br_ref[...] = b[:, :FNET_W].astype(BF16)
    bi_ref[...] = b[:, FNET_W:].astype(BF16)


def _inproj0(h, modv, g, w_in, d64):
    return pl.pallas_call(
        _inproj0_kernel,
        grid=(NT,),
        in_specs=[
            pl.BlockSpec((TM, D), lambda i: (i, 0)),
            _mod_spec(),
            _resident((1, D)),
            _resident((D, AB_IN)),
            _resident((FNET_W, 2 * FNET_W)),
        ],
        out_specs=[
            pl.BlockSpec((TM, QKV_W), lambda i: (i, 0)),
            pl.BlockSpec((TM, FNET_W), lambda i: (i, 0)),
            pl.BlockSpec((TM, FNET_W), lambda i: (i, 0)),
        ],
        out_shape=[
            jax.ShapeDtypeStruct((T, QKV_W), BF16),
            jax.ShapeDtypeStruct((T, FNET_W), BF16),
            jax.ShapeDtypeStruct((T, FNET_W), BF16),
        ],
        compiler_params=_params(("arbitrary",)),
        name="inproj0",
    )(h, modv, g, w_in, d64)


FFT_A_NB = 8
FFT_B_TN = 4096


def _fft_a_kernel(br_ref, bi_ref, ga_ref, v_ref):
    for u in range(FFT_A_NB):
        cols = slice(u * FNET_W, (u + 1) * FNET_W)
        z = _dot(ga_ref[u, :, 0:FFT_N2], br_ref[:, cols]) + _dot(ga_ref[u, :, FFT_N2:], bi_ref[:, cols])
        v_ref[0, u] = z[:FFT_N2].astype(BF16)
        v_ref[1, u] = z[FFT_N2:].astype(BF16)


def _fft_b_kernel(cs_ref, v_ref, o_ref):
    o_ref[...] = _dot(cs_ref[...], v_ref[...]) * (1.0 / math.sqrt(SEQ * FNET_GD))


def _fft_latent(br, bi, ga, csb):
    width = FFT_N1 * FNET_W
    br2 = br.reshape(T // FFT_N1, width)
    bi2 = bi.reshape(T // FFT_N1, width)
    v = pl.pallas_call(
        _fft_a_kernel,
        grid=(FFT_N1 // FFT_A_NB,),
        in_specs=[
            pl.BlockSpec((FFT_N2, FFT_A_NB * FNET_W), lambda j: (0, j)),
            pl.BlockSpec((FFT_N2, FFT_A_NB * FNET_W), lambda j: (0, j)),
            pl.BlockSpec((FFT_A_NB, 2 * FFT_N2, 2 * FFT_N2), lambda j: (j, 0, 0)),
        ],
        out_specs=pl.BlockSpec((2, FFT_A_NB, FFT_N2, FNET_W), lambda j: (0, j, 0, 0)),
        out_shape=jax.ShapeDtypeStruct((2, FFT_N1, FFT_N2, FNET_W), BF16),
        compiler_params=_params(("arbitrary",)),
        name="fft_stage_a",
    )(br2, bi2, ga)
    v2 = v.reshape(2 * FFT_N1, FFT_N2 * FNET_W)
    x = pl.pallas_call(
        _fft_b_kernel,
        grid=(FFT_N2 * FNET_W // FFT_B_TN,),
        in_specs=[
            _resident((FFT_N1, 2 * FFT_N1)),
            pl.BlockSpec((2 * FFT_N1, FFT_B_TN), lambda j: (0, j)),
        ],
        out_specs=pl.BlockSpec((FFT_N1, FFT_B_TN), lambda j: (0, j)),
        out_shape=jax.ShapeDtypeStruct((FFT_N1, FFT_N2 * FNET_W), F32),
        compiler_params=_params(("arbitrary",)),
        name="fft_stage_b",
    )(csb, v2)
    return x.reshape(SEQ, FNET_W)


def _head_pair_masks():
    lane = lax.broadcasted_iota(jnp.int32, (1, LANES), 1)
    return lane < NA_HD


def _ctx_kernel(qkv_ref, br_ref, bi_ref, cs_ref, o_ref):
    a = _dot(cs_ref[:, 0:CTX], br_ref[...]) + _dot(cs_ref[:, CTX:], bi_ref[...])
    o_ref[:, 0:FNET_W] = (a * (1.0 / math.sqrt(CTX * FNET_GD))).astype(BF16)
    m0 = _head_pair_masks()
    for hp in range(NA_HEADS // 2):
        q = qkv_ref[:, hp * LANES:(hp + 1) * LANES]
        k = qkv_ref[:, NA_W + hp * LANES:NA_W + (hp + 1) * LANES]
        v = qkv_ref[:, 2 * NA_W + hp * LANES:2 * NA_W + (hp + 1) * LANES]
        outs = []
        for a_ in range(2):
            qa = jnp.where(m0 if a_ == 0 else jnp.logical_not(m0), q, jnp.zeros_like(q))
            s = _dot_nt(qa, k) * (NA_HD ** -0.5)
            p = jnp.exp(s - jnp.max(s, axis=-1, keepdims=True))
            l = jnp.sum(p, axis=-1, keepdims=True)
            outs.append(_dot(p.astype(BF16), v) / l)
        o_ref[:, FNET_W + hp * LANES:FNET_W + (hp + 1) * LANES] = jnp.where(m0, outs[0], outs[1]).astype(BF16)


def _ctx_mixer(qkv, br, bi, csc):
    return pl.pallas_call(
        _ctx_kernel,
        grid=(1,),
        in_specs=[
            pl.BlockSpec((CTX, QKV_W), lambda i: (SEQ // CTX, 0)),
            pl.BlockSpec((CTX, FNET_W), lambda i: (SEQ // CTX, 0)),
            pl.BlockSpec((CTX, FNET_W), lambda i: (SEQ // CTX, 0)),
            pl.BlockSpec((CTX, 2 * CTX), lambda i: (0, 0)),
        ],
        out_specs=pl.BlockSpec((CTX, D), lambda i: (0, 0)),
        out_shape=jax.ShapeDtypeStruct((CTX, D), BF16),
        compiler_params=_params(("arbitrary",)),
        name="ctx_mixer",
    )(qkv, br, bi, csc)


NA_RB = 8
NA_WIN = NA_KH * GRID_W


NA_NK = NA_WIN + CTX
NA_SUB = 16


def _na_kernel(q_ref, k_ref, v_ref, kc_ref, vc_ref, bias_ref, o_ref, s_scr, p_scr, l_scr):
    b = pl.program_id(1)
    m0 = _head_pair_masks()
    nm0 = jnp.logical_not(m0)
    scale = NA_HD ** -0.5

    def window(i):
        r = b * NA_RB + i
        rs = jnp.clip(r - NA_KH // 2, 0, ROWS - NA_KH)
        return pl.multiple_of(rs * GRID_W, GRID_W), rs - r + (NA_KH - 1)

    def scores(i):
        start, _ = window(i)
        q = q_ref[i * GRID_W:(i + 1) * GRID_W, :] * scale
        qs = jnp.concatenate([jnp.where(m0, q, jnp.zeros_like(q)), jnp.where(nm0, q, jnp.zeros_like(q))], axis=0)
        s_scr[i, :, 0:NA_WIN] = _dot_nt(qs, k_ref[pl.ds(start, NA_WIN), :])
        s_scr[i, :, NA_WIN:NA_NK] = _dot_nt(qs, kc_ref[...])

    def softmax(i):
        _, e = window(i)
        for g in range(2 * GRID_W // NA_SUB):
            rows = slice(g * NA_SUB, (g + 1) * NA_SUB)
            a, c0 = divmod(g * NA_SUB, GRID_W)
            s1 = s_scr[i, rows, 0:NA_WIN] + bias_ref[a, e, c0:c0 + NA_SUB, :]
            s2 = s_scr[i, rows, NA_WIN:NA_NK]
            mx = jnp.maximum(jnp.max(s1, axis=-1, keepdims=True), jnp.max(s2, axis=-1, keepdims=True))
            p1 = jnp.exp(s1 - mx)
            p2 = jnp.exp(s2 - mx)
            l = jnp.sum(p1, axis=-1, keepdims=True) + jnp.sum(p2, axis=-1, keepdims=True)
            p_scr[i, rows, 0:NA_WIN] = p1.astype(BF16)
            p_scr[i, rows, NA_WIN:NA_NK] = p2.astype(BF16)
            l_scr[i, rows, :] = jnp.broadcast_to(l, (NA_SUB, LANES))

    def values(i):
        start, _ = window(i)
        o = _dot(p_scr[i, :, 0:NA_WIN], v_ref[pl.ds(start, NA_WIN), :]) + _dot(p_scr[i, :, NA_WIN:NA_NK], vc_ref[...])
        o = o / l_scr[i]
        o_ref[i * GRID_W:(i + 1) * GRID_W, :] = jnp.where(m0, o[0:GRID_W], o[GRID_W:]).astype(BF16)

    scores(0)
    for i in range(NA_RB):
        if i + 1 < NA_RB:
            scores(i + 1)
        softmax(i)
        values(i)


def _na_attention(qkv, bias):
    nq = NA_RB * GRID_W
    nhp = NA_HEADS // 2
    return pl.pallas_call(
        _na_kernel,
        grid=(nhp, ROWS // NA_RB),
        in_specs=[
            pl.BlockSpec((nq, LANES), lambda hp, b: (b, hp)),
            pl.BlockSpec((SEQ, LANES), lambda hp, b: (0, nhp + hp)),
            pl.BlockSpec((SEQ, LANES), lambda hp, b: (0, 2 * nhp + hp)),
            pl.BlockSpec((CTX, LANES), lambda hp, b: (SEQ // CTX, nhp + hp)),
            pl.BlockSpec((CTX, LANES), lambda hp, b: (SEQ // CTX, 2 * nhp + hp)),
            pl.BlockSpec((2, NA_KH, GRID_W, NA_WIN), lambda hp, b: (hp, 0, 0, 0)),
        ],
        out_specs=pl.BlockSpec((nq, LANES), lambda hp, b: (b, hp)),
        out_shape=jax.ShapeDtypeStruct((SEQ, NA_W), BF16),
        scratch_shapes=[
            pltpu.VMEM((NA_RB, 2 * GRID_W, NA_NK), F32),
            pltpu.VMEM((NA_RB, 2 * GRID_W, NA_NK), BF16),
            pltpu.VMEM((NA_RB, 2 * GRID_W, LANES), F32),
        ],
        compiler_params=_params(("arbitrary", "arbitrary")),
        name="na_attention",
    )(qkv, qkv, qkv, qkv, qkv, bias)


def _na_bias_rows(rpb):
    c = np.arange(GRID_W)[:, None]
    kc = np.arange(GRID_W)[None, :]
    ws = np.clip(c - NA_KW // 2, 0, GRID_W - NA_KW)
    col_ok = (kc >= ws) & (kc < ws + NA_KW)
    dc = np.clip(kc - c + NA_KW - 1, 0, 2 * NA_KW - 2)
    pick = (dc[None] == np.arange(2 * NA_KW - 1)[:, None, None]).astype(np.float32)
    full = jnp.einsum("hrd,dck->hrck", rpb.astype(F32), jnp.asarray(pick), precision=lax.Precision.HIGHEST)
    full = jnp.where(jnp.asarray(col_ok)[None, None], full, NEG_BIG)
    rows = [jnp.transpose(full[:, e:e + NA_KH], (0, 2, 1, 3)).reshape(NA_HEADS, GRID_W, NA_WIN)
            for e in range(NA_KH)]
    return jnp.stack(rows, axis=1)


def _lane_shift(x, d):
    return pltpu.roll(x, (LANES - d) % LANES, axis=1)


def _epilogue(h1, mod_ref, g2_ref, rw_ref, rb_ref, tril_ref, run_ref, xrow_ref, meta_ref, cnt_ref):
    xm2 = _rms_mod(h1, g2_ref[...], mod_ref[0, 3:4, :], mod_ref[0, 4:5, :])
    xrow_ref[:, 0:D] = xm2
    xh = xm2.astype(BF16)
    xl = (xm2 - xh.astype(F32)).astype(BF16)
    w = rw_ref[...]
    wh = w.astype(BF16)
    wl = (w - wh.astype(F32)).astype(BF16)
    logits = _dot(xh, wh) + (_dot(xl, wh) + _dot(xh, wl))
    scores = jax.nn.sigmoid(logits)
    biased = scores + rb_ref[...]
    lane = lax.broadcasted_iota(jnp.int32, (TM, LANES), 1)
    k = lane & (EPG - 1)
    valid = lane < N_EXPERTS
    offs = (-3, -2, -1, 1, 2, 3)
    in_grp = {d: (k + d >= 0) & (k + d < EPG) for d in offs}
    rank = jnp.zeros((TM, LANES), F32)
    for d in offs:
        v = _lane_shift(biased, d)
        beats = (v > biased) | (v == biased) if d < 0 else (v > biased)
        rank = rank + jnp.where(in_grp[d] & beats, 1.0, 0.0)
    top2 = rank < 2.0
    t = jnp.where(top2, biased, 0.0)
    gs = t
    for d in offs:
        gs = gs + jnp.where(in_grp[d], _lane_shift(t, d), 0.0)
    beaten = jnp.zeros((TM, LANES), jnp.bool_)
    for d in (-12, -8, -4, 4, 8, 12):
        v = _lane_shift(gs, d)
        ok = (lane + d >= 0) & (lane + d < N_EXPERTS)
        beats = (v > gs) | (v == gs) if d < 0 else (v > gs)
        beaten = beaten | (ok & beats)
    sel = valid & top2 & jnp.logical_not(beaten)
    wsel = jnp.where(sel, scores, 0.0)
    gate = wsel / jnp.sum(wsel, axis=-1, keepdims=True)
    self_f = jnp.where(sel, 1.0, 0.0)
    before = jnp.zeros((TM, LANES), F32)
    for d in (-3, -2, -1):
        before = before + jnp.where(in_grp[d], _lane_shift(self_f, d), 0.0)
    is_lo = sel & (before == 0.0)
    w_lo = jnp.sum(jnp.where(is_lo, gate, 0.0), axis=-1, keepdims=True)
    w_hi = jnp.sum(jnp.where(sel & jnp.logical_not(is_lo), gate, 0.0), axis=-1, keepdims=True)
    code = ((1 << k) + 8 * (lane >> 2)).astype(F32)
    bid = jnp.sum(jnp.where(sel, code, 0.0), axis=-1, keepdims=True)
    onehot = lane.astype(F32) == bid
    cum = _dot(tril_ref[...], jnp.where(onehot, 1.0, 0.0).astype(BF16))
    run = run_ref[0:1, :]
    rank_g = jnp.sum(jnp.where(onehot, cum + run, 0.0), axis=-1, keepdims=True)
    cnt = jnp.sum(jnp.where(onehot, 1.0, 0.0), axis=0, keepdims=True)
    new_run = jnp.broadcast_to(run + cnt, (8, LANES))
    run_ref[...] = new_run
    cnt_ref[...] = new_run
    meta = jnp.where(lane == 0, w_lo, jnp.where(lane == 1, w_hi, jnp.where(lane == 2, bid, jnp.where(lane == 3, rank_g, 0.0))))
    xrow_ref[:, D:XROW_W] = meta
    meta_ref[...] = meta


def _epilogue_specs(ntiles):
    in_specs = [
        _resident((1, D)),
        _resident((D, LANES)),
        _resident((1, LANES)),
        _resident((TM, TM)),
    ]
    out_specs = [
        pl.BlockSpec((TM, D), lambda i: (i, 0)),
        pl.BlockSpec((TM, XROW_W), lambda i: (i, 0)),
        pl.BlockSpec((TM, LANES), lambda i: (i, 0)),
        pl.BlockSpec((8, LANES), lambda i: (0, 0)),
    ]
    n = ntiles * TM
    out_shape = [
        jax.ShapeDtypeStruct((n, D), F32),
        jax.ShapeDtypeStruct((n, XROW_W), F32),
        jax.ShapeDtypeStruct((n, LANES), F32),
        jax.ShapeDtypeStruct((8, LANES), F32),
    ]
    return in_specs, out_specs, out_shape


def _outproj0_kernel(ao_ref, h_ref, mod_ref, w_ref, g2_ref, rw_ref, rb_ref, tril_ref,
                     h1_ref, xrow_ref, meta_ref, cnt_ref, run_ref):
    @pl.when(pl.program_id(0) == 0)
    def _():
        run_ref[...] = jnp.zeros_like(run_ref)

    y = _dot(ao_ref[...], w_ref[...])
    h1 = h_ref[...] + mod_ref[0, 2:3, :] * y
    h1_ref[...] = h1
    _epilogue(h1, mod_ref, g2_ref, rw_ref, rb_ref, tril_ref, run_ref, xrow_ref, meta_ref, cnt_ref)


def _outproj0(ao, h, modv, w_out, g2, rw, rb, tril):
    e_in, e_out, e_shape = _epilogue_specs(NT)
    return pl.pallas_call(
        _outproj0_kernel,
        grid=(NT,),
        in_specs=[
            pl.BlockSpec((TM, D), lambda i: (i, 0)),
            pl.BlockSpec((TM, D), lambda i: (i, 0)),
            _mod_spec(),
            _resident((D, D)),
        ] + e_in,
        out_specs=e_out,
        out_shape=e_shape,
        scratch_shapes=[pltpu.VMEM((8, LANES), F32)],
        compiler_params=_params(("arbitrary",)),
        name="outproj0",
    )(ao, h, modv, w_out, g2, rw, rb, tril)


def _bucket_tables():
    lo = np.zeros(N_BUCKET_IDS, np.int32)
    hi = np.zeros(N_BUCKET_IDS, np.int32)
    for g in range(N_EXPERTS // EPG):
        for a in range(EPG):
            for b in range(a + 1, EPG):
                i = 16 * g + (1 << a) + (1 << b)
                lo[i] = EPG * g + a
                hi[i] = EPG * g + b
    return lo, hi


def _slot_kernel(meta_ref, starts_ref, o_ref):
    lane = lax.broadcasted_iota(jnp.int32, (TM, LANES), 1).astype(F32)
    first = jnp.sum(jnp.where(lane == meta_ref[:, 2:3], starts_ref[...], 0.0), axis=-1, keepdims=True)
    o_ref[...] = jnp.broadcast_to((first + meta_ref[:, 3:4]).astype(jnp.int32), (TM, LANES))


def _token_slots(meta, first_row):
    n = meta.shape[0]
    out = pl.pallas_call(
        _slot_kernel,
        grid=(n // TM,),
        in_specs=[pl.BlockSpec((TM, LANES), lambda i: (i, 0)), pl.BlockSpec((1, LANES), lambda i: (0, 0))],
        out_specs=pl.BlockSpec((TM, LANES), lambda i: (i, 0)),
        out_shape=jax.ShapeDtypeStruct((n, LANES), jnp.int32),
        compiler_params=_params(("arbitrary",)),
        name="token_slots",
    )(meta, first_row)
    return out[:, 0]


def _route_plan(meta, cnt, n_tiles, layer):
    counts = cnt[0, :N_BUCKET_IDS].astype(jnp.int32)
    nt_b = (counts + TMM - 1) // TMM
    ends = jnp.cumsum(nt_b)
    starts = ends - nt_b
    first_row = jnp.pad((starts * TMM).astype(F32), (0, LANES - N_BUCKET_IDS)).reshape(1, LANES)
    pos = _token_slots(meta, first_row)
    n_used = ends[-1]
    tile = jnp.minimum(jnp.arange(n_tiles, dtype=jnp.int32), n_used - 1)
    tile_b = jnp.sum((ends[None, :] <= tile[:, None]).astype(jnp.int32), axis=1)
    lo, hi = _bucket_tables()
    e_lo = jnp.asarray(lo)[tile_b] + layer * N_EXPERTS
    e_hi = jnp.asarray(hi)[tile_b] + layer * N_EXPERTS
    return pos.astype(jnp.int32), e_lo.astype(jnp.int32), e_hi.astype(jnp.int32), n_used.reshape(1).astype(jnp.int32)


def _dispatch_kernel(pos_ref, x_ref, xs_in_ref, xs_ref, sem):
    del xs_in_ref
    base = pl.program_id(0) * TM

    def issue(r, carry):
        p = pos_ref[base + r]
        pltpu.make_async_copy(x_ref.at[pl.ds(r, 1), :], xs_ref.at[pl.ds(p, 1), :], sem).start()
        return carry

    lax.fori_loop(0, TM, issue, 0)
    pltpu.make_async_copy(x_ref, xs_ref.at[pl.ds(0, TM), :], sem).wait()


def _dispatch(pos, xrow, n_tiles):
    ntok = xrow.shape[0]
    xs0 = jnp.zeros((n_tiles * TMM, XROW_W), F32)
    return pl.pallas_call(
        _dispatch_kernel,
        grid_spec=pltpu.PrefetchScalarGridSpec(
            num_scalar_prefetch=1,
            grid=(ntok // TM,),
            in_specs=[
                pl.BlockSpec((TM, XROW_W), lambda i, pos: (i, 0)),
                pl.BlockSpec(memory_space=pl.ANY),
            ],
            out_specs=pl.BlockSpec(memory_space=pl.ANY),
            scratch_shapes=[pltpu.SemaphoreType.DMA(())],
        ),
        out_shape=jax.ShapeDtypeStruct((n_tiles * TMM, XROW_W), F32),
        input_output_aliases={2: 0},
        compiler_params=_params(("arbitrary",)),
        name="moe_dispatch",
    )(pos, xrow, xs0)


def _moe_kernel(elo_ref, ehi_ref, nu_ref, xs_ref, w1a, w3a, w2a, w1b, w3b, w2b, ys_ref):
    j = pl.program_id(0)

    @pl.when(j < nu_ref[0])
    def _():
        x = xs_ref[:, 0:D].astype(BF16)
        g_lo = xs_ref[:, D:D + 1]
        g_hi = xs_ref[:, D + 1:D + 2]

        def expert(w1, w3, w2):
            a = _dot(x, w1[0])
            hid = (a * jax.nn.sigmoid(a)) * _dot(x, w3[0])
            return _dot(hid.astype(BF16), w2[0])

        ys_ref[...] = g_lo * expert(w1a, w3a, w2a) + g_hi * expert(w1b, w3b, w2b)

    @pl.when(j >= nu_ref[0])
    def _():
        ys_ref[...] = jnp.zeros_like(ys_ref)


def _moe(e_lo, e_hi, n_used, xs, w1, w3, w2, n_tiles):
    def wspec(shape, which):
        if which == 0:
            return pl.BlockSpec(shape, lambda j, lo, hi, nu: (lo[j], 0, 0))
        return pl.BlockSpec(shape, lambda j, lo, hi, nu: (hi[j], 0, 0))

    s13 = (1, D, D_EXPERT)
    s2 = (1, D_EXPERT, D)
    return pl.pallas_call(
        _moe_kernel,
        grid_spec=pltpu.PrefetchScalarGridSpec(
            num_scalar_prefetch=3,
            grid=(n_tiles,),
            in_specs=[
                pl.BlockSpec((TMM, XROW_W), lambda j, lo, hi, nu: (j, 0)),
                wspec(s13, 0), wspec(s13, 0), wspec(s2, 0),
                wspec(s13, 1), wspec(s13, 1), wspec(s2, 1),
            ],
            out_specs=pl.BlockSpec((TMM, D), lambda j, lo, hi, nu: (j, 0)),
        ),
        out_shape=jax.ShapeDtypeStruct((n_tiles * TMM, D), F32),
        compiler_params=_params(("arbitrary",)),
        name="moe_experts",
    )(e_lo, e_hi, n_used, xs, w1, w3, w2, w1, w3, w2)


def _gather_issue(pos_ref, ys_hbm, buf, sem, tile, slot):
    base = tile * TM

    def issue(r, carry):
        p = pos_ref[base + r]
        pltpu.make_async_copy(ys_hbm.at[pl.ds(p, 1), :], buf.at[slot, pl.ds(r, 1), :], sem.at[slot]).start()
        return carry

    lax.fori_loop(0, TM, issue, 0)


def _gathered_rows(pos_ref, ys_hbm, buf, sem):
    i = pl.program_id(0)
    slot = i % 2

    @pl.when(i == 0)
    def _():
        _gather_issue(pos_ref, ys_hbm, buf, sem, 0, 0)

    @pl.when(i + 1 < pl.num_programs(0))
    def _():
        _gather_issue(pos_ref, ys_hbm, buf, sem, i + 1, 1 - slot)

    pltpu.make_async_copy(ys_hbm.at[pl.ds(0, TM), :], buf.at[slot], sem.at[slot]).wait()
    return buf[slot]


_GATHER_SCRATCH = [pltpu.VMEM((2, TM, D), F32), pltpu.SemaphoreType.DMA((2,))]


def _rope_store(x, tabs, out_ref):
    cr, sr, cc, sc = tabs
    for ch in range(RET_QK // LANES):
        xc = x[:, ch * LANES:(ch + 1) * LANES]
        xr = pltpu.roll(xc, LANES // 2, axis=1)
        y = xc * cr + xr * sr if ch % 2 == 0 else xc * cc + xr * sc
        out_ref[:, ch * LANES:(ch + 1) * LANES] = y.astype(BF16)


def _inproj1_kernel(pos_ref, h_ref, ys_hbm, mod0_ref, mod1_ref, g_ref, w_ref, cr_ref, sr_ref, cc_ref, sc_ref,
                    h2_ref, q_ref, k_ref, v_ref, gg_ref, buf, sem):
    f = _gathered_rows(pos_ref, ys_hbm, buf, sem)
    h2 = h_ref[...] + mod0_ref[0, 5:6, :] * f
    h2_ref[...] = h2
    xm = _rms_mod(h2, g_ref[...], mod1_ref[0, 0:1, :], mod1_ref[0, 1:2, :]).astype(BF16)
    tabs = (cr_ref[...], sr_ref[...], cc_ref[...], sc_ref[...])
    _rope_store(_dot(xm, w_ref[:, 0:RET_QK]), tabs, q_ref)
    _rope_store(_dot(xm, w_ref[:, RET_QK:2 * RET_QK]) * (RET_DK ** -0.5), tabs, k_ref)
    v_ref[...] = _dot(xm, w_ref[:, 2 * RET_QK:2 * RET_QK + RET_V]).astype(BF16)
    gg_ref[...] = _dot(xm, w_ref[:, 2 * RET_QK + RET_V:]).astype(BF16)


def _inproj1(pos, h1, ys, modv0, modv1, g, w_in, tabs):
    tok = lambda w: pl.BlockSpec((TM, w), lambda i, pos: (i, 0))
    res = lambda shape: pl.BlockSpec(shape, lambda i, pos: (0,) * len(shape), pipeline_mode=pl.Buffered(1))
    mod = pl.BlockSpec((1, 8, D), lambda i, pos: (i // NT_LAT, 0, 0))
    return pl.pallas_call(
        _inproj1_kernel,
        grid_spec=pltpu.PrefetchScalarGridSpec(
            num_scalar_prefetch=1,
            grid=(NT,),
            in_specs=[tok(D), pl.BlockSpec(memory_space=pl.ANY), mod, mod, res((1, D)), res((D, 2 * RET_QK + 2 * RET_V)),
                      tok(LANES), tok(LANES), tok(LANES), tok(LANES)],
            out_specs=[tok(D), tok(RET_QK), tok(RET_QK), tok(RET_V), tok(RET_V)],
            scratch_shapes=_GATHER_SCRATCH,
        ),
        out_shape=[
            jax.ShapeDtypeStruct((T, D), F32),
            jax.ShapeDtypeStruct((T, RET_QK), BF16),
            jax.ShapeDtypeStruct((T, RET_QK), BF16),
            jax.ShapeDtypeStruct((T, RET_V), BF16),
            jax.ShapeDtypeStruct((T, RET_V), BF16),
        ],
        compiler_params=_params(("arbitrary",)),
        name="inproj1",
    )(pos, h1, ys, modv0, modv1, g, w_in, *tabs)


def _rope_tables():
    n = LANES // 2
    inv_freq = ROPE_BASE ** (-jnp.arange(n, dtype=F32) / n)
    t = jnp.arange(SEQ)

    def tab(p):
        ang = p.astype(F32)[:, None] * inv_freq[None, :]
        cos, sin = jnp.cos(ang), jnp.sin(ang)
        c = jnp.concatenate([cos, cos], axis=1)
        s = jnp.concatenate([-sin, sin], axis=1)
        c = jnp.concatenate([c, jnp.ones((CTX, LANES), F32)], axis=0)
        s = jnp.concatenate([s, jnp.zeros((CTX, LANES), F32)], axis=0)
        return c, s

    cr, sr = tab(t // GRID_W)
    cc, sc = tab(t % GRID_W)
    return cr, sr, cc, sc


RET_NCHUNK = T // RET_CHUNK
RET_CTX_CHUNKS = CTX // RET_CHUNK
RET_LAT_CHUNKS = SEQ // RET_CHUNK
RET_NHD = 2 * RET_HEADS


def _ret_kernel(dec_ref, qf_ref, kf_ref, vf_ref, qb_ref, kb_ref, vb_ref, of_ref, ob_ref,
                state, dm, qd, kd, cd):
    C = RET_CHUNK

    @pl.when(pl.program_id(0) == 0)
    def _():
        state[...] = jnp.zeros_like(state)
        n = lax.broadcasted_iota(jnp.int32, (C, C), 0).astype(F32)
        m = lax.broadcasted_iota(jnp.int32, (C, C), 1).astype(F32)
        for d in range(2):
            for h in range(RET_HEADS):
                idx = d * RET_HEADS + h
                lg = jnp.log1p(-jnp.exp(jnp.full((C, C), dec_ref[d, h], F32)))
                diff = n - m if d == 0 else m - n
                low = diff >= 0.0
                dm[idx] = jnp.where(low, jnp.exp(jnp.where(low, diff, 0.0) * lg), 0.0)
                qpow = n + 1.0 if d == 0 else C - n
                kpow = (C - 1.0) - n if d == 0 else n
                qcol = jnp.exp(qpow * lg)
                kcol = jnp.exp(kpow * lg)
                qd[idx] = jnp.concatenate([qcol] * (RET_DV // C), axis=1)
                kd[idx] = jnp.concatenate([kcol] * (RET_DK // C), axis=1)
                cd[idx] = jnp.exp(C * lg[0:8, :])

    for d, (q_ref, k_ref, v_ref, o_ref) in enumerate(((qf_ref, kf_ref, vf_ref, of_ref),
                                                       (qb_ref, kb_ref, vb_ref, ob_ref))):
        for h in range(RET_HEADS):
            idx = d * RET_HEADS + h
            q = q_ref[:, h * RET_DK:(h + 1) * RET_DK]
            k = k_ref[:, h * RET_DK:(h + 1) * RET_DK]
            v = v_ref[:, h * RET_DV:(h + 1) * RET_DV]
            st = state[idx]
            s = _dot_nt(q, k) * dm[idx]
            o = _dot(s.astype(BF16), v) + qd[idx] * _dot(q, st.astype(BF16))
            o_ref[:, h * RET_DV:(h + 1) * RET_DV] = o.astype(BF16)
            kdk = (k.astype(F32) * kd[idx]).astype(BF16)
            cdv = jnp.concatenate([cd[idx, 0:1, :]] * (RET_DV // C), axis=1)
            state[idx] = st * cdv + _dot_tn(kdk, v)


def _ret_scan(decay, q, k, v):
    def fwd(s):
        return jnp.where(s < RET_CTX_CHUNKS, RET_LAT_CHUNKS + s, s - RET_CTX_CHUNKS)

    def bwd(s):
        return RET_NCHUNK - 1 - s

    def spec(w, f):
        return pl.BlockSpec((RET_CHUNK, w), lambda s: (f(s), 0))

    return pl.pallas_call(
        _ret_kernel,
        grid=(RET_NCHUNK,),
        in_specs=[
            pl.BlockSpec(memory_space=pltpu.SMEM),
            spec(RET_QK, fwd), spec(RET_QK, fwd), spec(RET_V, fwd),
            spec(RET_QK, bwd), spec(RET_QK, bwd), spec(RET_V, bwd),
        ],
        out_specs=[spec(RET_V, fwd), spec(RET_V, bwd)],
        out_shape=[jax.ShapeDtypeStruct((T, RET_V), BF16), jax.ShapeDtypeStruct((T, RET_V), BF16)],
        scratch_shapes=[
            pltpu.VMEM((RET_NHD, RET_DK, RET_DV), F32),
            pltpu.VMEM((RET_NHD, RET_CHUNK, RET_CHUNK), F32),
            pltpu.VMEM((RET_NHD, RET_CHUNK, RET_DV), F32),
            pltpu.VMEM((RET_NHD, RET_CHUNK, RET_DK), F32),
            pltpu.VMEM((RET_NHD, 8, RET_CHUNK), F32),
        ],
        compiler_params=_params(("arbitrary",)),
        name="retention_scan",
    )(decay, q, k, v, q, k, v)


def _outproj1_kernel(of_ref, ob_ref, gg_ref, h_ref, mod_ref, w_ref, g2_ref, rw_ref, rb_ref, tril_ref,
                     h1_ref, xrow_ref, meta_ref, cnt_ref, run_ref):
    @pl.when(pl.program_id(0) == 0)
    def _():
        run_ref[...] = jnp.zeros_like(run_ref)

    y = jnp.zeros((TM, D), F32)
    for hh in range(RET_HEADS):
        cols = slice(hh * RET_DV, (hh + 1) * RET_DV)
        o = of_ref[:, cols].astype(F32) + ob_ref[:, cols].astype(F32)
        on = o * lax.rsqrt(jnp.mean(o * o, axis=-1, keepdims=True) + RMS_EPS)
        g = gg_ref[:, cols].astype(F32)
        z = (g * jax.nn.sigmoid(g)) * on
        y = y + _dot(z.astype(BF16), w_ref[cols, :])
    h1 = h_ref[...] + mod_ref[0, 2:3, :] * y
    h1_ref[...] = h1
    _epilogue(h1, mod_ref, g2_ref, rw_ref, rb_ref, tril_ref, run_ref, xrow_ref, meta_ref, cnt_ref)


def _outproj1(o_f, o_b, gg, h, modv, w_out, g2, rw, rb, tril):
    e_in, e_out, e_shape = _epilogue_specs(NT_LAT)
    tok = lambda w: pl.BlockSpec((TM, w), lambda i: (i, 0))
    return pl.pallas_call(
        _outproj1_kernel,
        grid=(NT_LAT,),
        in_specs=[tok(RET_V), tok(RET_V), tok(RET_V), tok(D), _mod_spec(), _resident((RET_V, D))] + e_in,
        out_specs=e_out,
        out_shape=e_shape,
        scratch_shapes=[pltpu.VMEM((8, LANES), F32)],
        compiler_params=_params(("arbitrary",)),
        name="outproj1",
    )(o_f, o_b, gg, h, modv, w_out, g2, rw, rb, tril)


def _final_kernel(pos_ref, h_ref, ys_hbm, mod_ref, g_ref, o_ref, buf, sem):
    f = _gathered_rows(pos_ref, ys_hbm, buf, sem)
    h = h_ref[...] + mod_ref[0, 5:6, :] * f
    o_ref[...] = h * lax.rsqrt(jnp.mean(h * h, axis=-1, keepdims=True) + RMS_EPS) * g_ref[...]


def _final(pos, h, ys, modv, g):
    tok = pl.BlockSpec((TM, D), lambda i, pos: (i, 0))
    return pl.pallas_call(
        _final_kernel,
        grid_spec=pltpu.PrefetchScalarGridSpec(
            num_scalar_prefetch=1,
            grid=(NT_LAT,),
            in_specs=[tok, pl.BlockSpec(memory_space=pl.ANY),
                      pl.BlockSpec((1, 8, D), lambda i, pos: (0, 0, 0)),
                      pl.BlockSpec((1, D), lambda i, pos: (0, 0))],
            out_specs=tok,
            scratch_shapes=_GATHER_SCRATCH,
        ),
        out_shape=jax.ShapeDtypeStruct((SEQ, D), F32),
        compiler_params=_params(("arbitrary",)),
        name="final_norm",
    )(pos, h, ys, modv, g)


def _dft_constants():
    c = np.arange(FNET_GD)
    ang = 2.0 * np.pi * np.outer(c, c) / FNET_GD
    d64 = np.zeros((FNET_W, 2 * FNET_W))
    for g in range(FNET_W // FNET_GD):
        sl = slice(g * FNET_GD, (g + 1) * FNET_GD)
        d64[sl, sl] = np.cos(ang)
        d64[sl, FNET_W + g * FNET_GD:FNET_W + (g + 1) * FNET_GD] = -np.sin(ang)
    k2 = np.arange(FFT_N2)[:, None]
    n2 = np.arange(FFT_N2)[None, :]
    ga = np.zeros((FFT_N1, 2 * FFT_N2, 2 * FFT_N2))
    for n1 in range(FFT_N1):
        th = 2.0 * np.pi * ((k2 * (n1 + FFT_N1 * n2)) % SEQ) / SEQ
        cs, sn = np.cos(th), np.sin(th)
        ga[n1] = np.block([[cs, sn], [-sn, cs]])
    k1 = np.arange(FFT_N1)
    ph = 2.0 * np.pi * np.outer(k1, k1) / FFT_N1
    csb = np.concatenate([np.cos(ph), np.sin(ph)], axis=1)
    p = np.arange(CTX)
    pc = 2.0 * np.pi * (np.outer(p, p) % CTX) / CTX
    csc = np.concatenate([np.cos(pc), np.sin(pc)], axis=1)
    as_bf16 = lambda a: jnp.asarray(a, dtype=F32).astype(BF16)
    return as_bf16(d64), as_bf16(ga), as_bf16(csb), as_bf16(csc)


def _mod_table(mod_l):
    t = mod_l.reshape(2, 6, D)
    return jnp.concatenate([t, jnp.zeros((2, 2, D), F32)], axis=1)


def kernel(x, c, ctx, c_ctx, ada_w, ada_b, norm_g, final_norm_g, mixab_w_in, mixab_w_out, na_rpb,
           ret_w_in, ret_w_out, ret_decay, router_w, router_b, moe_w1, moe_w3, moe_w2):
    d64, ga, csb, csc = _dft_constants()
    tril = jnp.asarray(np.tril(np.ones((TM, TM)), -1), dtype=BF16)
    rw = jnp.pad(router_w, ((0, 0), (0, LANES - N_EXPERTS)))
    rb = jnp.pad(router_b, (0, LANES - N_EXPERTS)).reshape(1, LANES)
    w1 = moe_w1.reshape(DEPTH * N_EXPERTS, D, D_EXPERT).astype(BF16)
    w3 = moe_w3.reshape(DEPTH * N_EXPERTS, D, D_EXPERT).astype(BF16)
    w2 = moe_w2.reshape(DEPTH * N_EXPERTS, D_EXPERT, D).astype(BF16)

    mod = _mod_vectors(c, c_ctx, ada_w, ada_b)
    modv0, modv1 = _mod_table(mod[0]), _mod_table(mod[1])
    h0 = jnp.concatenate([x[0], ctx[0]], axis=0)

    qkv, br, bi = _inproj0(h0, modv0, norm_g[0, 0].reshape(1, D), mixab_w_in[0].astype(BF16), d64)
    a_lat = _fft_latent(br, bi, ga, csb)
    ao_ctx = _ctx_mixer(qkv, br, bi, csc)
    o_lat = _na_attention(qkv, _na_bias_rows(na_rpb[0]))
    ao = jnp.concatenate([jnp.concatenate([a_lat.astype(BF16), o_lat], axis=1), ao_ctx], axis=0)
    h1, xrow, meta, cnt = _outproj0(ao, h0, modv0, mixab_w_out[0].astype(BF16), norm_g[0, 1].reshape(1, D),
                                    rw, rb, tril)
    n_tiles0 = NT + N_REAL_BUCKETS
    pos0, e_lo, e_hi, n_used = _route_plan(meta, cnt, n_tiles0, 0)
    xs = _dispatch(pos0, xrow, n_tiles0)
    ys0 = _moe(e_lo, e_hi, n_used, xs, w1, w3, w2, n_tiles0)

    h2, q, k, v, gg = _inproj1(pos0, h1, ys0, modv0, modv1, norm_g[1, 0].reshape(1, D),
                               ret_w_in[0].astype(BF16), _rope_tables())
    o_f, o_b = _ret_scan(ret_decay[0].astype(F32), q, k, v)
    h3, xrow1, meta1, cnt1 = _outproj1(o_f, o_b, gg, h2, modv1, ret_w_out[0].astype(BF16),
                                       norm_g[1, 1].reshape(1, D), rw, rb, tril)
    n_tiles1 = NT_LAT + N_REAL_BUCKETS
    pos1, e_lo1, e_hi1, n_used1 = _route_plan(meta1, cnt1, n_tiles1, 1)
    xs1 = _dispatch(pos1, xrow1, n_tiles1)
    ys1 = _moe(e_lo1, e_hi1, n_used1, xs1, w1, w3, w2, n_tiles1)
    out = _final(pos1, h3, ys1, modv1, final_norm_g.reshape(1, D))
    return out[None]
```

```python
import functools
import math

import numpy as np
import jax
import jax.numpy as jnp
from jax import lax
from jax.experimental import pallas as pl
from jax.experimental.pallas import tpu as pltpu

F32 = jnp.float32
BF16 = jnp.bfloat16

D = 1024
SEQ = 16384
CTX = 256
T = SEQ + CTX
DEPTH = 2
GRID_W = 64
ROWS = SEQ // GRID_W
RMS_EPS = 1e-6
ROPE_BASE = 10000.0

FNET_W = 256
FNET_GD = 64
NA_HEADS = 12
NA_HD = 64
NA_W = NA_HEADS * NA_HD
NA_KH = 8
NA_KW = 16
AB_IN = FNET_W + 3 * NA_W
QKV_W = 3 * NA_W

RET_HEADS = 4
RET_DK = 256
RET_DV = 512
RET_QK = RET_HEADS * RET_DK
RET_V = RET_HEADS * RET_DV
RET_CHUNK = 128

N_EXPERTS = 16
EPG = 4
D_EXPERT = 512

LANES = 128
TM = 256
NT = T // TM
NT_LAT = SEQ // TM
TMM = 256
N_BUCKET_IDS = 64
N_REAL_BUCKETS = 24
XROW_W = D + LANES
FFT_N1 = 128
FFT_N2 = 128
NEG_BIG = -1e30
SUBLANES = 8

VMEM_LIMIT = 56 * 1024 * 1024


def _params(sem, vmem=VMEM_LIMIT):
    return pltpu.CompilerParams(dimension_semantics=sem, vmem_limit_bytes=vmem)


def _dot(a, b):
    return jnp.dot(a, b, preferred_element_type=F32)


def _dot_nt(a, b):
    return lax.dot_general(a, b, (((1,), (1,)), ((), ())), preferred_element_type=F32)


def _dot_tn(a, b):
    return lax.dot_general(a, b, (((0,), (0,)), ((), ())), preferred_element_type=F32)


def _rms_mod(x, g, sh, sc):
    ms = jnp.mean(x * x, axis=-1, keepdims=True)
    y = x * lax.rsqrt(ms + RMS_EPS) * g
    return y * (1.0 + sc) + sh


def _resident(shape):
    nd = len(shape)
    return pl.BlockSpec(shape, lambda *_: (0,) * nd, pipeline_mode=pl.Buffered(1))


MOD_TN = 768


def _mod_kernel(cs_ref, w_ref, b_ref, o_ref):
    cs = cs_ref[...]
    s = cs * jax.nn.sigmoid(cs)
    w = w_ref[0]
    r0 = jnp.sum(s[:, 0:1] * w, axis=0, keepdims=True)
    r1 = jnp.sum(s[:, 1:2] * w, axis=0, keepdims=True)
    o_ref[0] = jnp.concatenate([r0, r1], axis=0) + b_ref[0]


def _mod_vectors(c, c_ctx, ada_w, ada_b):
    cs = jnp.stack([c[0], c_ctx], axis=1)
    n = 6 * D
    return pl.pallas_call(
        _mod_kernel,
        grid=(DEPTH, n // MOD_TN),
        in_specs=[
            pl.BlockSpec((D, 2), lambda l, j: (0, 0)),
            pl.BlockSpec((1, D, MOD_TN), lambda l, j: (l, 0, j)),
            pl.BlockSpec((1, 1, MOD_TN), lambda l, j: (l, 0, j)),
        ],
        out_specs=pl.BlockSpec((1, 2, MOD_TN), lambda l, j: (l, 0, j)),
        out_shape=jax.ShapeDtypeStruct((DEPTH, 2, n), F32),
        compiler_params=_params(("arbitrary", "arbitrary")),
        name="mod_vectors",
    )(cs, ada_w, ada_b.reshape(DEPTH, 1, n))


def _mod_spec():
    return pl.BlockSpec((1, 8, D), lambda i, *_: (i // NT_LAT, 0, 0))


def _inproj0_kernel(h_ref, mod_ref, g_ref, w_ref, d64_ref, qkv_ref, br_ref, bi_ref):
    xm = _rms_mod(h_ref[...], g_ref[...], mod_ref[0, 0:1, :], mod_ref[0, 1:2, :]).astype(BF16)
    p = _dot(xm, w_ref[...])
    qkv_ref[...] = p[:, FNET_W:].astype(BF16)
    b = _dot(p[:, :FNET_W].astype(BF16), d64_ref[...])
    br_ref[...] = b[:, :FNET_W].astype(BF16)
    bi_ref[...] = b[:, FNET_W:].astype(BF16)


def _inproj0(h, modv, g, w_in, d64):
    return pl.pallas_call(
        _inproj0_kernel,
        grid=(NT,),
        in_specs=[
            pl.BlockSpec((TM, D), lambda i: (i, 0)),
            _mod_spec(),
            _resident((1, D)),
            _resident((D, AB_IN)),
            _resident((FNET_W, 2 * FNET_W)),
        ],
        out_specs=[
            pl.BlockSpec((TM, QKV_W), lambda i: (i, 0)),
            pl.BlockSpec((TM, FNET_W), lambda i: (i, 0)),
            pl.BlockSpec((TM, FNET_W), lambda i: (i, 0)),
        ],
        out_shape=[
            jax.ShapeDtypeStruct((T, QKV_W), BF16),
            jax.ShapeDtypeStruct((T, FNET_W), BF16),
            jax.ShapeDtypeStruct((T, FNET_W), BF16),
        ],
        compiler_params=_params(("arbitrary",)),
        name="inproj0",
    )(h, modv, g, w_in, d64)


FFT_A_NB = 8
FFT_B_TN = 4096


def _fft_a_kernel(br_ref, bi_ref, ga_ref, v_ref):
    for u in range(FFT_A_NB):
        cols = slice(u * FNET_W, (u + 1) * FNET_W)
        z = _dot(ga_ref[u, :, 0:FFT_N2], br_ref[:, cols]) + _dot(ga_ref[u, :, FFT_N2:], bi_ref[:, cols])
        v_ref[0, u] = z[:FFT_N2].astype(BF16)
        v_ref[1, u] = z[FFT_N2:].astype(BF16)


def _fft_b_kernel(cs_ref, v_ref, o_ref):
    o_ref[...] = _dot(cs_ref[...], v_ref[...]) * (1.0 / math.sqrt(SEQ * FNET_GD))


def _fft_latent(br, bi, ga, csb):
    width = FFT_N1 * FNET_W
    br2 = br.reshape(T // FFT_N1, width)
    bi2 = bi.reshape(T // FFT_N1, width)
    v = pl.pallas_call(
        _fft_a_kernel,
        grid=(FFT_N1 // FFT_A_NB,),
        in_specs=[
            pl.BlockSpec((FFT_N2, FFT_A_NB * FNET_W), lambda j: (0, j)),
            pl.BlockSpec((FFT_N2, FFT_A_NB * FNET_W), lambda j: (0, j)),
            pl.BlockSpec((FFT_A_NB, 2 * FFT_N2, 2 * FFT_N2), lambda j: (j, 0, 0)),
        ],
        out_specs=pl.BlockSpec((2, FFT_A_NB, FFT_N2, FNET_W), lambda j: (0, j, 0, 0)),
        out_shape=jax.ShapeDtypeStruct((2, FFT_N1, FFT_N2, FNET_W), BF16),
        compiler_params=_params(("arbitrary",)),
        name="fft_stage_a",
    )(br2, bi2, ga)
    v2 = v.reshape(2 * FFT_N1, FFT_N2 * FNET_W)
    x = pl.pallas_call(
        _fft_b_kernel,
        grid=(FFT_N2 * FNET_W // FFT_B_TN,),
        in_specs=[
            _resident((FFT_N1, 2 * FFT_N1)),
            pl.BlockSpec((2 * FFT_N1, FFT_B_TN), lambda j: (0, j)),
        ],
        out_specs=pl.BlockSpec((FFT_N1, FFT_B_TN), lambda j: (0, j)),
        out_shape=jax.ShapeDtypeStruct((FFT_N1, FFT_N2 * FNET_W), F32),
        compiler_params=_params(("arbitrary",)),
        name="fft_stage_b",
    )(csb, v2)
    return x.reshape(SEQ, FNET_W)


def _head_pair_masks():
    lane = lax.broadcasted_iota(jnp.int32, (1, LANES), 1)
    return lane < NA_HD


def _ctx_kernel(qkv_ref, br_ref, bi_ref, cs_ref, o_ref):
    a = _dot(cs_ref[:, 0:CTX], br_ref[...]) + _dot(cs_ref[:, CTX:], bi_ref[...])
    o_ref[:, 0:FNET_W] = (a * (1.0 / math.sqrt(CTX * FNET_GD))).astype(BF16)
    m0 = _head_pair_masks()
    for hp in range(NA_HEADS // 2):
        q = qkv_ref[:, hp * LANES:(hp + 1) * LANES]
        k = qkv_ref[:, NA_W + hp * LANES:NA_W + (hp + 1) * LANES]
        v = qkv_ref[:, 2 * NA_W + hp * LANES:2 * NA_W + (hp + 1) * LANES]
        outs = []
        for a_ in range(2):
            qa = jnp.where(m0 if a_ == 0 else jnp.logical_not(m0), q, jnp.zeros_like(q))
            s = _dot_nt(qa, k) * (NA_HD ** -0.5)
            p = jnp.exp(s - jnp.max(s, axis=-1, keepdims=True))
            l = jnp.sum(p, axis=-1, keepdims=True)
            outs.append(_dot(p.astype(BF16), v) / l)
        o_ref[:, FNET_W + hp * LANES:FNET_W + (hp + 1) * LANES] = jnp.where(m0, outs[0], outs[1]).astype(BF16)


def _ctx_mixer(qkv, br, bi, csc):
    return pl.pallas_call(
        _ctx_kernel,
        grid=(1,),
        in_specs=[
            pl.BlockSpec((CTX, QKV_W), lambda i: (SEQ // CTX, 0)),
            pl.BlockSpec((CTX, FNET_W), lambda i: (SEQ // CTX, 0)),
            pl.BlockSpec((CTX, FNET_W), lambda i: (SEQ // CTX, 0)),
            pl.BlockSpec((CTX, 2 * CTX), lambda i: (0, 0)),
        ],
        out_specs=pl.BlockSpec((CTX, D), lambda i: (0, 0)),
        out_shape=jax.ShapeDtypeStruct((CTX, D), BF16),
        compiler_params=_params(("arbitrary",)),
        name="ctx_mixer",
    )(qkv, br, bi, csc)


NA_RB = 8
NA_WIN = NA_KH * GRID_W


NA_NK = NA_WIN + CTX
NA_SUB = 16


def _na_kernel(q_ref, k_ref, v_ref, kc_ref, vc_ref, bias_ref, o_ref, s_scr, p_scr, l_scr):
    b = pl.program_id(1)
    m0 = _head_pair_masks()
    nm0 = jnp.logical_not(m0)
    scale = NA_HD ** -0.5

    def window(i):
        r = b * NA_RB + i
        rs = jnp.clip(r - NA_KH // 2, 0, ROWS - NA_KH)
        return pl.multiple_of(rs * GRID_W, GRID_W), rs - r + (NA_KH - 1)

    def scores(i):
        start, _ = window(i)
        q = q_ref[i * GRID_W:(i + 1) * GRID_W, :] * scale
        qs = jnp.concatenate([jnp.where(m0, q, jnp.zeros_like(q)), jnp.where(nm0, q, jnp.zeros_like(q))], axis=0)
        s_scr[i, :, 0:NA_WIN] = _dot_nt(qs, k_ref[pl.ds(start, NA_WIN), :])
        s_scr[i, :, NA_WIN:NA_NK] = _dot_nt(qs, kc_ref[...])

    def softmax(i):
        _, e = window(i)
        for g in range(2 * GRID_W // NA_SUB):
            rows = slice(g * NA_SUB, (g + 1) * NA_SUB)
            a, c0 = divmod(g * NA_SUB, GRID_W)
            s1 = s_scr[i, rows, 0:NA_WIN] + bias_ref[a, e, c0:c0 + NA_SUB, :]
            s2 = s_scr[i, rows, NA_WIN:NA_NK]
            mx = jnp.maximum(jnp.max(s1, axis=-1, keepdims=True), jnp.max(s2, axis=-1, keepdims=True))
            p1 = jnp.exp(s1 - mx)
            p2 = jnp.exp(s2 - mx)
            l = jnp.sum(p1, axis=-1, keepdims=True) + jnp.sum(p2, axis=-1, keepdims=True)
            p_scr[i, rows, 0:NA_WIN] = p1.astype(BF16)
            p_scr[i, rows, NA_WIN:NA_NK] = p2.astype(BF16)
            l_scr[i, rows, :] = jnp.broadcast_to(l, (NA_SUB, LANES))

    def values(i):
        start, _ = window(i)
        o = _dot(p_scr[i, :, 0:NA_WIN], v_ref[pl.ds(start, NA_WIN), :]) + _dot(p_scr[i, :, NA_WIN:NA_NK], vc_ref[...])
        o = o / l_scr[i]
        o_ref[i * GRID_W:(i + 1) * GRID_W, :] = jnp.where(m0, o[0:GRID_W], o[GRID_W:]).astype(BF16)

    scores(0)
    for i in range(NA_RB):
        if i + 1 < NA_RB:
            scores(i + 1)
        softmax(i)
        values(i)


def _na_attention(qkv, bias):
    nq = NA_RB * GRID_W
    nhp = NA_HEADS // 2
    return pl.pallas_call(
        _na_kernel,
        grid=(nhp, ROWS // NA_RB),
        in_specs=[
            pl.BlockSpec((nq, LANES), lambda hp, b: (b, hp)),
            pl.BlockSpec((SEQ, LANES), lambda hp, b: (0, nhp + hp)),
            pl.BlockSpec((SEQ, LANES), lambda hp, b: (0, 2 * nhp + hp)),
            pl.BlockSpec((CTX, LANES), lambda hp, b: (SEQ // CTX, nhp + hp)),
            pl.BlockSpec((CTX, LANES), lambda hp, b: (SEQ // CTX, 2 * nhp + hp)),
            pl.BlockSpec((2, NA_KH, GRID_W, NA_WIN), lambda hp, b: (hp, 0, 0, 0)),
        ],
        out_specs=pl.BlockSpec((nq, LANES), lambda hp, b: (b, hp)),
        out_shape=jax.ShapeDtypeStruct((SEQ, NA_W), BF16),
        scratch_shapes=[
            pltpu.VMEM((NA_RB, 2 * GRID_W, NA_NK), F32),
            pltpu.VMEM((NA_RB, 2 * GRID_W, NA_NK), BF16),
            pltpu.VMEM((NA_RB, 2 * GRID_W, LANES), F32),
        ],
        compiler_params=_params(("arbitrary", "arbitrary")),
        name="na_attention",
    )(qkv, qkv, qkv, qkv, qkv, bias)


def _na_bias_rows(rpb):
    c = np.arange(GRID_W)[:, None]
    kc = np.arange(GRID_W)[None, :]
    ws = np.clip(c - NA_KW // 2, 0, GRID_W - NA_KW)
    col_ok = (kc >= ws) & (kc < ws + NA_KW)
    dc = np.clip(kc - c + NA_KW - 1, 0, 2 * NA_KW - 2)
    pick = (dc[None] == np.arange(2 * NA_KW - 1)[:, None, None]).astype(np.float32)
    full = jnp.einsum("hrd,dck->hrck", rpb.astype(F32), jnp.asarray(pick), precision=lax.Precision.HIGHEST)
    full = jnp.where(jnp.asarray(col_ok)[None, None], full, NEG_BIG)
    rows = [jnp.transpose(full[:, e:e + NA_KH], (0, 2, 1, 3)).reshape(NA_HEADS, GRID_W, NA_WIN)
            for e in range(NA_KH)]
    return jnp.stack(rows, axis=1)


def _lane_shift(x, d):
    return pltpu.roll(x, (LANES - d) % LANES, axis=1)


def _epilogue(h1, mod_ref, g2_ref, rw_ref, rb_ref, tril_ref, run_ref, xrow_ref, meta_ref, cnt_ref):
    xm2 = _rms_mod(h1, g2_ref[...], mod_ref[0, 3:4, :], mod_ref[0, 4:5, :])
    xrow_ref[:, 0:D] = xm2
    xh = xm2.astype(BF16)
    xl = (xm2 - xh.astype(F32)).astype(BF16)
    w = rw_ref[...]
    wh = w.astype(BF16)
    wl = (w - wh.astype(F32)).astype(BF16)
    logits = _dot(xh, wh) + (_dot(xl, wh) + _dot(xh, wl))
    scores = jax.nn.sigmoid(logits)
    biased = scores + rb_ref[...]
    lane = lax.broadcasted_iota(jnp.int32, (TM, LANES), 1)
    k = lane & (EPG - 1)
    valid = lane < N_EXPERTS
    offs = (-3, -2, -1, 1, 2, 3)
    in_grp = {d: (k + d >= 0) & (k + d < EPG) for d in offs}
    rank = jnp.zeros((TM, LANES), F32)
    for d in offs:
        v = _lane_shift(biased, d)
        beats = (v > biased) | (v == biased) if d < 0 else (v > biased)
        rank = rank + jnp.where(in_grp[d] & beats, 1.0, 0.0)
    top2 = rank < 2.0
    t = jnp.where(top2, biased, 0.0)
    gs = t
    for d in offs:
        gs = gs + jnp.where(in_grp[d], _lane_shift(t, d), 0.0)
    beaten = jnp.zeros((TM, LANES), jnp.bool_)
    for d in (-12, -8, -4, 4, 8, 12):
        v = _lane_shift(gs, d)
        ok = (lane + d >= 0) & (lane + d < N_EXPERTS)
        beats = (v > gs) | (v == gs) if d < 0 else (v > gs)
        beaten = beaten | (ok & beats)
    sel = valid & top2 & jnp.logical_not(beaten)
    wsel = jnp.where(sel, scores, 0.0)
    gate = wsel / jnp.sum(wsel, axis=-1, keepdims=True)
    self_f = jnp.where(sel, 1.0, 0.0)
    before = jnp.zeros((TM, LANES), F32)
    for d in (-3, -2, -1):
        before = before + jnp.where(in_grp[d], _lane_shift(self_f, d), 0.0)
    is_lo = sel & (before == 0.0)
    w_lo = jnp.sum(jnp.where(is_lo, gate, 0.0), axis=-1, keepdims=True)
    w_hi = jnp.sum(jnp.where(sel & jnp.logical_not(is_lo), gate, 0.0), axis=-1, keepdims=True)
    code = ((1 << k) + 8 * (lane >> 2)).astype(F32)
    bid = jnp.sum(jnp.where(sel, code, 0.0), axis=-1, keepdims=True)
    onehot = lane.astype(F32) == bid
    cum = _dot(tril_ref[...], jnp.where(onehot, 1.0, 0.0).astype(BF16))
    run = run_ref[0:1, :]
    rank_g = jnp.sum(jnp.where(onehot, cum + run, 0.0), axis=-1, keepdims=True)
    cnt = jnp.sum(jnp.where(onehot, 1.0, 0.0), axis=0, keepdims=True)
    new_run = jnp.broadcast_to(run + cnt, (8, LANES))
    run_ref[...] = new_run
    cnt_ref[...] = new_run
    meta = jnp.where(lane == 0, w_lo, jnp.where(lane == 1, w_hi, jnp.where(lane == 2, bid, jnp.where(lane == 3, rank_g, 0.0))))
    xrow_ref[:, D:XROW_W] = meta
    meta_ref[...] = meta


def _epilogue_specs(ntiles):
    in_specs = [
        _resident((1, D)),
        _resident((D, LANES)),
        _resident((1, LANES)),
        _resident((TM, TM)),
    ]
    out_specs = [
        pl.BlockSpec((TM, D), lambda i: (i, 0)),
        pl.BlockSpec((TM, XROW_W), lambda i: (i, 0)),
        pl.BlockSpec((TM, LANES), lambda i: (i, 0)),
        pl.BlockSpec((8, LANES), lambda i: (0, 0)),
    ]
    n = ntiles * TM
    out_shape = [
        jax.ShapeDtypeStruct((n, D), F32),
        jax.ShapeDtypeStruct((n, XROW_W), F32),
        jax.ShapeDtypeStruct((n, LANES), F32),
        jax.ShapeDtypeStruct((8, LANES), F32),
    ]
    return in_specs, out_specs, out_shape


def _outproj0_kernel(ao_ref, h_ref, mod_ref, w_ref, g2_ref, rw_ref, rb_ref, tril_ref,
                     h1_ref, xrow_ref, meta_ref, cnt_ref, run_ref):
    @pl.when(pl.program_id(0) == 0)
    def _():
        run_ref[...] = jnp.zeros_like(run_ref)

    y = _dot(ao_ref[...], w_ref[...])
    h1 = h_ref[...] + mod_ref[0, 2:3, :] * y
    h1_ref[...] = h1
    _epilogue(h1, mod_ref, g2_ref, rw_ref, rb_ref, tril_ref, run_ref, xrow_ref, meta_ref, cnt_ref)


def _outproj0(ao, h, modv, w_out, g2, rw, rb, tril):
    e_in, e_out, e_shape = _epilogue_specs(NT)
    return pl.pallas_call(
        _outproj0_kernel,
        grid=(NT,),
        in_specs=[
            pl.BlockSpec((TM, D), lambda i: (i, 0)),
            pl.BlockSpec((TM, D), lambda i: (i, 0)),
            _mod_spec(),
            _resident((D, D)),
        ] + e_in,
        out_specs=e_out,
        out_shape=e_shape,
        scratch_shapes=[pltpu.VMEM((8, LANES), F32)],
        compiler_params=_params(("arbitrary",)),
        name="outproj0",
    )(ao, h, modv, w_out, g2, rw, rb, tril)


def _bucket_tables():
    lo = np.zeros(N_BUCKET_IDS, np.int32)
    hi = np.zeros(N_BUCKET_IDS, np.int32)
    for g in range(N_EXPERTS // EPG):
        for a in range(EPG):
            for b in range(a + 1, EPG):
                i = 16 * g + (1 << a) + (1 << b)
                lo[i] = EPG * g + a
                hi[i] = EPG * g + b
    return lo, hi


def _slot_kernel(meta_ref, starts_ref, o_ref):
    lane = lax.broadcasted_iota(jnp.int32, (TM, LANES), 1).astype(F32)
    first = jnp.sum(jnp.where(lane == meta_ref[:, 2:3], starts_ref[...], 0.0), axis=-1, keepdims=True)
    o_ref[...] = jnp.broadcast_to((first + meta_ref[:, 3:4]).astype(jnp.int32), (TM, LANES))


def _token_slots(meta, first_row):
    n = meta.shape[0]
    out = pl.pallas_call(
        _slot_kernel,
        grid=(n // TM,),
        in_specs=[pl.BlockSpec((TM, LANES), lambda i: (i, 0)), pl.BlockSpec((1, LANES), lambda i: (0, 0))],
        out_specs=pl.BlockSpec((TM, LANES), lambda i: (i, 0)),
        out_shape=jax.ShapeDtypeStruct((n, LANES), jnp.int32),
        compiler_params=_params(("arbitrary",)),
        name="token_slots",
    )(meta, first_row)
    return out[:, 0]


def _route_plan(meta, cnt, n_tiles, layer):
    counts = cnt[0, :N_BUCKET_IDS].astype(jnp.int32)
    nt_b = (counts + TMM - 1) // TMM
    ends = jnp.cumsum(nt_b)
    starts = ends - nt_b
    first_row = jnp.pad((starts * TMM).astype(F32), (0, LANES - N_BUCKET_IDS)).reshape(1, LANES)
    pos = _token_slots(meta, first_row)
    n_used = ends[-1]
    tile = jnp.minimum(jnp.arange(n_tiles, dtype=jnp.int32), n_used - 1)
    tile_b = jnp.sum((ends[None, :] <= tile[:, None]).astype(jnp.int32), axis=1)
    lo, hi = _bucket_tables()
    e_lo = jnp.asarray(lo)[tile_b] + layer * N_EXPERTS
    e_hi = jnp.asarray(hi)[tile_b] + layer * N_EXPERTS
    return pos.astype(jnp.int32), e_lo.astype(jnp.int32), e_hi.astype(jnp.int32), n_used.reshape(1).astype(jnp.int32)


def _dispatch_kernel(pos_ref, x_ref, xs_in_ref, xs_ref, sem):
    del xs_in_ref
    base = pl.program_id(0) * TM

    def issue(j, carry):
        for u in range(SUBLANES):
            p = pos_ref[base + j * SUBLANES + u]
            pltpu.make_async_copy(x_ref.at[j, pl.ds(u, 1), :],
                                  xs_ref.at[p >> 3, pl.ds(p & (SUBLANES - 1), 1), :], sem).start()
        return carry

    lax.fori_loop(0, TM // SUBLANES, issue, 0)
    pltpu.make_async_copy(x_ref, xs_ref.at[pl.ds(0, TM // SUBLANES)], sem).wait()


def _dispatch(pos, xrow, n_tiles):
    ntok = xrow.shape[0]
    xs0 = jnp.zeros((n_tiles * TMM // SUBLANES, SUBLANES, XROW_W), F32)
    xs = pl.pallas_call(
        _dispatch_kernel,
        grid_spec=pltpu.PrefetchScalarGridSpec(
            num_scalar_prefetch=1,
            grid=(ntok // TM,),
            in_specs=[
                pl.BlockSpec((TM // SUBLANES, SUBLANES, XROW_W), lambda i, pos: (i, 0, 0)),
                pl.BlockSpec(memory_space=pl.ANY),
            ],
            out_specs=pl.BlockSpec(memory_space=pl.ANY),
            scratch_shapes=[pltpu.SemaphoreType.DMA(())],
        ),
        out_shape=jax.ShapeDtypeStruct(xs0.shape, F32),
        input_output_aliases={2: 0},
        compiler_params=_params(("arbitrary",)),
        name="moe_dispatch",
    )(pos, _by_sublane_tile(xrow), xs0)
    return xs.reshape(n_tiles * TMM, XROW_W)


def _moe_kernel(elo_ref, ehi_ref, nu_ref, xs_ref, w1a, w3a, w2a, w1b, w3b, w2b, ys_ref):
    j = pl.program_id(0)

    @pl.when(j < nu_ref[0])
    def _():
        x = xs_ref[:, 0:D].astype(BF16)
        g_lo = xs_ref[:, D:D + 1]
        g_hi = xs_ref[:, D + 1:D + 2]

        def expert(w1, w3, w2):
            a = _dot(x, w1[0])
            hid = (a * jax.nn.sigmoid(a)) * _dot(x, w3[0])
            return _dot(hid.astype(BF16), w2[0])

        ys_ref[...] = g_lo * expert(w1a, w3a, w2a) + g_hi * expert(w1b, w3b, w2b)

    @pl.when(j >= nu_ref[0])
    def _():
        ys_ref[...] = jnp.zeros_like(ys_ref)


def _moe(e_lo, e_hi, n_used, xs, w1, w3, w2, n_tiles):
    def wspec(shape, which):
        if which == 0:
            return pl.BlockSpec(shape, lambda j, lo, hi, nu: (lo[j], 0, 0))
        return pl.BlockSpec(shape, lambda j, lo, hi, nu: (hi[j], 0, 0))

    s13 = (1, D, D_EXPERT)
    s2 = (1, D_EXPERT, D)
    return pl.pallas_call(
        _moe_kernel,
        grid_spec=pltpu.PrefetchScalarGridSpec(
            num_scalar_prefetch=3,
            grid=(n_tiles,),
            in_specs=[
                pl.BlockSpec((TMM, XROW_W), lambda j, lo, hi, nu: (j, 0)),
                wspec(s13, 0), wspec(s13, 0), wspec(s2, 0),
                wspec(s13, 1), wspec(s13, 1), wspec(s2, 1),
            ],
            out_specs=pl.BlockSpec((TMM, D), lambda j, lo, hi, nu: (j, 0)),
        ),
        out_shape=jax.ShapeDtypeStruct((n_tiles * TMM, D), F32),
        compiler_params=_params(("arbitrary",)),
        name="moe_experts",
    )(e_lo, e_hi, n_used, xs, w1, w3, w2, w1, w3, w2)


def _gather_issue(pos_ref, ys_hbm, buf, sem, tile, slot):
    base = tile * TM

    def issue(j, carry):
        for u in range(SUBLANES):
            p = pos_ref[base + j * SUBLANES + u]
            pltpu.make_async_copy(ys_hbm.at[p >> 3, pl.ds(p & (SUBLANES - 1), 1), :],
                                  buf.at[slot, j, pl.ds(u, 1), :], sem.at[slot]).start()
        return carry

    lax.fori_loop(0, TM // SUBLANES, issue, 0)


def _gathered_rows(pos_ref, ys_hbm, buf, sem):
    i = pl.program_id(0)
    slot = i % 2

    @pl.when(i == 0)
    def _():
        _gather_issue(pos_ref, ys_hbm, buf, sem, 0, 0)

    @pl.when(i + 1 < pl.num_programs(0))
    def _():
        _gather_issue(pos_ref, ys_hbm, buf, sem, i + 1, 1 - slot)

    pltpu.make_async_copy(ys_hbm.at[pl.ds(0, TM // SUBLANES)], buf.at[slot], sem.at[slot]).wait()
    return buf[slot].reshape(TM, D)


_GATHER_SCRATCH = [pltpu.VMEM((2, TM // SUBLANES, SUBLANES, D), F32), pltpu.SemaphoreType.DMA((2,))]


def _by_sublane_tile(a):
    return a.reshape(a.shape[0] // SUBLANES, SUBLANES, a.shape[1])


def _rope_store(x, tabs, out_ref):
    cr, sr, cc, sc = tabs
    for ch in range(RET_QK // LANES):
        xc = x[:, ch * LANES:(ch + 1) * LANES]
        xr = pltpu.roll(xc, LANES // 2, axis=1)
        y = xc * cr + xr * sr if ch % 2 == 0 else xc * cc + xr * sc
        out_ref[:, ch * LANES:(ch + 1) * LANES] = y.astype(BF16)


def _inproj1_kernel(pos_ref, h_ref, ys_hbm, mod0_ref, mod1_ref, g_ref, w_ref, cr_ref, sr_ref, cc_ref, sc_ref,
                    h2_ref, q_ref, k_ref, v_ref, gg_ref, buf, sem):
    f = _gathered_rows(pos_ref, ys_hbm, buf, sem)
    h2 = h_ref[...] + mod0_ref[0, 5:6, :] * f
    h2_ref[...] = h2
    xm = _rms_mod(h2, g_ref[...], mod1_ref[0, 0:1, :], mod1_ref[0, 1:2, :]).astype(BF16)
    tabs = (cr_ref[...], sr_ref[...], cc_ref[...], sc_ref[...])
    _rope_store(_dot(xm, w_ref[:, 0:RET_QK]), tabs, q_ref)
    _rope_store(_dot(xm, w_ref[:, RET_QK:2 * RET_QK]) * (RET_DK ** -0.5), tabs, k_ref)
    v_ref[...] = _dot(xm, w_ref[:, 2 * RET_QK:2 * RET_QK + RET_V]).astype(BF16)
    gg_ref[...] = _dot(xm, w_ref[:, 2 * RET_QK + RET_V:]).astype(BF16)


def _inproj1(pos, h1, ys, modv0, modv1, g, w_in, tabs):
    tok = lambda w: pl.BlockSpec((TM, w), lambda i, pos: (i, 0))
    res = lambda shape: pl.BlockSpec(shape, lambda i, pos: (0,) * len(shape), pipeline_mode=pl.Buffered(1))
    mod = pl.BlockSpec((1, 8, D), lambda i, pos: (i // NT_LAT, 0, 0))
    return pl.pallas_call(
        _inproj1_kernel,
        grid_spec=pltpu.PrefetchScalarGridSpec(
            num_scalar_prefetch=1,
            grid=(NT,),
            in_specs=[tok(D), pl.BlockSpec(memory_space=pl.ANY), mod, mod, res((1, D)), res((D, 2 * RET_QK + 2 * RET_V)),
                      tok(LANES), tok(LANES), tok(LANES), tok(LANES)],
            out_specs=[tok(D), tok(RET_QK), tok(RET_QK), tok(RET_V), tok(RET_V)],
            scratch_shapes=_GATHER_SCRATCH,
        ),
        out_shape=[
            jax.ShapeDtypeStruct((T, D), F32),
            jax.ShapeDtypeStruct((T, RET_QK), BF16),
            jax.ShapeDtypeStruct((T, RET_QK), BF16),
            jax.ShapeDtypeStruct((T, RET_V), BF16),
            jax.ShapeDtypeStruct((T, RET_V), BF16),
        ],
        compiler_params=_params(("arbitrary",)),
        name="inproj1",
    )(pos, h1, _by_sublane_tile(ys), modv0, modv1, g, w_in, *tabs)


def _rope_tables():
    n = LANES // 2
    inv_freq = ROPE_BASE ** (-jnp.arange(n, dtype=F32) / n)
    t = jnp.arange(SEQ)

    def tab(p):
        ang = p.astype(F32)[:, None] * inv_freq[None, :]
        cos, sin = jnp.cos(ang), jnp.sin(ang)
        c = jnp.concatenate([cos, cos], axis=1)
        s = jnp.concatenate([-sin, sin], axis=1)
        c = jnp.concatenate([c, jnp.ones((CTX, LANES), F32)], axis=0)
        s = jnp.concatenate([s, jnp.zeros((CTX, LANES), F32)], axis=0)
        return c, s

    cr, sr = tab(t // GRID_W)
    cc, sc = tab(t % GRID_W)
    return cr, sr, cc, sc


RET_NCHUNK = T // RET_CHUNK
RET_CTX_CHUNKS = CTX // RET_CHUNK
RET_LAT_CHUNKS = SEQ // RET_CHUNK
RET_NHD = 2 * RET_HEADS


def _ret_kernel(dec_ref, qf_ref, kf_ref, vf_ref, qb_ref, kb_ref, vb_ref, of_ref, ob_ref,
                state, dm, qd, kd, cd):
    C = RET_CHUNK

    @pl.when(pl.program_id(0) == 0)
    def _():
        state[...] = jnp.zeros_like(state)
        n = lax.broadcasted_iota(jnp.int32, (C, C), 0).astype(F32)
        m = lax.broadcasted_iota(jnp.int32, (C, C), 1).astype(F32)
        for d in range(2):
            for h in range(RET_HEADS):
                idx = d * RET_HEADS + h
                lg = jnp.log1p(-jnp.exp(jnp.full((C, C), dec_ref[d, h], F32)))
                diff = n - m if d == 0 else m - n
                low = diff >= 0.0
                dm[idx] = jnp.where(low, jnp.exp(jnp.where(low, diff, 0.0) * lg), 0.0)
                qpow = n + 1.0 if d == 0 else C - n
                kpow = (C - 1.0) - n if d == 0 else n
                qcol = jnp.exp(qpow * lg)
                kcol = jnp.exp(kpow * lg)
                qd[idx] = jnp.concatenate([qcol] * (RET_DV // C), axis=1)
                kd[idx] = jnp.concatenate([kcol] * (RET_DK // C), axis=1)
                cd[idx] = jnp.exp(C * lg[0:8, :])

    for d, (q_ref, k_ref, v_ref, o_ref) in enumerate(((qf_ref, kf_ref, vf_ref, of_ref),
                                                       (qb_ref, kb_ref, vb_ref, ob_ref))):
        for h in range(RET_HEADS):
            idx = d * RET_HEADS + h
            q = q_ref[:, h * RET_DK:(h + 1) * RET_DK]
            k = k_ref[:, h * RET_DK:(h + 1) * RET_DK]
            v = v_ref[:, h * RET_DV:(h + 1) * RET_DV]
            st = state[idx]
            s = _dot_nt(q, k) * dm[idx]
            o = _dot(s.astype(BF16), v) + qd[idx] * _dot(q, st.astype(BF16))
            o_ref[:, h * RET_DV:(h + 1) * RET_DV] = o.astype(BF16)
            kdk = (k.astype(F32) * kd[idx]).astype(BF16)
            cdv = jnp.concatenate([cd[idx, 0:1, :]] * (RET_DV // C), axis=1)
            state[idx] = st * cdv + _dot_tn(kdk, v)


def _ret_scan(decay, q, k, v):
    def fwd(s):
        return jnp.where(s < RET_CTX_CHUNKS, RET_LAT_CHUNKS + s, s - RET_CTX_CHUNKS)

    def bwd(s):
        return RET_NCHUNK - 1 - s

    def spec(w, f):
        return pl.BlockSpec((RET_CHUNK, w), lambda s: (f(s), 0))

    return pl.pallas_call(
        _ret_kernel,
        grid=(RET_NCHUNK,),
        in_specs=[
            pl.BlockSpec(memory_space=pltpu.SMEM),
            spec(RET_QK, fwd), spec(RET_QK, fwd), spec(RET_V, fwd),
            spec(RET_QK, bwd), spec(RET_QK, bwd), spec(RET_V, bwd),
        ],
        out_specs=[spec(RET_V, fwd), spec(RET_V, bwd)],
        out_shape=[jax.ShapeDtypeStruct((T, RET_V), BF16), jax.ShapeDtypeStruct((T, RET_V), BF16)],
        scratch_shapes=[
            pltpu.VMEM((RET_NHD, RET_DK, RET_DV), F32),
            pltpu.VMEM((RET_NHD, RET_CHUNK, RET_CHUNK), F32),
            pltpu.VMEM((RET_NHD, RET_CHUNK, RET_DV), F32),
            pltpu.VMEM((RET_NHD, RET_CHUNK, RET_DK), F32),
            pltpu.VMEM((RET_NHD, 8, RET_CHUNK), F32),
        ],
        compiler_params=_params(("arbitrary",)),
        name="retention_scan",
    )(decay, q, k, v, q, k, v)


def _outproj1_kernel(of_ref, ob_ref, gg_ref, h_ref, mod_ref, w_ref, g2_ref, rw_ref, rb_ref, tril_ref,
                     h1_ref, xrow_ref, meta_ref, cnt_ref, run_ref):
    @pl.when(pl.program_id(0) == 0)
    def _():
        run_ref[...] = jnp.zeros_like(run_ref)

    y = jnp.zeros((TM, D), F32)
    for hh in range(RET_HEADS):
        cols = slice(hh * RET_DV, (hh + 1) * RET_DV)
        o = of_ref[:, cols].astype(F32) + ob_ref[:, cols].astype(F32)
        on = o * lax.rsqrt(jnp.mean(o * o, axis=-1, keepdims=True) + RMS_EPS)
        g = gg_ref[:, cols].astype(F32)
        z = (g * jax.nn.sigmoid(g)) * on
        y = y + _dot(z.astype(BF16), w_ref[cols, :])
    h1 = h_ref[...] + mod_ref[0, 2:3, :] * y
    h1_ref[...] = h1
    _epilogue(h1, mod_ref, g2_ref, rw_ref, rb_ref, tril_ref, run_ref, xrow_ref, meta_ref, cnt_ref)


def _outproj1(o_f, o_b, gg, h, modv, w_out, g2, rw, rb, tril):
    e_in, e_out, e_shape = _epilogue_specs(NT_LAT)
    tok = lambda w: pl.BlockSpec((TM, w), lambda i: (i, 0))
    return pl.pallas_call(
        _outproj1_kernel,
        grid=(NT_LAT,),
        in_specs=[tok(RET_V), tok(RET_V), tok(RET_V), tok(D), _mod_spec(), _resident((RET_V, D))] + e_in,
        out_specs=e_out,
        out_shape=e_shape,
        scratch_shapes=[pltpu.VMEM((8, LANES), F32)],
        compiler_params=_params(("arbitrary",)),
        name="outproj1",
    )(o_f, o_b, gg, h, modv, w_out, g2, rw, rb, tril)


def _final_kernel(pos_ref, h_ref, ys_hbm, mod_ref, g_ref, o_ref, buf, sem):
    f = _gathered_rows(pos_ref, ys_hbm, buf, sem)
    h = h_ref[...] + mod_ref[0, 5:6, :] * f
    o_ref[...] = h * lax.rsqrt(jnp.mean(h * h, axis=-1, keepdims=True) + RMS_EPS) * g_ref[...]


def _final(pos, h, ys, modv, g):
    tok = pl.BlockSpec((TM, D), lambda i, pos: (i, 0))
    return pl.pallas_call(
        _final_kernel,
        grid_spec=pltpu.PrefetchScalarGridSpec(
            num_scalar_prefetch=1,
            grid=(NT_LAT,),
            in_specs=[tok, pl.BlockSpec(memory_space=pl.ANY),
                      pl.BlockSpec((1, 8, D), lambda i, pos: (0, 0, 0)),
                      pl.BlockSpec((1, D), lambda i, pos: (0, 0))],
            out_specs=tok,
            scratch_shapes=_GATHER_SCRATCH,
        ),
        out_shape=jax.ShapeDtypeStruct((SEQ, D), F32),
        compiler_params=_params(("arbitrary",)),
        name="final_norm",
    )(pos, h, _by_sublane_tile(ys), modv, g)


def _dft_constants():
    c = np.arange(FNET_GD)
    ang = 2.0 * np.pi * np.outer(c, c) / FNET_GD
    d64 = np.zeros((FNET_W, 2 * FNET_W))
    for g in range(FNET_W // FNET_GD):
        sl = slice(g * FNET_GD, (g + 1) * FNET_GD)
        d64[sl, sl] = np.cos(ang)
        d64[sl, FNET_W + g * FNET_GD:FNET_W + (g + 1) * FNET_GD] = -np.sin(ang)
    k2 = np.arange(FFT_N2)[:, None]
    n2 = np.arange(FFT_N2)[None, :]
    ga = np.zeros((FFT_N1, 2 * FFT_N2, 2 * FFT_N2))
    for n1 in range(FFT_N1):
        th = 2.0 * np.pi * ((k2 * (n1 + FFT_N1 * n2)) % SEQ) / SEQ
        cs, sn = np.cos(th), np.sin(th)
        ga[n1] = np.block([[cs, sn], [-sn, cs]])
    k1 = np.arange(FFT_N1)
    ph = 2.0 * np.pi * np.outer(k1, k1) / FFT_N1
    csb = np.concatenate([np.cos(ph), np.sin(ph)], axis=1)
    p = np.arange(CTX)
    pc = 2.0 * np.pi * (np.outer(p, p) % CTX) / CTX
    csc = np.concatenate([np.cos(pc), np.sin(pc)], axis=1)
    as_bf16 = lambda a: jnp.asarray(a, dtype=F32).astype(BF16)
    return as_bf16(d64), as_bf16(ga), as_bf16(csb), as_bf16(csc)


def _mod_table(mod_l):
    t = mod_l.reshape(2, 6, D)
    return jnp.concatenate([t, jnp.zeros((2, 2, D), F32)], axis=1)


def kernel(x, c, ctx, c_ctx, ada_w, ada_b, norm_g, final_norm_g, mixab_w_in, mixab_w_out, na_rpb,
           ret_w_in, ret_w_out, ret_decay, router_w, router_b, moe_w1, moe_w3, moe_w2):
    d64, ga, csb, csc = _dft_constants()
    tril = jnp.asarray(np.tril(np.ones((TM, TM)), -1), dtype=BF16)
    rw = jnp.pad(router_w, ((0, 0), (0, LANES - N_EXPERTS)))
    rb = jnp.pad(router_b, (0, LANES - N_EXPERTS)).reshape(1, LANES)
    w1 = moe_w1.reshape(DEPTH * N_EXPERTS, D, D_EXPERT).astype(BF16)
    w3 = moe_w3.reshape(DEPTH * N_EXPERTS, D, D_EXPERT).astype(BF16)
    w2 = moe_w2.reshape(DEPTH * N_EXPERTS, D_EXPERT, D).astype(BF16)

    mod = _mod_vectors(c, c_ctx, ada_w, ada_b)
    modv0, modv1 = _mod_table(mod[0]), _mod_table(mod[1])
    h0 = jnp.concatenate([x[0], ctx[0]], axis=0)

    qkv, br, bi = _inproj0(h0, modv0, norm_g[0, 0].reshape(1, D), mixab_w_in[0].astype(BF16), d64)
    a_lat = _fft_latent(br, bi, ga, csb)
    ao_ctx = _ctx_mixer(qkv, br, bi, csc)
    o_lat = _na_attention(qkv, _na_bias_rows(na_rpb[0]))
    ao = jnp.concatenate([jnp.concatenate([a_lat.astype(BF16), o_lat], axis=1), ao_ctx], axis=0)
    h1, xrow, meta, cnt = _outproj0(ao, h0, modv0, mixab_w_out[0].astype(BF16), norm_g[0, 1].reshape(1, D),
                                    rw, rb, tril)
    n_tiles0 = NT + N_REAL_BUCKETS
    pos0, e_lo, e_hi, n_used = _route_plan(meta, cnt, n_tiles0, 0)
    xs = _dispatch(pos0, xrow, n_tiles0)
    ys0 = _moe(e_lo, e_hi, n_used, xs, w1, w3, w2, n_tiles0)

    h2, q, k, v, gg = _inproj1(pos0, h1, ys0, modv0, modv1, norm_g[1, 0].reshape(1, D),
                               ret_w_in[0].astype(BF16), _rope_tables())
    o_f, o_b = _ret_scan(ret_decay[0].astype(F32), q, k, v)
    h3, xrow1, meta1, cnt1 = _outproj1(o_f, o_b, gg, h2, modv1, ret_w_out[0].astype(BF16),
                                       norm_g[1, 1].reshape(1, D), rw, rb, tril)
    n_tiles1 = NT_LAT + N_REAL_BUCKETS
    pos1, e_lo1, e_hi1, n_used1 = _route_plan(meta1, cnt1, n_tiles1, 1)
    xs1 = _dispatch(pos1, xrow1, n_tiles1)
    ys1 = _moe(e_lo1, e_hi1, n_used1, xs1, w1, w3, w2, n_tiles1)
    out = _final(pos1, h3, ys1, modv1, final_norm_g.reshape(1, D))
    return out[None]
```

```python
import functools
import math

import numpy as np
import jax
import jax.numpy as jnp
from jax import lax
from jax.experimental import pallas as pl
from jax.experimental.pallas import tpu as pltpu

F32 = jnp.float32
BF16 = jnp.bfloat16

D = 1024
SEQ = 16384
CTX = 256
T = SEQ + CTX
DEPTH = 2
GRID_W = 64
ROWS = SEQ // GRID_W
RMS_EPS = 1e-6
ROPE_BASE = 10000.0

FNET_W = 256
FNET_GD = 64
NA_HEADS = 12
NA_HD = 64
NA_W = NA_HEADS * NA_HD
NA_KH = 8
NA_KW = 16
AB_IN = FNET_W + 3 * NA_W
QKV_W = 3 * NA_W

RET_HEADS = 4
RET_DK = 256
RET_DV = 512
RET_QK = RET_HEADS * RET_DK
RET_V = RET_HEADS * RET_DV
RET_CHUNK = 128

N_EXPERTS = 16
EPG = 4
D_EXPERT = 512

LANES = 128
TM = 256
NT = T // TM
NT_LAT = SEQ // TM
TMM = 256
N_BUCKET_IDS = 64
N_REAL_BUCKETS = 24
XROW_W = D + LANES
FFT_N1 = 128
FFT_N2 = 128
NEG_BIG = -1e30
SUBLANES = 8

VMEM_LIMIT = 56 * 1024 * 1024


def _params(sem, vmem=VMEM_LIMIT):
    return pltpu.CompilerParams(dimension_semantics=sem, vmem_limit_bytes=vmem)


def _dot(a, b):
    return jnp.dot(a, b, preferred_element_type=F32)


def _dot_nt(a, b):
    return lax.dot_general(a, b, (((1,), (1,)), ((), ())), preferred_element_type=F32)


def _dot_tn(a, b):
    return lax.dot_general(a, b, (((0,), (0,)), ((), ())), preferred_element_type=F32)


def _rms_mod(x, g, sh, sc):
    ms = jnp.mean(x * x, axis=-1, keepdims=True)
    y = x * lax.rsqrt(ms + RMS_EPS) * g
    return y * (1.0 + sc) + sh


def _resident(shape):
    nd = len(shape)
    return pl.BlockSpec(shape, lambda *_: (0,) * nd, pipeline_mode=pl.Buffered(1))


MOD_TN = 768


def _mod_kernel(cs_ref, w_ref, b_ref, o_ref):
    cs = cs_ref[...]
    s = cs * jax.nn.sigmoid(cs)
    w = w_ref[0]
    r0 = jnp.sum(s[:, 0:1] * w, axis=0, keepdims=True)
    r1 = jnp.sum(s[:, 1:2] * w, axis=0, keepdims=True)
    o_ref[0] = jnp.concatenate([r0, r1], axis=0) + b_ref[0]


def _mod_vectors(c, c_ctx, ada_w, ada_b):
    cs = jnp.stack([c[0], c_ctx], axis=1)
    n = 6 * D
    return pl.pallas_call(
        _mod_kernel,
        grid=(DEPTH, n // MOD_TN),
        in_specs=[
            pl.BlockSpec((D, 2), lambda l, j: (0, 0)),
            pl.BlockSpec((1, D, MOD_TN), lambda l, j: (l, 0, j)),
            pl.BlockSpec((1, 1, MOD_TN), lambda l, j: (l, 0, j)),
        ],
        out_specs=pl.BlockSpec((1, 2, MOD_TN), lambda l, j: (l, 0, j)),
        out_shape=jax.ShapeDtypeStruct((DEPTH, 2, n), F32),
        compiler_params=_params(("arbitrary", "arbitrary")),
        name="mod_vectors",
    )(cs, ada_w, ada_b.reshape(DEPTH, 1, n))


def _mod_spec():
    return pl.BlockSpec((1, 8, D), lambda i, *_: (i // NT_LAT, 0, 0))


def _stream_tile(x_ref, ctx_ref):
    return jnp.where(pl.program_id(0) >= NT_LAT, ctx_ref[...], x_ref[...])


def _stream_specs():
    return [pl.BlockSpec((TM, D), lambda i: (jnp.minimum(i, NT_LAT - 1), 0)),
            pl.BlockSpec((CTX, D), lambda i: (0, 0))]


def _inproj0_kernel(x_ref, ctx_ref, mod_ref, g_ref, w_ref, d64_ref, qkv_ref, br_ref, bi_ref):
    h = _stream_tile(x_ref, ctx_ref)
    xm = _rms_mod(h, g_ref[...], mod_ref[0, 0:1, :], mod_ref[0, 1:2, :]).astype(BF16)
    p = _dot(xm, w_ref[...])
    qkv_ref[...] = p[:, FNET_W:].astype(BF16)
    b = _dot(p[:, :FNET_W].astype(BF16), d64_ref[...])
    br_ref[...] = b[:, :FNET_W].astype(BF16)
    bi_ref[...] = b[:, FNET_W:].astype(BF16)


def _inproj0(x, ctx, modv, g, w_in, d64):
    return pl.pallas_call(
        _inproj0_kernel,
        grid=(NT,),
        in_specs=_stream_specs() + [
            _mod_spec(),
            _resident((1, D)),
            _resident((D, AB_IN)),
            _resident((FNET_W, 2 * FNET_W)),
        ],
        out_specs=[
            pl.BlockSpec((TM, QKV_W), lambda i: (i, 0)),
            pl.BlockSpec((TM, FNET_W), lambda i: (i, 0)),
            pl.BlockSpec((TM, FNET_W), lambda i: (i, 0)),
        ],
        out_shape=[
            jax.ShapeDtypeStruct((T, QKV_W), BF16),
            jax.ShapeDtypeStruct((T, FNET_W), BF16),
            jax.ShapeDtypeStruct((T, FNET_W), BF16),
        ],
        compiler_params=_params(("arbitrary",)),
        name="inproj0",
    )(x, ctx, modv, g, w_in, d64)


FFT_A_NB = 8
FFT_B_TN = 4096


def _fft_a_kernel(br_ref, bi_ref, ga_ref, v_ref):
    for u in range(FFT_A_NB):
        cols = slice(u * FNET_W, (u + 1) * FNET_W)
        z = _dot(ga_ref[u, :, 0:FFT_N2], br_ref[:, cols]) + _dot(ga_ref[u, :, FFT_N2:], bi_ref[:, cols])
        v_ref[0, u] = z[:FFT_N2].astype(BF16)
        v_ref[1, u] = z[FFT_N2:].astype(BF16)


def _fft_b_kernel(cs_ref, v_ref, o_ref):
    o_ref[...] = _dot(cs_ref[...], v_ref[...]) * (1.0 / math.sqrt(SEQ * FNET_GD))


def _fft_latent(br, bi, ga, csb):
    width = FFT_N1 * FNET_W
    br2 = br.reshape(T // FFT_N1, width)
    bi2 = bi.reshape(T // FFT_N1, width)
    v = pl.pallas_call(
        _fft_a_kernel,
        grid=(FFT_N1 // FFT_A_NB,),
        in_specs=[
            pl.BlockSpec((FFT_N2, FFT_A_NB * FNET_W), lambda j: (0, j)),
            pl.BlockSpec((FFT_N2, FFT_A_NB * FNET_W), lambda j: (0, j)),
            pl.BlockSpec((FFT_A_NB, 2 * FFT_N2, 2 * FFT_N2), lambda j: (j, 0, 0)),
        ],
        out_specs=pl.BlockSpec((2, FFT_A_NB, FFT_N2, FNET_W), lambda j: (0, j, 0, 0)),
        out_shape=jax.ShapeDtypeStruct((2, FFT_N1, FFT_N2, FNET_W), BF16),
        compiler_params=_params(("arbitrary",)),
        name="fft_stage_a",
    )(br2, bi2, ga)
    v2 = v.reshape(2 * FFT_N1, FFT_N2 * FNET_W)
    x = pl.pallas_call(
        _fft_b_kernel,
        grid=(FFT_N2 * FNET_W // FFT_B_TN,),
        in_specs=[
            _resident((FFT_N1, 2 * FFT_N1)),
            pl.BlockSpec((2 * FFT_N1, FFT_B_TN), lambda j: (0, j)),
        ],
        out_specs=pl.BlockSpec((FFT_N1, FFT_B_TN), lambda j: (0, j)),
        out_shape=jax.ShapeDtypeStruct((FFT_N1, FFT_N2 * FNET_W), F32),
        compiler_params=_params(("arbitrary",)),
        name="fft_stage_b",
    )(csb, v2)
    return x.reshape(SEQ, FNET_W)


def _head_pair_masks():
    lane = lax.broadcasted_iota(jnp.int32, (1, LANES), 1)
    return lane < NA_HD


def _ctx_kernel(qkv_ref, br_ref, bi_ref, cs_ref, o_ref):
    a = _dot(cs_ref[:, 0:CTX], br_ref[...]) + _dot(cs_ref[:, CTX:], bi_ref[...])
    o_ref[:, 0:FNET_W] = (a * (1.0 / math.sqrt(CTX * FNET_GD))).astype(BF16)
    m0 = _head_pair_masks()
    for hp in range(NA_HEADS // 2):
        q = qkv_ref[:, hp * LANES:(hp + 1) * LANES]
        k = qkv_ref[:, NA_W + hp * LANES:NA_W + (hp + 1) * LANES]
        v = qkv_ref[:, 2 * NA_W + hp * LANES:2 * NA_W + (hp + 1) * LANES]
        outs = []
        for a_ in range(2):
            qa = jnp.where(m0 if a_ == 0 else jnp.logical_not(m0), q, jnp.zeros_like(q))
            s = _dot_nt(qa, k) * (NA_HD ** -0.5)
            p = jnp.exp(s - jnp.max(s, axis=-1, keepdims=True))
            l = jnp.sum(p, axis=-1, keepdims=True)
            outs.append(_dot(p.astype(BF16), v) / l)
        o_ref[:, FNET_W + hp * LANES:FNET_W + (hp + 1) * LANES] = jnp.where(m0, outs[0], outs[1]).astype(BF16)


def _ctx_mixer(qkv, br, bi, csc):
    return pl.pallas_call(
        _ctx_kernel,
        grid=(1,),
        in_specs=[
            pl.BlockSpec((CTX, QKV_W), lambda i: (SEQ // CTX, 0)),
            pl.BlockSpec((CTX, FNET_W), lambda i: (SEQ // CTX, 0)),
            pl.BlockSpec((CTX, FNET_W), lambda i: (SEQ // CTX, 0)),
            pl.BlockSpec((CTX, 2 * CTX), lambda i: (0, 0)),
        ],
        out_specs=pl.BlockSpec((CTX, D), lambda i: (0, 0)),
        out_shape=jax.ShapeDtypeStruct((CTX, D), BF16),
        compiler_params=_params(("arbitrary",)),
        name="ctx_mixer",
    )(qkv, br, bi, csc)


NA_RB = 8
NA_WIN = NA_KH * GRID_W


NA_NK = NA_WIN + CTX
NA_SUB = 16


def _na_kernel(q_ref, k_ref, v_ref, kc_ref, vc_ref, bias_ref, o_ref, s_scr, p_scr, l_scr):
    b = pl.program_id(1)
    m0 = _head_pair_masks()
    nm0 = jnp.logical_not(m0)
    scale = NA_HD ** -0.5

    def window(i):
        r = b * NA_RB + i
        rs = jnp.clip(r - NA_KH // 2, 0, ROWS - NA_KH)
        return pl.multiple_of(rs * GRID_W, GRID_W), rs - r + (NA_KH - 1)

    def scores(i):
        start, _ = window(i)
        q = q_ref[i * GRID_W:(i + 1) * GRID_W, :] * scale
        qs = jnp.concatenate([jnp.where(m0, q, jnp.zeros_like(q)), jnp.where(nm0, q, jnp.zeros_like(q))], axis=0)
        s_scr[i, :, 0:NA_WIN] = _dot_nt(qs, k_ref[pl.ds(start, NA_WIN), :])
        s_scr[i, :, NA_WIN:NA_NK] = _dot_nt(qs, kc_ref[...])

    def softmax(i):
        _, e = window(i)
        for g in range(2 * GRID_W // NA_SUB):
            rows = slice(g * NA_SUB, (g + 1) * NA_SUB)
            a, c0 = divmod(g * NA_SUB, GRID_W)
            s1 = s_scr[i, rows, 0:NA_WIN] + bias_ref[a, e, c0:c0 + NA_SUB, :]
            s2 = s_scr[i, rows, NA_WIN:NA_NK]
            mx = jnp.maximum(jnp.max(s1, axis=-1, keepdims=True), jnp.max(s2, axis=-1, keepdims=True))
            p1 = jnp.exp(s1 - mx)
            p2 = jnp.exp(s2 - mx)
            l = jnp.sum(p1, axis=-1, keepdims=True) + jnp.sum(p2, axis=-1, keepdims=True)
            p_scr[i, rows, 0:NA_WIN] = p1.astype(BF16)
            p_scr[i, rows, NA_WIN:NA_NK] = p2.astype(BF16)
            l_scr[i, rows, :] = jnp.broadcast_to(l, (NA_SUB, LANES))

    def values(i):
        start, _ = window(i)
        o = _dot(p_scr[i, :, 0:NA_WIN], v_ref[pl.ds(start, NA_WIN), :]) + _dot(p_scr[i, :, NA_WIN:NA_NK], vc_ref[...])
        o = o / l_scr[i]
        o_ref[i * GRID_W:(i + 1) * GRID_W, :] = jnp.where(m0, o[0:GRID_W], o[GRID_W:]).astype(BF16)

    scores(0)
    for i in range(NA_RB):
        if i + 1 < NA_RB:
            scores(i + 1)
        softmax(i)
        values(i)


def _na_attention(qkv, bias):
    nq = NA_RB * GRID_W
    nhp = NA_HEADS // 2
    return pl.pallas_call(
        _na_kernel,
        grid=(nhp, ROWS // NA_RB),
        in_specs=[
            pl.BlockSpec((nq, LANES), lambda hp, b: (b, hp)),
            pl.BlockSpec((SEQ, LANES), lambda hp, b: (0, nhp + hp)),
            pl.BlockSpec((SEQ, LANES), lambda hp, b: (0, 2 * nhp + hp)),
            pl.BlockSpec((CTX, LANES), lambda hp, b: (SEQ // CTX, nhp + hp)),
            pl.BlockSpec((CTX, LANES), lambda hp, b: (SEQ // CTX, 2 * nhp + hp)),
            pl.BlockSpec((2, NA_KH, GRID_W, NA_WIN), lambda hp, b: (hp, 0, 0, 0)),
        ],
        out_specs=pl.BlockSpec((nq, LANES), lambda hp, b: (b, hp)),
        out_shape=jax.ShapeDtypeStruct((SEQ, NA_W), BF16),
        scratch_shapes=[
            pltpu.VMEM((NA_RB, 2 * GRID_W, NA_NK), F32),
            pltpu.VMEM((NA_RB, 2 * GRID_W, NA_NK), BF16),
            pltpu.VMEM((NA_RB, 2 * GRID_W, LANES), F32),
        ],
        compiler_params=_params(("arbitrary", "arbitrary")),
        name="na_attention",
    )(qkv, qkv, qkv, qkv, qkv, bias)


def _na_bias_rows(rpb):
    c = np.arange(GRID_W)[:, None]
    kc = np.arange(GRID_W)[None, :]
    ws = np.clip(c - NA_KW // 2, 0, GRID_W - NA_KW)
    col_ok = (kc >= ws) & (kc < ws + NA_KW)
    dc = np.clip(kc - c + NA_KW - 1, 0, 2 * NA_KW - 2)
    pick = (dc[None] == np.arange(2 * NA_KW - 1)[:, None, None]).astype(np.float32)
    full = jnp.einsum("hrd,dck->hrck", rpb.astype(F32), jnp.asarray(pick), precision=lax.Precision.HIGHEST)
    full = jnp.where(jnp.asarray(col_ok)[None, None], full, NEG_BIG)
    rows = [jnp.transpose(full[:, e:e + NA_KH], (0, 2, 1, 3)).reshape(NA_HEADS, GRID_W, NA_WIN)
            for e in range(NA_KH)]
    return jnp.stack(rows, axis=1)


def _lane_shift(x, d):
    return pltpu.roll(x, (LANES - d) % LANES, axis=1)


def _epilogue(h1, mod_ref, g2_ref, rw_ref, rb_ref, tril_ref, run_ref, xrow_ref, meta_ref, cnt_ref):
    xm2 = _rms_mod(h1, g2_ref[...], mod_ref[0, 3:4, :], mod_ref[0, 4:5, :])
    xrow_ref[:, 0:D] = xm2
    xh = xm2.astype(BF16)
    xl = (xm2 - xh.astype(F32)).astype(BF16)
    w = rw_ref[...]
    wh = w.astype(BF16)
    wl = (w - wh.astype(F32)).astype(BF16)
    logits = _dot(xh, wh) + (_dot(xl, wh) + _dot(xh, wl))
    scores = jax.nn.sigmoid(logits)
    biased = scores + rb_ref[...]
    lane = lax.broadcasted_iota(jnp.int32, (TM, LANES), 1)
    k = lane & (EPG - 1)
    valid = lane < N_EXPERTS
    offs = (-3, -2, -1, 1, 2, 3)
    in_grp = {d: (k + d >= 0) & (k + d < EPG) for d in offs}
    rank = jnp.zeros((TM, LANES), F32)
    for d in offs:
        v = _lane_shift(biased, d)
        beats = (v > biased) | (v == biased) if d < 0 else (v > biased)
        rank = rank + jnp.where(in_grp[d] & beats, 1.0, 0.0)
    top2 = rank < 2.0
    t = jnp.where(top2, biased, 0.0)
    gs = t
    for d in offs:
        gs = gs + jnp.where(in_grp[d], _lane_shift(t, d), 0.0)
    beaten = jnp.zeros((TM, LANES), jnp.bool_)
    for d in (-12, -8, -4, 4, 8, 12):
        v = _lane_shift(gs, d)
        ok = (lane + d >= 0) & (lane + d < N_EXPERTS)
        beats = (v > gs) | (v == gs) if d < 0 else (v > gs)
        beaten = beaten | (ok & beats)
    sel = valid & top2 & jnp.logical_not(beaten)
    wsel = jnp.where(sel, scores, 0.0)
    gate = wsel / jnp.sum(wsel, axis=-1, keepdims=True)
    self_f = jnp.where(sel, 1.0, 0.0)
    before = jnp.zeros((TM, LANES), F32)
    for d in (-3, -2, -1):
        before = before + jnp.where(in_grp[d], _lane_shift(self_f, d), 0.0)
    is_lo = sel & (before == 0.0)
    w_lo = jnp.sum(jnp.where(is_lo, gate, 0.0), axis=-1, keepdims=True)
    w_hi = jnp.sum(jnp.where(sel & jnp.logical_not(is_lo), gate, 0.0), axis=-1, keepdims=True)
    code = ((1 << k) + 8 * (lane >> 2)).astype(F32)
    bid = jnp.sum(jnp.where(sel, code, 0.0), axis=-1, keepdims=True)
    onehot = lane.astype(F32) == bid
    cum = _dot(tril_ref[...], jnp.where(onehot, 1.0, 0.0).astype(BF16))
    run = run_ref[0:1, :]
    rank_g = jnp.sum(jnp.where(onehot, cum + run, 0.0), axis=-1, keepdims=True)
    cnt = jnp.sum(jnp.where(onehot, 1.0, 0.0), axis=0, keepdims=True)
    new_run = jnp.broadcast_to(run + cnt, (8, LANES))
    run_ref[...] = new_run
    cnt_ref[...] = new_run
    meta = jnp.where(lane == 0, w_lo, jnp.where(lane == 1, w_hi, jnp.where(lane == 2, bid, jnp.where(lane == 3, rank_g, 0.0))))
    xrow_ref[:, D:XROW_W] = meta
    meta_ref[...] = meta


def _epilogue_specs(ntiles):
    in_specs = [
        _resident((1, D)),
        _resident((D, LANES)),
        _resident((1, LANES)),
        _resident((TM, TM)),
    ]
    out_specs = [
        pl.BlockSpec((TM, D), lambda i: (i, 0)),
        pl.BlockSpec((TM, XROW_W), lambda i: (i, 0)),
        pl.BlockSpec((TM, LANES), lambda i: (i, 0)),
        pl.BlockSpec((8, LANES), lambda i: (0, 0)),
    ]
    n = ntiles * TM
    out_shape = [
        jax.ShapeDtypeStruct((n, D), F32),
        jax.ShapeDtypeStruct((n, XROW_W), F32),
        jax.ShapeDtypeStruct((n, LANES), F32),
        jax.ShapeDtypeStruct((8, LANES), F32),
    ]
    return in_specs, out_specs, out_shape


def _outproj0_kernel(a_ref, o_ref, aoc_ref, x_ref, ctx_ref, mod_ref, w_ref, g2_ref, rw_ref, rb_ref, tril_ref,
                     h1_ref, xrow_ref, meta_ref, cnt_ref, run_ref):
    @pl.when(pl.program_id(0) == 0)
    def _():
        run_ref[...] = jnp.zeros_like(run_ref)

    is_ctx = pl.program_id(0) >= NT_LAT
    a = jnp.where(is_ctx, aoc_ref[:, 0:FNET_W], a_ref[...].astype(BF16))
    o = jnp.where(is_ctx, aoc_ref[:, FNET_W:], o_ref[...])
    y = _dot(a, w_ref[0:FNET_W, :]) + _dot(o, w_ref[FNET_W:, :])
    h1 = _stream_tile(x_ref, ctx_ref) + mod_ref[0, 2:3, :] * y
    h1_ref[...] = h1
    _epilogue(h1, mod_ref, g2_ref, rw_ref, rb_ref, tril_ref, run_ref, xrow_ref, meta_ref, cnt_ref)


def _outproj0(a_lat, o_lat, ao_ctx, x, ctx, modv, w_out, g2, rw, rb, tril):
    e_in, e_out, e_shape = _epilogue_specs(NT)
    lat = lambda w: pl.BlockSpec((TM, w), lambda i: (jnp.minimum(i, NT_LAT - 1), 0))
    return pl.pallas_call(
        _outproj0_kernel,
        grid=(NT,),
        in_specs=[lat(FNET_W), lat(NA_W), pl.BlockSpec((CTX, D), lambda i: (0, 0))] + _stream_specs() + [
            _mod_spec(),
            _resident((D, D)),
        ] + e_in,
        out_specs=e_out,
        out_shape=e_shape,
        scratch_shapes=[pltpu.VMEM((8, LANES), F32)],
        compiler_params=_params(("arbitrary",)),
        name="outproj0",
    )(a_lat, o_lat, ao_ctx, x, ctx, modv, w_out, g2, rw, rb, tril)


def _bucket_tables():
    lo = np.zeros(N_BUCKET_IDS, np.int32)
    hi = np.zeros(N_BUCKET_IDS, np.int32)
    for g in range(N_EXPERTS // EPG):
        for a in range(EPG):
            for b in range(a + 1, EPG):
                i = 16 * g + (1 << a) + (1 << b)
                lo[i] = EPG * g + a
                hi[i] = EPG * g + b
    return lo, hi


def _slot_kernel(meta_ref, starts_ref, o_ref):
    rows = meta_ref.shape[0]
    lane = lax.broadcasted_iota(jnp.int32, (rows, LANES), 1).astype(F32)
    first = jnp.sum(jnp.where(lane == meta_ref[:, 2:3], starts_ref[...], 0.0), axis=-1, keepdims=True)
    o_ref[...] = jnp.broadcast_to((first + meta_ref[:, 3:4]).astype(jnp.int32), (rows, LANES))


SLOT_STEPS = 4


def _token_slots(meta, first_row):
    n = meta.shape[0]
    rows = n // SLOT_STEPS
    out = pl.pallas_call(
        _slot_kernel,
        grid=(SLOT_STEPS,),
        in_specs=[pl.BlockSpec((rows, LANES), lambda i: (i, 0)), pl.BlockSpec((1, LANES), lambda i: (0, 0))],
        out_specs=pl.BlockSpec((rows, LANES), lambda i: (i, 0)),
        out_shape=jax.ShapeDtypeStruct((n, LANES), jnp.int32),
        compiler_params=_params(("arbitrary",)),
        name="token_slots",
    )(meta, first_row)
    return out[:, 0]


def _route_plan(meta, cnt, n_tiles, layer):
    counts = cnt[0, :N_BUCKET_IDS].astype(jnp.int32)
    nt_b = (counts + TMM - 1) // TMM
    ends = jnp.cumsum(nt_b)
    starts = ends - nt_b
    first_row = jnp.pad((starts * TMM).astype(F32), (0, LANES - N_BUCKET_IDS)).reshape(1, LANES)
    pos = _token_slots(meta, first_row)
    n_used = ends[-1]
    tile = jnp.minimum(jnp.arange(n_tiles, dtype=jnp.int32), n_used - 1)
    tile_b = jnp.sum((ends[None, :] <= tile[:, None]).astype(jnp.int32), axis=1)
    lo, hi = _bucket_tables()
    e_lo = jnp.asarray(lo)[tile_b] + layer * N_EXPERTS
    e_hi = jnp.asarray(hi)[tile_b] + layer * N_EXPERTS
    return pos.astype(jnp.int32), e_lo.astype(jnp.int32), e_hi.astype(jnp.int32), n_used.reshape(1).astype(jnp.int32)


def _dispatch_kernel(pos_ref, x_ref, xs_in_ref, xs_ref, sem):
    del xs_in_ref
    base = pl.program_id(0) * TM

    def issue(j, carry):
        for u in range(SUBLANES):
            p = pos_ref[base + j * SUBLANES + u]
            pltpu.make_async_copy(x_ref.at[j, pl.ds(u, 1), :],
                                  xs_ref.at[p >> 3, pl.ds(p & (SUBLANES - 1), 1), :], sem).start()
        return carry

    lax.fori_loop(0, TM // SUBLANES, issue, 0)
    pltpu.make_async_copy(x_ref, xs_ref.at[pl.ds(0, TM // SUBLANES)], sem).wait()


def _dispatch(pos, xrow, n_tiles):
    ntok = xrow.shape[0]
    xs0 = jnp.zeros((n_tiles * TMM // SUBLANES, SUBLANES, XROW_W), F32)
    xs = pl.pallas_call(
        _dispatch_kernel,
        grid_spec=pltpu.PrefetchScalarGridSpec(
            num_scalar_prefetch=1,
            grid=(ntok // TM,),
            in_specs=[
                pl.BlockSpec((TM // SUBLANES, SUBLANES, XROW_W), lambda i, pos: (i, 0, 0)),
                pl.BlockSpec(memory_space=pl.ANY),
            ],
            out_specs=pl.BlockSpec(memory_space=pl.ANY),
            scratch_shapes=[pltpu.SemaphoreType.DMA(())],
        ),
        out_shape=jax.ShapeDtypeStruct(xs0.shape, F32),
        input_output_aliases={2: 0},
        compiler_params=_params(("arbitrary",)),
        name="moe_dispatch",
    )(pos, _by_sublane_tile(xrow), xs0)
    return xs.reshape(n_tiles * TMM, XROW_W)


def _moe_kernel(elo_ref, ehi_ref, nu_ref, xs_ref, w1a, w3a, w2a, w1b, w3b, w2b, ys_ref):
    j = pl.program_id(0)

    @pl.when(j < nu_ref[0])
    def _():
        x = xs_ref[:, 0:D].astype(BF16)
        g_lo = xs_ref[:, D:D + 1]
        g_hi = xs_ref[:, D + 1:D + 2]

        def expert(w1, w3, w2):
            a = _dot(x, w1[0])
            hid = (a * jax.nn.sigmoid(a)) * _dot(x, w3[0])
            return _dot(hid.astype(BF16), w2[0])

        ys_ref[...] = g_lo * expert(w1a, w3a, w2a) + g_hi * expert(w1b, w3b, w2b)

    @pl.when(j >= nu_ref[0])
    def _():
        ys_ref[...] = jnp.zeros_like(ys_ref)


def _moe(e_lo, e_hi, n_used, xs, w1, w3, w2, n_tiles):
    def wspec(shape, which):
        if which == 0:
            return pl.BlockSpec(shape, lambda j, lo, hi, nu: (lo[j], 0, 0))
        return pl.BlockSpec(shape, lambda j, lo, hi, nu: (hi[j], 0, 0))

    s13 = (1, D, D_EXPERT)
    s2 = (1, D_EXPERT, D)
    return pl.pallas_call(
        _moe_kernel,
        grid_spec=pltpu.PrefetchScalarGridSpec(
            num_scalar_prefetch=3,
            grid=(n_tiles,),
            in_specs=[
                pl.BlockSpec((TMM, XROW_W), lambda j, lo, hi, nu: (j, 0)),
                wspec(s13, 0), wspec(s13, 0), wspec(s2, 0),
                wspec(s13, 1), wspec(s13, 1), wspec(s2, 1),
            ],
            out_specs=pl.BlockSpec((TMM, D), lambda j, lo, hi, nu: (j, 0)),
        ),
        out_shape=jax.ShapeDtypeStruct((n_tiles * TMM, D), F32),
        compiler_params=_params(("arbitrary",)),
        name="moe_experts",
    )(e_lo, e_hi, n_used, xs, w1, w3, w2, w1, w3, w2)


def _gather_issue(pos_ref, ys_hbm, buf, sem, tile, slot):
    base = tile * TM

    def issue(j, carry):
        for u in range(SUBLANES):
            p = pos_ref[base + j * SUBLANES + u]
            pltpu.make_async_copy(ys_hbm.at[p >> 3, pl.ds(p & (SUBLANES - 1), 1), :],
                                  buf.at[slot, j, pl.ds(u, 1), :], sem.at[slot]).start()
        return carry

    lax.fori_loop(0, TM // SUBLANES, issue, 0)


def _gathered_rows(pos_ref, ys_hbm, buf, sem):
    i = pl.program_id(0)
    slot = i % 2

    @pl.when(i == 0)
    def _():
        _gather_issue(pos_ref, ys_hbm, buf, sem, 0, 0)

    @pl.when(i + 1 < pl.num_programs(0))
    def _():
        _gather_issue(pos_ref, ys_hbm, buf, sem, i + 1, 1 - slot)

    pltpu.make_async_copy(ys_hbm.at[pl.ds(0, TM // SUBLANES)], buf.at[slot], sem.at[slot]).wait()
    return buf[slot].reshape(TM, D)


_GATHER_SCRATCH = [pltpu.VMEM((2, TM // SUBLANES, SUBLANES, D), F32), pltpu.SemaphoreType.DMA((2,))]


def _by_sublane_tile(a):
    return a.reshape(a.shape[0] // SUBLANES, SUBLANES, a.shape[1])


def _rope_store(x, tabs, out_ref):
    cr, sr, cc, sc = tabs
    for ch in range(RET_QK // LANES):
        xc = x[:, ch * LANES:(ch + 1) * LANES]
        xr = pltpu.roll(xc, LANES // 2, axis=1)
        y = xc * cr + xr * sr if ch % 2 == 0 else xc * cc + xr * sc
        out_ref[:, ch * LANES:(ch + 1) * LANES] = y.astype(BF16)


def _inproj1_kernel(pos_ref, h_ref, ys_hbm, mod0_ref, mod1_ref, g_ref, w_ref, rt_ref, ct_ref,
                    h2_ref, q_ref, k_ref, v_ref, gg_ref, buf, sem):
    f = _gathered_rows(pos_ref, ys_hbm, buf, sem)
    h2 = h_ref[...] + mod0_ref[0, 5:6, :] * f
    h2_ref[...] = h2
    xm = _rms_mod(h2, g_ref[...], mod1_ref[0, 0:1, :], mod1_ref[0, 1:2, :]).astype(BF16)
    off = (pl.program_id(0) % (SUBLANES // ROPE_TILE_ROWS)) * ROPE_TILE_ROWS

    def row_table(which):
        return jnp.concatenate([jnp.broadcast_to(rt_ref[which, pl.ds(off + j, 1), :], (GRID_W, LANES))
                                for j in range(ROPE_TILE_ROWS)], axis=0)

    tabs = (row_table(0), row_table(1), ct_ref[0, 0], ct_ref[0, 1])
    _rope_store(_dot(xm, w_ref[:, 0:RET_QK]), tabs, q_ref)
    _rope_store(_dot(xm, w_ref[:, RET_QK:2 * RET_QK]) * (RET_DK ** -0.5), tabs, k_ref)
    v_ref[...] = _dot(xm, w_ref[:, 2 * RET_QK:2 * RET_QK + RET_V]).astype(BF16)
    gg_ref[...] = _dot(xm, w_ref[:, 2 * RET_QK + RET_V:]).astype(BF16)


def _inproj1(pos, h1, ys, modv0, modv1, g, w_in, tabs):
    tok = lambda w: pl.BlockSpec((TM, w), lambda i, pos: (i, 0))
    res = lambda shape: pl.BlockSpec(shape, lambda i, pos: (0,) * len(shape), pipeline_mode=pl.Buffered(1))
    mod = pl.BlockSpec((1, 8, D), lambda i, pos: (i // NT_LAT, 0, 0))
    row_tab = pl.BlockSpec((2, SUBLANES, LANES), lambda i, pos: (0, i // (SUBLANES // ROPE_TILE_ROWS), 0))
    col_tab = pl.BlockSpec((1, 2, TM, LANES), lambda i, pos: (i // NT_LAT, 0, 0, 0))
    return pl.pallas_call(
        _inproj1_kernel,
        grid_spec=pltpu.PrefetchScalarGridSpec(
            num_scalar_prefetch=1,
            grid=(NT,),
            in_specs=[tok(D), pl.BlockSpec(memory_space=pl.ANY), mod, mod, res((1, D)), res((D, 2 * RET_QK + 2 * RET_V)),
                      row_tab, col_tab],
            out_specs=[tok(D), tok(RET_QK), tok(RET_QK), tok(RET_V), tok(RET_V)],
            scratch_shapes=_GATHER_SCRATCH,
        ),
        out_shape=[
            jax.ShapeDtypeStruct((T, D), F32),
            jax.ShapeDtypeStruct((T, RET_QK), BF16),
            jax.ShapeDtypeStruct((T, RET_QK), BF16),
            jax.ShapeDtypeStruct((T, RET_V), BF16),
            jax.ShapeDtypeStruct((T, RET_V), BF16),
        ],
        compiler_params=_params(("arbitrary",)),
        name="inproj1",
    )(pos, h1, _by_sublane_tile(ys), modv0, modv1, g, w_in, *tabs)


ROPE_TILE_ROWS = TM // GRID_W


def _rope_tables():
    n = LANES // 2
    inv_freq = ROPE_BASE ** (-np.arange(n, dtype=np.float64) / n)

    def lines(p):
        ang = p[:, None] * inv_freq[None, :]
        return np.stack([np.concatenate([np.cos(ang), np.cos(ang)], axis=1),
                         np.concatenate([-np.sin(ang), np.sin(ang)], axis=1)])

    ident = np.stack([np.ones((TM, LANES)), np.zeros((TM, LANES))])
    row_tab = np.concatenate([lines(np.arange(ROWS, dtype=np.float64)), ident[:, :SUBLANES]], axis=1)
    col_lat = lines(np.tile(np.arange(GRID_W, dtype=np.float64), ROPE_TILE_ROWS))
    col_tab = np.stack([col_lat, ident])
    return jnp.asarray(row_tab, dtype=F32), jnp.asarray(col_tab, dtype=F32)


RET_NCHUNK = T // RET_CHUNK
RET_CTX_CHUNKS = CTX // RET_CHUNK
RET_LAT_CHUNKS = SEQ // RET_CHUNK
RET_NHD = 2 * RET_HEADS


def _ret_kernel(dec_ref, qf_ref, kf_ref, vf_ref, qb_ref, kb_ref, vb_ref, of_ref, ob_ref,
                state, dm, qd, kd, cd):
    C = RET_CHUNK

    @pl.when(pl.program_id(0) == 0)
    def _():
        state[...] = jnp.zeros_like(state)
        n = lax.broadcasted_iota(jnp.int32, (C, C), 0).astype(F32)
        m = lax.broadcasted_iota(jnp.int32, (C, C), 1).astype(F32)
        for d in range(2):
            for h in range(RET_HEADS):
                idx = d * RET_HEADS + h
                lg = jnp.log1p(-jnp.exp(jnp.full((C, C), dec_ref[d, h], F32)))
                diff = n - m if d == 0 else m - n
                low = diff >= 0.0
                dm[idx] = jnp.where(low, jnp.exp(jnp.where(low, diff, 0.0) * lg), 0.0)
                qpow = n + 1.0 if d == 0 else C - n
                kpow = (C - 1.0) - n if d == 0 else n
                qcol = jnp.exp(qpow * lg)
                kcol = jnp.exp(kpow * lg)
                qd[idx] = jnp.concatenate([qcol] * (RET_DV // C), axis=1)
                kd[idx] = jnp.concatenate([kcol] * (RET_DK // C), axis=1)
                cd[idx] = jnp.exp(C * lg[0:8, :])

    for d, (q_ref, k_ref, v_ref, o_ref) in enumerate(((qf_ref, kf_ref, vf_ref, of_ref),
                                                       (qb_ref, kb_ref, vb_ref, ob_ref))):
        for h in range(RET_HEADS):
            idx = d * RET_HEADS + h
            q = q_ref[:, h * RET_DK:(h + 1) * RET_DK]
            k = k_ref[:, h * RET_DK:(h + 1) * RET_DK]
            v = v_ref[:, h * RET_DV:(h + 1) * RET_DV]
            st = state[idx]
            s = _dot_nt(q, k) * dm[idx]
            o = _dot(s.astype(BF16), v) + qd[idx] * _dot(q, st.astype(BF16))
            o_ref[:, h * RET_DV:(h + 1) * RET_DV] = o.astype(BF16)
            kdk = (k.astype(F32) * kd[idx]).astype(BF16)
            cdv = jnp.concatenate([cd[idx, 0:1, :]] * (RET_DV // C), axis=1)
            state[idx] = st * cdv + _dot_tn(kdk, v)


def _ret_scan(decay, q, k, v):
    def fwd(s):
        return jnp.where(s < RET_CTX_CHUNKS, RET_LAT_CHUNKS + s, s - RET_CTX_CHUNKS)

    def bwd(s):
        return RET_NCHUNK - 1 - s

    def spec(w, f):
        return pl.BlockSpec((RET_CHUNK, w), lambda s: (f(s), 0))

    return pl.pallas_call(
        _ret_kernel,
        grid=(RET_NCHUNK,),
        in_specs=[
            pl.BlockSpec(memory_space=pltpu.SMEM),
            spec(RET_QK, fwd), spec(RET_QK, fwd), spec(RET_V, fwd),
            spec(RET_QK, bwd), spec(RET_QK, bwd), spec(RET_V, bwd),
        ],
        out_specs=[spec(RET_V, fwd), spec(RET_V, bwd)],
        out_shape=[jax.ShapeDtypeStruct((T, RET_V), BF16), jax.ShapeDtypeStruct((T, RET_V), BF16)],
        scratch_shapes=[
            pltpu.VMEM((RET_NHD, RET_DK, RET_DV), F32),
            pltpu.VMEM((RET_NHD, RET_CHUNK, RET_CHUNK), F32),
            pltpu.VMEM((RET_NHD, RET_CHUNK, RET_DV), F32),
            pltpu.VMEM((RET_NHD, RET_CHUNK, RET_DK), F32),
            pltpu.VMEM((RET_NHD, 8, RET_CHUNK), F32),
        ],
        compiler_params=_params(("arbitrary",)),
        name="retention_scan",
    )(decay, q, k, v, q, k, v)


def _outproj1_kernel(of_ref, ob_ref, gg_ref, h_ref, mod_ref, w_ref, g2_ref, rw_ref, rb_ref, tril_ref,
                     h1_ref, xrow_ref, meta_ref, cnt_ref, run_ref):
    @pl.when(pl.program_id(0) == 0)
    def _():
        run_ref[...] = jnp.zeros_like(run_ref)

    y = jnp.zeros((TM, D), F32)
    for hh in range(RET_HEADS):
        cols = slice(hh * RET_DV, (hh + 1) * RET_DV)
        o = of_ref[:, cols].astype(F32) + ob_ref[:, cols].astype(F32)
        on = o * lax.rsqrt(jnp.mean(o * o, axis=-1, keepdims=True) + RMS_EPS)
        g = gg_ref[:, cols].astype(F32)
        z = (g * jax.nn.sigmoid(g)) * on
        y = y + _dot(z.astype(BF16), w_ref[cols, :])
    h1 = h_ref[...] + mod_ref[0, 2:3, :] * y
    h1_ref[...] = h1
    _epilogue(h1, mod_ref, g2_ref, rw_ref, rb_ref, tril_ref, run_ref, xrow_ref, meta_ref, cnt_ref)


def _outproj1(o_f, o_b, gg, h, modv, w_out, g2, rw, rb, tril):
    e_in, e_out, e_shape = _epilogue_specs(NT_LAT)
    tok = lambda w: pl.BlockSpec((TM, w), lambda i: (i, 0))
    return pl.pallas_call(
        _outproj1_kernel,
        grid=(NT_LAT,),
        in_specs=[tok(RET_V), tok(RET_V), tok(RET_V), tok(D), _mod_spec(), _resident((RET_V, D))] + e_in,
        out_specs=e_out,
        out_shape=e_shape,
        scratch_shapes=[pltpu.VMEM((8, LANES), F32)],
        compiler_params=_params(("arbitrary",)),
        name="outproj1",
    )(o_f, o_b, gg, h, modv, w_out, g2, rw, rb, tril)


def _final_kernel(pos_ref, h_ref, ys_hbm, mod_ref, g_ref, o_ref, buf, sem):
    f = _gathered_rows(pos_ref, ys_hbm, buf, sem)
    h = h_ref[...] + mod_ref[0, 5:6, :] * f
    o_ref[...] = h * lax.rsqrt(jnp.mean(h * h, axis=-1, keepdims=True) + RMS_EPS) * g_ref[...]


def _final(pos, h, ys, modv, g):
    tok = pl.BlockSpec((TM, D), lambda i, pos: (i, 0))
    return pl.pallas_call(
        _final_kernel,
        grid_spec=pltpu.PrefetchScalarGridSpec(
            num_scalar_prefetch=1,
            grid=(NT_LAT,),
            in_specs=[tok, pl.BlockSpec(memory_space=pl.ANY),
                      pl.BlockSpec((1, 8, D), lambda i, pos: (0, 0, 0)),
                      pl.BlockSpec((1, D), lambda i, pos: (0, 0))],
            out_specs=tok,
            scratch_shapes=_GATHER_SCRATCH,
        ),
        out_shape=jax.ShapeDtypeStruct((SEQ, D), F32),
        compiler_params=_params(("arbitrary",)),
        name="final_norm",
    )(pos, h, _by_sublane_tile(ys), modv, g)


def _dft_constants():
    c = np.arange(FNET_GD)
    ang = 2.0 * np.pi * np.outer(c, c) / FNET_GD
    d64 = np.zeros((FNET_W, 2 * FNET_W))
    for g in range(FNET_W // FNET_GD):
        sl = slice(g * FNET_GD, (g + 1) * FNET_GD)
        d64[sl, sl] = np.cos(ang)
        d64[sl, FNET_W + g * FNET_GD:FNET_W + (g + 1) * FNET_GD] = -np.sin(ang)
    k2 = np.arange(FFT_N2)[:, None]
    n2 = np.arange(FFT_N2)[None, :]
    ga = np.zeros((FFT_N1, 2 * FFT_N2, 2 * FFT_N2))
    for n1 in range(FFT_N1):
        th = 2.0 * np.pi * ((k2 * (n1 + FFT_N1 * n2)) % SEQ) / SEQ
        cs, sn = np.cos(th), np.sin(th)
        ga[n1] = np.block([[cs, sn], [-sn, cs]])
    k1 = np.arange(FFT_N1)
    ph = 2.0 * np.pi * np.outer(k1, k1) / FFT_N1
    csb = np.concatenate([np.cos(ph), np.sin(ph)], axis=1)
    p = np.arange(CTX)
    pc = 2.0 * np.pi * (np.outer(p, p) % CTX) / CTX
    csc = np.concatenate([np.cos(pc), np.sin(pc)], axis=1)
    as_bf16 = lambda a: jnp.asarray(a, dtype=F32).astype(BF16)
    return as_bf16(d64), as_bf16(ga), as_bf16(csb), as_bf16(csc)


def _mod_table(mod_l):
    t = mod_l.reshape(2, 6, D)
    return jnp.concatenate([t, jnp.zeros((2, 2, D), F32)], axis=1)


def kernel(x, c, ctx, c_ctx, ada_w, ada_b, norm_g, final_norm_g, mixab_w_in, mixab_w_out, na_rpb,
           ret_w_in, ret_w_out, ret_decay, router_w, router_b, moe_w1, moe_w3, moe_w2):
    d64, ga, csb, csc = _dft_constants()
    tril = jnp.asarray(np.tril(np.ones((TM, TM)), -1), dtype=BF16)
    rw = jnp.pad(router_w, ((0, 0), (0, LANES - N_EXPERTS)))
    rb = jnp.pad(router_b, (0, LANES - N_EXPERTS)).reshape(1, LANES)
    w1 = moe_w1.reshape(DEPTH * N_EXPERTS, D, D_EXPERT).astype(BF16)
    w3 = moe_w3.reshape(DEPTH * N_EXPERTS, D, D_EXPERT).astype(BF16)
    w2 = moe_w2.reshape(DEPTH * N_EXPERTS, D_EXPERT, D).astype(BF16)

    mod = _mod_vectors(c, c_ctx, ada_w, ada_b)
    modv0, modv1 = _mod_table(mod[0]), _mod_table(mod[1])

    qkv, br, bi = _inproj0(x[0], ctx[0], modv0, norm_g[0, 0].reshape(1, D), mixab_w_in[0].astype(BF16), d64)
    a_lat = _fft_latent(br, bi, ga, csb)
    ao_ctx = _ctx_mixer(qkv, br, bi, csc)
    o_lat = _na_attention(qkv, _na_bias_rows(na_rpb[0]))
    h1, xrow, meta, cnt = _outproj0(a_lat, o_lat, ao_ctx, x[0], ctx[0], modv0, mixab_w_out[0].astype(BF16),
                                    norm_g[0, 1].reshape(1, D), rw, rb, tril)
    n_tiles0 = NT + N_REAL_BUCKETS
    pos0, e_lo, e_hi, n_used = _route_plan(meta, cnt, n_tiles0, 0)
    xs = _dispatch(pos0, xrow, n_tiles0)
    ys0 = _moe(e_lo, e_hi, n_used, xs, w1, w3, w2, n_tiles0)

    h2, q, k, v, gg = _inproj1(pos0, h1, ys0, modv0, modv1, norm_g[1, 0].reshape(1, D),
                               ret_w_in[0].astype(BF16), _rope_tables())
    o_f, o_b = _ret_scan(ret_decay[0].astype(F32), q, k, v)
    h3, xrow1, meta1, cnt1 = _outproj1(o_f, o_b, gg, h2, modv1, ret_w_out[0].astype(BF16),
                                       norm_g[1, 1].reshape(1, D), rw, rb, tril)
    n_tiles1 = NT_LAT + N_REAL_BUCKETS
    pos1, e_lo1, e_hi1, n_used1 = _route_plan(meta1, cnt1, n_tiles1, 1)
    xs1 = _dispatch(pos1, xrow1, n_tiles1)
    ys1 = _moe(e_lo1, e_hi1, n_used1, xs1, w1, w3, w2, n_tiles1)
    out = _final(pos1, h3, ys1, modv1, final_norm_g.reshape(1, D))
    return out[None]
```

```python
import functools
import math

import numpy as np
import jax
import jax.numpy as jnp
from jax import lax
from jax.experimental import pallas as pl
from jax.experimental.pallas import tpu as pltpu

F32 = jnp.float32
BF16 = jnp.bfloat16

D = 1024
SEQ = 16384
CTX = 256
T = SEQ + CTX
DEPTH = 2
GRID_W = 64
ROWS = SEQ // GRID_W
RMS_EPS = 1e-6
ROPE_BASE = 10000.0

FNET_W = 256
FNET_GD = 64
NA_HEADS = 12
NA_HD = 64
NA_W = NA_HEADS * NA_HD
NA_KH = 8
NA_KW = 16
AB_IN = FNET_W + 3 * NA_W
QKV_W = 3 * NA_W

RET_HEADS = 4
RET_DK = 256
RET_DV = 512
RET_QK = RET_HEADS * RET_DK
RET_V = RET_HEADS * RET_DV
RET_CHUNK = 128

N_EXPERTS = 16
EPG = 4
D_EXPERT = 512

LANES = 128
TM = 256
NT = T // TM
NT_LAT = SEQ // TM
TMM = TM
N_BUCKET_IDS = 64
N_REAL_BUCKETS = 24
XROW_W = D + LANES
FFT_N1 = 128
FFT_N2 = 128
NEG_BIG = -1e30
SUBLANES = 8

VMEM_LIMIT = 56 * 1024 * 1024


def _params(sem, vmem=VMEM_LIMIT):
    return pltpu.CompilerParams(dimension_semantics=sem, vmem_limit_bytes=vmem)


def _dot(a, b):
    return jnp.dot(a, b, preferred_element_type=F32)


def _dot_nt(a, b):
    return lax.dot_general(a, b, (((1,), (1,)), ((), ())), preferred_element_type=F32)


def _dot_tn(a, b):
    return lax.dot_general(a, b, (((0,), (0,)), ((), ())), preferred_element_type=F32)


def _rms_mod(x, g, sh, sc):
    ms = jnp.mean(x * x, axis=-1, keepdims=True)
    y = x * lax.rsqrt(ms + RMS_EPS) * g
    return y * (1.0 + sc) + sh


def _resident(shape):
    nd = len(shape)
    return pl.BlockSpec(shape, lambda *_: (0,) * nd, pipeline_mode=pl.Buffered(1))


MOD_TN = 768


def _mod_kernel(cs_ref, w_ref, b_ref, o_ref):
    cs = cs_ref[...]
    s = cs * jax.nn.sigmoid(cs)
    w = w_ref[0]
    r0 = jnp.sum(s[:, 0:1] * w, axis=0, keepdims=True)
    r1 = jnp.sum(s[:, 1:2] * w, axis=0, keepdims=True)
    o_ref[0] = jnp.concatenate([r0, r1], axis=0) + b_ref[0]


def _mod_vectors(c, c_ctx, ada_w, ada_b):
    cs = jnp.stack([c[0], c_ctx], axis=1)
    n = 6 * D
    return pl.pallas_call(
        _mod_kernel,
        grid=(DEPTH, n // MOD_TN),
        in_specs=[
            pl.BlockSpec((D, 2), lambda l, j: (0, 0)),
            pl.BlockSpec((1, D, MOD_TN), lambda l, j: (l, 0, j)),
            pl.BlockSpec((1, 1, MOD_TN), lambda l, j: (l, 0, j)),
        ],
        out_specs=pl.BlockSpec((1, 2, MOD_TN), lambda l, j: (l, 0, j)),
        out_shape=jax.ShapeDtypeStruct((DEPTH, 2, n), F32),
        compiler_params=_params(("arbitrary", "arbitrary")),
        name="mod_vectors",
    )(cs, ada_w, ada_b.reshape(DEPTH, 1, n))


def _mod_spec():
    return pl.BlockSpec((1, 8, D), lambda i, *_: (i // NT_LAT, 0, 0))


def _stream_tile(x_ref, ctx_ref):
    return jnp.where(pl.program_id(0) >= NT_LAT, ctx_ref[...], x_ref[...])


def _stream_specs():
    return [pl.BlockSpec((TM, D), lambda i: (jnp.minimum(i, NT_LAT - 1), 0)),
            pl.BlockSpec((CTX, D), lambda i: (0, 0))]


def _inproj0_kernel(x_ref, ctx_ref, mod_ref, g_ref, w_ref, d64_ref, qkv_ref, br_ref, bi_ref):
    h = _stream_tile(x_ref, ctx_ref)
    xm = _rms_mod(h, g_ref[...], mod_ref[0, 0:1, :], mod_ref[0, 1:2, :]).astype(BF16)
    p = _dot(xm, w_ref[...])
    qkv_ref[...] = p[:, FNET_W:].astype(BF16)
    b = _dot(p[:, :FNET_W].astype(BF16), d64_ref[...])
    br_ref[...] = b[:, :FNET_W].astype(BF16)
    bi_ref[...] = b[:, FNET_W:].astype(BF16)


def _inproj0(x, ctx, modv, g, w_in, d64):
    return pl.pallas_call(
        _inproj0_kernel,
        grid=(NT,),
        in_specs=_stream_specs() + [
            _mod_spec(),
            _resident((1, D)),
            _resident((D, AB_IN)),
            _resident((FNET_W, 2 * FNET_W)),
        ],
        out_specs=[
            pl.BlockSpec((TM, QKV_W), lambda i: (i, 0)),
            pl.BlockSpec((TM, FNET_W), lambda i: (i, 0)),
            pl.BlockSpec((TM, FNET_W), lambda i: (i, 0)),
        ],
        out_shape=[
            jax.ShapeDtypeStruct((T, QKV_W), BF16),
            jax.ShapeDtypeStruct((T, FNET_W), BF16),
            jax.ShapeDtypeStruct((T, FNET_W), BF16),
        ],
        compiler_params=_params(("arbitrary",)),
        name="inproj0",
    )(x, ctx, modv, g, w_in, d64)


FFT_A_NB = 8
FFT_B_TN = 4096


def _fft_a_kernel(br_ref, bi_ref, ga_ref, v_ref):
    for u in range(FFT_A_NB):
        cols = slice(u * FNET_W, (u + 1) * FNET_W)
        z = _dot(ga_ref[u, :, 0:FFT_N2], br_ref[:, cols]) + _dot(ga_ref[u, :, FFT_N2:], bi_ref[:, cols])
        v_ref[0, u] = z[:FFT_N2].astype(BF16)
        v_ref[1, u] = z[FFT_N2:].astype(BF16)


def _fft_b_kernel(cs_ref, v_ref, o_ref):
    o_ref[...] = _dot(cs_ref[...], v_ref[...]) * (1.0 / math.sqrt(SEQ * FNET_GD))


def _fft_latent(br, bi, ga, csb):
    width = FFT_N1 * FNET_W
    br2 = br.reshape(T // FFT_N1, width)
    bi2 = bi.reshape(T // FFT_N1, width)
    v = pl.pallas_call(
        _fft_a_kernel,
        grid=(FFT_N1 // FFT_A_NB,),
        in_specs=[
            pl.BlockSpec((FFT_N2, FFT_A_NB * FNET_W), lambda j: (0, j)),
            pl.BlockSpec((FFT_N2, FFT_A_NB * FNET_W), lambda j: (0, j)),
            pl.BlockSpec((FFT_A_NB, 2 * FFT_N2, 2 * FFT_N2), lambda j: (j, 0, 0)),
        ],
        out_specs=pl.BlockSpec((2, FFT_A_NB, FFT_N2, FNET_W), lambda j: (0, j, 0, 0)),
        out_shape=jax.ShapeDtypeStruct((2, FFT_N1, FFT_N2, FNET_W), BF16),
        compiler_params=_params(("arbitrary",)),
        name="fft_stage_a",
    )(br2, bi2, ga)
    v2 = v.reshape(2 * FFT_N1, FFT_N2 * FNET_W)
    x = pl.pallas_call(
        _fft_b_kernel,
        grid=(FFT_N2 * FNET_W // FFT_B_TN,),
        in_specs=[
            _resident((FFT_N1, 2 * FFT_N1)),
            pl.BlockSpec((2 * FFT_N1, FFT_B_TN), lambda j: (0, j)),
        ],
        out_specs=pl.BlockSpec((FFT_N1, FFT_B_TN), lambda j: (0, j)),
        out_shape=jax.ShapeDtypeStruct((FFT_N1, FFT_N2 * FNET_W), F32),
        compiler_params=_params(("arbitrary",)),
        name="fft_stage_b",
    )(csb, v2)
    return x.reshape(SEQ, FNET_W)


def _head_pair_masks():
    lane = lax.broadcasted_iota(jnp.int32, (1, LANES), 1)
    return lane < NA_HD


def _ctx_kernel(qkv_ref, br_ref, bi_ref, cs_ref, o_ref):
    a = _dot(cs_ref[:, 0:CTX], br_ref[...]) + _dot(cs_ref[:, CTX:], bi_ref[...])
    o_ref[:, 0:FNET_W] = (a * (1.0 / math.sqrt(CTX * FNET_GD))).astype(BF16)
    m0 = _head_pair_masks()
    for hp in range(NA_HEADS // 2):
        q = qkv_ref[:, hp * LANES:(hp + 1) * LANES]
        k = qkv_ref[:, NA_W + hp * LANES:NA_W + (hp + 1) * LANES]
        v = qkv_ref[:, 2 * NA_W + hp * LANES:2 * NA_W + (hp + 1) * LANES]
        outs = []
        for a_ in range(2):
            qa = jnp.where(m0 if a_ == 0 else jnp.logical_not(m0), q, jnp.zeros_like(q))
            s = _dot_nt(qa, k) * (NA_HD ** -0.5)
            p = jnp.exp(s - jnp.max(s, axis=-1, keepdims=True))
            l = jnp.sum(p, axis=-1, keepdims=True)
            outs.append(_dot(p.astype(BF16), v) / l)
        o_ref[:, FNET_W + hp * LANES:FNET_W + (hp + 1) * LANES] = jnp.where(m0, outs[0], outs[1]).astype(BF16)


def _ctx_mixer(qkv, br, bi, csc):
    return pl.pallas_call(
        _ctx_kernel,
        grid=(1,),
        in_specs=[
            pl.BlockSpec((CTX, QKV_W), lambda i: (SEQ // CTX, 0)),
            pl.BlockSpec((CTX, FNET_W), lambda i: (SEQ // CTX, 0)),
            pl.BlockSpec((CTX, FNET_W), lambda i: (SEQ // CTX, 0)),
            pl.BlockSpec((CTX, 2 * CTX), lambda i: (0, 0)),
        ],
        out_specs=pl.BlockSpec((CTX, D), lambda i: (0, 0)),
        out_shape=jax.ShapeDtypeStruct((CTX, D), BF16),
        compiler_params=_params(("arbitrary",)),
        name="ctx_mixer",
    )(qkv, br, bi, csc)


NA_RB = 8
NA_WIN = NA_KH * GRID_W


NA_NK = NA_WIN + CTX
NA_SUB = 16


def _na_kernel(q_ref, k_ref, v_ref, kc_ref, vc_ref, bias_ref, o_ref, s_scr, p_scr, l_scr):
    b = pl.program_id(1)
    m0 = _head_pair_masks()
    nm0 = jnp.logical_not(m0)
    scale = NA_HD ** -0.5

    def window(i):
        r = b * NA_RB + i
        rs = jnp.clip(r - NA_KH // 2, 0, ROWS - NA_KH)
        return pl.multiple_of(rs * GRID_W, GRID_W), rs - r + (NA_KH - 1)

    def scores(i):
        start, _ = window(i)
        q = q_ref[i * GRID_W:(i + 1) * GRID_W, :] * scale
        qs = jnp.concatenate([jnp.where(m0, q, jnp.zeros_like(q)), jnp.where(nm0, q, jnp.zeros_like(q))], axis=0)
        s_scr[i, :, 0:NA_WIN] = _dot_nt(qs, k_ref[pl.ds(start, NA_WIN), :])
        s_scr[i, :, NA_WIN:NA_NK] = _dot_nt(qs, kc_ref[...])

    def softmax(i):
        _, e = window(i)
        for g in range(2 * GRID_W // NA_SUB):
            rows = slice(g * NA_SUB, (g + 1) * NA_SUB)
            a, c0 = divmod(g * NA_SUB, GRID_W)
            s1 = s_scr[i, rows, 0:NA_WIN] + bias_ref[a, e, c0:c0 + NA_SUB, :]
            s2 = s_scr[i, rows, NA_WIN:NA_NK]
            mx = jnp.maximum(jnp.max(s1, axis=-1, keepdims=True), jnp.max(s2, axis=-1, keepdims=True))
            p1 = jnp.exp(s1 - mx)
            p2 = jnp.exp(s2 - mx)
            l = jnp.sum(p1, axis=-1, keepdims=True) + jnp.sum(p2, axis=-1, keepdims=True)
            p_scr[i, rows, 0:NA_WIN] = p1.astype(BF16)
            p_scr[i, rows, NA_WIN:NA_NK] = p2.astype(BF16)
            l_scr[i, rows, :] = jnp.broadcast_to(l, (NA_SUB, LANES))

    def values(i):
        start, _ = window(i)
        o = _dot(p_scr[i, :, 0:NA_WIN], v_ref[pl.ds(start, NA_WIN), :]) + _dot(p_scr[i, :, NA_WIN:NA_NK], vc_ref[...])
        o = o / l_scr[i]
        o_ref[i * GRID_W:(i + 1) * GRID_W, :] = jnp.where(m0, o[0:GRID_W], o[GRID_W:]).astype(BF16)

    scores(0)
    for i in range(NA_RB):
        if i + 1 < NA_RB:
            scores(i + 1)
        softmax(i)
        values(i)


def _na_attention(qkv, bias):
    nq = NA_RB * GRID_W
    nhp = NA_HEADS // 2
    return pl.pallas_call(
        _na_kernel,
        grid=(nhp, ROWS // NA_RB),
        in_specs=[
            pl.BlockSpec((nq, LANES), lambda hp, b: (b, hp)),
            pl.BlockSpec((SEQ, LANES), lambda hp, b: (0, nhp + hp)),
            pl.BlockSpec((SEQ, LANES), lambda hp, b: (0, 2 * nhp + hp)),
            pl.BlockSpec((CTX, LANES), lambda hp, b: (SEQ // CTX, nhp + hp)),
            pl.BlockSpec((CTX, LANES), lambda hp, b: (SEQ // CTX, 2 * nhp + hp)),
            pl.BlockSpec((2, NA_KH, GRID_W, NA_WIN), lambda hp, b: (hp, 0, 0, 0)),
        ],
        out_specs=pl.BlockSpec((nq, LANES), lambda hp, b: (b, hp)),
        out_shape=jax.ShapeDtypeStruct((SEQ, NA_W), BF16),
        scratch_shapes=[
            pltpu.VMEM((NA_RB, 2 * GRID_W, NA_NK), F32),
            pltpu.VMEM((NA_RB, 2 * GRID_W, NA_NK), BF16),
            pltpu.VMEM((NA_RB, 2 * GRID_W, LANES), F32),
        ],
        compiler_params=_params(("arbitrary", "arbitrary")),
        name="na_attention",
    )(qkv, qkv, qkv, qkv, qkv, bias)


def _na_bias_rows(rpb):
    c = np.arange(GRID_W)[:, None]
    kc = np.arange(GRID_W)[None, :]
    ws = np.clip(c - NA_KW // 2, 0, GRID_W - NA_KW)
    col_ok = (kc >= ws) & (kc < ws + NA_KW)
    dc = np.clip(kc - c + NA_KW - 1, 0, 2 * NA_KW - 2)
    pick = (dc[None] == np.arange(2 * NA_KW - 1)[:, None, None]).astype(np.float32)
    full = jnp.einsum("hrd,dck->hrck", rpb.astype(F32), jnp.asarray(pick), precision=lax.Precision.HIGHEST)
    full = jnp.where(jnp.asarray(col_ok)[None, None], full, NEG_BIG)
    rows = [jnp.transpose(full[:, e:e + NA_KH], (0, 2, 1, 3)).reshape(NA_HEADS, GRID_W, NA_WIN)
            for e in range(NA_KH)]
    return jnp.stack(rows, axis=1)


def _lane_shift(x, d):
    return pltpu.roll(x, (LANES - d) % LANES, axis=1)


def _epilogue(h1, mod_ref, g2_ref, rw_ref, rb_ref, tril_ref, run_ref, xrow_ref, meta_ref, cnt_ref):
    xm2 = _rms_mod(h1, g2_ref[...], mod_ref[0, 3:4, :], mod_ref[0, 4:5, :])
    xrow_ref[:, 0:D] = xm2
    xh = xm2.astype(BF16)
    xl = (xm2 - xh.astype(F32)).astype(BF16)
    w = rw_ref[...]
    wh = w.astype(BF16)
    wl = (w - wh.astype(F32)).astype(BF16)
    logits = _dot(xh, wh) + (_dot(xl, wh) + _dot(xh, wl))
    scores = jax.nn.sigmoid(logits)
    biased = scores + rb_ref[...]
    lane = lax.broadcasted_iota(jnp.int32, (TM, LANES), 1)
    k = lane & (EPG - 1)
    valid = lane < N_EXPERTS
    offs = (-3, -2, -1, 1, 2, 3)
    in_grp = {d: (k + d >= 0) & (k + d < EPG) for d in offs}
    rank = jnp.zeros((TM, LANES), F32)
    for d in offs:
        v = _lane_shift(biased, d)
        beats = (v > biased) | (v == biased) if d < 0 else (v > biased)
        rank = rank + jnp.where(in_grp[d] & beats, 1.0, 0.0)
    top2 = rank < 2.0
    t = jnp.where(top2, biased, 0.0)
    gs = t
    for d in offs:
        gs = gs + jnp.where(in_grp[d], _lane_shift(t, d), 0.0)
    beaten = jnp.zeros((TM, LANES), jnp.bool_)
    for d in (-12, -8, -4, 4, 8, 12):
        v = _lane_shift(gs, d)
        ok = (lane + d >= 0) & (lane + d < N_EXPERTS)
        beats = (v > gs) | (v == gs) if d < 0 else (v > gs)
        beaten = beaten | (ok & beats)
    sel = valid & top2 & jnp.logical_not(beaten)
    wsel = jnp.where(sel, scores, 0.0)
    gate = wsel / jnp.sum(wsel, axis=-1, keepdims=True)
    self_f = jnp.where(sel, 1.0, 0.0)
    before = jnp.zeros((TM, LANES), F32)
    for d in (-3, -2, -1):
        before = before + jnp.where(in_grp[d], _lane_shift(self_f, d), 0.0)
    is_lo = sel & (before == 0.0)
    w_lo = jnp.sum(jnp.where(is_lo, gate, 0.0), axis=-1, keepdims=True)
    w_hi = jnp.sum(jnp.where(sel & jnp.logical_not(is_lo), gate, 0.0), axis=-1, keepdims=True)
    code = ((1 << k) + 8 * (lane >> 2)).astype(F32)
    bid = jnp.sum(jnp.where(sel, code, 0.0), axis=-1, keepdims=True)
    onehot = lane.astype(F32) == bid
    cum = _dot(tril_ref[...], jnp.where(onehot, 1.0, 0.0).astype(BF16))
    run = run_ref[0:1, :]
    rank_g = jnp.sum(jnp.where(onehot, cum + run, 0.0), axis=-1, keepdims=True)
    cnt = jnp.sum(jnp.where(onehot, 1.0, 0.0), axis=0, keepdims=True)
    new_run = jnp.broadcast_to(run + cnt, (8, LANES))
    run_ref[...] = new_run
    cnt_ref[...] = new_run
    meta = jnp.where(lane == 0, w_lo, jnp.where(lane == 1, w_hi, jnp.where(lane == 2, bid, jnp.where(lane == 3, rank_g, 0.0))))
    xrow_ref[:, D:XROW_W] = meta
    meta_ref[...] = meta


def _epilogue_specs(ntiles):
    in_specs = [
        _resident((1, D)),
        _resident((D, LANES)),
        _resident((1, LANES)),
        _resident((TM, TM)),
    ]
    out_specs = [
        pl.BlockSpec((TM, D), lambda i: (i, 0)),
        pl.BlockSpec((TM, XROW_W), lambda i: (i, 0)),
        pl.BlockSpec((TM, LANES), lambda i: (i, 0)),
        pl.BlockSpec((8, LANES), lambda i: (0, 0)),
    ]
    n = ntiles * TM
    out_shape = [
        jax.ShapeDtypeStruct((n, D), F32),
        jax.ShapeDtypeStruct((n, XROW_W), F32),
        jax.ShapeDtypeStruct((n, LANES), F32),
        jax.ShapeDtypeStruct((8, LANES), F32),
    ]
    return in_specs, out_specs, out_shape


def _outproj0_kernel(a_ref, o_ref, aoc_ref, x_ref, ctx_ref, mod_ref, w_ref, g2_ref, rw_ref, rb_ref, tril_ref,
                     h1_ref, xrow_ref, meta_ref, cnt_ref, run_ref):
    @pl.when(pl.program_id(0) == 0)
    def _():
        run_ref[...] = jnp.zeros_like(run_ref)

    is_ctx = pl.program_id(0) >= NT_LAT
    a = jnp.where(is_ctx, aoc_ref[:, 0:FNET_W], a_ref[...].astype(BF16))
    o = jnp.where(is_ctx, aoc_ref[:, FNET_W:], o_ref[...])
    y = _dot(a, w_ref[0:FNET_W, :]) + _dot(o, w_ref[FNET_W:, :])
    h1 = _stream_tile(x_ref, ctx_ref) + mod_ref[0, 2:3, :] * y
    h1_ref[...] = h1
    _epilogue(h1, mod_ref, g2_ref, rw_ref, rb_ref, tril_ref, run_ref, xrow_ref, meta_ref, cnt_ref)


def _outproj0(a_lat, o_lat, ao_ctx, x, ctx, modv, w_out, g2, rw, rb, tril):
    e_in, e_out, e_shape = _epilogue_specs(NT)
    lat = lambda w: pl.BlockSpec((TM, w), lambda i: (jnp.minimum(i, NT_LAT - 1), 0))
    return pl.pallas_call(
        _outproj0_kernel,
        grid=(NT,),
        in_specs=[lat(FNET_W), lat(NA_W), pl.BlockSpec((CTX, D), lambda i: (0, 0))] + _stream_specs() + [
            _mod_spec(),
            _resident((D, D)),
        ] + e_in,
        out_specs=e_out,
        out_shape=e_shape,
        scratch_shapes=[pltpu.VMEM((8, LANES), F32)],
        compiler_params=_params(("arbitrary",)),
        name="outproj0",
    )(a_lat, o_lat, ao_ctx, x, ctx, modv, w_out, g2, rw, rb, tril)


def _bucket_tables():
    lo = np.zeros(N_BUCKET_IDS, np.int32)
    hi = np.zeros(N_BUCKET_IDS, np.int32)
    for g in range(N_EXPERTS // EPG):
        for a in range(EPG):
            for b in range(a + 1, EPG):
                i = 16 * g + (1 << a) + (1 << b)
                lo[i] = EPG * g + a
                hi[i] = EPG * g + b
    return lo, hi


def _slot_kernel(meta_ref, starts_ref, o_ref):
    rows = meta_ref.shape[0]
    lane = lax.broadcasted_iota(jnp.int32, (rows, LANES), 1).astype(F32)
    first = jnp.sum(jnp.where(lane == meta_ref[:, 2:3], starts_ref[...], 0.0), axis=-1, keepdims=True)
    o_ref[...] = jnp.broadcast_to((first + meta_ref[:, 3:4]).astype(jnp.int32), (rows, LANES))


SLOT_STEPS = 4


def _token_slots(meta, first_row):
    n = meta.shape[0]
    rows = n // SLOT_STEPS
    out = pl.pallas_call(
        _slot_kernel,
        grid=(SLOT_STEPS,),
        in_specs=[pl.BlockSpec((rows, LANES), lambda i: (i, 0)), pl.BlockSpec((1, LANES), lambda i: (0, 0))],
        out_specs=pl.BlockSpec((rows, LANES), lambda i: (i, 0)),
        out_shape=jax.ShapeDtypeStruct((n, LANES), jnp.int32),
        compiler_params=_params(("arbitrary",)),
        name="token_slots",
    )(meta, first_row)
    return out[:, 0]


def _route_plan(meta, cnt, n_tiles, layer):
    counts = cnt[0, :N_BUCKET_IDS].astype(jnp.int32)
    nt_b = (counts + TMM - 1) // TMM
    ends = jnp.cumsum(nt_b)
    starts = ends - nt_b
    first_row = jnp.pad((starts * TMM).astype(F32), (0, LANES - N_BUCKET_IDS)).reshape(1, LANES)
    pos = _token_slots(meta, first_row)
    n_used = ends[-1]
    tile = jnp.minimum(jnp.arange(n_tiles, dtype=jnp.int32), n_used - 1)
    tile_b = jnp.sum((ends[None, :] <= tile[:, None]).astype(jnp.int32), axis=1)
    lo, hi = _bucket_tables()
    e_lo = jnp.asarray(lo)[tile_b] + layer * N_EXPERTS
    e_hi = jnp.asarray(hi)[tile_b] + layer * N_EXPERTS
    inv = _slot_tokens(pos, n_tiles * TMM)
    return pos, inv, e_lo.astype(jnp.int32), e_hi.astype(jnp.int32), n_used.reshape(1).astype(jnp.int32)


def _slot_tokens_kernel(pos_ref, inv_ref):
    def clear(j, carry):
        for u in range(SUBLANES):
            inv_ref[j * SUBLANES + u] = 0
        return carry

    lax.fori_loop(0, inv_ref.shape[0] // SUBLANES, clear, 0)

    def put(j, carry):
        for u in range(SUBLANES):
            t = j * SUBLANES + u
            inv_ref[pos_ref[t]] = t
        return carry

    lax.fori_loop(0, pos_ref.shape[0] // SUBLANES, put, 0)


def _slot_tokens(pos, n_slots):
    return pl.pallas_call(
        _slot_tokens_kernel,
        grid_spec=pltpu.PrefetchScalarGridSpec(
            num_scalar_prefetch=1,
            grid=(1,),
            in_specs=[],
            out_specs=pl.BlockSpec(memory_space=pltpu.SMEM),
        ),
        out_shape=jax.ShapeDtypeStruct((n_slots,), jnp.int32),
        compiler_params=_params(("arbitrary",)),
        name="slot_tokens",
    )(pos)


def _moe_kernel(elo_ref, ehi_ref, nu_ref, inv_ref, xrow_hbm, w1a, w3a, w2a, w1b, w3b, w2b, ys_ref,
                buf, sem, x_scr, g_scr):
    j = pl.program_id(0)
    n_used = nu_ref[0]
    slot = j % 2

    @pl.when(j == 0)
    def _():
        _gather_issue(inv_ref, xrow_hbm, buf, sem, 0, 0, inline=False)

    @pl.when(j < n_used)
    def _():
        _gather_wait(xrow_hbm, buf, sem, slot)
        rows = buf[slot].reshape(TMM, XROW_W)
        x_scr[...] = rows[:, 0:D].astype(BF16)
        g_scr[...] = rows[:, D:XROW_W]
        _gather_issue(inv_ref, xrow_hbm, buf, sem, jnp.where(j + 1 < n_used, j + 1, 0), 1 - slot, inline=True)
        x = x_scr[...]
        g_lo = g_scr[:, 0:1]
        g_hi = g_scr[:, 1:2]

        def expert(w1, w3, w2):
            a = _dot(x, w1[0])
            hid = (a * jax.nn.sigmoid(a)) * _dot(x, w3[0])
            return _dot(hid.astype(BF16), w2[0])

        ys_ref[...] = g_lo * expert(w1a, w3a, w2a) + g_hi * expert(w1b, w3b, w2b)

        @pl.when(j + 1 == n_used)
        def _():
            _gather_wait(xrow_hbm, buf, sem, 1 - slot)

    @pl.when(j >= n_used)
    def _():
        ys_ref[...] = jnp.zeros_like(ys_ref)


def _moe(e_lo, e_hi, n_used, inv, xrow, w1, w3, w2, n_tiles):
    def wspec(shape, which):
        if which == 0:
            return pl.BlockSpec(shape, lambda j, lo, hi, nu, inv: (lo[j], 0, 0))
        return pl.BlockSpec(shape, lambda j, lo, hi, nu, inv: (hi[j], 0, 0))

    s13 = (1, D, D_EXPERT)
    s2 = (1, D_EXPERT, D)
    return pl.pallas_call(
        _moe_kernel,
        grid_spec=pltpu.PrefetchScalarGridSpec(
            num_scalar_prefetch=4,
            grid=(n_tiles,),
            in_specs=[
                pl.BlockSpec(memory_space=pl.ANY),
                wspec(s13, 0), wspec(s13, 0), wspec(s2, 0),
                wspec(s13, 1), wspec(s13, 1), wspec(s2, 1),
            ],
            out_specs=pl.BlockSpec((TMM, D), lambda j, lo, hi, nu, inv: (j, 0)),
            scratch_shapes=[
                pltpu.VMEM((2, TMM // SUBLANES, SUBLANES, XROW_W), F32),
                pltpu.SemaphoreType.DMA((2,)),
                pltpu.VMEM((TMM, D), BF16),
                pltpu.VMEM((TMM, LANES), F32),
            ],
        ),
        out_shape=jax.ShapeDtypeStruct((n_tiles * TMM, D), F32),
        compiler_params=_params(("arbitrary",)),
        name="moe_experts",
    )(e_lo, e_hi, n_used, inv, _by_sublane_tile(xrow), w1, w3, w2, w1, w3, w2)


def _gather_issue(pos_ref, ys_hbm, buf, sem, tile, slot, inline):
    base = tile * TM

    def issue(j, carry):
        for u in range(SUBLANES):
            p = pos_ref[base + j * SUBLANES + u]
            pltpu.make_async_copy(ys_hbm.at[p >> 3, pl.ds(p & (SUBLANES - 1), 1), :],
                                  buf.at[slot, j, pl.ds(u, 1), :], sem.at[slot]).start()
        return carry

    if inline:
        for j in range(TM // SUBLANES):
            issue(j, 0)
    else:
        lax.fori_loop(0, TM // SUBLANES, issue, 0)


def _gather_wait(ys_hbm, buf, sem, slot):
    pltpu.make_async_copy(ys_hbm.at[pl.ds(0, TM // SUBLANES)], buf.at[slot], sem.at[slot]).wait()


def _gathered_rows(pos_ref, ys_hbm, buf, sem):
    i = pl.program_id(0)
    slot = i % 2

    @pl.when(i == 0)
    def _():
        _gather_issue(pos_ref, ys_hbm, buf, sem, 0, 0, inline=False)

    _gather_wait(ys_hbm, buf, sem, slot)
    return buf[slot].reshape(TM, D)


def _gather_request_next(pos_ref, ys_hbm, buf, sem):
    i = pl.program_id(0)
    nxt = jnp.where(i + 1 < pl.num_programs(0), i + 1, 0)
    _gather_issue(pos_ref, ys_hbm, buf, sem, nxt, 1 - i % 2, inline=True)


def _gather_drain(ys_hbm, buf, sem):
    i = pl.program_id(0)

    @pl.when(i + 1 == pl.num_programs(0))
    def _():
        _gather_wait(ys_hbm, buf, sem, 1 - i % 2)


_GATHER_SCRATCH = [pltpu.VMEM((2, TM // SUBLANES, SUBLANES, D), F32), pltpu.SemaphoreType.DMA((2,))]


def _by_sublane_tile(a):
    return a.reshape(a.shape[0] // SUBLANES, SUBLANES, a.shape[1])


def _rope_store(x, tabs, out_ref):
    cr, sr, cc, sc = tabs
    for ch in range(RET_QK // LANES):
        xc = x[:, ch * LANES:(ch + 1) * LANES]
        xr = pltpu.roll(xc, LANES // 2, axis=1)
        y = xc * cr + xr * sr if ch % 2 == 0 else xc * cc + xr * sc
        out_ref[:, ch * LANES:(ch + 1) * LANES] = y.astype(BF16)


def _inproj1_kernel(pos_ref, h_ref, ys_hbm, mod0_ref, mod1_ref, g_ref, w_ref, rt_ref, ct_ref,
                    h2_ref, q_ref, k_ref, v_ref, gg_ref, buf, sem, xm_scr):
    f = _gathered_rows(pos_ref, ys_hbm, buf, sem)
    h2 = h_ref[...] + mod0_ref[0, 5:6, :] * f
    h2_ref[...] = h2
    xm_scr[...] = _rms_mod(h2, g_ref[...], mod1_ref[0, 0:1, :], mod1_ref[0, 1:2, :]).astype(BF16)
    _gather_request_next(pos_ref, ys_hbm, buf, sem)
    xm = xm_scr[...]
    off = (pl.program_id(0) % (SUBLANES // ROPE_TILE_ROWS)) * ROPE_TILE_ROWS

    def row_table(which):
        return jnp.concatenate([jnp.broadcast_to(rt_ref[which, pl.ds(off + j, 1), :], (GRID_W, LANES))
                                for j in range(ROPE_TILE_ROWS)], axis=0)

    tabs = (row_table(0), row_table(1), ct_ref[0, 0], ct_ref[0, 1])
    _rope_store(_dot(xm, w_ref[:, 0:RET_QK]), tabs, q_ref)
    _rope_store(_dot(xm, w_ref[:, RET_QK:2 * RET_QK]) * (RET_DK ** -0.5), tabs, k_ref)
    v_ref[...] = _dot(xm, w_ref[:, 2 * RET_QK:2 * RET_QK + RET_V]).astype(BF16)
    gg_ref[...] = _dot(xm, w_ref[:, 2 * RET_QK + RET_V:]).astype(BF16)
    _gather_drain(ys_hbm, buf, sem)


def _inproj1(pos, h1, ys, modv0, modv1, g, w_in, tabs):
    tok = lambda w: pl.BlockSpec((TM, w), lambda i, pos: (i, 0))
    res = lambda shape: pl.BlockSpec(shape, lambda i, pos: (0,) * len(shape), pipeline_mode=pl.Buffered(1))
    mod = pl.BlockSpec((1, 8, D), lambda i, pos: (i // NT_LAT, 0, 0))
    row_tab = pl.BlockSpec((2, SUBLANES, LANES), lambda i, pos: (0, i // (SUBLANES // ROPE_TILE_ROWS), 0))
    col_tab = pl.BlockSpec((1, 2, TM, LANES), lambda i, pos: (i // NT_LAT, 0, 0, 0))
    return pl.pallas_call(
        _inproj1_kernel,
        grid_spec=pltpu.PrefetchScalarGridSpec(
            num_scalar_prefetch=1,
            grid=(NT,),
            in_specs=[tok(D), pl.BlockSpec(memory_space=pl.ANY), mod, mod, res((1, D)), res((D, 2 * RET_QK + 2 * RET_V)),
                      row_tab, col_tab],
            out_specs=[tok(D), tok(RET_QK), tok(RET_QK), tok(RET_V), tok(RET_V)],
            scratch_shapes=_GATHER_SCRATCH + [pltpu.VMEM((TM, D), BF16)],
        ),
        out_shape=[
            jax.ShapeDtypeStruct((T, D), F32),
            jax.ShapeDtypeStruct((T, RET_QK), BF16),
            jax.ShapeDtypeStruct((T, RET_QK), BF16),
            jax.ShapeDtypeStruct((T, RET_V), BF16),
            jax.ShapeDtypeStruct((T, RET_V), BF16),
        ],
        compiler_params=_params(("arbitrary",)),
        name="inproj1",
    )(pos, h1, _by_sublane_tile(ys), modv0, modv1, g, w_in, *tabs)


ROPE_TILE_ROWS = TM // GRID_W


def _rope_tables():
    n = LANES // 2
    inv_freq = ROPE_BASE ** (-np.arange(n, dtype=np.float64) / n)

    def lines(p):
        ang = p[:, None] * inv_freq[None, :]
        return np.stack([np.concatenate([np.cos(ang), np.cos(ang)], axis=1),
                         np.concatenate([-np.sin(ang), np.sin(ang)], axis=1)])

    ident = np.stack([np.ones((TM, LANES)), np.zeros((TM, LANES))])
    row_tab = np.concatenate([lines(np.arange(ROWS, dtype=np.float64)), ident[:, :SUBLANES]], axis=1)
    col_lat = lines(np.tile(np.arange(GRID_W, dtype=np.float64), ROPE_TILE_ROWS))
    col_tab = np.stack([col_lat, ident])
    return jnp.asarray(row_tab, dtype=F32), jnp.asarray(col_tab, dtype=F32)


RET_NCHUNK = T // RET_CHUNK
RET_CTX_CHUNKS = CTX // RET_CHUNK
RET_LAT_CHUNKS = SEQ // RET_CHUNK
RET_NHD = 2 * RET_HEADS


def _ret_kernel(dec_ref, qf_ref, kf_ref, vf_ref, qb_ref, kb_ref, vb_ref, of_ref, ob_ref,
                state, dm, qd, kd, cd):
    C = RET_CHUNK

    @pl.when(pl.program_id(0) == 0)
    def _():
        state[...] = jnp.zeros_like(state)
        n = lax.broadcasted_iota(jnp.int32, (C, C), 0).astype(F32)
        m = lax.broadcasted_iota(jnp.int32, (C, C), 1).astype(F32)
        for d in range(2):
            for h in range(RET_HEADS):
                idx = d * RET_HEADS + h
                lg = jnp.log1p(-jnp.exp(jnp.full((C, C), dec_ref[d, h], F32)))
                diff = n - m if d == 0 else m - n
                low = diff >= 0.0
                dm[idx] = jnp.where(low, jnp.exp(jnp.where(low, diff, 0.0) * lg), 0.0)
                qpow = n + 1.0 if d == 0 else C - n
                kpow = (C - 1.0) - n if d == 0 else n
                qcol = jnp.exp(qpow * lg)
                kcol = jnp.exp(kpow * lg)
                qd[idx] = jnp.concatenate([qcol] * (RET_DV // C), axis=1)
                kd[idx] = jnp.concatenate([kcol] * (RET_DK // C), axis=1)
                cd[idx] = jnp.exp(C * lg[0:8, :])

    for d, (q_ref, k_ref, v_ref, o_ref) in enumerate(((qf_ref, kf_ref, vf_ref, of_ref),
                                                       (qb_ref, kb_ref, vb_ref, ob_ref))):
        for h in range(RET_HEADS):
            idx = d * RET_HEADS + h
            q = q_ref[:, h * RET_DK:(h + 1) * RET_DK]
            k = k_ref[:, h * RET_DK:(h + 1) * RET_DK]
            v = v_ref[:, h * RET_DV:(h + 1) * RET_DV]
            st = state[idx]
            s = _dot_nt(q, k) * dm[idx]
            o = _dot(s.astype(BF16), v) + qd[idx] * _dot(q, st.astype(BF16))
            o_ref[:, h * RET_DV:(h + 1) * RET_DV] = o.astype(BF16)
            kdk = (k.astype(F32) * kd[idx]).astype(BF16)
            cdv = jnp.concatenate([cd[idx, 0:1, :]] * (RET_DV // C), axis=1)
            state[idx] = st * cdv + _dot_tn(kdk, v)


def _ret_scan(decay, q, k, v):
    def fwd(s):
        return jnp.where(s < RET_CTX_CHUNKS, RET_LAT_CHUNKS + s, s - RET_CTX_CHUNKS)

    def bwd(s):
        return RET_NCHUNK - 1 - s

    def spec(w, f):
        return pl.BlockSpec((RET_CHUNK, w), lambda s: (f(s), 0))

    return pl.pallas_call(
        _ret_kernel,
        grid=(RET_NCHUNK,),
        in_specs=[
            pl.BlockSpec(memory_space=pltpu.SMEM),
            spec(RET_QK, fwd), spec(RET_QK, fwd), spec(RET_V, fwd),
            spec(RET_QK, bwd), spec(RET_QK, bwd), spec(RET_V, bwd),
        ],
        out_specs=[spec(RET_V, fwd), spec(RET_V, bwd)],
        out_shape=[jax.ShapeDtypeStruct((T, RET_V), BF16), jax.ShapeDtypeStruct((T, RET_V), BF16)],
        scratch_shapes=[
            pltpu.VMEM((RET_NHD, RET_DK, RET_DV), F32),
            pltpu.VMEM((RET_NHD, RET_CHUNK, RET_CHUNK), F32),
            pltpu.VMEM((RET_NHD, RET_CHUNK, RET_DV), F32),
            pltpu.VMEM((RET_NHD, RET_CHUNK, RET_DK), F32),
            pltpu.VMEM((RET_NHD, 8, RET_CHUNK), F32),
        ],
        compiler_params=_params(("arbitrary",)),
        name="retention_scan",
    )(decay, q, k, v, q, k, v)


def _outproj1_kernel(of_ref, ob_ref, gg_ref, h_ref, mod_ref, w_ref, g2_ref, rw_ref, rb_ref, tril_ref,
                     h1_ref, xrow_ref, meta_ref, cnt_ref, run_ref):
    @pl.when(pl.program_id(0) == 0)
    def _():
        run_ref[...] = jnp.zeros_like(run_ref)

    y = jnp.zeros((TM, D), F32)
    for hh in range(RET_HEADS):
        cols = slice(hh * RET_DV, (hh + 1) * RET_DV)
        o = of_ref[:, cols].astype(F32) + ob_ref[:, cols].astype(F32)
        on = o * lax.rsqrt(jnp.mean(o * o, axis=-1, keepdims=True) + RMS_EPS)
        g = gg_ref[:, cols].astype(F32)
        z = (g * jax.nn.sigmoid(g)) * on
        y = y + _dot(z.astype(BF16), w_ref[cols, :])
    h1 = h_ref[...] + mod_ref[0, 2:3, :] * y
    h1_ref[...] = h1
    _epilogue(h1, mod_ref, g2_ref, rw_ref, rb_ref, tril_ref, run_ref, xrow_ref, meta_ref, cnt_ref)


def _outproj1(o_f, o_b, gg, h, modv, w_out, g2, rw, rb, tril):
    e_in, e_out, e_shape = _epilogue_specs(NT_LAT)
    tok = lambda w: pl.BlockSpec((TM, w), lambda i: (i, 0))
    return pl.pallas_call(
        _outproj1_kernel,
        grid=(NT_LAT,),
        in_specs=[tok(RET_V), tok(RET_V), tok(RET_V), tok(D), _mod_spec(), _resident((RET_V, D))] + e_in,
        out_specs=e_out,
        out_shape=e_shape,
        scratch_shapes=[pltpu.VMEM((8, LANES), F32)],
        compiler_params=_params(("arbitrary",)),
        name="outproj1",
    )(o_f, o_b, gg, h, modv, w_out, g2, rw, rb, tril)


def _final_kernel(pos_ref, h_ref, ys_hbm, mod_ref, g_ref, o_ref, buf, sem):
    f = _gathered_rows(pos_ref, ys_hbm, buf, sem)
    h = h_ref[...] + mod_ref[0, 5:6, :] * f
    o_ref[...] = h * lax.rsqrt(jnp.mean(h * h, axis=-1, keepdims=True) + RMS_EPS) * g_ref[...]
    _gather_request_next(pos_ref, ys_hbm, buf, sem)
    _gather_drain(ys_hbm, buf, sem)


def _final(pos, h, ys, modv, g):
    tok = pl.BlockSpec((TM, D), lambda i, pos: (i, 0))
    return pl.pallas_call(
        _final_kernel,
        grid_spec=pltpu.PrefetchScalarGridSpec(
            num_scalar_prefetch=1,
            grid=(NT_LAT,),
            in_specs=[tok, pl.BlockSpec(memory_space=pl.ANY),
                      pl.BlockSpec((1, 8, D), lambda i, pos: (0, 0, 0)),
                      pl.BlockSpec((1, D), lambda i, pos: (0, 0))],
            out_specs=tok,
            scratch_shapes=_GATHER_SCRATCH,
        ),
        out_shape=jax.ShapeDtypeStruct((SEQ, D), F32),
        compiler_params=_params(("arbitrary",)),
        name="final_norm",
    )(pos, h, _by_sublane_tile(ys), modv, g)


def _dft_constants():
    c = np.arange(FNET_GD)
    ang = 2.0 * np.pi * np.outer(c, c) / FNET_GD
    d64 = np.zeros((FNET_W, 2 * FNET_W))
    for g in range(FNET_W // FNET_GD):
        sl = slice(g * FNET_GD, (g + 1) * FNET_GD)
        d64[sl, sl] = np.cos(ang)
        d64[sl, FNET_W + g * FNET_GD:FNET_W + (g + 1) * FNET_GD] = -np.sin(ang)
    k2 = np.arange(FFT_N2)[:, None]
    n2 = np.arange(FFT_N2)[None, :]
    ga = np.zeros((FFT_N1, 2 * FFT_N2, 2 * FFT_N2))
    for n1 in range(FFT_N1):
        th = 2.0 * np.pi * ((k2 * (n1 + FFT_N1 * n2)) % SEQ) / SEQ
        cs, sn = np.cos(th), np.sin(th)
        ga[n1] = np.block([[cs, sn], [-sn, cs]])
    k1 = np.arange(FFT_N1)
    ph = 2.0 * np.pi * np.outer(k1, k1) / FFT_N1
    csb = np.concatenate([np.cos(ph), np.sin(ph)], axis=1)
    p = np.arange(CTX)
    pc = 2.0 * np.pi * (np.outer(p, p) % CTX) / CTX
    csc = np.concatenate([np.cos(pc), np.sin(pc)], axis=1)
    as_bf16 = lambda a: jnp.asarray(a, dtype=F32).astype(BF16)
    return as_bf16(d64), as_bf16(ga), as_bf16(csb), as_bf16(csc)


def _mod_table(mod_l):
    t = mod_l.reshape(2, 6, D)
    return jnp.concatenate([t, jnp.zeros((2, 2, D), F32)], axis=1)


def kernel(x, c, ctx, c_ctx, ada_w, ada_b, norm_g, final_norm_g, mixab_w_in, mixab_w_out, na_rpb,
           ret_w_in, ret_w_out, ret_decay, router_w, router_b, moe_w1, moe_w3, moe_w2):
    d64, ga, csb, csc = _dft_constants()
    tril = jnp.asarray(np.tril(np.ones((TM, TM)), -1), dtype=BF16)
    rw = jnp.pad(router_w, ((0, 0), (0, LANES - N_EXPERTS)))
    rb = jnp.pad(router_b, (0, LANES - N_EXPERTS)).reshape(1, LANES)
    w1 = moe_w1.reshape(DEPTH * N_EXPERTS, D, D_EXPERT).astype(BF16)
    w3 = moe_w3.reshape(DEPTH * N_EXPERTS, D, D_EXPERT).astype(BF16)
    w2 = moe_w2.reshape(DEPTH * N_EXPERTS, D_EXPERT, D).astype(BF16)

    mod = _mod_vectors(c, c_ctx, ada_w, ada_b)
    modv0, modv1 = _mod_table(mod[0]), _mod_table(mod[1])

    qkv, br, bi = _inproj0(x[0], ctx[0], modv0, norm_g[0, 0].reshape(1, D), mixab_w_in[0].astype(BF16), d64)
    a_lat = _fft_latent(br, bi, ga, csb)
    ao_ctx = _ctx_mixer(qkv, br, bi, csc)
    o_lat = _na_attention(qkv, _na_bias_rows(na_rpb[0]))
    h1, xrow, meta, cnt = _outproj0(a_lat, o_lat, ao_ctx, x[0], ctx[0], modv0, mixab_w_out[0].astype(BF16),
                                    norm_g[0, 1].reshape(1, D), rw, rb, tril)
    n_tiles0 = NT + N_REAL_BUCKETS
    pos0, inv0, e_lo, e_hi, n_used = _route_plan(meta, cnt, n_tiles0, 0)
    ys0 = _moe(e_lo, e_hi, n_used, inv0, xrow, w1, w3, w2, n_tiles0)

    h2, q, k, v, gg = _inproj1(pos0, h1, ys0, modv0, modv1, norm_g[1, 0].reshape(1, D),
                               ret_w_in[0].astype(BF16), _rope_tables())
    o_f, o_b = _ret_scan(ret_decay[0].astype(F32), q, k, v)
    h3, xrow1, meta1, cnt1 = _outproj1(o_f, o_b, gg, h2, modv1, ret_w_out[0].astype(BF16),
                                       norm_g[1, 1].reshape(1, D), rw, rb, tril)
    n_tiles1 = NT_LAT + N_REAL_BUCKETS
    pos1, inv1, e_lo1, e_hi1, n_used1 = _route_plan(meta1, cnt1, n_tiles1, 1)
    ys1 = _moe(e_lo1, e_hi1, n_used1, inv1, xrow1, w1, w3, w2, n_tiles1)
    out = _final(pos1, h3, ys1, modv1, final_norm_g.reshape(1, D))
    return out[None]
```

```python
import functools
import math

import numpy as np
import jax
import jax.numpy as jnp
from jax import lax
from jax.experimental import pallas as pl
from jax.experimental.pallas import tpu as pltpu

F32 = jnp.float32
BF16 = jnp.bfloat16

D = 1024
SEQ = 16384
CTX = 256
T = SEQ + CTX
DEPTH = 2
GRID_W = 64
ROWS = SEQ // GRID_W
RMS_EPS = 1e-6
ROPE_BASE = 10000.0

FNET_W = 256
FNET_GD = 64
NA_HEADS = 12
NA_HD = 64
NA_W = NA_HEADS * NA_HD
NA_KH = 8
NA_KW = 16
AB_IN = FNET_W + 3 * NA_W
QKV_W = 3 * NA_W

RET_HEADS = 4
RET_DK = 256
RET_DV = 512
RET_QK = RET_HEADS * RET_DK
RET_V = RET_HEADS * RET_DV
RET_CHUNK = 128

N_EXPERTS = 16
EPG = 4
D_EXPERT = 512

LANES = 128
TM = 256
NT = T // TM
NT_LAT = SEQ // TM
TMM = TM
N_BUCKET_IDS = 64
N_REAL_BUCKETS = 24
XROW_W = D + LANES
FFT_N1 = 128
FFT_N2 = 128
NEG_BIG = -1e30
SUBLANES = 8

VMEM_LIMIT = 56 * 1024 * 1024


def _params(sem, vmem=VMEM_LIMIT):
    return pltpu.CompilerParams(dimension_semantics=sem, vmem_limit_bytes=vmem)


def _dot(a, b):
    return jnp.dot(a, b, preferred_element_type=F32)


def _dot_nt(a, b):
    return lax.dot_general(a, b, (((1,), (1,)), ((), ())), preferred_element_type=F32)


def _dot_tn(a, b):
    return lax.dot_general(a, b, (((0,), (0,)), ((), ())), preferred_element_type=F32)


def _rms_mod(x, g, sh, sc):
    ms = jnp.mean(x * x, axis=-1, keepdims=True)
    y = x * lax.rsqrt(ms + RMS_EPS) * g
    return y * (1.0 + sc) + sh


def _resident(shape):
    nd = len(shape)
    return pl.BlockSpec(shape, lambda *_: (0,) * nd, pipeline_mode=pl.Buffered(1))


MOD_TN = 768


def _mod_kernel(cs_ref, w_ref, b_ref, o_ref):
    cs = cs_ref[...]
    s = cs * jax.nn.sigmoid(cs)
    w = w_ref[0]
    r0 = jnp.sum(s[:, 0:1] * w, axis=0, keepdims=True)
    r1 = jnp.sum(s[:, 1:2] * w, axis=0, keepdims=True)
    o_ref[0] = jnp.concatenate([r0, r1], axis=0) + b_ref[0]


def _mod_vectors(c, c_ctx, ada_w, ada_b):
    cs = jnp.stack([c[0], c_ctx], axis=1)
    n = 6 * D
    return pl.pallas_call(
        _mod_kernel,
        grid=(DEPTH, n // MOD_TN),
        in_specs=[
            pl.BlockSpec((D, 2), lambda l, j: (0, 0)),
            pl.BlockSpec((1, D, MOD_TN), lambda l, j: (l, 0, j)),
            pl.BlockSpec((1, 1, MOD_TN), lambda l, j: (l, 0, j)),
        ],
        out_specs=pl.BlockSpec((1, 2, MOD_TN), lambda l, j: (l, 0, j)),
        out_shape=jax.ShapeDtypeStruct((DEPTH, 2, n), F32),
        compiler_params=_params(("arbitrary", "arbitrary")),
        name="mod_vectors",
    )(cs, ada_w, ada_b.reshape(DEPTH, 1, n))


def _mod_spec():
    return pl.BlockSpec((1, 8, D), lambda i, *_: (i // NT_LAT, 0, 0))


def _stream_tile(x_ref, ctx_ref):
    return jnp.where(pl.program_id(0) >= NT_LAT, ctx_ref[...], x_ref[...])


def _stream_specs():
    return [pl.BlockSpec((TM, D), lambda i: (jnp.minimum(i, NT_LAT - 1), 0)),
            pl.BlockSpec((CTX, D), lambda i: (0, 0))]


def _inproj0_kernel(x_ref, ctx_ref, mod_ref, g_ref, w_ref, d64_ref, qkv_ref, br_ref, bi_ref):
    h = _stream_tile(x_ref, ctx_ref)
    xm = _rms_mod(h, g_ref[...], mod_ref[0, 0:1, :], mod_ref[0, 1:2, :]).astype(BF16)
    p = _dot(xm, w_ref[...])
    qkv_ref[...] = p[:, FNET_W:].astype(BF16)
    b = _dot(p[:, :FNET_W].astype(BF16), d64_ref[...])
    br_ref[...] = b[:, :FNET_W].astype(BF16)
    bi_ref[...] = b[:, FNET_W:].astype(BF16)


def _inproj0(x, ctx, modv, g, w_in, d64):
    return pl.pallas_call(
        _inproj0_kernel,
        grid=(NT,),
        in_specs=_stream_specs() + [
            _mod_spec(),
            _resident((1, D)),
            _resident((D, AB_IN)),
            _resident((FNET_W, 2 * FNET_W)),
        ],
        out_specs=[
            pl.BlockSpec((TM, QKV_W), lambda i: (i, 0)),
            pl.BlockSpec((TM, FNET_W), lambda i: (i, 0)),
            pl.BlockSpec((TM, FNET_W), lambda i: (i, 0)),
        ],
        out_shape=[
            jax.ShapeDtypeStruct((T, QKV_W), BF16),
            jax.ShapeDtypeStruct((T, FNET_W), BF16),
            jax.ShapeDtypeStruct((T, FNET_W), BF16),
        ],
        compiler_params=_params(("arbitrary",)),
        name="inproj0",
    )(x, ctx, modv, g, w_in, d64)


FFT_A_NB = 8
FFT_B_TN = 4096


def _fft_a_kernel(br_ref, bi_ref, ga_ref, v_ref):
    for u in range(FFT_A_NB):
        cols = slice(u * FNET_W, (u + 1) * FNET_W)
        z = _dot(ga_ref[u, :, 0:FFT_N2], br_ref[:, cols]) + _dot(ga_ref[u, :, FFT_N2:], bi_ref[:, cols])
        v_ref[0, u] = z[:FFT_N2].astype(BF16)
        v_ref[1, u] = z[FFT_N2:].astype(BF16)


def _fft_b_kernel(cs_ref, v_ref, o_ref):
    o_ref[...] = _dot(cs_ref[...], v_ref[...]) * (1.0 / math.sqrt(SEQ * FNET_GD))


def _fft_latent(br, bi, ga, csb):
    width = FFT_N1 * FNET_W
    br2 = br.reshape(T // FFT_N1, width)
    bi2 = bi.reshape(T // FFT_N1, width)
    v = pl.pallas_call(
        _fft_a_kernel,
        grid=(FFT_N1 // FFT_A_NB,),
        in_specs=[
            pl.BlockSpec((FFT_N2, FFT_A_NB * FNET_W), lambda j: (0, j)),
            pl.BlockSpec((FFT_N2, FFT_A_NB * FNET_W), lambda j: (0, j)),
            pl.BlockSpec((FFT_A_NB, 2 * FFT_N2, 2 * FFT_N2), lambda j: (j, 0, 0)),
        ],
        out_specs=pl.BlockSpec((2, FFT_A_NB, FFT_N2, FNET_W), lambda j: (0, j, 0, 0)),
        out_shape=jax.ShapeDtypeStruct((2, FFT_N1, FFT_N2, FNET_W), BF16),
        compiler_params=_params(("arbitrary",)),
        name="fft_stage_a",
    )(br2, bi2, ga)
    v2 = v.reshape(2 * FFT_N1, FFT_N2 * FNET_W)
    x = pl.pallas_call(
        _fft_b_kernel,
        grid=(FFT_N2 * FNET_W // FFT_B_TN,),
        in_specs=[
            _resident((FFT_N1, 2 * FFT_N1)),
            pl.BlockSpec((2 * FFT_N1, FFT_B_TN), lambda j: (0, j)),
        ],
        out_specs=pl.BlockSpec((FFT_N1, FFT_B_TN), lambda j: (0, j)),
        out_shape=jax.ShapeDtypeStruct((FFT_N1, FFT_N2 * FNET_W), F32),
        compiler_params=_params(("arbitrary",)),
        name="fft_stage_b",
    )(csb, v2)
    return x.reshape(SEQ, FNET_W)


def _head_pair_masks():
    lane = lax.broadcasted_iota(jnp.int32, (1, LANES), 1)
    return lane < NA_HD


def _ctx_kernel(qkv_ref, br_ref, bi_ref, cs_ref, o_ref):
    a = _dot(cs_ref[:, 0:CTX], br_ref[...]) + _dot(cs_ref[:, CTX:], bi_ref[...])
    o_ref[:, 0:FNET_W] = (a * (1.0 / math.sqrt(CTX * FNET_GD))).astype(BF16)
    m0 = _head_pair_masks()
    for hp in range(NA_HEADS // 2):
        q = qkv_ref[:, hp * LANES:(hp + 1) * LANES]
        k = qkv_ref[:, NA_W + hp * LANES:NA_W + (hp + 1) * LANES]
        v = qkv_ref[:, 2 * NA_W + hp * LANES:2 * NA_W + (hp + 1) * LANES]
        outs = []
        for a_ in range(2):
            qa = jnp.where(m0 if a_ == 0 else jnp.logical_not(m0), q, jnp.zeros_like(q))
            s = _dot_nt(qa, k) * (NA_HD ** -0.5)
            p = jnp.exp(s - jnp.max(s, axis=-1, keepdims=True))
            l = jnp.sum(p, axis=-1, keepdims=True)
            outs.append(_dot(p.astype(BF16), v) / l)
        o_ref[:, FNET_W + hp * LANES:FNET_W + (hp + 1) * LANES] = jnp.where(m0, outs[0], outs[1]).astype(BF16)


def _ctx_mixer(qkv, br, bi, csc):
    return pl.pallas_call(
        _ctx_kernel,
        grid=(1,),
        in_specs=[
            pl.BlockSpec((CTX, QKV_W), lambda i: (SEQ // CTX, 0)),
            pl.BlockSpec((CTX, FNET_W), lambda i: (SEQ // CTX, 0)),
            pl.BlockSpec((CTX, FNET_W), lambda i: (SEQ // CTX, 0)),
            pl.BlockSpec((CTX, 2 * CTX), lambda i: (0, 0)),
        ],
        out_specs=pl.BlockSpec((CTX, D), lambda i: (0, 0)),
        out_shape=jax.ShapeDtypeStruct((CTX, D), BF16),
        compiler_params=_params(("arbitrary",)),
        name="ctx_mixer",
    )(qkv, br, bi, csc)


NA_RB = 8
NA_WIN = NA_KH * GRID_W


NA_NK = NA_WIN + CTX
NA_SUB = 16


def _na_kernel(q_ref, k_ref, v_ref, kc_ref, vc_ref, bias_ref, o_ref, s_scr, p_scr, l_scr):
    b = pl.program_id(1)
    m0 = _head_pair_masks()
    nm0 = jnp.logical_not(m0)
    scale = NA_HD ** -0.5

    def window(i):
        r = b * NA_RB + i
        rs = jnp.clip(r - NA_KH // 2, 0, ROWS - NA_KH)
        return pl.multiple_of(rs * GRID_W, GRID_W), rs - r + (NA_KH - 1)

    def scores(i):
        start, _ = window(i)
        q = q_ref[i * GRID_W:(i + 1) * GRID_W, :] * scale
        qs = jnp.concatenate([jnp.where(m0, q, jnp.zeros_like(q)), jnp.where(nm0, q, jnp.zeros_like(q))], axis=0)
        s_scr[i, :, 0:NA_WIN] = _dot_nt(qs, k_ref[pl.ds(start, NA_WIN), :])
        s_scr[i, :, NA_WIN:NA_NK] = _dot_nt(qs, kc_ref[...])

    def softmax(i):
        _, e = window(i)
        for g in range(2 * GRID_W // NA_SUB):
            rows = slice(g * NA_SUB, (g + 1) * NA_SUB)
            a, c0 = divmod(g * NA_SUB, GRID_W)
            s1 = s_scr[i, rows, 0:NA_WIN] + bias_ref[a, e, c0:c0 + NA_SUB, :]
            s2 = s_scr[i, rows, NA_WIN:NA_NK]
            mx = jnp.maximum(jnp.max(s1, axis=-1, keepdims=True), jnp.max(s2, axis=-1, keepdims=True))
            p1 = jnp.exp(s1 - mx)
            p2 = jnp.exp(s2 - mx)
            l = jnp.sum(p1, axis=-1, keepdims=True) + jnp.sum(p2, axis=-1, keepdims=True)
            p_scr[i, rows, 0:NA_WIN] = p1.astype(BF16)
            p_scr[i, rows, NA_WIN:NA_NK] = p2.astype(BF16)
            l_scr[i, rows, :] = jnp.broadcast_to(l, (NA_SUB, LANES))

    def values(i):
        start, _ = window(i)
        o = _dot(p_scr[i, :, 0:NA_WIN], v_ref[pl.ds(start, NA_WIN), :]) + _dot(p_scr[i, :, NA_WIN:NA_NK], vc_ref[...])
        o = o / l_scr[i]
        o_ref[i * GRID_W:(i + 1) * GRID_W, :] = jnp.where(m0, o[0:GRID_W], o[GRID_W:]).astype(BF16)

    scores(0)
    for i in range(NA_RB):
        if i + 1 < NA_RB:
            scores(i + 1)
        softmax(i)
        values(i)


def _na_attention(qkv, bias):
    nq = NA_RB * GRID_W
    nhp = NA_HEADS // 2
    return pl.pallas_call(
        _na_kernel,
        grid=(nhp, ROWS // NA_RB),
        in_specs=[
            pl.BlockSpec((nq, LANES), lambda hp, b: (b, hp)),
            pl.BlockSpec((SEQ, LANES), lambda hp, b: (0, nhp + hp)),
            pl.BlockSpec((SEQ, LANES), lambda hp, b: (0, 2 * nhp + hp)),
            pl.BlockSpec((CTX, LANES), lambda hp, b: (SEQ // CTX, nhp + hp)),
            pl.BlockSpec((CTX, LANES), lambda hp, b: (SEQ // CTX, 2 * nhp + hp)),
            pl.BlockSpec((2, NA_KH, GRID_W, NA_WIN), lambda hp, b: (hp, 0, 0, 0)),
        ],
        out_specs=pl.BlockSpec((nq, LANES), lambda hp, b: (b, hp)),
        out_shape=jax.ShapeDtypeStruct((SEQ, NA_W), BF16),
        scratch_shapes=[
            pltpu.VMEM((NA_RB, 2 * GRID_W, NA_NK), F32),
            pltpu.VMEM((NA_RB, 2 * GRID_W, NA_NK), BF16),
            pltpu.VMEM((NA_RB, 2 * GRID_W, LANES), F32),
        ],
        compiler_params=_params(("arbitrary", "arbitrary")),
        name="na_attention",
    )(qkv, qkv, qkv, qkv, qkv, bias)


def _na_bias_rows(rpb):
    c = np.arange(GRID_W)[:, None]
    kc = np.arange(GRID_W)[None, :]
    ws = np.clip(c - NA_KW // 2, 0, GRID_W - NA_KW)
    col_ok = (kc >= ws) & (kc < ws + NA_KW)
    dc = np.clip(kc - c + NA_KW - 1, 0, 2 * NA_KW - 2)
    pick = (dc[None] == np.arange(2 * NA_KW - 1)[:, None, None]).astype(np.float32)
    full = jnp.einsum("hrd,dck->hrck", rpb.astype(F32), jnp.asarray(pick), precision=lax.Precision.HIGHEST)
    full = jnp.where(jnp.asarray(col_ok)[None, None], full, NEG_BIG)
    rows = [jnp.transpose(full[:, e:e + NA_KH], (0, 2, 1, 3)).reshape(NA_HEADS, GRID_W, NA_WIN)
            for e in range(NA_KH)]
    return jnp.stack(rows, axis=1)


def _lane_shift(x, d):
    return pltpu.roll(x, (LANES - d) % LANES, axis=1)


def _epilogue(h1, mod_ref, g2_ref, rw_ref, rb_ref, tril_ref, run_ref, xrow_ref, meta_ref, cnt_ref):
    xm2 = _rms_mod(h1, g2_ref[...], mod_ref[0, 3:4, :], mod_ref[0, 4:5, :])
    xrow_ref[:, 0:D] = xm2
    xh = xm2.astype(BF16)
    xl = (xm2 - xh.astype(F32)).astype(BF16)
    w = rw_ref[...]
    wh = w.astype(BF16)
    wl = (w - wh.astype(F32)).astype(BF16)
    logits = _dot(xh, wh) + (_dot(xl, wh) + _dot(xh, wl))
    scores = jax.nn.sigmoid(logits)
    biased = scores + rb_ref[...]
    lane = lax.broadcasted_iota(jnp.int32, (TM, LANES), 1)
    k = lane & (EPG - 1)
    valid = lane < N_EXPERTS
    offs = (-3, -2, -1, 1, 2, 3)
    in_grp = {d: (k + d >= 0) & (k + d < EPG) for d in offs}
    rank = jnp.zeros((TM, LANES), F32)
    for d in offs:
        v = _lane_shift(biased, d)
        beats = (v > biased) | (v == biased) if d < 0 else (v > biased)
        rank = rank + jnp.where(in_grp[d] & beats, 1.0, 0.0)
    top2 = rank < 2.0
    t = jnp.where(top2, biased, 0.0)
    gs = t
    for d in offs:
        gs = gs + jnp.where(in_grp[d], _lane_shift(t, d), 0.0)
    beaten = jnp.zeros((TM, LANES), jnp.bool_)
    for d in (-12, -8, -4, 4, 8, 12):
        v = _lane_shift(gs, d)
        ok = (lane + d >= 0) & (lane + d < N_EXPERTS)
        beats = (v > gs) | (v == gs) if d < 0 else (v > gs)
        beaten = beaten | (ok & beats)
    sel = valid & top2 & jnp.logical_not(beaten)
    wsel = jnp.where(sel, scores, 0.0)
    gate = wsel / jnp.sum(wsel, axis=-1, keepdims=True)
    self_f = jnp.where(sel, 1.0, 0.0)
    before = jnp.zeros((TM, LANES), F32)
    for d in (-3, -2, -1):
        before = before + jnp.where(in_grp[d], _lane_shift(self_f, d), 0.0)
    is_lo = sel & (before == 0.0)
    w_lo = jnp.sum(jnp.where(is_lo, gate, 0.0), axis=-1, keepdims=True)
    w_hi = jnp.sum(jnp.where(sel & jnp.logical_not(is_lo), gate, 0.0), axis=-1, keepdims=True)
    code = ((1 << k) + 8 * (lane >> 2)).astype(F32)
    bid = jnp.sum(jnp.where(sel, code, 0.0), axis=-1, keepdims=True)
    onehot = lane.astype(F32) == bid
    cum = _dot(tril_ref[...], jnp.where(onehot, 1.0, 0.0).astype(BF16))
    run = run_ref[0:1, :]
    rank_g = jnp.sum(jnp.where(onehot, cum + run, 0.0), axis=-1, keepdims=True)
    cnt = jnp.sum(jnp.where(onehot, 1.0, 0.0), axis=0, keepdims=True)
    new_run = jnp.broadcast_to(run + cnt, (8, LANES))
    run_ref[...] = new_run
    cnt_ref[...] = new_run
    meta = jnp.where(lane == 0, w_lo, jnp.where(lane == 1, w_hi, jnp.where(lane == 2, bid, jnp.where(lane == 3, rank_g, 0.0))))
    xrow_ref[:, D:XROW_W] = meta
    meta_ref[...] = meta


def _epilogue_specs(ntiles):
    in_specs = [
        _resident((1, D)),
        _resident((D, LANES)),
        _resident((1, LANES)),
        _resident((TM, TM)),
    ]
    out_specs = [
        pl.BlockSpec((TM, D), lambda i: (i, 0)),
        pl.BlockSpec((TM, XROW_W), lambda i: (i, 0)),
        pl.BlockSpec((TM, LANES), lambda i: (i, 0)),
        pl.BlockSpec((8, LANES), lambda i: (0, 0)),
    ]
    n = ntiles * TM
    out_shape = [
        jax.ShapeDtypeStruct((n, D), F32),
        jax.ShapeDtypeStruct((n, XROW_W), F32),
        jax.ShapeDtypeStruct((n, LANES), F32),
        jax.ShapeDtypeStruct((8, LANES), F32),
    ]
    return in_specs, out_specs, out_shape


def _outproj0_kernel(a_ref, o_ref, aoc_ref, x_ref, ctx_ref, mod_ref, w_ref, g2_ref, rw_ref, rb_ref, tril_ref,
                     h1_ref, xrow_ref, meta_ref, cnt_ref, run_ref):
    @pl.when(pl.program_id(0) == 0)
    def _():
        run_ref[...] = jnp.zeros_like(run_ref)

    is_ctx = pl.program_id(0) >= NT_LAT
    a = jnp.where(is_ctx, aoc_ref[:, 0:FNET_W], a_ref[...].astype(BF16))
    o = jnp.where(is_ctx, aoc_ref[:, FNET_W:], o_ref[...])
    y = _dot(a, w_ref[0:FNET_W, :]) + _dot(o, w_ref[FNET_W:, :])
    h1 = _stream_tile(x_ref, ctx_ref) + mod_ref[0, 2:3, :] * y
    h1_ref[...] = h1
    _epilogue(h1, mod_ref, g2_ref, rw_ref, rb_ref, tril_ref, run_ref, xrow_ref, meta_ref, cnt_ref)


def _outproj0(a_lat, o_lat, ao_ctx, x, ctx, modv, w_out, g2, rw, rb, tril):
    e_in, e_out, e_shape = _epilogue_specs(NT)
    lat = lambda w: pl.BlockSpec((TM, w), lambda i: (jnp.minimum(i, NT_LAT - 1), 0))
    return pl.pallas_call(
        _outproj0_kernel,
        grid=(NT,),
        in_specs=[lat(FNET_W), lat(NA_W), pl.BlockSpec((CTX, D), lambda i: (0, 0))] + _stream_specs() + [
            _mod_spec(),
            _resident((D, D)),
        ] + e_in,
        out_specs=e_out,
        out_shape=e_shape,
        scratch_shapes=[pltpu.VMEM((8, LANES), F32)],
        compiler_params=_params(("arbitrary",)),
        name="outproj0",
    )(a_lat, o_lat, ao_ctx, x, ctx, modv, w_out, g2, rw, rb, tril)


def _bucket_tables():
    lo = np.zeros(N_BUCKET_IDS, np.int32)
    hi = np.zeros(N_BUCKET_IDS, np.int32)
    for g in range(N_EXPERTS // EPG):
        for a in range(EPG):
            for b in range(a + 1, EPG):
                i = 16 * g + (1 << a) + (1 << b)
                lo[i] = EPG * g + a
                hi[i] = EPG * g + b
    return lo, hi


def _slot_kernel(meta_ref, starts_ref, o_ref):
    rows = meta_ref.shape[0]
    lane = lax.broadcasted_iota(jnp.int32, (rows, LANES), 1).astype(F32)
    first = jnp.sum(jnp.where(lane == meta_ref[:, 2:3], starts_ref[...], 0.0), axis=-1, keepdims=True)
    o_ref[...] = jnp.broadcast_to((first + meta_ref[:, 3:4]).astype(jnp.int32), (rows, LANES))


SLOT_STEPS = 4


def _token_slots(meta, first_row):
    n = meta.shape[0]
    rows = n // SLOT_STEPS
    out = pl.pallas_call(
        _slot_kernel,
        grid=(SLOT_STEPS,),
        in_specs=[pl.BlockSpec((rows, LANES), lambda i: (i, 0)), pl.BlockSpec((1, LANES), lambda i: (0, 0))],
        out_specs=pl.BlockSpec((rows, LANES), lambda i: (i, 0)),
        out_shape=jax.ShapeDtypeStruct((n, LANES), jnp.int32),
        compiler_params=_params(("arbitrary",)),
        name="token_slots",
    )(meta, first_row)
    return out[:, 0]


def _route_plan(meta, cnt, n_tiles, layer):
    counts = cnt[0, :N_BUCKET_IDS].astype(jnp.int32)
    nt_b = (counts + TMM - 1) // TMM
    ends = jnp.cumsum(nt_b)
    starts = ends - nt_b
    first_row = jnp.pad((starts * TMM).astype(F32), (0, LANES - N_BUCKET_IDS)).reshape(1, LANES)
    pos = _token_slots(meta, first_row)
    n_used = ends[-1]
    tile = jnp.minimum(jnp.arange(n_tiles, dtype=jnp.int32), n_used - 1)
    tile_b = jnp.sum((ends[None, :] <= tile[:, None]).astype(jnp.int32), axis=1)
    lo, hi = _bucket_tables()
    e_lo = jnp.asarray(lo)[tile_b] + layer * N_EXPERTS
    e_hi = jnp.asarray(hi)[tile_b] + layer * N_EXPERTS
    inv = _slot_tokens(pos, n_tiles * TMM)
    return pos, inv, e_lo.astype(jnp.int32), e_hi.astype(jnp.int32), n_used.reshape(1).astype(jnp.int32)


def _slot_tokens_kernel(pos_ref, inv_ref):
    def clear(j, carry):
        for u in range(SUBLANES):
            inv_ref[j * SUBLANES + u] = 0
        return carry

    lax.fori_loop(0, inv_ref.shape[0] // SUBLANES, clear, 0)

    def put(j, carry):
        for u in range(SUBLANES):
            t = j * SUBLANES + u
            inv_ref[pos_ref[t]] = t
        return carry

    lax.fori_loop(0, pos_ref.shape[0] // SUBLANES, put, 0)


def _slot_tokens(pos, n_slots):
    return pl.pallas_call(
        _slot_tokens_kernel,
        grid_spec=pltpu.PrefetchScalarGridSpec(
            num_scalar_prefetch=1,
            grid=(1,),
            in_specs=[],
            out_specs=pl.BlockSpec(memory_space=pltpu.SMEM),
        ),
        out_shape=jax.ShapeDtypeStruct((n_slots,), jnp.int32),
        compiler_params=_params(("arbitrary",)),
        name="slot_tokens",
    )(pos)


def _moe_kernel(elo_ref, ehi_ref, nu_ref, inv_ref, xrow_hbm, w1a, w3a, w2a, w1b, w3b, w2b, ys_ref,
                buf, sem):
    j = pl.program_id(0)
    n_used = nu_ref[0]
    slot = j % 2

    @pl.when(j == 0)
    def _():
        _gather_issue(inv_ref, xrow_hbm, buf, sem, 0, 0, inline=False)

    @pl.when(j < n_used)
    def _():
        _gather_wait(xrow_hbm, buf, sem, slot)
        nxt = jnp.where(j + 1 < n_used, j + 1, 0)
        quarter = TMM // SUBLANES // 4

        def x_after_requests(part):
            _gather_issue(inv_ref, xrow_hbm, buf, sem, nxt, 1 - slot, inline=True,
                          part=(part * quarter, (part + 1) * quarter))
            return buf[slot].reshape(TMM, XROW_W)[:, 0:D].astype(BF16)

        def expert(w1, w3, w2, part):
            a = _dot(x_after_requests(part), w1[0])
            hid = (a * jax.nn.sigmoid(a)) * _dot(x_after_requests(part + 1), w3[0])
            return _dot(hid.astype(BF16), w2[0])

        y_lo = expert(w1a, w3a, w2a, 0)
        y_hi = expert(w1b, w3b, w2b, 2)
        gates = buf[slot].reshape(TMM, XROW_W)[:, D:XROW_W]
        ys_ref[...] = gates[:, 0:1] * y_lo + gates[:, 1:2] * y_hi

        @pl.when(j + 1 == n_used)
        def _():
            _gather_wait(xrow_hbm, buf, sem, 1 - slot)

    @pl.when(j >= n_used)
    def _():
        ys_ref[...] = jnp.zeros_like(ys_ref)


def _moe(e_lo, e_hi, n_used, inv, xrow, w1, w3, w2, n_tiles):
    def wspec(shape, which):
        if which == 0:
            return pl.BlockSpec(shape, lambda j, lo, hi, nu, inv: (lo[j], 0, 0))
        return pl.BlockSpec(shape, lambda j, lo, hi, nu, inv: (hi[j], 0, 0))

    s13 = (1, D, D_EXPERT)
    s2 = (1, D_EXPERT, D)
    return pl.pallas_call(
        _moe_kernel,
        grid_spec=pltpu.PrefetchScalarGridSpec(
            num_scalar_prefetch=4,
            grid=(n_tiles,),
            in_specs=[
                pl.BlockSpec(memory_space=pl.ANY),
                wspec(s13, 0), wspec(s13, 0), wspec(s2, 0),
                wspec(s13, 1), wspec(s13, 1), wspec(s2, 1),
            ],
            out_specs=pl.BlockSpec((TMM, D), lambda j, lo, hi, nu, inv: (j, 0)),
            scratch_shapes=[
                pltpu.VMEM((2, TMM // SUBLANES, SUBLANES, XROW_W), F32),
                pltpu.SemaphoreType.DMA((2,)),
            ],
        ),
        out_shape=jax.ShapeDtypeStruct((n_tiles * TMM, D), F32),
        compiler_params=_params(("arbitrary",)),
        name="moe_experts",
    )(e_lo, e_hi, n_used, inv, _by_sublane_tile(xrow), w1, w3, w2, w1, w3, w2)


def _gather_issue(pos_ref, ys_hbm, buf, sem, tile, slot, inline, part=(0, TM // SUBLANES)):
    base = tile * TM

    def issue(j, carry):
        for u in range(SUBLANES):
            p = pos_ref[base + j * SUBLANES + u]
            pltpu.make_async_copy(ys_hbm.at[p >> 3, pl.ds(p & (SUBLANES - 1), 1), :],
                                  buf.at[slot, j, pl.ds(u, 1), :], sem.at[slot]).start()
        return carry

    if inline:
        for j in range(*part):
            issue(j, 0)
    else:
        lax.fori_loop(0, TM // SUBLANES, issue, 0)


def _gather_wait(ys_hbm, buf, sem, slot):
    pltpu.make_async_copy(ys_hbm.at[pl.ds(0, TM // SUBLANES)], buf.at[slot], sem.at[slot]).wait()


def _gathered_rows(pos_ref, ys_hbm, buf, sem, request_next):
    i = pl.program_id(0)
    slot = i % 2

    @pl.when(i == 0)
    def _():
        _gather_issue(pos_ref, ys_hbm, buf, sem, 0, 0, inline=False)

    if request_next:
        @pl.when(i + 1 < pl.num_programs(0))
        def _():
            _gather_issue(pos_ref, ys_hbm, buf, sem, i + 1, 1 - slot, inline=False)

    _gather_wait(ys_hbm, buf, sem, slot)
    return buf[slot].reshape(TM, D)


def _gather_drain(ys_hbm, buf, sem):
    i = pl.program_id(0)

    @pl.when(i + 1 == pl.num_programs(0))
    def _():
        _gather_wait(ys_hbm, buf, sem, 1 - i % 2)


_GATHER_SCRATCH = [pltpu.VMEM((2, TM // SUBLANES, SUBLANES, D), F32), pltpu.SemaphoreType.DMA((2,))]


def _by_sublane_tile(a):
    return a.reshape(a.shape[0] // SUBLANES, SUBLANES, a.shape[1])


def _rope_store(x, tabs, out_ref):
    cr, sr, cc, sc = tabs
    for ch in range(RET_QK // LANES):
        xc = x[:, ch * LANES:(ch + 1) * LANES]
        xr = pltpu.roll(xc, LANES // 2, axis=1)
        y = xc * cr + xr * sr if ch % 2 == 0 else xc * cc + xr * sc
        out_ref[:, ch * LANES:(ch + 1) * LANES] = y.astype(BF16)


def _inproj1_kernel(pos_ref, h_ref, ys_hbm, mod0_ref, mod1_ref, g_ref, w_ref, rt_ref, ct_ref,
                    h2_ref, q_ref, k_ref, v_ref, gg_ref, buf, sem):
    f = _gathered_rows(pos_ref, ys_hbm, buf, sem, request_next=False)
    h2 = h_ref[...] + mod0_ref[0, 5:6, :] * f
    h2_ref[...] = h2
    buf[2] = _rms_mod(h2, g_ref[...], mod1_ref[0, 0:1, :], mod1_ref[0, 1:2, :]).reshape(TM // SUBLANES, SUBLANES, D)
    i = pl.program_id(0)
    nxt = jnp.where(i + 1 < pl.num_programs(0), i + 1, 0)
    bounds = [round(k * (TM // SUBLANES) / 6) for k in range(7)]

    def xm_after_requests(part):
        _gather_issue(pos_ref, ys_hbm, buf, sem, nxt, 1 - i % 2, inline=True, part=(bounds[part], bounds[part + 1]))
        return buf[2].reshape(TM, D).astype(BF16)

    off = (pl.program_id(0) % (SUBLANES // ROPE_TILE_ROWS)) * ROPE_TILE_ROWS

    def row_table(which):
        return jnp.concatenate([jnp.broadcast_to(rt_ref[which, pl.ds(off + j, 1), :], (GRID_W, LANES))
                                for j in range(ROPE_TILE_ROWS)], axis=0)

    tabs = (row_table(0), row_table(1), ct_ref[0, 0], ct_ref[0, 1])
    _rope_store(_dot(xm_after_requests(0), w_ref[:, 0:RET_QK]), tabs, q_ref)
    _rope_store(_dot(xm_after_requests(1), w_ref[:, RET_QK:2 * RET_QK]) * (RET_DK ** -0.5), tabs, k_ref)
    half = RET_V // 2
    for part, (out_ref, col0) in enumerate(((v_ref, 2 * RET_QK), (gg_ref, 2 * RET_QK + RET_V))):
        for s in range(2):
            xm = xm_after_requests(2 + 2 * part + s)
            out_ref[:, s * half:(s + 1) * half] = _dot(
                xm, w_ref[:, col0 + s * half:col0 + (s + 1) * half]).astype(BF16)
    _gather_drain(ys_hbm, buf, sem)


def _inproj1(pos, h1, ys, modv0, modv1, g, w_in, tabs):
    tok = lambda w: pl.BlockSpec((TM, w), lambda i, pos: (i, 0))
    res = lambda shape: pl.BlockSpec(shape, lambda i, pos: (0,) * len(shape), pipeline_mode=pl.Buffered(1))
    mod = pl.BlockSpec((1, 8, D), lambda i, pos: (i // NT_LAT, 0, 0))
    row_tab = pl.BlockSpec((2, SUBLANES, LANES), lambda i, pos: (0, i // (SUBLANES // ROPE_TILE_ROWS), 0))
    col_tab = pl.BlockSpec((1, 2, TM, LANES), lambda i, pos: (i // NT_LAT, 0, 0, 0))
    return pl.pallas_call(
        _inproj1_kernel,
        grid_spec=pltpu.PrefetchScalarGridSpec(
            num_scalar_prefetch=1,
            grid=(NT,),
            in_specs=[tok(D), pl.BlockSpec(memory_space=pl.ANY), mod, mod, res((1, D)), res((D, 2 * RET_QK + 2 * RET_V)),
                      row_tab, col_tab],
            out_specs=[tok(D), tok(RET_QK), tok(RET_QK), tok(RET_V), tok(RET_V)],
            scratch_shapes=[pltpu.VMEM((3, TM // SUBLANES, SUBLANES, D), F32), pltpu.SemaphoreType.DMA((2,))],
        ),
        out_shape=[
            jax.ShapeDtypeStruct((T, D), F32),
            jax.ShapeDtypeStruct((T, RET_QK), BF16),
            jax.ShapeDtypeStruct((T, RET_QK), BF16),
            jax.ShapeDtypeStruct((T, RET_V), BF16),
            jax.ShapeDtypeStruct((T, RET_V), BF16),
        ],
        compiler_params=_params(("arbitrary",)),
        name="inproj1",
    )(pos, h1, _by_sublane_tile(ys), modv0, modv1, g, w_in, *tabs)


ROPE_TILE_ROWS = TM // GRID_W


def _rope_tables():
    n = LANES // 2
    inv_freq = ROPE_BASE ** (-np.arange(n, dtype=np.float64) / n)

    def lines(p):
        ang = p[:, None] * inv_freq[None, :]
        return np.stack([np.concatenate([np.cos(ang), np.cos(ang)], axis=1),
                         np.concatenate([-np.sin(ang), np.sin(ang)], axis=1)])

    ident = np.stack([np.ones((TM, LANES)), np.zeros((TM, LANES))])
    row_tab = np.concatenate([lines(np.arange(ROWS, dtype=np.float64)), ident[:, :SUBLANES]], axis=1)
    col_lat = lines(np.tile(np.arange(GRID_W, dtype=np.float64), ROPE_TILE_ROWS))
    col_tab = np.stack([col_lat, ident])
    return jnp.asarray(row_tab, dtype=F32), jnp.asarray(col_tab, dtype=F32)


RET_NCHUNK = T // RET_CHUNK
RET_CTX_CHUNKS = CTX // RET_CHUNK
RET_LAT_CHUNKS = SEQ // RET_CHUNK
RET_NHD = 2 * RET_HEADS


def _ret_kernel(dec_ref, qf_ref, kf_ref, vf_ref, qb_ref, kb_ref, vb_ref, of_ref, ob_ref,
                state, dm, qd, kd, cd):
    C = RET_CHUNK

    @pl.when(pl.program_id(0) == 0)
    def _():
        state[...] = jnp.zeros_like(state)
        n = lax.broadcasted_iota(jnp.int32, (C, C), 0).astype(F32)
        m = lax.broadcasted_iota(jnp.int32, (C, C), 1).astype(F32)
        for d in range(2):
            for h in range(RET_HEADS):
                idx = d * RET_HEADS + h
                lg = jnp.log1p(-jnp.exp(jnp.full((C, C), dec_ref[d, h], F32)))
                diff = n - m if d == 0 else m - n
                low = diff >= 0.0
                dm[idx] = jnp.where(low, jnp.exp(jnp.where(low, diff, 0.0) * lg), 0.0)
                qpow = n + 1.0 if d == 0 else C - n
                kpow = (C - 1.0) - n if d == 0 else n
                qcol = jnp.exp(qpow * lg)
                kcol = jnp.exp(kpow * lg)
                qd[idx] = jnp.concatenate([qcol] * (RET_DV // C), axis=1)
                kd[idx] = jnp.concatenate([kcol] * (RET_DK // C), axis=1)
                cd[idx] = jnp.exp(C * lg[0:8, :])

    for d, (q_ref, k_ref, v_ref, o_ref) in enumerate(((qf_ref, kf_ref, vf_ref, of_ref),
                                                       (qb_ref, kb_ref, vb_ref, ob_ref))):
        for h in range(RET_HEADS):
            idx = d * RET_HEADS + h
            q = q_ref[:, h * RET_DK:(h + 1) * RET_DK]
            k = k_ref[:, h * RET_DK:(h + 1) * RET_DK]
            v = v_ref[:, h * RET_DV:(h + 1) * RET_DV]
            st = state[idx]
            s = _dot_nt(q, k) * dm[idx]
            o = _dot(s.astype(BF16), v) + qd[idx] * _dot(q, st.astype(BF16))
            o_ref[:, h * RET_DV:(h + 1) * RET_DV] = o.astype(BF16)
            kdk = (k.astype(F32) * kd[idx]).astype(BF16)
            cdv = jnp.concatenate([cd[idx, 0:1, :]] * (RET_DV // C), axis=1)
            state[idx] = st * cdv + _dot_tn(kdk, v)


def _ret_scan(decay, q, k, v):
    def fwd(s):
        return jnp.where(s < RET_CTX_CHUNKS, RET_LAT_CHUNKS + s, s - RET_CTX_CHUNKS)

    def bwd(s):
        return RET_NCHUNK - 1 - s

    def spec(w, f):
        return pl.BlockSpec((RET_CHUNK, w), lambda s: (f(s), 0))

    return pl.pallas_call(
        _ret_kernel,
        grid=(RET_NCHUNK,),
        in_specs=[
            pl.BlockSpec(memory_space=pltpu.SMEM),
            spec(RET_QK, fwd), spec(RET_QK, fwd), spec(RET_V, fwd),
            spec(RET_QK, bwd), spec(RET_QK, bwd), spec(RET_V, bwd),
        ],
        out_specs=[spec(RET_V, fwd), spec(RET_V, bwd)],
        out_shape=[jax.ShapeDtypeStruct((T, RET_V), BF16), jax.ShapeDtypeStruct((T, RET_V), BF16)],
        scratch_shapes=[
            pltpu.VMEM((RET_NHD, RET_DK, RET_DV), F32),
            pltpu.VMEM((RET_NHD, RET_CHUNK, RET_CHUNK), F32),
            pltpu.VMEM((RET_NHD, RET_CHUNK, RET_DV), F32),
            pltpu.VMEM((RET_NHD, RET_CHUNK, RET_DK), F32),
            pltpu.VMEM((RET_NHD, 8, RET_CHUNK), F32),
        ],
        compiler_params=_params(("arbitrary",)),
        name="retention_scan",
    )(decay, q, k, v, q, k, v)


def _outproj1_kernel(of_ref, ob_ref, gg_ref, h_ref, mod_ref, w_ref, g2_ref, rw_ref, rb_ref, tril_ref,
                     h1_ref, xrow_ref, meta_ref, cnt_ref, run_ref):
    @pl.when(pl.program_id(0) == 0)
    def _():
        run_ref[...] = jnp.zeros_like(run_ref)

    y = jnp.zeros((TM, D), F32)
    for hh in range(RET_HEADS):
        cols = slice(hh * RET_DV, (hh + 1) * RET_DV)
        o = of_ref[:, cols].astype(F32) + ob_ref[:, cols].astype(F32)
        on = o * lax.rsqrt(jnp.mean(o * o, axis=-1, keepdims=True) + RMS_EPS)
        g = gg_ref[:, cols].astype(F32)
        z = (g * jax.nn.sigmoid(g)) * on
        y = y + _dot(z.astype(BF16), w_ref[cols, :])
    h1 = h_ref[...] + mod_ref[0, 2:3, :] * y
    h1_ref[...] = h1
    _epilogue(h1, mod_ref, g2_ref, rw_ref, rb_ref, tril_ref, run_ref, xrow_ref, meta_ref, cnt_ref)


def _outproj1(o_f, o_b, gg, h, modv, w_out, g2, rw, rb, tril):
    e_in, e_out, e_shape = _epilogue_specs(NT_LAT)
    tok = lambda w: pl.BlockSpec((TM, w), lambda i: (i, 0))
    return pl.pallas_call(
        _outproj1_kernel,
        grid=(NT_LAT,),
        in_specs=[tok(RET_V), tok(RET_V), tok(RET_V), tok(D), _mod_spec(), _resident((RET_V, D))] + e_in,
        out_specs=e_out,
        out_shape=e_shape,
        scratch_shapes=[pltpu.VMEM((8, LANES), F32)],
        compiler_params=_params(("arbitrary",)),
        name="outproj1",
    )(o_f, o_b, gg, h, modv, w_out, g2, rw, rb, tril)


def _final_kernel(pos_ref, h_ref, ys_hbm, mod_ref, g_ref, o_ref, buf, sem):
    f = _gathered_rows(pos_ref, ys_hbm, buf, sem, request_next=True)
    h = h_ref[...] + mod_ref[0, 5:6, :] * f
    o_ref[...] = h * lax.rsqrt(jnp.mean(h * h, axis=-1, keepdims=True) + RMS_EPS) * g_ref[...]


def _final(pos, h, ys, modv, g):
    tok = pl.BlockSpec((TM, D), lambda i, pos: (i, 0))
    return pl.pallas_call(
        _final_kernel,
        grid_spec=pltpu.PrefetchScalarGridSpec(
            num_scalar_prefetch=1,
            grid=(NT_LAT,),
            in_specs=[tok, pl.BlockSpec(memory_space=pl.ANY),
                      pl.BlockSpec((1, 8, D), lambda i, pos: (0, 0, 0)),
                      pl.BlockSpec((1, D), lambda i, pos: (0, 0))],
            out_specs=tok,
            scratch_shapes=_GATHER_SCRATCH,
        ),
        out_shape=jax.ShapeDtypeStruct((SEQ, D), F32),
        compiler_params=_params(("arbitrary",)),
        name="final_norm",
    )(pos, h, _by_sublane_tile(ys), modv, g)


def _dft_constants():
    c = np.arange(FNET_GD)
    ang = 2.0 * np.pi * np.outer(c, c) / FNET_GD
    d64 = np.zeros((FNET_W, 2 * FNET_W))
    for g in range(FNET_W // FNET_GD):
        sl = slice(g * FNET_GD, (g + 1) * FNET_GD)
        d64[sl, sl] = np.cos(ang)
        d64[sl, FNET_W + g * FNET_GD:FNET_W + (g + 1) * FNET_GD] = -np.sin(ang)
    k2 = np.arange(FFT_N2)[:, None]
    n2 = np.arange(FFT_N2)[None, :]
    ga = np.zeros((FFT_N1, 2 * FFT_N2, 2 * FFT_N2))
    for n1 in range(FFT_N1):
        th = 2.0 * np.pi * ((k2 * (n1 + FFT_N1 * n2)) % SEQ) / SEQ
        cs, sn = np.cos(th), np.sin(th)
        ga[n1] = np.block([[cs, sn], [-sn, cs]])
    k1 = np.arange(FFT_N1)
    ph = 2.0 * np.pi * np.outer(k1, k1) / FFT_N1
    csb = np.concatenate([np.cos(ph), np.sin(ph)], axis=1)
    p = np.arange(CTX)
    pc = 2.0 * np.pi * (np.outer(p, p) % CTX) / CTX
    csc = np.concatenate([np.cos(pc), np.sin(pc)], axis=1)
    as_bf16 = lambda a: jnp.asarray(a, dtype=F32).astype(BF16)
    return as_bf16(d64), as_bf16(ga), as_bf16(csb), as_bf16(csc)


def _mod_table(mod_l):
    t = mod_l.reshape(2, 6, D)
    return jnp.concatenate([t, jnp.zeros((2, 2, D), F32)], axis=1)


def kernel(x, c, ctx, c_ctx, ada_w, ada_b, norm_g, final_norm_g, mixab_w_in, mixab_w_out, na_rpb,
           ret_w_in, ret_w_out, ret_decay, router_w, router_b, moe_w1, moe_w3, moe_w2):
    d64, ga, csb, csc = _dft_constants()
    tril = jnp.asarray(np.tril(np.ones((TM, TM)), -1), dtype=BF16)
    rw = jnp.pad(router_w, ((0, 0), (0, LANES - N_EXPERTS)))
    rb = jnp.pad(router_b, (0, LANES - N_EXPERTS)).reshape(1, LANES)
    w1 = moe_w1.reshape(DEPTH * N_EXPERTS, D, D_EXPERT).astype(BF16)
    w3 = moe_w3.reshape(DEPTH * N_EXPERTS, D, D_EXPERT).astype(BF16)
    w2 = moe_w2.reshape(DEPTH * N_EXPERTS, D_EXPERT, D).astype(BF16)

    mod = _mod_vectors(c, c_ctx, ada_w, ada_b)
    modv0, modv1 = _mod_table(mod[0]), _mod_table(mod[1])

    qkv, br, bi = _inproj0(x[0], ctx[0], modv0, norm_g[0, 0].reshape(1, D), mixab_w_in[0].astype(BF16), d64)
    a_lat = _fft_latent(br, bi, ga, csb)
    ao_ctx = _ctx_mixer(qkv, br, bi, csc)
    o_lat = _na_attention(qkv, _na_bias_rows(na_rpb[0]))
    h1, xrow, meta, cnt = _outproj0(a_lat, o_lat, ao_ctx, x[0], ctx[0], modv0, mixab_w_out[0].astype(BF16),
                                    norm_g[0, 1].reshape(1, D), rw, rb, tril)
    n_tiles0 = NT + N_REAL_BUCKETS
    pos0, inv0, e_lo, e_hi, n_used = _route_plan(meta, cnt, n_tiles0, 0)
    ys0 = _moe(e_lo, e_hi, n_used, inv0, xrow, w1, w3, w2, n_tiles0)

    h2, q, k, v, gg = _inproj1(pos0, h1, ys0, modv0, modv1, norm_g[1, 0].reshape(1, D),
                               ret_w_in[0].astype(BF16), _rope_tables())
    o_f, o_b = _ret_scan(ret_decay[0].astype(F32), q, k, v)
    h3, xrow1, meta1, cnt1 = _outproj1(o_f, o_b, gg, h2, modv1, ret_w_out[0].astype(BF16),
                                       norm_g[1, 1].reshape(1, D), rw, rb, tril)
    n_tiles1 = NT_LAT + N_REAL_BUCKETS
    pos1, inv1, e_lo1, e_hi1, n_used1 = _route_plan(meta1, cnt1, n_tiles1, 1)
    ys1 = _moe(e_lo1, e_hi1, n_used1, inv1, xrow1, w1, w3, w2, n_tiles1)
    out = _final(pos1, h3, ys1, modv1, final_norm_g.reshape(1, D))
    return out[None]
```

```python
import functools
import math

import numpy as np
import jax
import jax.numpy as jnp
from jax import lax
from jax.experimental import pallas as pl
from jax.experimental.pallas import tpu as pltpu

F32 = jnp.float32
BF16 = jnp.bfloat16

D = 1024
SEQ = 16384
CTX = 256
T = SEQ + CTX
DEPTH = 2
GRID_W = 64
ROWS = SEQ // GRID_W
RMS_EPS = 1e-6
ROPE_BASE = 10000.0

FNET_W = 256
FNET_GD = 64
NA_HEADS = 12
NA_HD = 64
NA_W = NA_HEADS * NA_HD
NA_KH = 8
NA_KW = 16
AB_IN = FNET_W + 3 * NA_W
QKV_W = 3 * NA_W

RET_HEADS = 4
RET_DK = 256
RET_DV = 512
RET_QK = RET_HEADS * RET_DK
RET_V = RET_HEADS * RET_DV
RET_CHUNK = 128

N_EXPERTS = 16
EPG = 4
D_EXPERT = 512

LANES = 128
TM = 256
NT = T // TM
NT_LAT = SEQ // TM
TMM = TM
N_BUCKET_IDS = 64
N_REAL_BUCKETS = 24
XROW_W = D + LANES
FFT_N1 = 128
FFT_N2 = 128
NEG_BIG = -1e30
SUBLANES = 8

VMEM_LIMIT = 56 * 1024 * 1024


def _params(sem, vmem=VMEM_LIMIT):
    return pltpu.CompilerParams(dimension_semantics=sem, vmem_limit_bytes=vmem)


def _dot(a, b):
    return jnp.dot(a, b, preferred_element_type=F32)


def _dot_nt(a, b):
    return lax.dot_general(a, b, (((1,), (1,)), ((), ())), preferred_element_type=F32)


def _dot_tn(a, b):
    return lax.dot_general(a, b, (((0,), (0,)), ((), ())), preferred_element_type=F32)


def _rms_mod(x, g, sh, sc):
    ms = jnp.mean(x * x, axis=-1, keepdims=True)
    y = x * lax.rsqrt(ms + RMS_EPS) * g
    return y * (1.0 + sc) + sh


def _resident(shape):
    nd = len(shape)
    return pl.BlockSpec(shape, lambda *_: (0,) * nd, pipeline_mode=pl.Buffered(1))


MOD_TN = 768


def _mod_kernel(cs_ref, w_ref, b_ref, o_ref):
    cs = cs_ref[...]
    s = cs * jax.nn.sigmoid(cs)
    w = w_ref[0]
    r0 = jnp.sum(s[:, 0:1] * w, axis=0, keepdims=True)
    r1 = jnp.sum(s[:, 1:2] * w, axis=0, keepdims=True)
    o_ref[0] = jnp.concatenate([r0, r1], axis=0) + b_ref[0]


def _mod_vectors(c, c_ctx, ada_w, ada_b):
    cs = jnp.stack([c[0], c_ctx], axis=1)
    n = 6 * D
    return pl.pallas_call(
        _mod_kernel,
        grid=(DEPTH, n // MOD_TN),
        in_specs=[
            pl.BlockSpec((D, 2), lambda l, j: (0, 0)),
            pl.BlockSpec((1, D, MOD_TN), lambda l, j: (l, 0, j)),
            pl.BlockSpec((1, 1, MOD_TN), lambda l, j: (l, 0, j)),
        ],
        out_specs=pl.BlockSpec((1, 2, MOD_TN), lambda l, j: (l, 0, j)),
        out_shape=jax.ShapeDtypeStruct((DEPTH, 2, n), F32),
        compiler_params=_params(("arbitrary", "arbitrary")),
        name="mod_vectors",
    )(cs, ada_w, ada_b.reshape(DEPTH, 1, n))


def _mod_spec():
    return pl.BlockSpec((1, 8, D), lambda i, *_: (i // NT_LAT, 0, 0))


def _stream_tile(x_ref, ctx_ref):
    return jnp.where(pl.program_id(0) >= NT_LAT, ctx_ref[...], x_ref[...])


def _stream_specs():
    return [pl.BlockSpec((TM, D), lambda i: (jnp.minimum(i, NT_LAT - 1), 0)),
            pl.BlockSpec((CTX, D), lambda i: (0, 0))]


def _inproj0_kernel(x_ref, ctx_ref, mod_ref, g_ref, w_ref, d64_ref, qkv_ref, br_ref, bi_ref):
    h = _stream_tile(x_ref, ctx_ref)
    xm = _rms_mod(h, g_ref[...], mod_ref[0, 0:1, :], mod_ref[0, 1:2, :]).astype(BF16)
    p = _dot(xm, w_ref[...])
    qkv_ref[...] = p[:, FNET_W:].astype(BF16)
    b = _dot(p[:, :FNET_W].astype(BF16), d64_ref[...])
    br_ref[...] = b[:, :FNET_W].astype(BF16)
    bi_ref[...] = b[:, FNET_W:].astype(BF16)


def _inproj0(x, ctx, modv, g, w_in, d64):
    return pl.pallas_call(
        _inproj0_kernel,
        grid=(NT,),
        in_specs=_stream_specs() + [
            _mod_spec(),
            _resident((1, D)),
            _resident((D, AB_IN)),
            _resident((FNET_W, 2 * FNET_W)),
        ],
        out_specs=[
            pl.BlockSpec((TM, QKV_W), lambda i: (i, 0)),
            pl.BlockSpec((TM, FNET_W), lambda i: (i, 0)),
            pl.BlockSpec((TM, FNET_W), lambda i: (i, 0)),
        ],
        out_shape=[
            jax.ShapeDtypeStruct((T, QKV_W), BF16),
            jax.ShapeDtypeStruct((T, FNET_W), BF16),
            jax.ShapeDtypeStruct((T, FNET_W), BF16),
        ],
        compiler_params=_params(("arbitrary",)),
        name="inproj0",
    )(x, ctx, modv, g, w_in, d64)


FFT_A_NB = 8
FFT_B_TN = 4096


def _fft_a_kernel(br_ref, bi_ref, ga_ref, v_ref):
    for u in range(FFT_A_NB):
        cols = slice(u * FNET_W, (u + 1) * FNET_W)
        z = _dot(ga_ref[u, :, 0:FFT_N2], br_ref[:, cols]) + _dot(ga_ref[u, :, FFT_N2:], bi_ref[:, cols])
        v_ref[0, u] = z[:FFT_N2].astype(BF16)
        v_ref[1, u] = z[FFT_N2:].astype(BF16)


def _fft_b_kernel(cs_ref, v_ref, o_ref):
    o_ref[...] = _dot(cs_ref[...], v_ref[...]) * (1.0 / math.sqrt(SEQ * FNET_GD))


def _fft_latent(br, bi, ga, csb):
    width = FFT_N1 * FNET_W
    br2 = br.reshape(T // FFT_N1, width)
    bi2 = bi.reshape(T // FFT_N1, width)
    v = pl.pallas_call(
        _fft_a_kernel,
        grid=(FFT_N1 // FFT_A_NB,),
        in_specs=[
            pl.BlockSpec((FFT_N2, FFT_A_NB * FNET_W), lambda j: (0, j)),
            pl.BlockSpec((FFT_N2, FFT_A_NB * FNET_W), lambda j: (0, j)),
            pl.BlockSpec((FFT_A_NB, 2 * FFT_N2, 2 * FFT_N2), lambda j: (j, 0, 0)),
        ],
        out_specs=pl.BlockSpec((2, FFT_A_NB, FFT_N2, FNET_W), lambda j: (0, j, 0, 0)),
        out_shape=jax.ShapeDtypeStruct((2, FFT_N1, FFT_N2, FNET_W), BF16),
        compiler_params=_params(("arbitrary",)),
        name="fft_stage_a",
    )(br2, bi2, ga)
    v2 = v.reshape(2 * FFT_N1, FFT_N2 * FNET_W)
    x = pl.pallas_call(
        _fft_b_kernel,
        grid=(FFT_N2 * FNET_W // FFT_B_TN,),
        in_specs=[
            _resident((FFT_N1, 2 * FFT_N1)),
            pl.BlockSpec((2 * FFT_N1, FFT_B_TN), lambda j: (0, j)),
        ],
        out_specs=pl.BlockSpec((FFT_N1, FFT_B_TN), lambda j: (0, j)),
        out_shape=jax.ShapeDtypeStruct((FFT_N1, FFT_N2 * FNET_W), F32),
        compiler_params=_params(("arbitrary",)),
        name="fft_stage_b",
    )(csb, v2)
    return x.reshape(SEQ, FNET_W)


def _head_pair_masks():
    lane = lax.broadcasted_iota(jnp.int32, (1, LANES), 1)
    return lane < NA_HD


def _ctx_kernel(qkv_ref, br_ref, bi_ref, cs_ref, o_ref):
    a = _dot(cs_ref[:, 0:CTX], br_ref[...]) + _dot(cs_ref[:, CTX:], bi_ref[...])
    o_ref[:, 0:FNET_W] = (a * (1.0 / math.sqrt(CTX * FNET_GD))).astype(BF16)
    m0 = _head_pair_masks()
    for hp in range(NA_HEADS // 2):
        q = qkv_ref[:, hp * LANES:(hp + 1) * LANES]
        k = qkv_ref[:, NA_W + hp * LANES:NA_W + (hp + 1) * LANES]
        v = qkv_ref[:, 2 * NA_W + hp * LANES:2 * NA_W + (hp + 1) * LANES]
        outs = []
        for a_ in range(2):
            qa = jnp.where(m0 if a_ == 0 else jnp.logical_not(m0), q, jnp.zeros_like(q))
            s = _dot_nt(qa, k) * (NA_HD ** -0.5)
            p = jnp.exp(s - jnp.max(s, axis=-1, keepdims=True))
            l = jnp.sum(p, axis=-1, keepdims=True)
            outs.append(_dot(p.astype(BF16), v) / l)
        o_ref[:, FNET_W + hp * LANES:FNET_W + (hp + 1) * LANES] = jnp.where(m0, outs[0], outs[1]).astype(BF16)


def _ctx_mixer(qkv, br, bi, csc):
    return pl.pallas_call(
        _ctx_kernel,
        grid=(1,),
        in_specs=[
            pl.BlockSpec((CTX, QKV_W), lambda i: (SEQ // CTX, 0)),
            pl.BlockSpec((CTX, FNET_W), lambda i: (SEQ // CTX, 0)),
            pl.BlockSpec((CTX, FNET_W), lambda i: (SEQ // CTX, 0)),
            pl.BlockSpec((CTX, 2 * CTX), lambda i: (0, 0)),
        ],
        out_specs=pl.BlockSpec((CTX, D), lambda i: (0, 0)),
        out_shape=jax.ShapeDtypeStruct((CTX, D), BF16),
        compiler_params=_params(("arbitrary",)),
        name="ctx_mixer",
    )(qkv, br, bi, csc)


NA_RB = 8
NA_WIN = NA_KH * GRID_W


NA_NK = NA_WIN + CTX
NA_SUB = 16


def _na_kernel(q_ref, k_ref, v_ref, kc_ref, vc_ref, bias_ref, o_ref, s_scr, p_scr, l_scr):
    b = pl.program_id(1)
    m0 = _head_pair_masks()
    nm0 = jnp.logical_not(m0)
    scale = NA_HD ** -0.5

    def window(i):
        r = b * NA_RB + i
        rs = jnp.clip(r - NA_KH // 2, 0, ROWS - NA_KH)
        return pl.multiple_of(rs * GRID_W, GRID_W), rs - r + (NA_KH - 1)

    def scores(i):
        start, _ = window(i)
        q = q_ref[i * GRID_W:(i + 1) * GRID_W, :] * scale
        qs = jnp.concatenate([jnp.where(m0, q, jnp.zeros_like(q)), jnp.where(nm0, q, jnp.zeros_like(q))], axis=0)
        s_scr[i, :, 0:NA_WIN] = _dot_nt(qs, k_ref[pl.ds(start, NA_WIN), :])
        s_scr[i, :, NA_WIN:NA_NK] = _dot_nt(qs, kc_ref[...])

    def softmax(i):
        _, e = window(i)
        for g in range(2 * GRID_W // NA_SUB):
            rows = slice(g * NA_SUB, (g + 1) * NA_SUB)
            a, c0 = divmod(g * NA_SUB, GRID_W)
            s1 = s_scr[i, rows, 0:NA_WIN] + bias_ref[a, e, c0:c0 + NA_SUB, :]
            s2 = s_scr[i, rows, NA_WIN:NA_NK]
            mx = jnp.maximum(jnp.max(s1, axis=-1, keepdims=True), jnp.max(s2, axis=-1, keepdims=True))
            p1 = jnp.exp(s1 - mx)
            p2 = jnp.exp(s2 - mx)
            l = jnp.sum(p1, axis=-1, keepdims=True) + jnp.sum(p2, axis=-1, keepdims=True)
            p_scr[i, rows, 0:NA_WIN] = p1.astype(BF16)
            p_scr[i, rows, NA_WIN:NA_NK] = p2.astype(BF16)
            l_scr[i, rows, :] = jnp.broadcast_to(l, (NA_SUB, LANES))

    def values(i):
        start, _ = window(i)
        o = _dot(p_scr[i, :, 0:NA_WIN], v_ref[pl.ds(start, NA_WIN), :]) + _dot(p_scr[i, :, NA_WIN:NA_NK], vc_ref[...])
        o = o / l_scr[i]
        o_ref[i * GRID_W:(i + 1) * GRID_W, :] = jnp.where(m0, o[0:GRID_W], o[GRID_W:]).astype(BF16)

    scores(0)
    for i in range(NA_RB):
        if i + 1 < NA_RB:
            scores(i + 1)
        softmax(i)
        values(i)


def _na_attention(qkv, bias):
    nq = NA_RB * GRID_W
    nhp = NA_HEADS // 2
    return pl.pallas_call(
        _na_kernel,
        grid=(nhp, ROWS // NA_RB),
        in_specs=[
            pl.BlockSpec((nq, LANES), lambda hp, b: (b, hp)),
            pl.BlockSpec((SEQ, LANES), lambda hp, b: (0, nhp + hp)),
            pl.BlockSpec((SEQ, LANES), lambda hp, b: (0, 2 * nhp + hp)),
            pl.BlockSpec((CTX, LANES), lambda hp, b: (SEQ // CTX, nhp + hp)),
            pl.BlockSpec((CTX, LANES), lambda hp, b: (SEQ // CTX, 2 * nhp + hp)),
            pl.BlockSpec((2, NA_KH, GRID_W, NA_WIN), lambda hp, b: (hp, 0, 0, 0)),
        ],
        out_specs=pl.BlockSpec((nq, LANES), lambda hp, b: (b, hp)),
        out_shape=jax.ShapeDtypeStruct((SEQ, NA_W), BF16),
        scratch_shapes=[
            pltpu.VMEM((NA_RB, 2 * GRID_W, NA_NK), F32),
            pltpu.VMEM((NA_RB, 2 * GRID_W, NA_NK), BF16),
            pltpu.VMEM((NA_RB, 2 * GRID_W, LANES), F32),
        ],
        compiler_params=_params(("arbitrary", "arbitrary")),
        name="na_attention",
    )(qkv, qkv, qkv, qkv, qkv, bias)


def _na_bias_rows(rpb):
    c = np.arange(GRID_W)[:, None]
    kc = np.arange(GRID_W)[None, :]
    ws = np.clip(c - NA_KW // 2, 0, GRID_W - NA_KW)
    col_ok = (kc >= ws) & (kc < ws + NA_KW)
    dc = np.clip(kc - c + NA_KW - 1, 0, 2 * NA_KW - 2)
    pick = (dc[None] == np.arange(2 * NA_KW - 1)[:, None, None]).astype(np.float32)
    full = jnp.einsum("hrd,dck->hrck", rpb.astype(F32), jnp.asarray(pick), precision=lax.Precision.HIGHEST)
    full = jnp.where(jnp.asarray(col_ok)[None, None], full, NEG_BIG)
    rows = [jnp.transpose(full[:, e:e + NA_KH], (0, 2, 1, 3)).reshape(NA_HEADS, GRID_W, NA_WIN)
            for e in range(NA_KH)]
    return jnp.stack(rows, axis=1)


def _route_stage(xm2, valid, rw_ref, rb_ref, triu_ref, run_ref, xrow_ref, meta_ref, cnt_ref):
    xrow_ref[:, 0:D] = xm2
    xh = xm2.astype(BF16)
    xl = (xm2 - xh.astype(F32)).astype(BF16)
    w = rw_ref[...]
    wh = w.astype(BF16)
    wl = (w - wh.astype(F32)).astype(BF16)
    logits = _dot(xh, wh) + (_dot(xl, wh) + _dot(xh, wl))
    scores = jax.nn.sigmoid(logits.T[0:N_EXPERTS, :])
    biased = scores + rb_ref[...]
    s = [scores[e:e + 1, :] for e in range(N_EXPERTS)]
    b = [biased[e:e + 1, :] for e in range(N_EXPERTS)]
    n_groups = N_EXPERTS // EPG
    top2, group_score = [], []
    for g in range(n_groups):
        m = b[EPG * g:EPG * (g + 1)]
        total = None
        for k in range(EPG):
            rank = None
            for j in range(EPG):
                if j != k:
                    beats = jnp.where((m[j] >= m[k]) if j < k else (m[j] > m[k]), 1.0, 0.0)
                    rank = beats if rank is None else rank + beats
            top2.append(rank < 2.0)
            term = jnp.where(top2[-1], m[k], 0.0)
            total = term if total is None else total + term
        group_score.append(total)
    best, g_sel = group_score[0], jnp.zeros((1, TM), jnp.int32)
    for g in range(1, n_groups):
        better = group_score[g] > best
        g_sel = jnp.where(better, g, g_sel)
        best = jnp.where(better, group_score[g], best)
    sel = [top2[e] & (g_sel == e // EPG) for e in range(N_EXPERTS)]
    w_sum = None
    for e in range(N_EXPERTS):
        term = jnp.where(sel[e], s[e], 0.0)
        w_sum = term if w_sum is None else w_sum + term
    zero = jnp.zeros((1, TM), F32)
    w_lo, w_hi, bid = zero, zero, zero
    for g in range(n_groups):
        seen = None
        for k in range(EPG):
            e = EPG * g + k
            if seen is None:
                w_lo = w_lo + jnp.where(sel[e], s[e], 0.0)
                seen = sel[e]
            else:
                w_lo = w_lo + jnp.where(sel[e] & jnp.logical_not(seen), s[e], 0.0)
                w_hi = w_hi + jnp.where(sel[e] & seen, s[e], 0.0)
                seen = seen | sel[e]
            bid = bid + jnp.where(sel[e], float((1 << k) + 8 * g), 0.0)
    w_lo = w_lo / w_sum
    w_hi = w_hi / w_sum
    bucket = lax.broadcasted_iota(jnp.int32, (N_BUCKET_IDS, TM), 0).astype(F32)
    onehot = bucket == bid
    ones = jnp.where(onehot & valid, 1.0, 0.0)
    earlier = _dot(ones.astype(BF16), triu_ref[...])
    run = run_ref[...]
    run_t = jnp.concatenate([run] * (TM // LANES), axis=1)
    rank_row = jnp.sum(jnp.where(onehot, earlier + run_t, 0.0), axis=0, keepdims=True)
    new_run = run + jnp.sum(ones, axis=1, keepdims=True)
    run_ref[...] = new_run
    cnt_ref[...] = new_run
    meta_rows = jnp.concatenate([w_lo, w_hi, bid, rank_row, jnp.zeros((SUBLANES - 4, TM), F32)], axis=0)
    meta_ref[...] = meta_rows
    padded = jnp.concatenate([meta_rows, jnp.zeros((LANES - SUBLANES, TM), F32)], axis=0)
    xrow_ref[:, D:XROW_W] = padded.T


def _epilogue_specs(ntiles):
    in_specs = [
        _resident((1, D)),
        _resident((D, LANES)),
        _resident((N_EXPERTS, TM)),
        _resident((TM, TM)),
    ]
    out_specs = [
        pl.BlockSpec((TM, D), lambda i: (jnp.minimum(i, ntiles - 1), 0)),
        pl.BlockSpec((TM, XROW_W), lambda i: (jnp.maximum(i - 1, 0), 0)),
        pl.BlockSpec((SUBLANES, TM), lambda i: (0, jnp.maximum(i - 1, 0))),
        pl.BlockSpec((N_BUCKET_IDS, LANES), lambda i: (0, 0)),
    ]
    n = ntiles * TM
    out_shape = [
        jax.ShapeDtypeStruct((n, D), F32),
        jax.ShapeDtypeStruct((n, XROW_W), F32),
        jax.ShapeDtypeStruct((SUBLANES, n), F32),
        jax.ShapeDtypeStruct((N_BUCKET_IDS, LANES), F32),
    ]
    return in_specs, out_specs, out_shape


_EPILOGUE_SCRATCH = [pltpu.VMEM((N_BUCKET_IDS, LANES), F32), pltpu.VMEM((2, TM, D), F32)]


def _two_stage(residual_tile, mod_ref, g2_ref, rw_ref, rb_ref, triu_ref,
               h1_ref, xrow_ref, meta_ref, cnt_ref, run_ref, stash):
    i = pl.program_id(0)

    @pl.when(i == 0)
    def _():
        run_ref[...] = jnp.zeros_like(run_ref)
        stash[1] = jnp.zeros((TM, D), F32)

    _route_stage(stash[1 - i % 2], i >= 1, rw_ref, rb_ref, triu_ref, run_ref, xrow_ref, meta_ref, cnt_ref)
    h1 = residual_tile()
    h1_ref[...] = h1
    stash[i % 2] = _rms_mod(h1, g2_ref[...], mod_ref[0, 3:4, :], mod_ref[0, 4:5, :])


def _outproj0_kernel(a_ref, o_ref, aoc_ref, x_ref, ctx_ref, mod_ref, w_ref, *rest):
    def residual_tile():
        is_ctx = pl.program_id(0) >= NT_LAT
        a = jnp.where(is_ctx, aoc_ref[:, 0:FNET_W], a_ref[...].astype(BF16))
        o = jnp.where(is_ctx, aoc_ref[:, FNET_W:], o_ref[...])
        y = _dot(a, w_ref[0:FNET_W, :]) + _dot(o, w_ref[FNET_W:, :])
        return _stream_tile(x_ref, ctx_ref) + mod_ref[0, 2:3, :] * y

    _two_stage(residual_tile, mod_ref, *rest)


def _outproj0(a_lat, o_lat, ao_ctx, x, ctx, modv, w_out, g2, rw, rb, triu):
    e_in, e_out, e_shape = _epilogue_specs(NT)
    lat = lambda w: pl.BlockSpec((TM, w), lambda i: (jnp.minimum(i, NT_LAT - 1), 0))
    return pl.pallas_call(
        _outproj0_kernel,
        grid=(NT + 1,),
        in_specs=[lat(FNET_W), lat(NA_W), pl.BlockSpec((CTX, D), lambda i: (0, 0))] + _stream_specs() + [
            _mod_spec(),
            _resident((D, D)),
        ] + e_in,
        out_specs=e_out,
        out_shape=e_shape,
        scratch_shapes=_EPILOGUE_SCRATCH,
        compiler_params=_params(("arbitrary",)),
        name="outproj0",
    )(a_lat, o_lat, ao_ctx, x, ctx, modv, w_out, g2, rw, rb, triu)


def _bucket_tables():
    lo = np.zeros(N_BUCKET_IDS, np.int32)
    hi = np.zeros(N_BUCKET_IDS, np.int32)
    for g in range(N_EXPERTS // EPG):
        for a in range(EPG):
            for b in range(a + 1, EPG):
                i = 16 * g + (1 << a) + (1 << b)
                lo[i] = EPG * g + a
                hi[i] = EPG * g + b
    return lo, hi


def _slot_kernel(meta_ref, starts_ref, o_ref):
    n = meta_ref.shape[1]
    bucket = lax.broadcasted_iota(jnp.int32, (N_BUCKET_IDS, n), 0).astype(F32)
    first = jnp.sum(jnp.where(bucket == meta_ref[2:3, :], starts_ref[:, 0:1], 0.0), axis=0, keepdims=True)
    o_ref[...] = jnp.broadcast_to((first + meta_ref[3:4, :]).astype(jnp.int32), (SUBLANES, n))


def _token_slots(meta, first_row):
    n = meta.shape[1]
    out = pl.pallas_call(
        _slot_kernel,
        grid=(1,),
        in_specs=[pl.BlockSpec((SUBLANES, n), lambda i: (0, 0)),
                  pl.BlockSpec((N_BUCKET_IDS, LANES), lambda i: (0, 0))],
        out_specs=pl.BlockSpec((SUBLANES, n), lambda i: (0, 0)),
        out_shape=jax.ShapeDtypeStruct((SUBLANES, n), jnp.int32),
        compiler_params=_params(("arbitrary",)),
        name="token_slots",
    )(meta, first_row)
    return out[0]


def _route_plan(meta, cnt, n_tiles, layer):
    counts = cnt[:, 0].astype(jnp.int32)
    nt_b = (counts + TMM - 1) // TMM
    ends = jnp.cumsum(nt_b)
    starts = ends - nt_b
    first_row = jnp.broadcast_to((starts * TMM).astype(F32)[:, None], (N_BUCKET_IDS, LANES))
    pos = _token_slots(meta, first_row)
    n_used = ends[-1]
    tile = jnp.minimum(jnp.arange(n_tiles, dtype=jnp.int32), n_used - 1)
    tile_b = jnp.sum((ends[None, :] <= tile[:, None]).astype(jnp.int32), axis=1)
    lo, hi = _bucket_tables()
    e_lo = jnp.asarray(lo)[tile_b] + layer * N_EXPERTS
    e_hi = jnp.asarray(hi)[tile_b] + layer * N_EXPERTS
    return pos, e_lo.astype(jnp.int32), e_hi.astype(jnp.int32), n_used.reshape(1).astype(jnp.int32)


def _dispatch_kernel(pos_ref, x_ref, xs_in_ref, xs_ref, sem):
    del xs_in_ref
    base = pl.program_id(0) * TM

    def issue(j, carry):
        for u in range(SUBLANES):
            p = pos_ref[base + j * SUBLANES + u]
            pltpu.make_async_copy(x_ref.at[j, pl.ds(u, 1), :],
                                  xs_ref.at[p >> 3, pl.ds(p & (SUBLANES - 1), 1), :], sem).start()
        return carry

    lax.fori_loop(0, TM // SUBLANES, issue, 0)
    pltpu.make_async_copy(x_ref, xs_ref.at[pl.ds(0, TM // SUBLANES)], sem).wait()


def _dispatch(pos, xrow, n_tiles):
    ntok = xrow.shape[0]
    xs0 = jnp.zeros((n_tiles * TMM // SUBLANES, SUBLANES, XROW_W), F32)
    xs = pl.pallas_call(
        _dispatch_kernel,
        grid_spec=pltpu.PrefetchScalarGridSpec(
            num_scalar_prefetch=1,
            grid=(ntok // TM,),
            in_specs=[
                pl.BlockSpec((TM // SUBLANES, SUBLANES, XROW_W), lambda i, pos: (i, 0, 0)),
                pl.BlockSpec(memory_space=pl.ANY),
            ],
            out_specs=pl.BlockSpec(memory_space=pl.ANY),
            scratch_shapes=[pltpu.SemaphoreType.DMA(())],
        ),
        out_shape=jax.ShapeDtypeStruct(xs0.shape, F32),
        input_output_aliases={2: 0},
        compiler_params=_params(("arbitrary",)),
        name="moe_dispatch",
    )(pos, _by_sublane_tile(xrow), xs0)
    return xs.reshape(n_tiles * TMM, XROW_W)


def _moe_kernel(elo_ref, ehi_ref, nu_ref, xs_ref, w1a, w3a, w2a, w1b, w3b, w2b, ys_ref):
    j = pl.program_id(0)

    @pl.when(j < nu_ref[0])
    def _():
        x = xs_ref[:, 0:D].astype(BF16)
        g_lo = xs_ref[:, D:D + 1]
        g_hi = xs_ref[:, D + 1:D + 2]

        def expert(w1, w3, w2):
            a = _dot(x, w1[0])
            hid = (a * jax.nn.sigmoid(a)) * _dot(x, w3[0])
            return _dot(hid.astype(BF16), w2[0])

        ys_ref[...] = g_lo * expert(w1a, w3a, w2a) + g_hi * expert(w1b, w3b, w2b)

    @pl.when(j >= nu_ref[0])
    def _():
        ys_ref[...] = jnp.zeros_like(ys_ref)


def _moe(e_lo, e_hi, n_used, xs, w1, w3, w2, n_tiles):
    def wspec(shape, which):
        if which == 0:
            return pl.BlockSpec(shape, lambda j, lo, hi, nu: (lo[j], 0, 0))
        return pl.BlockSpec(shape, lambda j, lo, hi, nu: (hi[j], 0, 0))

    s13 = (1, D, D_EXPERT)
    s2 = (1, D_EXPERT, D)
    return pl.pallas_call(
        _moe_kernel,
        grid_spec=pltpu.PrefetchScalarGridSpec(
            num_scalar_prefetch=3,
            grid=(n_tiles,),
            in_specs=[
                pl.BlockSpec((TMM, XROW_W), lambda j, lo, hi, nu: (j, 0)),
                wspec(s13, 0), wspec(s13, 0), wspec(s2, 0),
                wspec(s13, 1), wspec(s13, 1), wspec(s2, 1),
            ],
            out_specs=pl.BlockSpec((TMM, D), lambda j, lo, hi, nu: (j, 0)),
        ),
        out_shape=jax.ShapeDtypeStruct((n_tiles * TMM, D), F32),
        compiler_params=_params(("arbitrary",)),
        name="moe_experts",
    )(e_lo, e_hi, n_used, xs, w1, w3, w2, w1, w3, w2)


def _gather_issue(pos_ref, ys_hbm, buf, sem, tile, slot, inline, part=(0, TM // SUBLANES)):
    base = tile * TM

    def issue(j, carry):
        for u in range(SUBLANES):
            p = pos_ref[base + j * SUBLANES + u]
            pltpu.make_async_copy(ys_hbm.at[p >> 3, pl.ds(p & (SUBLANES - 1), 1), :],
                                  buf.at[slot, j, pl.ds(u, 1), :], sem.at[slot]).start()
        return carry

    if inline:
        for j in range(*part):
            issue(j, 0)
    else:
        lax.fori_loop(0, TM // SUBLANES, issue, 0)


def _gather_wait(ys_hbm, buf, sem, slot):
    pltpu.make_async_copy(ys_hbm.at[pl.ds(0, TM // SUBLANES)], buf.at[slot], sem.at[slot]).wait()


def _gathered_rows(pos_ref, ys_hbm, buf, sem, request_next):
    i = pl.program_id(0)
    slot = i % 2

    @pl.when(i == 0)
    def _():
        _gather_issue(pos_ref, ys_hbm, buf, sem, 0, 0, inline=False)

    if request_next:
        @pl.when(i + 1 < pl.num_programs(0))
        def _():
            _gather_issue(pos_ref, ys_hbm, buf, sem, i + 1, 1 - slot, inline=False)

    _gather_wait(ys_hbm, buf, sem, slot)
    return buf[slot].reshape(TM, D)


def _gather_drain(ys_hbm, buf, sem):
    i = pl.program_id(0)

    @pl.when(i + 1 == pl.num_programs(0))
    def _():
        _gather_wait(ys_hbm, buf, sem, 1 - i % 2)


_GATHER_SCRATCH = [pltpu.VMEM((2, TM // SUBLANES, SUBLANES, D), F32), pltpu.SemaphoreType.DMA((2,))]


def _by_sublane_tile(a):
    return a.reshape(a.shape[0] // SUBLANES, SUBLANES, a.shape[1])


def _rope_store(x, tabs, out_ref):
    cr, sr, cc, sc = tabs
    for ch in range(RET_QK // LANES):
        xc = x[:, ch * LANES:(ch + 1) * LANES]
        xr = pltpu.roll(xc, LANES // 2, axis=1)
        y = xc * cr + xr * sr if ch % 2 == 0 else xc * cc + xr * sc
        out_ref[:, ch * LANES:(ch + 1) * LANES] = y.astype(BF16)


def _inproj1_kernel(pos_ref, h_ref, ys_hbm, mod0_ref, mod1_ref, g_ref, w_ref, rt_ref, ct_ref,
                    h2_ref, q_ref, k_ref, v_ref, gg_ref, buf, sem):
    f = _gathered_rows(pos_ref, ys_hbm, buf, sem, request_next=False)
    h2 = h_ref[...] + mod0_ref[0, 5:6, :] * f
    h2_ref[...] = h2
    buf[2] = _rms_mod(h2, g_ref[...], mod1_ref[0, 0:1, :], mod1_ref[0, 1:2, :]).reshape(TM // SUBLANES, SUBLANES, D)
    i = pl.program_id(0)
    nxt = jnp.where(i + 1 < pl.num_programs(0), i + 1, 0)
    bounds = [round(k * (TM // SUBLANES) / 6) for k in range(7)]

    def xm_after_requests(part):
        _gather_issue(pos_ref, ys_hbm, buf, sem, nxt, 1 - i % 2, inline=True, part=(bounds[part], bounds[part + 1]))
        return buf[2].reshape(TM, D).astype(BF16)

    off = (pl.program_id(0) % (SUBLANES // ROPE_TILE_ROWS)) * ROPE_TILE_ROWS

    def row_table(which):
        return jnp.concatenate([jnp.broadcast_to(rt_ref[which, pl.ds(off + j, 1), :], (GRID_W, LANES))
                                for j in range(ROPE_TILE_ROWS)], axis=0)

    tabs = (row_table(0), row_table(1), ct_ref[0, 0], ct_ref[0, 1])
    _rope_store(_dot(xm_after_requests(0), w_ref[:, 0:RET_QK]), tabs, q_ref)
    _rope_store(_dot(xm_after_requests(1), w_ref[:, RET_QK:2 * RET_QK]) * (RET_DK ** -0.5), tabs, k_ref)
    half = RET_V // 2
    for part, (out_ref, col0) in enumerate(((v_ref, 2 * RET_QK), (gg_ref, 2 * RET_QK + RET_V))):
        for s in range(2):
            xm = xm_after_requests(2 + 2 * part + s)
            out_ref[:, s * half:(s + 1) * half] = _dot(
                xm, w_ref[:, col0 + s * half:col0 + (s + 1) * half]).astype(BF16)
    _gather_drain(ys_hbm, buf, sem)


def _inproj1(pos, h1, ys, modv0, modv1, g, w_in, tabs):
    tok = lambda w: pl.BlockSpec((TM, w), lambda i, pos: (i, 0))
    res = lambda shape: pl.BlockSpec(shape, lambda i, pos: (0,) * len(shape), pipeline_mode=pl.Buffered(1))
    mod = pl.BlockSpec((1, 8, D), lambda i, pos: (i // NT_LAT, 0, 0))
    row_tab = pl.BlockSpec((2, SUBLANES, LANES), lambda i, pos: (0, i // (SUBLANES // ROPE_TILE_ROWS), 0))
    col_tab = pl.BlockSpec((1, 2, TM, LANES), lambda i, pos: (i // NT_LAT, 0, 0, 0))
    return pl.pallas_call(
        _inproj1_kernel,
        grid_spec=pltpu.PrefetchScalarGridSpec(
            num_scalar_prefetch=1,
            grid=(NT,),
            in_specs=[tok(D), pl.BlockSpec(memory_space=pl.ANY), mod, mod, res((1, D)), res((D, 2 * RET_QK + 2 * RET_V)),
                      row_tab, col_tab],
            out_specs=[tok(D), tok(RET_QK), tok(RET_QK), tok(RET_V), tok(RET_V)],
            scratch_shapes=[pltpu.VMEM((3, TM // SUBLANES, SUBLANES, D), F32), pltpu.SemaphoreType.DMA((2,))],
        ),
        out_shape=[
            jax.ShapeDtypeStruct((T, D), F32),
            jax.ShapeDtypeStruct((T, RET_QK), BF16),
            jax.ShapeDtypeStruct((T, RET_QK), BF16),
            jax.ShapeDtypeStruct((T, RET_V), BF16),
            jax.ShapeDtypeStruct((T, RET_V), BF16),
        ],
        compiler_params=_params(("arbitrary",)),
        name="inproj1",
    )(pos, h1, _by_sublane_tile(ys), modv0, modv1, g, w_in, *tabs)


ROPE_TILE_ROWS = TM // GRID_W


def _rope_tables():
    n = LANES // 2
    inv_freq = ROPE_BASE ** (-np.arange(n, dtype=np.float64) / n)

    def lines(p):
        ang = p[:, None] * inv_freq[None, :]
        return np.stack([np.concatenate([np.cos(ang), np.cos(ang)], axis=1),
                         np.concatenate([-np.sin(ang), np.sin(ang)], axis=1)])

    ident = np.stack([np.ones((TM, LANES)), np.zeros((TM, LANES))])
    row_tab = np.concatenate([lines(np.arange(ROWS, dtype=np.float64)), ident[:, :SUBLANES]], axis=1)
    col_lat = lines(np.tile(np.arange(GRID_W, dtype=np.float64), ROPE_TILE_ROWS))
    col_tab = np.stack([col_lat, ident])
    return jnp.asarray(row_tab, dtype=F32), jnp.asarray(col_tab, dtype=F32)


RET_NCHUNK = T // RET_CHUNK
RET_CTX_CHUNKS = CTX // RET_CHUNK
RET_LAT_CHUNKS = SEQ // RET_CHUNK
RET_NHD = 2 * RET_HEADS


def _ret_kernel(dec_ref, qf_ref, kf_ref, vf_ref, qb_ref, kb_ref, vb_ref, of_ref, ob_ref,
                state, dm, qd, kd, cd):
    C = RET_CHUNK

    @pl.when(pl.program_id(0) == 0)
    def _():
        state[...] = jnp.zeros_like(state)
        n = lax.broadcasted_iota(jnp.int32, (C, C), 0).astype(F32)
        m = lax.broadcasted_iota(jnp.int32, (C, C), 1).astype(F32)
        for d in range(2):
            for h in range(RET_HEADS):
                idx = d * RET_HEADS + h
                lg = jnp.log1p(-jnp.exp(jnp.full((C, C), dec_ref[d, h], F32)))
                diff = n - m if d == 0 else m - n
                low = diff >= 0.0
                dm[idx] = jnp.where(low, jnp.exp(jnp.where(low, diff, 0.0) * lg), 0.0)
                qpow = n + 1.0 if d == 0 else C - n
                kpow = (C - 1.0) - n if d == 0 else n
                qcol = jnp.exp(qpow * lg)
                kcol = jnp.exp(kpow * lg)
                qd[idx] = jnp.concatenate([qcol] * (RET_DV // C), axis=1)
                kd[idx] = jnp.concatenate([kcol] * (RET_DK // C), axis=1)
                cd[idx] = jnp.exp(C * lg[0:8, :])

    for d, (q_ref, k_ref, v_ref, o_ref) in enumerate(((qf_ref, kf_ref, vf_ref, of_ref),
                                                       (qb_ref, kb_ref, vb_ref, ob_ref))):
        for h in range(RET_HEADS):
            idx = d * RET_HEADS + h
            q = q_ref[:, h * RET_DK:(h + 1) * RET_DK]
            k = k_ref[:, h * RET_DK:(h + 1) * RET_DK]
            v = v_ref[:, h * RET_DV:(h + 1) * RET_DV]
            st = state[idx]
            s = _dot_nt(q, k) * dm[idx]
            o = _dot(s.astype(BF16), v) + qd[idx] * _dot(q, st.astype(BF16))
            o_ref[:, h * RET_DV:(h + 1) * RET_DV] = o.astype(BF16)
            kdk = (k.astype(F32) * kd[idx]).astype(BF16)
            cdv = jnp.concatenate([cd[idx, 0:1, :]] * (RET_DV // C), axis=1)
            state[idx] = st * cdv + _dot_tn(kdk, v)


def _ret_scan(decay, q, k, v):
    def fwd(s):
        return jnp.where(s < RET_CTX_CHUNKS, RET_LAT_CHUNKS + s, s - RET_CTX_CHUNKS)

    def bwd(s):
        return RET_NCHUNK - 1 - s

    def spec(w, f):
        return pl.BlockSpec((RET_CHUNK, w), lambda s: (f(s), 0))

    return pl.pallas_call(
        _ret_kernel,
        grid=(RET_NCHUNK,),
        in_specs=[
            pl.BlockSpec(memory_space=pltpu.SMEM),
            spec(RET_QK, fwd), spec(RET_QK, fwd), spec(RET_V, fwd),
            spec(RET_QK, bwd), spec(RET_QK, bwd), spec(RET_V, bwd),
        ],
        out_specs=[spec(RET_V, fwd), spec(RET_V, bwd)],
        out_shape=[jax.ShapeDtypeStruct((T, RET_V), BF16), jax.ShapeDtypeStruct((T, RET_V), BF16)],
        scratch_shapes=[
            pltpu.VMEM((RET_NHD, RET_DK, RET_DV), F32),
            pltpu.VMEM((RET_NHD, RET_CHUNK, RET_CHUNK), F32),
            pltpu.VMEM((RET_NHD, RET_CHUNK, RET_DV), F32),
            pltpu.VMEM((RET_NHD, RET_CHUNK, RET_DK), F32),
            pltpu.VMEM((RET_NHD, 8, RET_CHUNK), F32),
        ],
        compiler_params=_params(("arbitrary",)),
        name="retention_scan",
    )(decay, q, k, v, q, k, v)


def _outproj1_kernel(of_ref, ob_ref, gg_ref, h_ref, mod_ref, w_ref, *rest):
    def residual_tile():
        y = jnp.zeros((TM, D), F32)
        for hh in range(RET_HEADS):
            cols = slice(hh * RET_DV, (hh + 1) * RET_DV)
            o = of_ref[:, cols].astype(F32) + ob_ref[:, cols].astype(F32)
            on = o * lax.rsqrt(jnp.mean(o * o, axis=-1, keepdims=True) + RMS_EPS)
            g = gg_ref[:, cols].astype(F32)
            z = (g * jax.nn.sigmoid(g)) * on
            y = y + _dot(z.astype(BF16), w_ref[cols, :])
        return h_ref[...] + mod_ref[0, 2:3, :] * y

    _two_stage(residual_tile, mod_ref, *rest)


def _outproj1(o_f, o_b, gg, h, modv, w_out, g2, rw, rb, triu):
    e_in, e_out, e_shape = _epilogue_specs(NT_LAT)
    tok = lambda w: pl.BlockSpec((TM, w), lambda i: (jnp.minimum(i, NT_LAT - 1), 0))
    latent_mod = pl.BlockSpec((1, 8, D), lambda i: (0, 0, 0))
    return pl.pallas_call(
        _outproj1_kernel,
        grid=(NT_LAT + 1,),
        in_specs=[tok(RET_V), tok(RET_V), tok(RET_V), tok(D), latent_mod, _resident((RET_V, D))] + e_in,
        out_specs=e_out,
        out_shape=e_shape,
        scratch_shapes=_EPILOGUE_SCRATCH,
        compiler_params=_params(("arbitrary",)),
        name="outproj1",
    )(o_f, o_b, gg, h, modv, w_out, g2, rw, rb, triu)


def _final_kernel(pos_ref, h_ref, ys_hbm, mod_ref, g_ref, o_ref, buf, sem):
    f = _gathered_rows(pos_ref, ys_hbm, buf, sem, request_next=True)
    h = h_ref[...] + mod_ref[0, 5:6, :] * f
    o_ref[...] = h * lax.rsqrt(jnp.mean(h * h, axis=-1, keepdims=True) + RMS_EPS) * g_ref[...]


def _final(pos, h, ys, modv, g):
    tok = pl.BlockSpec((TM, D), lambda i, pos: (i, 0))
    return pl.pallas_call(
        _final_kernel,
        grid_spec=pltpu.PrefetchScalarGridSpec(
            num_scalar_prefetch=1,
            grid=(NT_LAT,),
            in_specs=[tok, pl.BlockSpec(memory_space=pl.ANY),
                      pl.BlockSpec((1, 8, D), lambda i, pos: (0, 0, 0)),
                      pl.BlockSpec((1, D), lambda i, pos: (0, 0))],
            out_specs=tok,
            scratch_shapes=_GATHER_SCRATCH,
        ),
        out_shape=jax.ShapeDtypeStruct((SEQ, D), F32),
        compiler_params=_params(("arbitrary",)),
        name="final_norm",
    )(pos, h, _by_sublane_tile(ys), modv, g)


def _dft_constants():
    c = np.arange(FNET_GD)
    ang = 2.0 * np.pi * np.outer(c, c) / FNET_GD
    d64 = np.zeros((FNET_W, 2 * FNET_W))
    for g in range(FNET_W // FNET_GD):
        sl = slice(g * FNET_GD, (g + 1) * FNET_GD)
        d64[sl, sl] = np.cos(ang)
        d64[sl, FNET_W + g * FNET_GD:FNET_W + (g + 1) * FNET_GD] = -np.sin(ang)
    k2 = np.arange(FFT_N2)[:, None]
    n2 = np.arange(FFT_N2)[None, :]
    ga = np.zeros((FFT_N1, 2 * FFT_N2, 2 * FFT_N2))
    for n1 in range(FFT_N1):
        th = 2.0 * np.pi * ((k2 * (n1 + FFT_N1 * n2)) % SEQ) / SEQ
        cs, sn = np.cos(th), np.sin(th)
        ga[n1] = np.block([[cs, sn], [-sn, cs]])
    k1 = np.arange(FFT_N1)
    ph = 2.0 * np.pi * np.outer(k1, k1) / FFT_N1
    csb = np.concatenate([np.cos(ph), np.sin(ph)], axis=1)
    p = np.arange(CTX)
    pc = 2.0 * np.pi * (np.outer(p, p) % CTX) / CTX
    csc = np.concatenate([np.cos(pc), np.sin(pc)], axis=1)
    as_bf16 = lambda a: jnp.asarray(a, dtype=F32).astype(BF16)
    return as_bf16(d64), as_bf16(ga), as_bf16(csb), as_bf16(csc)


def _mod_table(mod_l):
    t = mod_l.reshape(2, 6, D)
    return jnp.concatenate([t, jnp.zeros((2, 2, D), F32)], axis=1)


def kernel(x, c, ctx, c_ctx, ada_w, ada_b, norm_g, final_norm_g, mixab_w_in, mixab_w_out, na_rpb,
           ret_w_in, ret_w_out, ret_decay, router_w, router_b, moe_w1, moe_w3, moe_w2):
    d64, ga, csb, csc = _dft_constants()
    triu = jnp.asarray(np.triu(np.ones((TM, TM)), 1), dtype=BF16)
    rw = jnp.pad(router_w, ((0, 0), (0, LANES - N_EXPERTS)))
    rb = jnp.broadcast_to(router_b[:, None], (N_EXPERTS, TM))
    w1 = moe_w1.reshape(DEPTH * N_EXPERTS, D, D_EXPERT).astype(BF16)
    w3 = moe_w3.reshape(DEPTH * N_EXPERTS, D, D_EXPERT).astype(BF16)
    w2 = moe_w2.reshape(DEPTH * N_EXPERTS, D_EXPERT, D).astype(BF16)

    mod = _mod_vectors(c, c_ctx, ada_w, ada_b)
    modv0, modv1 = _mod_table(mod[0]), _mod_table(mod[1])

    qkv, br, bi = _inproj0(x[0], ctx[0], modv0, norm_g[0, 0].reshape(1, D), mixab_w_in[0].astype(BF16), d64)
    a_lat = _fft_latent(br, bi, ga, csb)
    ao_ctx = _ctx_mixer(qkv, br, bi, csc)
    o_lat = _na_attention(qkv, _na_bias_rows(na_rpb[0]))
    h1, xrow, meta, cnt = _outproj0(a_lat, o_lat, ao_ctx, x[0], ctx[0], modv0, mixab_w_out[0].astype(BF16),
                                    norm_g[0, 1].reshape(1, D), rw, rb, triu)
    n_tiles0 = NT + N_REAL_BUCKETS
    pos0, e_lo, e_hi, n_used = _route_plan(meta, cnt, n_tiles0, 0)
    ys0 = _moe(e_lo, e_hi, n_used, _dispatch(pos0, xrow, n_tiles0), w1, w3, w2, n_tiles0)

    h2, q, k, v, gg = _inproj1(pos0, h1, ys0, modv0, modv1, norm_g[1, 0].reshape(1, D),
                               ret_w_in[0].astype(BF16), _rope_tables())
    o_f, o_b = _ret_scan(ret_decay[0].astype(F32), q, k, v)
    h3, xrow1, meta1, cnt1 = _outproj1(o_f, o_b, gg, h2, modv1, ret_w_out[0].astype(BF16),
                                       norm_g[1, 1].reshape(1, D), rw, rb, triu)
    n_tiles1 = NT_LAT + N_REAL_BUCKETS
    pos1, e_lo1, e_hi1, n_used1 = _route_plan(meta1, cnt1, n_tiles1, 1)
    ys1 = _moe(e_lo1, e_hi1, n_used1, _dispatch(pos1, xrow1, n_tiles1), w1, w3, w2, n_tiles1)
    out = _final(pos1, h3, ys1, modv1, final_norm_g.reshape(1, D))
    return out[None]
```

```python
import functools
import math

import numpy as np
import jax
import jax.numpy as jnp
from jax import lax
from jax.experimental import pallas as pl
from jax.experimental.pallas import tpu as pltpu

F32 = jnp.float32
BF16 = jnp.bfloat16

D = 1024
SEQ = 16384
CTX = 256
T = SEQ + CTX
DEPTH = 2
GRID_W = 64
ROWS = SEQ // GRID_W
RMS_EPS = 1e-6
ROPE_BASE = 10000.0

FNET_W = 256
FNET_GD = 64
NA_HEADS = 12
NA_HD = 64
NA_W = NA_HEADS * NA_HD
NA_KH = 8
NA_KW = 16
AB_IN = FNET_W + 3 * NA_W
QKV_W = 3 * NA_W

RET_HEADS = 4
RET_DK = 256
RET_DV = 512
RET_QK = RET_HEADS * RET_DK
RET_V = RET_HEADS * RET_DV
RET_CHUNK = 256

N_EXPERTS = 16
EPG = 4
D_EXPERT = 512

LANES = 128
TM = 256
NT = T // TM
NT_LAT = SEQ // TM
TMM = TM
N_BUCKET_IDS = 64
N_REAL_BUCKETS = 24
XROW_W = D + LANES
FFT_N1 = 128
FFT_N2 = 128
NEG_BIG = -1e30
SUBLANES = 8

VMEM_LIMIT = 56 * 1024 * 1024


def _params(sem, vmem=VMEM_LIMIT):
    return pltpu.CompilerParams(dimension_semantics=sem, vmem_limit_bytes=vmem)


def _dot(a, b):
    return jnp.dot(a, b, preferred_element_type=F32)


def _dot_nt(a, b):
    return lax.dot_general(a, b, (((1,), (1,)), ((), ())), preferred_element_type=F32)


def _dot_tn(a, b):
    return lax.dot_general(a, b, (((0,), (0,)), ((), ())), preferred_element_type=F32)


def _rms_mod(x, g, sh, sc):
    ms = jnp.mean(x * x, axis=-1, keepdims=True)
    y = x * lax.rsqrt(ms + RMS_EPS) * g
    return y * (1.0 + sc) + sh


def _resident(shape):
    nd = len(shape)
    return pl.BlockSpec(shape, lambda *_: (0,) * nd, pipeline_mode=pl.Buffered(1))


MOD_TN = 768


def _mod_kernel(cs_ref, w_ref, b_ref, o_ref):
    cs = cs_ref[...]
    s = cs * jax.nn.sigmoid(cs)
    w = w_ref[0]
    r0 = jnp.sum(s[:, 0:1] * w, axis=0, keepdims=True)
    r1 = jnp.sum(s[:, 1:2] * w, axis=0, keepdims=True)
    o_ref[0] = jnp.concatenate([r0, r1], axis=0) + b_ref[0]


def _mod_vectors(c, c_ctx, ada_w, ada_b):
    cs = jnp.stack([c[0], c_ctx], axis=1)
    n = 6 * D
    return pl.pallas_call(
        _mod_kernel,
        grid=(DEPTH, n // MOD_TN),
        in_specs=[
            pl.BlockSpec((D, 2), lambda l, j: (0, 0)),
            pl.BlockSpec((1, D, MOD_TN), lambda l, j: (l, 0, j)),
            pl.BlockSpec((1, 1, MOD_TN), lambda l, j: (l, 0, j)),
        ],
        out_specs=pl.BlockSpec((1, 2, MOD_TN), lambda l, j: (l, 0, j)),
        out_shape=jax.ShapeDtypeStruct((DEPTH, 2, n), F32),
        compiler_params=_params(("arbitrary", "arbitrary")),
        name="mod_vectors",
    )(cs, ada_w, ada_b.reshape(DEPTH, 1, n))


def _mod_spec():
    return pl.BlockSpec((1, 8, D), lambda i, *_: (i // NT_LAT, 0, 0))


def _stream_tile(x_ref, ctx_ref):
    return jnp.where(pl.program_id(0) >= NT_LAT, ctx_ref[...], x_ref[...])


def _stream_specs():
    return [pl.BlockSpec((TM, D), lambda i: (jnp.minimum(i, NT_LAT - 1), 0)),
            pl.BlockSpec((CTX, D), lambda i: (0, 0))]


def _inproj0_kernel(x_ref, ctx_ref, mod_ref, g_ref, w_ref, d64_ref, qkv_ref, br_ref, bi_ref):
    h = _stream_tile(x_ref, ctx_ref)
    xm = _rms_mod(h, g_ref[...], mod_ref[0, 0:1, :], mod_ref[0, 1:2, :]).astype(BF16)
    p = _dot(xm, w_ref[...])
    qkv_ref[...] = p[:, FNET_W:].astype(BF16)
    b = _dot(p[:, :FNET_W].astype(BF16), d64_ref[...])
    br_ref[...] = b[:, :FNET_W].astype(BF16)
    bi_ref[...] = b[:, FNET_W:].astype(BF16)


def _inproj0(x, ctx, modv, g, w_in, d64):
    return pl.pallas_call(
        _inproj0_kernel,
        grid=(NT,),
        in_specs=_stream_specs() + [
            _mod_spec(),
            _resident((1, D)),
            _resident((D, AB_IN)),
            _resident((FNET_W, 2 * FNET_W)),
        ],
        out_specs=[
            pl.BlockSpec((TM, QKV_W), lambda i: (i, 0)),
            pl.BlockSpec((TM, FNET_W), lambda i: (i, 0)),
            pl.BlockSpec((TM, FNET_W), lambda i: (i, 0)),
        ],
        out_shape=[
            jax.ShapeDtypeStruct((T, QKV_W), BF16),
            jax.ShapeDtypeStruct((T, FNET_W), BF16),
            jax.ShapeDtypeStruct((T, FNET_W), BF16),
        ],
        compiler_params=_params(("arbitrary",)),
        name="inproj0",
    )(x, ctx, modv, g, w_in, d64)


FFT_A_NB = 8
FFT_B_TN = 4096


def _fft_a_kernel(br_ref, bi_ref, ga_ref, v_ref):
    for u in range(FFT_A_NB):
        cols = slice(u * FNET_W, (u + 1) * FNET_W)
        z = _dot(ga_ref[u, :, 0:FFT_N2], br_ref[:, cols]) + _dot(ga_ref[u, :, FFT_N2:], bi_ref[:, cols])
        v_ref[0, u] = z[:FFT_N2].astype(BF16)
        v_ref[1, u] = z[FFT_N2:].astype(BF16)


def _fft_b_kernel(cs_ref, v_ref, o_ref):
    o_ref[...] = _dot(cs_ref[...], v_ref[...]) * (1.0 / math.sqrt(SEQ * FNET_GD))


def _fft_latent(br, bi, ga, csb):
    width = FFT_N1 * FNET_W
    br2 = br.reshape(T // FFT_N1, width)
    bi2 = bi.reshape(T // FFT_N1, width)
    v = pl.pallas_call(
        _fft_a_kernel,
        grid=(FFT_N1 // FFT_A_NB,),
        in_specs=[
            pl.BlockSpec((FFT_N2, FFT_A_NB * FNET_W), lambda j: (0, j)),
            pl.BlockSpec((FFT_N2, FFT_A_NB * FNET_W), lambda j: (0, j)),
            pl.BlockSpec((FFT_A_NB, 2 * FFT_N2, 2 * FFT_N2), lambda j: (j, 0, 0)),
        ],
        out_specs=pl.BlockSpec((2, FFT_A_NB, FFT_N2, FNET_W), lambda j: (0, j, 0, 0)),
        out_shape=jax.ShapeDtypeStruct((2, FFT_N1, FFT_N2, FNET_W), BF16),
        compiler_params=_params(("arbitrary",)),
        name="fft_stage_a",
    )(br2, bi2, ga)
    v2 = v.reshape(2 * FFT_N1, FFT_N2 * FNET_W)
    x = pl.pallas_call(
        _fft_b_kernel,
        grid=(FFT_N2 * FNET_W // FFT_B_TN,),
        in_specs=[
            _resident((FFT_N1, 2 * FFT_N1)),
            pl.BlockSpec((2 * FFT_N1, FFT_B_TN), lambda j: (0, j)),
        ],
        out_specs=pl.BlockSpec((FFT_N1, FFT_B_TN), lambda j: (0, j)),
        out_shape=jax.ShapeDtypeStruct((FFT_N1, FFT_N2 * FNET_W), F32),
        compiler_params=_params(("arbitrary",)),
        name="fft_stage_b",
    )(csb, v2)
    return x.reshape(SEQ, FNET_W)


def _head_pair_masks():
    lane = lax.broadcasted_iota(jnp.int32, (1, LANES), 1)
    return lane < NA_HD


def _ctx_kernel(qkv_ref, br_ref, bi_ref, cs_ref, o_ref):
    a = _dot(cs_ref[:, 0:CTX], br_ref[...]) + _dot(cs_ref[:, CTX:], bi_ref[...])
    o_ref[:, 0:FNET_W] = (a * (1.0 / math.sqrt(CTX * FNET_GD))).astype(BF16)
    m0 = _head_pair_masks()
    for hp in range(NA_HEADS // 2):
        q = qkv_ref[:, hp * LANES:(hp + 1) * LANES]
        k = qkv_ref[:, NA_W + hp * LANES:NA_W + (hp + 1) * LANES]
        v = qkv_ref[:, 2 * NA_W + hp * LANES:2 * NA_W + (hp + 1) * LANES]
        outs = []
        for a_ in range(2):
            qa = jnp.where(m0 if a_ == 0 else jnp.logical_not(m0), q, jnp.zeros_like(q))
            s = _dot_nt(qa, k) * (NA_HD ** -0.5)
            p = jnp.exp(s - jnp.max(s, axis=-1, keepdims=True))
            l = jnp.sum(p, axis=-1, keepdims=True)
            outs.append(_dot(p.astype(BF16), v) / l)
        o_ref[:, FNET_W + hp * LANES:FNET_W + (hp + 1) * LANES] = jnp.where(m0, outs[0], outs[1]).astype(BF16)


def _ctx_mixer(qkv, br, bi, csc):
    return pl.pallas_call(
        _ctx_kernel,
        grid=(1,),
        in_specs=[
            pl.BlockSpec((CTX, QKV_W), lambda i: (SEQ // CTX, 0)),
            pl.BlockSpec((CTX, FNET_W), lambda i: (SEQ // CTX, 0)),
            pl.BlockSpec((CTX, FNET_W), lambda i: (SEQ // CTX, 0)),
            pl.BlockSpec((CTX, 2 * CTX), lambda i: (0, 0)),
        ],
        out_specs=pl.BlockSpec((CTX, D), lambda i: (0, 0)),
        out_shape=jax.ShapeDtypeStruct((CTX, D), BF16),
        compiler_params=_params(("arbitrary",)),
        name="ctx_mixer",
    )(qkv, br, bi, csc)


NA_RB = 8
NA_WIN = NA_KH * GRID_W


NA_NK = NA_WIN + CTX
NA_SUB = 16


def _na_kernel(q_ref, k_ref, v_ref, kc_ref, vc_ref, bias_ref, o_ref, s_scr, p_scr, l_scr):
    b = pl.program_id(1)
    m0 = _head_pair_masks()
    nm0 = jnp.logical_not(m0)
    scale = NA_HD ** -0.5

    def window(i):
        r = b * NA_RB + i
        rs = jnp.clip(r - NA_KH // 2, 0, ROWS - NA_KH)
        return pl.multiple_of(rs * GRID_W, GRID_W), rs - r + (NA_KH - 1)

    def scores(i):
        start, _ = window(i)
        q = q_ref[i * GRID_W:(i + 1) * GRID_W, :] * scale
        qs = jnp.concatenate([jnp.where(m0, q, jnp.zeros_like(q)), jnp.where(nm0, q, jnp.zeros_like(q))], axis=0)
        s_scr[i, :, 0:NA_WIN] = _dot_nt(qs, k_ref[pl.ds(start, NA_WIN), :])
        s_scr[i, :, NA_WIN:NA_NK] = _dot_nt(qs, kc_ref[...])

    def softmax(i):
        _, e = window(i)
        for g in range(2 * GRID_W // NA_SUB):
            rows = slice(g * NA_SUB, (g + 1) * NA_SUB)
            a, c0 = divmod(g * NA_SUB, GRID_W)
            s1 = s_scr[i, rows, 0:NA_WIN] + bias_ref[a, e, c0:c0 + NA_SUB, :]
            s2 = s_scr[i, rows, NA_WIN:NA_NK]
            mx = jnp.maximum(jnp.max(s1, axis=-1, keepdims=True), jnp.max(s2, axis=-1, keepdims=True))
            p1 = jnp.exp(s1 - mx)
            p2 = jnp.exp(s2 - mx)
            l = jnp.sum(p1, axis=-1, keepdims=True) + jnp.sum(p2, axis=-1, keepdims=True)
            p_scr[i, rows, 0:NA_WIN] = p1.astype(BF16)
            p_scr[i, rows, NA_WIN:NA_NK] = p2.astype(BF16)
            l_scr[i, rows, :] = jnp.broadcast_to(l, (NA_SUB, LANES))

    def values(i):
        start, _ = window(i)
        o = _dot(p_scr[i, :, 0:NA_WIN], v_ref[pl.ds(start, NA_WIN), :]) + _dot(p_scr[i, :, NA_WIN:NA_NK], vc_ref[...])
        o = o / l_scr[i]
        o_ref[i * GRID_W:(i + 1) * GRID_W, :] = jnp.where(m0, o[0:GRID_W], o[GRID_W:]).astype(BF16)

    scores(0)
    for i in range(NA_RB):
        if i + 1 < NA_RB:
            scores(i + 1)
        softmax(i)
        values(i)


def _na_attention(qkv, bias):
    nq = NA_RB * GRID_W
    nhp = NA_HEADS // 2
    return pl.pallas_call(
        _na_kernel,
        grid=(nhp, ROWS // NA_RB),
        in_specs=[
            pl.BlockSpec((nq, LANES), lambda hp, b: (b, hp)),
            pl.BlockSpec((SEQ, LANES), lambda hp, b: (0, nhp + hp)),
            pl.BlockSpec((SEQ, LANES), lambda hp, b: (0, 2 * nhp + hp)),
            pl.BlockSpec((CTX, LANES), lambda hp, b: (SEQ // CTX, nhp + hp)),
            pl.BlockSpec((CTX, LANES), lambda hp, b: (SEQ // CTX, 2 * nhp + hp)),
            pl.BlockSpec((2, NA_KH, GRID_W, NA_WIN), lambda hp, b: (hp, 0, 0, 0)),
        ],
        out_specs=pl.BlockSpec((nq, LANES), lambda hp, b: (b, hp)),
        out_shape=jax.ShapeDtypeStruct((SEQ, NA_W), BF16),
        scratch_shapes=[
            pltpu.VMEM((NA_RB, 2 * GRID_W, NA_NK), F32),
            pltpu.VMEM((NA_RB, 2 * GRID_W, NA_NK), BF16),
            pltpu.VMEM((NA_RB, 2 * GRID_W, LANES), F32),
        ],
        compiler_params=_params(("arbitrary", "arbitrary")),
        name="na_attention",
    )(qkv, qkv, qkv, qkv, qkv, bias)


def _na_bias_rows(rpb):
    c = np.arange(GRID_W)[:, None]
    kc = np.arange(GRID_W)[None, :]
    ws = np.clip(c - NA_KW // 2, 0, GRID_W - NA_KW)
    col_ok = (kc >= ws) & (kc < ws + NA_KW)
    dc = np.clip(kc - c + NA_KW - 1, 0, 2 * NA_KW - 2)
    pick = (dc[None] == np.arange(2 * NA_KW - 1)[:, None, None]).astype(np.float32)
    full = jnp.einsum("hrd,dck->hrck", rpb.astype(F32), jnp.asarray(pick), precision=lax.Precision.HIGHEST)
    full = jnp.where(jnp.asarray(col_ok)[None, None], full, NEG_BIG)
    rows = [jnp.transpose(full[:, e:e + NA_KH], (0, 2, 1, 3)).reshape(NA_HEADS, GRID_W, NA_WIN)
            for e in range(NA_KH)]
    return jnp.stack(rows, axis=1)


def _route_stage(xm2, valid, rw_ref, rb_ref, triu_ref, run_ref, xrow_ref, meta_ref, cnt_ref):
    xrow_ref[:, 0:D] = xm2
    xh = xm2.astype(BF16)
    xl = (xm2 - xh.astype(F32)).astype(BF16)
    w = rw_ref[...]
    wh = w.astype(BF16)
    wl = (w - wh.astype(F32)).astype(BF16)
    logits = _dot(xh, wh) + (_dot(xl, wh) + _dot(xh, wl))
    scores = jax.nn.sigmoid(logits.T[0:N_EXPERTS, :])
    biased = scores + rb_ref[...]
    s = [scores[e:e + 1, :] for e in range(N_EXPERTS)]
    b = [biased[e:e + 1, :] for e in range(N_EXPERTS)]
    n_groups = N_EXPERTS // EPG
    top2, group_score = [], []
    for g in range(n_groups):
        m = b[EPG * g:EPG * (g + 1)]
        total = None
        for k in range(EPG):
            rank = None
            for j in range(EPG):
                if j != k:
                    beats = jnp.where((m[j] >= m[k]) if j < k else (m[j] > m[k]), 1.0, 0.0)
                    rank = beats if rank is None else rank + beats
            top2.append(rank < 2.0)
            term = jnp.where(top2[-1], m[k], 0.0)
            total = term if total is None else total + term
        group_score.append(total)
    best, g_sel = group_score[0], jnp.zeros((1, TM), jnp.int32)
    for g in range(1, n_groups):
        better = group_score[g] > best
        g_sel = jnp.where(better, g, g_sel)
        best = jnp.where(better, group_score[g], best)
    sel = [top2[e] & (g_sel == e // EPG) for e in range(N_EXPERTS)]
    w_sum = None
    for e in range(N_EXPERTS):
        term = jnp.where(sel[e], s[e], 0.0)
        w_sum = term if w_sum is None else w_sum + term
    zero = jnp.zeros((1, TM), F32)
    w_lo, w_hi, bid = zero, zero, zero
    for g in range(n_groups):
        seen = None
        for k in range(EPG):
            e = EPG * g + k
            if seen is None:
                w_lo = w_lo + jnp.where(sel[e], s[e], 0.0)
                seen = sel[e]
            else:
                w_lo = w_lo + jnp.where(sel[e] & jnp.logical_not(seen), s[e], 0.0)
                w_hi = w_hi + jnp.where(sel[e] & seen, s[e], 0.0)
                seen = seen | sel[e]
            bid = bid + jnp.where(sel[e], float((1 << k) + 8 * g), 0.0)
    w_lo = w_lo / w_sum
    w_hi = w_hi / w_sum
    bucket = lax.broadcasted_iota(jnp.int32, (N_BUCKET_IDS, TM), 0).astype(F32)
    onehot = bucket == bid
    ones = jnp.where(onehot & valid, 1.0, 0.0)
    earlier = _dot(ones.astype(BF16), triu_ref[...])
    run = run_ref[...]
    run_t = jnp.concatenate([run] * (TM // LANES), axis=1)
    rank_row = jnp.sum(jnp.where(onehot, earlier + run_t, 0.0), axis=0, keepdims=True)
    new_run = run + jnp.sum(ones, axis=1, keepdims=True)
    run_ref[...] = new_run
    cnt_ref[...] = new_run
    meta_rows = jnp.concatenate([w_lo, w_hi, bid, rank_row, jnp.zeros((SUBLANES - 4, TM), F32)], axis=0)
    meta_ref[...] = meta_rows
    padded = jnp.concatenate([meta_rows, jnp.zeros((LANES - SUBLANES, TM), F32)], axis=0)
    xrow_ref[:, D:XROW_W] = padded.T


def _epilogue_specs(ntiles):
    in_specs = [
        _resident((1, D)),
        _resident((D, LANES)),
        _resident((N_EXPERTS, TM)),
        _resident((TM, TM)),
    ]
    out_specs = [
        pl.BlockSpec((TM, D), lambda i: (jnp.minimum(i, ntiles - 1), 0)),
        pl.BlockSpec((TM, XROW_W), lambda i: (jnp.maximum(i - 1, 0), 0)),
        pl.BlockSpec((SUBLANES, TM), lambda i: (0, jnp.maximum(i - 1, 0))),
        pl.BlockSpec((N_BUCKET_IDS, LANES), lambda i: (0, 0)),
    ]
    n = ntiles * TM
    out_shape = [
        jax.ShapeDtypeStruct((n, D), F32),
        jax.ShapeDtypeStruct((n, XROW_W), F32),
        jax.ShapeDtypeStruct((SUBLANES, n), F32),
        jax.ShapeDtypeStruct((N_BUCKET_IDS, LANES), F32),
    ]
    return in_specs, out_specs, out_shape


_EPILOGUE_SCRATCH = [pltpu.VMEM((N_BUCKET_IDS, LANES), F32), pltpu.VMEM((2, TM, D), F32)]


def _two_stage(residual_tile, mod_ref, g2_ref, rw_ref, rb_ref, triu_ref,
               h1_ref, xrow_ref, meta_ref, cnt_ref, run_ref, stash):
    i = pl.program_id(0)

    @pl.when(i == 0)
    def _():
        run_ref[...] = jnp.zeros_like(run_ref)
        stash[1] = jnp.zeros((TM, D), F32)

    _route_stage(stash[1 - i % 2], i >= 1, rw_ref, rb_ref, triu_ref, run_ref, xrow_ref, meta_ref, cnt_ref)
    h1 = residual_tile()
    h1_ref[...] = h1
    stash[i % 2] = _rms_mod(h1, g2_ref[...], mod_ref[0, 3:4, :], mod_ref[0, 4:5, :])


def _outproj0_kernel(a_ref, o_ref, aoc_ref, x_ref, ctx_ref, mod_ref, w_ref, *rest):
    def residual_tile():
        is_ctx = pl.program_id(0) >= NT_LAT
        a = jnp.where(is_ctx, aoc_ref[:, 0:FNET_W], a_ref[...].astype(BF16))
        o = jnp.where(is_ctx, aoc_ref[:, FNET_W:], o_ref[...])
        y = _dot(a, w_ref[0:FNET_W, :]) + _dot(o, w_ref[FNET_W:, :])
        return _stream_tile(x_ref, ctx_ref) + mod_ref[0, 2:3, :] * y

    _two_stage(residual_tile, mod_ref, *rest)


def _outproj0(a_lat, o_lat, ao_ctx, x, ctx, modv, w_out, g2, rw, rb, triu):
    e_in, e_out, e_shape = _epilogue_specs(NT)
    lat = lambda w: pl.BlockSpec((TM, w), lambda i: (jnp.minimum(i, NT_LAT - 1), 0))
    return pl.pallas_call(
        _outproj0_kernel,
        grid=(NT + 1,),
        in_specs=[lat(FNET_W), lat(NA_W), pl.BlockSpec((CTX, D), lambda i: (0, 0))] + _stream_specs() + [
            _mod_spec(),
            _resident((D, D)),
        ] + e_in,
        out_specs=e_out,
        out_shape=e_shape,
        scratch_shapes=_EPILOGUE_SCRATCH,
        compiler_params=_params(("arbitrary",)),
        name="outproj0",
    )(a_lat, o_lat, ao_ctx, x, ctx, modv, w_out, g2, rw, rb, triu)


def _bucket_tables():
    lo = np.zeros(N_BUCKET_IDS, np.int32)
    hi = np.zeros(N_BUCKET_IDS, np.int32)
    for g in range(N_EXPERTS // EPG):
        for a in range(EPG):
            for b in range(a + 1, EPG):
                i = 16 * g + (1 << a) + (1 << b)
                lo[i] = EPG * g + a
                hi[i] = EPG * g + b
    return lo, hi


def _slot_kernel(meta_ref, starts_ref, o_ref):
    n = meta_ref.shape[1]
    bucket = lax.broadcasted_iota(jnp.int32, (N_BUCKET_IDS, n), 0).astype(F32)
    first = jnp.sum(jnp.where(bucket == meta_ref[2:3, :], starts_ref[:, 0:1], 0.0), axis=0, keepdims=True)
    o_ref[...] = jnp.broadcast_to((first + meta_ref[3:4, :]).astype(jnp.int32), (SUBLANES, n))


def _token_slots(meta, first_row):
    n = meta.shape[1]
    out = pl.pallas_call(
        _slot_kernel,
        grid=(1,),
        in_specs=[pl.BlockSpec((SUBLANES, n), lambda i: (0, 0)),
                  pl.BlockSpec((N_BUCKET_IDS, LANES), lambda i: (0, 0))],
        out_specs=pl.BlockSpec((SUBLANES, n), lambda i: (0, 0)),
        out_shape=jax.ShapeDtypeStruct((SUBLANES, n), jnp.int32),
        compiler_params=_params(("arbitrary",)),
        name="token_slots",
    )(meta, first_row)
    return out[0]


def _route_plan(meta, cnt, n_tiles, layer):
    counts = cnt[:, 0].astype(jnp.int32)
    nt_b = (counts + TMM - 1) // TMM
    ends = jnp.cumsum(nt_b)
    starts = ends - nt_b
    first_row = jnp.broadcast_to((starts * TMM).astype(F32)[:, None], (N_BUCKET_IDS, LANES))
    pos = _token_slots(meta, first_row)
    n_used = ends[-1]
    tile = jnp.minimum(jnp.arange(n_tiles, dtype=jnp.int32), n_used - 1)
    tile_b = jnp.sum((ends[None, :] <= tile[:, None]).astype(jnp.int32), axis=1)
    lo, hi = _bucket_tables()
    e_lo = jnp.asarray(lo)[tile_b] + layer * N_EXPERTS
    e_hi = jnp.asarray(hi)[tile_b] + layer * N_EXPERTS
    return pos, e_lo.astype(jnp.int32), e_hi.astype(jnp.int32), n_used.reshape(1).astype(jnp.int32)


def _dispatch_kernel(pos_ref, x_ref, xs_in_ref, xs_ref, sem):
    del xs_in_ref
    base = pl.program_id(0) * TM

    def issue(j, carry):
        for u in range(SUBLANES):
            p = pos_ref[base + j * SUBLANES + u]
            pltpu.make_async_copy(x_ref.at[j, pl.ds(u, 1), :],
                                  xs_ref.at[p >> 3, pl.ds(p & (SUBLANES - 1), 1), :], sem).start()
        return carry

    lax.fori_loop(0, TM // SUBLANES, issue, 0)
    pltpu.make_async_copy(x_ref, xs_ref.at[pl.ds(0, TM // SUBLANES)], sem).wait()


def _dispatch(pos, xrow, n_tiles):
    ntok = xrow.shape[0]
    xs0 = jnp.zeros((n_tiles * TMM // SUBLANES, SUBLANES, XROW_W), F32)
    xs = pl.pallas_call(
        _dispatch_kernel,
        grid_spec=pltpu.PrefetchScalarGridSpec(
            num_scalar_prefetch=1,
            grid=(ntok // TM,),
            in_specs=[
                pl.BlockSpec((TM // SUBLANES, SUBLANES, XROW_W), lambda i, pos: (i, 0, 0)),
                pl.BlockSpec(memory_space=pl.ANY),
            ],
            out_specs=pl.BlockSpec(memory_space=pl.ANY),
            scratch_shapes=[pltpu.SemaphoreType.DMA(())],
        ),
        out_shape=jax.ShapeDtypeStruct(xs0.shape, F32),
        input_output_aliases={2: 0},
        compiler_params=_params(("arbitrary",)),
        name="moe_dispatch",
    )(pos, _by_sublane_tile(xrow), xs0)
    return xs.reshape(n_tiles * TMM, XROW_W)


def _moe_kernel(elo_ref, ehi_ref, nu_ref, xs_ref, w1a, w3a, w2a, w1b, w3b, w2b, ys_ref):
    j = pl.program_id(0)

    @pl.when(j < nu_ref[0])
    def _():
        x = xs_ref[:, 0:D].astype(BF16)
        g_lo = xs_ref[:, D:D + 1]
        g_hi = xs_ref[:, D + 1:D + 2]

        def expert(w1, w3, w2):
            a = _dot(x, w1[0].astype(BF16))
            hid = (a * jax.nn.sigmoid(a)) * _dot(x, w3[0].astype(BF16))
            return _dot(hid.astype(BF16), w2[0].astype(BF16))

        ys_ref[...] = g_lo * expert(w1a, w3a, w2a) + g_hi * expert(w1b, w3b, w2b)

    @pl.when(j >= nu_ref[0])
    def _():
        ys_ref[...] = jnp.zeros_like(ys_ref)


def _moe(e_lo, e_hi, n_used, xs, w1, w3, w2, n_tiles):
    def wspec(shape, which):
        if which == 0:
            return pl.BlockSpec(shape, lambda j, lo, hi, nu: (lo[j], 0, 0))
        return pl.BlockSpec(shape, lambda j, lo, hi, nu: (hi[j], 0, 0))

    s13 = (1, D, D_EXPERT)
    s2 = (1, D_EXPERT, D)
    return pl.pallas_call(
        _moe_kernel,
        grid_spec=pltpu.PrefetchScalarGridSpec(
            num_scalar_prefetch=3,
            grid=(n_tiles,),
            in_specs=[
                pl.BlockSpec((TMM, XROW_W), lambda j, lo, hi, nu: (j, 0)),
                wspec(s13, 0), wspec(s13, 0), wspec(s2, 0),
                wspec(s13, 1), wspec(s13, 1), wspec(s2, 1),
            ],
            out_specs=pl.BlockSpec((TMM, D), lambda j, lo, hi, nu: (j, 0)),
        ),
        out_shape=jax.ShapeDtypeStruct((n_tiles * TMM, D), F32),
        compiler_params=_params(("arbitrary",)),
        name="moe_experts",
    )(e_lo, e_hi, n_used, xs, w1, w3, w2, w1, w3, w2)


def _gather_issue(pos_ref, ys_hbm, buf, sem, tile, slot, inline, part=(0, TM // SUBLANES)):
    base = tile * TM

    def issue(j, carry):
        for u in range(SUBLANES):
            p = pos_ref[base + j * SUBLANES + u]
            pltpu.make_async_copy(ys_hbm.at[p >> 3, pl.ds(p & (SUBLANES - 1), 1), :],
                                  buf.at[slot, j, pl.ds(u, 1), :], sem.at[slot]).start()
        return carry

    if inline:
        for j in range(*part):
            issue(j, 0)
    else:
        lax.fori_loop(0, TM // SUBLANES, issue, 0)


def _gather_wait(ys_hbm, buf, sem, slot):
    pltpu.make_async_copy(ys_hbm.at[pl.ds(0, TM // SUBLANES)], buf.at[slot], sem.at[slot]).wait()


def _gathered_rows(pos_ref, ys_hbm, buf, sem, request_next):
    i = pl.program_id(0)
    slot = i % 2

    @pl.when(i == 0)
    def _():
        _gather_issue(pos_ref, ys_hbm, buf, sem, 0, 0, inline=False)

    if request_next:
        @pl.when(i + 1 < pl.num_programs(0))
        def _():
            _gather_issue(pos_ref, ys_hbm, buf, sem, i + 1, 1 - slot, inline=False)

    _gather_wait(ys_hbm, buf, sem, slot)
    return buf[slot].reshape(TM, D)


def _gather_drain(ys_hbm, buf, sem):
    i = pl.program_id(0)

    @pl.when(i + 1 == pl.num_programs(0))
    def _():
        _gather_wait(ys_hbm, buf, sem, 1 - i % 2)


_GATHER_SCRATCH = [pltpu.VMEM((2, TM // SUBLANES, SUBLANES, D), F32), pltpu.SemaphoreType.DMA((2,))]


def _by_sublane_tile(a):
    return a.reshape(a.shape[0] // SUBLANES, SUBLANES, a.shape[1])


def _rope_store(x, tabs, out_ref):
    cr, sr, cc, sc = tabs
    for ch in range(RET_QK // LANES):
        xc = x[:, ch * LANES:(ch + 1) * LANES]
        xr = pltpu.roll(xc, LANES // 2, axis=1)
        y = xc * cr + xr * sr if ch % 2 == 0 else xc * cc + xr * sc
        out_ref[:, ch * LANES:(ch + 1) * LANES] = y.astype(BF16)


def _inproj1_kernel(pos_ref, h_ref, ys_hbm, mod0_ref, mod1_ref, g_ref, w_ref, rt_ref, ct_ref,
                    h2_ref, q_ref, k_ref, v_ref, gg_ref, buf, sem):
    f = _gathered_rows(pos_ref, ys_hbm, buf, sem, request_next=False)
    h2 = h_ref[...] + mod0_ref[0, 5:6, :] * f
    h2_ref[...] = h2
    buf[2] = _rms_mod(h2, g_ref[...], mod1_ref[0, 0:1, :], mod1_ref[0, 1:2, :]).reshape(TM // SUBLANES, SUBLANES, D)
    i = pl.program_id(0)
    nxt = jnp.where(i + 1 < pl.num_programs(0), i + 1, 0)
    bounds = [round(k * (TM // SUBLANES) / 6) for k in range(7)]

    def xm_after_requests(part):
        _gather_issue(pos_ref, ys_hbm, buf, sem, nxt, 1 - i % 2, inline=True, part=(bounds[part], bounds[part + 1]))
        return buf[2].reshape(TM, D).astype(BF16)

    off = (pl.program_id(0) % (SUBLANES // ROPE_TILE_ROWS)) * ROPE_TILE_ROWS

    def row_table(which):
        return jnp.concatenate([jnp.broadcast_to(rt_ref[which, pl.ds(off + j, 1), :], (GRID_W, LANES))
                                for j in range(ROPE_TILE_ROWS)], axis=0)

    tabs = (row_table(0), row_table(1), ct_ref[0, 0], ct_ref[0, 1])
    _rope_store(_dot(xm_after_requests(0), w_ref[:, 0:RET_QK]), tabs, q_ref)
    _rope_store(_dot(xm_after_requests(1), w_ref[:, RET_QK:2 * RET_QK]) * (RET_DK ** -0.5), tabs, k_ref)
    half = RET_V // 2
    for part, (out_ref, col0) in enumerate(((v_ref, 2 * RET_QK), (gg_ref, 2 * RET_QK + RET_V))):
        for s in range(2):
            xm = xm_after_requests(2 + 2 * part + s)
            out_ref[:, s * half:(s + 1) * half] = _dot(
                xm, w_ref[:, col0 + s * half:col0 + (s + 1) * half]).astype(BF16)
    _gather_drain(ys_hbm, buf, sem)


def _inproj1(pos, h1, ys, modv0, modv1, g, w_in, tabs):
    tok = lambda w: pl.BlockSpec((TM, w), lambda i, pos: (i, 0))
    res = lambda shape: pl.BlockSpec(shape, lambda i, pos: (0,) * len(shape), pipeline_mode=pl.Buffered(1))
    mod = pl.BlockSpec((1, 8, D), lambda i, pos: (i // NT_LAT, 0, 0))
    row_tab = pl.BlockSpec((2, SUBLANES, LANES), lambda i, pos: (0, i // (SUBLANES // ROPE_TILE_ROWS), 0))
    col_tab = pl.BlockSpec((1, 2, TM, LANES), lambda i, pos: (i // NT_LAT, 0, 0, 0))
    return pl.pallas_call(
        _inproj1_kernel,
        grid_spec=pltpu.PrefetchScalarGridSpec(
            num_scalar_prefetch=1,
            grid=(NT,),
            in_specs=[tok(D), pl.BlockSpec(memory_space=pl.ANY), mod, mod, res((1, D)), res((D, 2 * RET_QK + 2 * RET_V)),
                      row_tab, col_tab],
            out_specs=[tok(D), tok(RET_QK), tok(RET_QK), tok(RET_V), tok(RET_V)],
            scratch_shapes=[pltpu.VMEM((3, TM // SUBLANES, SUBLANES, D), F32), pltpu.SemaphoreType.DMA((2,))],
        ),
        out_shape=[
            jax.ShapeDtypeStruct((T, D), F32),
            jax.ShapeDtypeStruct((T, RET_QK), BF16),
            jax.ShapeDtypeStruct((T, RET_QK), BF16),
            jax.ShapeDtypeStruct((T, RET_V), BF16),
            jax.ShapeDtypeStruct((T, RET_V), BF16),
        ],
        compiler_params=_params(("arbitrary",)),
        name="inproj1",
    )(pos, h1, _by_sublane_tile(ys), modv0, modv1, g, w_in, *tabs)


ROPE_TILE_ROWS = TM // GRID_W


def _rope_tables():
    n = LANES // 2
    inv_freq = ROPE_BASE ** (-np.arange(n, dtype=np.float64) / n)

    def lines(p):
        ang = p[:, None] * inv_freq[None, :]
        return np.stack([np.concatenate([np.cos(ang), np.cos(ang)], axis=1),
                         np.concatenate([-np.sin(ang), np.sin(ang)], axis=1)])

    ident = np.stack([np.ones((TM, LANES)), np.zeros((TM, LANES))])
    row_tab = np.concatenate([lines(np.arange(ROWS, dtype=np.float64)), ident[:, :SUBLANES]], axis=1)
    col_lat = lines(np.tile(np.arange(GRID_W, dtype=np.float64), ROPE_TILE_ROWS))
    col_tab = np.stack([col_lat, ident])
    return jnp.asarray(row_tab, dtype=F32), jnp.asarray(col_tab, dtype=F32)


RET_NCHUNK = T // RET_CHUNK
RET_CTX_CHUNKS = CTX // RET_CHUNK
RET_LAT_CHUNKS = SEQ // RET_CHUNK
RET_NHD = 2 * RET_HEADS


def _ret_kernel(dec_ref, qf_ref, kf_ref, vf_ref, qb_ref, kb_ref, vb_ref, of_ref, ob_ref,
                state, dm, qd, kd, cd):
    C = RET_CHUNK

    @pl.when(pl.program_id(0) == 0)
    def _():
        state[...] = jnp.zeros_like(state)
        n = lax.broadcasted_iota(jnp.int32, (C, C), 0).astype(F32)
        m = lax.broadcasted_iota(jnp.int32, (C, C), 1).astype(F32)
        for d in range(2):
            for h in range(RET_HEADS):
                idx = d * RET_HEADS + h
                lg = jnp.log1p(-jnp.exp(jnp.full((C, C), dec_ref[d, h], F32)))
                diff = n - m if d == 0 else m - n
                low = diff >= 0.0
                dm[idx] = jnp.where(low, jnp.exp(jnp.where(low, diff, 0.0) * lg), 0.0)
                qpow = n + 1.0 if d == 0 else C - n
                kpow = (C - 1.0) - n if d == 0 else n
                qcol = jnp.exp(qpow * lg)
                kcol = jnp.exp(kpow * lg)
                for c0 in range(0, RET_DV, C):
                    qd[idx, :, c0:c0 + C] = qcol
                for c0 in range(0, RET_DK, C):
                    kd[idx, :, c0:c0 + C] = kcol
                for c0 in range(0, RET_DV, LANES):
                    cd[idx, :, c0:c0 + LANES] = jnp.exp(C * lg[0:8, 0:LANES])

    for d, (q_ref, k_ref, v_ref, o_ref) in enumerate(((qf_ref, kf_ref, vf_ref, of_ref),
                                                       (qb_ref, kb_ref, vb_ref, ob_ref))):
        for h in range(RET_HEADS):
            idx = d * RET_HEADS + h
            q = q_ref[:, h * RET_DK:(h + 1) * RET_DK]
            k = k_ref[:, h * RET_DK:(h + 1) * RET_DK]
            v = v_ref[:, h * RET_DV:(h + 1) * RET_DV]
            st = state[idx]
            s = _dot_nt(q, k) * dm[idx]
            o = _dot(s.astype(BF16), v) + qd[idx] * _dot(q, st.astype(BF16))
            o_ref[:, h * RET_DV:(h + 1) * RET_DV] = o.astype(BF16)
            kdk = (k.astype(F32) * kd[idx]).astype(BF16)
            cdv = cd[idx, 0:1, :]
            state[idx] = st * cdv + _dot(kdk.T, v)


def _ret_scan(decay, q, k, v):
    def fwd(s):
        return jnp.where(s < RET_CTX_CHUNKS, RET_LAT_CHUNKS + s, s - RET_CTX_CHUNKS)

    def bwd(s):
        return RET_NCHUNK - 1 - s

    def spec(w, f):
        return pl.BlockSpec((RET_CHUNK, w), lambda s: (f(s), 0))

    return pl.pallas_call(
        _ret_kernel,
        grid=(RET_NCHUNK,),
        in_specs=[
            pl.BlockSpec(memory_space=pltpu.SMEM),
            spec(RET_QK, fwd), spec(RET_QK, fwd), spec(RET_V, fwd),
            spec(RET_QK, bwd), spec(RET_QK, bwd), spec(RET_V, bwd),
        ],
        out_specs=[spec(RET_V, fwd), spec(RET_V, bwd)],
        out_shape=[jax.ShapeDtypeStruct((T, RET_V), BF16), jax.ShapeDtypeStruct((T, RET_V), BF16)],
        scratch_shapes=[
            pltpu.VMEM((RET_NHD, RET_DK, RET_DV), F32),
            pltpu.VMEM((RET_NHD, RET_CHUNK, RET_CHUNK), F32),
            pltpu.VMEM((RET_NHD, RET_CHUNK, RET_DV), F32),
            pltpu.VMEM((RET_NHD, RET_CHUNK, RET_DK), F32),
            pltpu.VMEM((RET_NHD, 8, RET_DV), F32),
        ],
        compiler_params=_params(("arbitrary",)),
        name="retention_scan",
    )(decay, q, k, v, q, k, v)


def _outproj1_kernel(of_ref, ob_ref, gg_ref, h_ref, mod_ref, w_ref, *rest):
    def residual_tile():
        y = jnp.zeros((TM, D), F32)
        for hh in range(RET_HEADS):
            cols = slice(hh * RET_DV, (hh + 1) * RET_DV)
            o = of_ref[:, cols].astype(F32) + ob_ref[:, cols].astype(F32)
            on = o * lax.rsqrt(jnp.mean(o * o, axis=-1, keepdims=True) + RMS_EPS)
            g = gg_ref[:, cols].astype(F32)
            z = (g * jax.nn.sigmoid(g)) * on
            y = y + _dot(z.astype(BF16), w_ref[cols, :])
        return h_ref[...] + mod_ref[0, 2:3, :] * y

    _two_stage(residual_tile, mod_ref, *rest)


def _outproj1(o_f, o_b, gg, h, modv, w_out, g2, rw, rb, triu):
    e_in, e_out, e_shape = _epilogue_specs(NT_LAT)
    tok = lambda w: pl.BlockSpec((TM, w), lambda i: (jnp.minimum(i, NT_LAT - 1), 0))
    latent_mod = pl.BlockSpec((1, 8, D), lambda i: (0, 0, 0))
    return pl.pallas_call(
        _outproj1_kernel,
        grid=(NT_LAT + 1,),
        in_specs=[tok(RET_V), tok(RET_V), tok(RET_V), tok(D), latent_mod, _resident((RET_V, D))] + e_in,
        out_specs=e_out,
        out_shape=e_shape,
        scratch_shapes=_EPILOGUE_SCRATCH,
        compiler_params=_params(("arbitrary",)),
        name="outproj1",
    )(o_f, o_b, gg, h, modv, w_out, g2, rw, rb, triu)


def _final_kernel(pos_ref, h_ref, ys_hbm, mod_ref, g_ref, o_ref, buf, sem):
    f = _gathered_rows(pos_ref, ys_hbm, buf, sem, request_next=True)
    h = h_ref[...] + mod_ref[0, 5:6, :] * f
    o_ref[...] = h * lax.rsqrt(jnp.mean(h * h, axis=-1, keepdims=True) + RMS_EPS) * g_ref[...]


def _final(pos, h, ys, modv, g):
    tok = pl.BlockSpec((TM, D), lambda i, pos: (i, 0))
    return pl.pallas_call(
        _final_kernel,
        grid_spec=pltpu.PrefetchScalarGridSpec(
            num_scalar_prefetch=1,
            grid=(NT_LAT,),
            in_specs=[tok, pl.BlockSpec(memory_space=pl.ANY),
                      pl.BlockSpec((1, 8, D), lambda i, pos: (0, 0, 0)),
                      pl.BlockSpec((1, D), lambda i, pos: (0, 0))],
            out_specs=tok,
            scratch_shapes=_GATHER_SCRATCH,
        ),
        out_shape=jax.ShapeDtypeStruct((SEQ, D), F32),
        compiler_params=_params(("arbitrary",)),
        name="final_norm",
    )(pos, h, _by_sublane_tile(ys), modv, g)


def _dft_constants():
    c = np.arange(FNET_GD)
    ang = 2.0 * np.pi * np.outer(c, c) / FNET_GD
    d64 = np.zeros((FNET_W, 2 * FNET_W))
    for g in range(FNET_W // FNET_GD):
        sl = slice(g * FNET_GD, (g + 1) * FNET_GD)
        d64[sl, sl] = np.cos(ang)
        d64[sl, FNET_W + g * FNET_GD:FNET_W + (g + 1) * FNET_GD] = -np.sin(ang)
    k2 = np.arange(FFT_N2)[:, None]
    n2 = np.arange(FFT_N2)[None, :]
    ga = np.zeros((FFT_N1, 2 * FFT_N2, 2 * FFT_N2))
    for n1 in range(FFT_N1):
        th = 2.0 * np.pi * ((k2 * (n1 + FFT_N1 * n2)) % SEQ) / SEQ
        cs, sn = np.cos(th), np.sin(th)
        ga[n1] = np.block([[cs, sn], [-sn, cs]])
    k1 = np.arange(FFT_N1)
    ph = 2.0 * np.pi * np.outer(k1, k1) / FFT_N1
    csb = np.concatenate([np.cos(ph), np.sin(ph)], axis=1)
    p = np.arange(CTX)
    pc = 2.0 * np.pi * (np.outer(p, p) % CTX) / CTX
    csc = np.concatenate([np.cos(pc), np.sin(pc)], axis=1)
    as_bf16 = lambda a: jnp.asarray(a, dtype=F32).astype(BF16)
    return as_bf16(d64), as_bf16(ga), as_bf16(csb), as_bf16(csc)


def _mod_table(mod_l):
    t = mod_l.reshape(2, 6, D)
    return jnp.concatenate([t, jnp.zeros((2, 2, D), F32)], axis=1)


def kernel(x, c, ctx, c_ctx, ada_w, ada_b, norm_g, final_norm_g, mixab_w_in, mixab_w_out, na_rpb,
           ret_w_in, ret_w_out, ret_decay, router_w, router_b, moe_w1, moe_w3, moe_w2):
    d64, ga, csb, csc = _dft_constants()
    triu = jnp.asarray(np.triu(np.ones((TM, TM)), 1), dtype=BF16)
    rw = jnp.pad(router_w, ((0, 0), (0, LANES - N_EXPERTS)))
    rb = jnp.broadcast_to(router_b[:, None], (N_EXPERTS, TM))
    w1 = moe_w1.reshape(DEPTH * N_EXPERTS, D, D_EXPERT)
    w3 = moe_w3.reshape(DEPTH * N_EXPERTS, D, D_EXPERT)
    w2 = moe_w2.reshape(DEPTH * N_EXPERTS, D_EXPERT, D)

    mod = _mod_vectors(c, c_ctx, ada_w, ada_b)
    modv0, modv1 = _mod_table(mod[0]), _mod_table(mod[1])

    qkv, br, bi = _inproj0(x[0], ctx[0], modv0, norm_g[0, 0].reshape(1, D), mixab_w_in[0].astype(BF16), d64)
    a_lat = _fft_latent(br, bi, ga, csb)
    ao_ctx = _ctx_mixer(qkv, br, bi, csc)
    o_lat = _na_attention(qkv, _na_bias_rows(na_rpb[0]))
    h1, xrow, meta, cnt = _outproj0(a_lat, o_lat, ao_ctx, x[0], ctx[0], modv0, mixab_w_out[0].astype(BF16),
                                    norm_g[0, 1].reshape(1, D), rw, rb, triu)
    n_tiles0 = NT + N_REAL_BUCKETS
    pos0, e_lo, e_hi, n_used = _route_plan(meta, cnt, n_tiles0, 0)
    ys0 = _moe(e_lo, e_hi, n_used, _dispatch(pos0, xrow, n_tiles0), w1, w3, w2, n_tiles0)

    h2, q, k, v, gg = _inproj1(pos0, h1, ys0, modv0, modv1, norm_g[1, 0].reshape(1, D),
                               ret_w_in[0].astype(BF16), _rope_tables())
    o_f, o_b = _ret_scan(ret_decay[0].astype(F32), q, k, v)
    h3, xrow1, meta1, cnt1 = _outproj1(o_f, o_b, gg, h2, modv1, ret_w_out[0].astype(BF16),
                                       norm_g[1, 1].reshape(1, D), rw, rb, triu)
    n_tiles1 = NT_LAT + N_REAL_BUCKETS
    pos1, e_lo1, e_hi1, n_used1 = _route_plan(meta1, cnt1, n_tiles1, 1)
    ys1 = _moe(e_lo1, e_hi1, n_used1, _dispatch(pos1, xrow1, n_tiles1), w1, w3, w2, n_tiles1)
    out = _final(pos1, h3, ys1, modv1, final_norm_g.reshape(1, D))
    return out[None]
```

```python
import functools
import math

import numpy as np
import jax
import jax.numpy as jnp
from jax import lax
from jax.experimental import pallas as pl
from jax.experimental.pallas import tpu as pltpu

F32 = jnp.float32
BF16 = jnp.bfloat16

D = 1024
SEQ = 16384
CTX = 256
T = SEQ + CTX
DEPTH = 2
GRID_W = 64
ROWS = SEQ // GRID_W
RMS_EPS = 1e-6
ROPE_BASE = 10000.0

FNET_W = 256
FNET_GD = 64
NA_HEADS = 12
NA_HD = 64
NA_W = NA_HEADS * NA_HD
NA_KH = 8
NA_KW = 16
AB_IN = FNET_W + 3 * NA_W
QKV_W = 3 * NA_W

RET_HEADS = 4
RET_DK = 256
RET_DV = 512
RET_QK = RET_HEADS * RET_DK
RET_V = RET_HEADS * RET_DV
RET_CHUNK = 256

N_EXPERTS = 16
EPG = 4
D_EXPERT = 512

LANES = 128
TM = 256
NT = T // TM
NT_LAT = SEQ // TM
TMM = TM
N_BUCKET_IDS = 64
N_REAL_BUCKETS = 24
XROW_W = D + LANES
FFT_N1 = 128
FFT_N2 = 128
NEG_BIG = -1e30
SUBLANES = 8

VMEM_LIMIT = 56 * 1024 * 1024


def _params(sem, vmem=VMEM_LIMIT):
    return pltpu.CompilerParams(dimension_semantics=sem, vmem_limit_bytes=vmem)


def _dot(a, b):
    return jnp.dot(a, b, preferred_element_type=F32)


def _dot_nt(a, b):
    return lax.dot_general(a, b, (((1,), (1,)), ((), ())), preferred_element_type=F32)


def _dot_tn(a, b):
    return lax.dot_general(a, b, (((0,), (0,)), ((), ())), preferred_element_type=F32)


def _rms_mod(x, g, sh, sc):
    ms = jnp.mean(x * x, axis=-1, keepdims=True)
    y = x * lax.rsqrt(ms + RMS_EPS) * g
    return y * (1.0 + sc) + sh


def _resident(shape):
    nd = len(shape)
    return pl.BlockSpec(shape, lambda *_: (0,) * nd, pipeline_mode=pl.Buffered(1))


MOD_TN = 768


def _mod_kernel(cs_ref, w_ref, b_ref, o_ref):
    cs = cs_ref[...]
    s = cs * jax.nn.sigmoid(cs)
    w = w_ref[0]
    r0 = jnp.sum(s[:, 0:1] * w, axis=0, keepdims=True)
    r1 = jnp.sum(s[:, 1:2] * w, axis=0, keepdims=True)
    o_ref[0] = jnp.concatenate([r0, r1], axis=0) + b_ref[0]


def _mod_vectors(c, c_ctx, ada_w, ada_b):
    cs = jnp.stack([c[0], c_ctx], axis=1)
    n = 6 * D
    return pl.pallas_call(
        _mod_kernel,
        grid=(DEPTH, n // MOD_TN),
        in_specs=[
            pl.BlockSpec((D, 2), lambda l, j: (0, 0)),
            pl.BlockSpec((1, D, MOD_TN), lambda l, j: (l, 0, j)),
            pl.BlockSpec((1, 1, MOD_TN), lambda l, j: (l, 0, j)),
        ],
        out_specs=pl.BlockSpec((1, 2, MOD_TN), lambda l, j: (l, 0, j)),
        out_shape=jax.ShapeDtypeStruct((DEPTH, 2, n), F32),
        compiler_params=_params(("arbitrary", "arbitrary")),
        name="mod_vectors",
    )(cs, ada_w, ada_b.reshape(DEPTH, 1, n))


def _mod_spec():
    return pl.BlockSpec((1, 8, D), lambda i, *_: (i // NT_LAT, 0, 0))


def _stream_tile(x_ref, ctx_ref):
    return jnp.where(pl.program_id(0) >= NT_LAT, ctx_ref[...], x_ref[...])


def _stream_specs():
    return [pl.BlockSpec((TM, D), lambda i: (jnp.minimum(i, NT_LAT - 1), 0)),
            pl.BlockSpec((CTX, D), lambda i: (0, 0))]


def _inproj0_kernel(x_ref, ctx_ref, mod_ref, g_ref, w_ref, d64_ref, qkv_ref, br_ref, bi_ref):
    h = _stream_tile(x_ref, ctx_ref)
    xm = _rms_mod(h, g_ref[...], mod_ref[0, 0:1, :], mod_ref[0, 1:2, :]).astype(BF16)
    p = _dot(xm, w_ref[...])
    qkv_ref[...] = p[:, FNET_W:].astype(BF16)
    b = _dot(p[:, :FNET_W].astype(BF16), d64_ref[...])
    br_ref[...] = b[:, :FNET_W].astype(BF16)
    bi_ref[...] = b[:, FNET_W:].astype(BF16)


def _inproj0(x, ctx, modv, g, w_in, d64):
    return pl.pallas_call(
        _inproj0_kernel,
        grid=(NT,),
        in_specs=_stream_specs() + [
            _mod_spec(),
            _resident((1, D)),
            _resident((D, AB_IN)),
            _resident((FNET_W, 2 * FNET_W)),
        ],
        out_specs=[
            pl.BlockSpec((TM, QKV_W), lambda i: (i, 0)),
            pl.BlockSpec((TM, FNET_W), lambda i: (i, 0)),
            pl.BlockSpec((TM, FNET_W), lambda i: (i, 0)),
        ],
        out_shape=[
            jax.ShapeDtypeStruct((T, QKV_W), BF16),
            jax.ShapeDtypeStruct((T, FNET_W), BF16),
            jax.ShapeDtypeStruct((T, FNET_W), BF16),
        ],
        compiler_params=_params(("arbitrary",)),
        name="inproj0",
    )(x, ctx, modv, g, w_in, d64)


FFT_A_NB = 8
FFT_B_TN = 4096


def _fft_a_kernel(br_ref, bi_ref, ga_ref, v_ref):
    for u in range(FFT_A_NB):
        cols = slice(u * FNET_W, (u + 1) * FNET_W)
        z = _dot(ga_ref[u, :, 0:FFT_N2], br_ref[:, cols]) + _dot(ga_ref[u, :, FFT_N2:], bi_ref[:, cols])
        v_ref[0, u] = z[:FFT_N2].astype(BF16)
        v_ref[1, u] = z[FFT_N2:].astype(BF16)


def _fft_b_kernel(cs_ref, v_ref, o_ref):
    o_ref[...] = _dot(cs_ref[...], v_ref[...]) * (1.0 / math.sqrt(SEQ * FNET_GD))


def _fft_latent(br, bi, ga, csb):
    width = FFT_N1 * FNET_W
    br2 = br.reshape(T // FFT_N1, width)
    bi2 = bi.reshape(T // FFT_N1, width)
    v = pl.pallas_call(
        _fft_a_kernel,
        grid=(FFT_N1 // FFT_A_NB,),
        in_specs=[
            pl.BlockSpec((FFT_N2, FFT_A_NB * FNET_W), lambda j: (0, j)),
            pl.BlockSpec((FFT_N2, FFT_A_NB * FNET_W), lambda j: (0, j)),
            pl.BlockSpec((FFT_A_NB, 2 * FFT_N2, 2 * FFT_N2), lambda j: (j, 0, 0)),
        ],
        out_specs=pl.BlockSpec((2, FFT_A_NB, FFT_N2, FNET_W), lambda j: (0, j, 0, 0)),
        out_shape=jax.ShapeDtypeStruct((2, FFT_N1, FFT_N2, FNET_W), BF16),
        compiler_params=_params(("arbitrary",)),
        name="fft_stage_a",
    )(br2, bi2, ga)
    v2 = v.reshape(2 * FFT_N1, FFT_N2 * FNET_W)
    x = pl.pallas_call(
        _fft_b_kernel,
        grid=(FFT_N2 * FNET_W // FFT_B_TN,),
        in_specs=[
            _resident((FFT_N1, 2 * FFT_N1)),
            pl.BlockSpec((2 * FFT_N1, FFT_B_TN), lambda j: (0, j)),
        ],
        out_specs=pl.BlockSpec((FFT_N1, FFT_B_TN), lambda j: (0, j)),
        out_shape=jax.ShapeDtypeStruct((FFT_N1, FFT_N2 * FNET_W), F32),
        compiler_params=_params(("arbitrary",)),
        name="fft_stage_b",
    )(csb, v2)
    return x.reshape(SEQ, FNET_W)


def _head_pair_masks():
    lane = lax.broadcasted_iota(jnp.int32, (1, LANES), 1)
    return lane < NA_HD


def _ctx_kernel(qkv_ref, br_ref, bi_ref, cs_ref, o_ref):
    a = _dot(cs_ref[:, 0:CTX], br_ref[...]) + _dot(cs_ref[:, CTX:], bi_ref[...])
    o_ref[:, 0:FNET_W] = (a * (1.0 / math.sqrt(CTX * FNET_GD))).astype(BF16)
    m0 = _head_pair_masks()
    for hp in range(NA_HEADS // 2):
        q = qkv_ref[:, hp * LANES:(hp + 1) * LANES]
        k = qkv_ref[:, NA_W + hp * LANES:NA_W + (hp + 1) * LANES]
        v = qkv_ref[:, 2 * NA_W + hp * LANES:2 * NA_W + (hp + 1) * LANES]
        outs = []
        for a_ in range(2):
            qa = jnp.where(m0 if a_ == 0 else jnp.logical_not(m0), q, jnp.zeros_like(q))
            s = _dot_nt(qa, k) * (NA_HD ** -0.5)
            p = jnp.exp(s - jnp.max(s, axis=-1, keepdims=True))
            l = jnp.sum(p, axis=-1, keepdims=True)
            outs.append(_dot(p.astype(BF16), v) / l)
        o_ref[:, FNET_W + hp * LANES:FNET_W + (hp + 1) * LANES] = jnp.where(m0, outs[0], outs[1]).astype(BF16)


def _ctx_mixer(qkv, br, bi, csc):
    return pl.pallas_call(
        _ctx_kernel,
        grid=(1,),
        in_specs=[
            pl.BlockSpec((CTX, QKV_W), lambda i: (SEQ // CTX, 0)),
            pl.BlockSpec((CTX, FNET_W), lambda i: (SEQ // CTX, 0)),
            pl.BlockSpec((CTX, FNET_W), lambda i: (SEQ // CTX, 0)),
            pl.BlockSpec((CTX, 2 * CTX), lambda i: (0, 0)),
        ],
        out_specs=pl.BlockSpec((CTX, D), lambda i: (0, 0)),
        out_shape=jax.ShapeDtypeStruct((CTX, D), BF16),
        compiler_params=_params(("arbitrary",)),
        name="ctx_mixer",
    )(qkv, br, bi, csc)


NA_RB = 8
NA_WIN = NA_KH * GRID_W


NA_NK = NA_WIN + CTX
NA_SUB = 16


def _na_kernel(q_ref, k_ref, v_ref, kc_ref, vc_ref, bias_ref, o_ref, s_scr, p_scr, l_scr):
    b = pl.program_id(1)
    m0 = _head_pair_masks()
    nm0 = jnp.logical_not(m0)
    scale = NA_HD ** -0.5

    def window(i):
        r = b * NA_RB + i
        rs = jnp.clip(r - NA_KH // 2, 0, ROWS - NA_KH)
        return pl.multiple_of(rs * GRID_W, GRID_W), rs - r + (NA_KH - 1)

    def scores(i):
        start, _ = window(i)
        q = q_ref[i * GRID_W:(i + 1) * GRID_W, :] * scale
        qs = jnp.concatenate([jnp.where(m0, q, jnp.zeros_like(q)), jnp.where(nm0, q, jnp.zeros_like(q))], axis=0)
        s_scr[i, :, 0:NA_WIN] = _dot_nt(qs, k_ref[pl.ds(start, NA_WIN), :])
        s_scr[i, :, NA_WIN:NA_NK] = _dot_nt(qs, kc_ref[...])

    def softmax(i):
        _, e = window(i)
        for g in range(2 * GRID_W // NA_SUB):
            rows = slice(g * NA_SUB, (g + 1) * NA_SUB)
            a, c0 = divmod(g * NA_SUB, GRID_W)
            s1 = s_scr[i, rows, 0:NA_WIN] + bias_ref[a, e, c0:c0 + NA_SUB, :]
            s2 = s_scr[i, rows, NA_WIN:NA_NK]
            mx = jnp.maximum(jnp.max(s1, axis=-1, keepdims=True), jnp.max(s2, axis=-1, keepdims=True))
            p1 = jnp.exp(s1 - mx)
            p2 = jnp.exp(s2 - mx)
            l = jnp.sum(p1, axis=-1, keepdims=True) + jnp.sum(p2, axis=-1, keepdims=True)
            p_scr[i, rows, 0:NA_WIN] = p1.astype(BF16)
            p_scr[i, rows, NA_WIN:NA_NK] = p2.astype(BF16)
            l_scr[i, rows, :] = jnp.broadcast_to(l, (NA_SUB, LANES))

    def values(i):
        start, _ = window(i)
        o = _dot(p_scr[i, :, 0:NA_WIN], v_ref[pl.ds(start, NA_WIN), :]) + _dot(p_scr[i, :, NA_WIN:NA_NK], vc_ref[...])
        o = o / l_scr[i]
        o_ref[i * GRID_W:(i + 1) * GRID_W, :] = jnp.where(m0, o[0:GRID_W], o[GRID_W:]).astype(BF16)

    scores(0)
    for i in range(NA_RB):
        if i + 1 < NA_RB:
            scores(i + 1)
        softmax(i)
        values(i)


def _na_attention(qkv, bias):
    nq = NA_RB * GRID_W
    nhp = NA_HEADS // 2
    return pl.pallas_call(
        _na_kernel,
        grid=(nhp, ROWS // NA_RB),
        in_specs=[
            pl.BlockSpec((nq, LANES), lambda hp, b: (b, hp)),
            pl.BlockSpec((SEQ, LANES), lambda hp, b: (0, nhp + hp)),
            pl.BlockSpec((SEQ, LANES), lambda hp, b: (0, 2 * nhp + hp)),
            pl.BlockSpec((CTX, LANES), lambda hp, b: (SEQ // CTX, nhp + hp)),
            pl.BlockSpec((CTX, LANES), lambda hp, b: (SEQ // CTX, 2 * nhp + hp)),
            pl.BlockSpec((2, NA_KH, GRID_W, NA_WIN), lambda hp, b: (hp, 0, 0, 0)),
        ],
        out_specs=pl.BlockSpec((nq, LANES), lambda hp, b: (b, hp)),
        out_shape=jax.ShapeDtypeStruct((SEQ, NA_W), BF16),
        scratch_shapes=[
            pltpu.VMEM((NA_RB, 2 * GRID_W, NA_NK), F32),
            pltpu.VMEM((NA_RB, 2 * GRID_W, NA_NK), BF16),
            pltpu.VMEM((NA_RB, 2 * GRID_W, LANES), F32),
        ],
        compiler_params=_params(("arbitrary", "arbitrary")),
        name="na_attention",
    )(qkv, qkv, qkv, qkv, qkv, bias)


def _na_bias_rows(rpb):
    c = np.arange(GRID_W)[:, None]
    kc = np.arange(GRID_W)[None, :]
    ws = np.clip(c - NA_KW // 2, 0, GRID_W - NA_KW)
    col_ok = (kc >= ws) & (kc < ws + NA_KW)
    dc = np.clip(kc - c + NA_KW - 1, 0, 2 * NA_KW - 2)
    pick = (dc[None] == np.arange(2 * NA_KW - 1)[:, None, None]).astype(np.float32)
    full = jnp.einsum("hrd,dck->hrck", rpb.astype(F32), jnp.asarray(pick), precision=lax.Precision.HIGHEST)
    full = jnp.where(jnp.asarray(col_ok)[None, None], full, NEG_BIG)
    rows = [jnp.transpose(full[:, e:e + NA_KH], (0, 2, 1, 3)).reshape(NA_HEADS, GRID_W, NA_WIN)
            for e in range(NA_KH)]
    return jnp.stack(rows, axis=1)


def _route_stage(xm2, valid, rw_ref, rb_ref, triu_ref, run_ref, xrow_ref, meta_ref, cnt_ref):
    xrow_ref[:, 0:D] = xm2
    xh = xm2.astype(BF16)
    xl = (xm2 - xh.astype(F32)).astype(BF16)
    w = rw_ref[...]
    wh = w.astype(BF16)
    wl = (w - wh.astype(F32)).astype(BF16)
    logits = _dot(xh, wh) + (_dot(xl, wh) + _dot(xh, wl))
    scores = jax.nn.sigmoid(logits.T[0:N_EXPERTS, :])
    biased = scores + rb_ref[...]
    s = [scores[e:e + 1, :] for e in range(N_EXPERTS)]
    b = [biased[e:e + 1, :] for e in range(N_EXPERTS)]
    n_groups = N_EXPERTS // EPG
    top2, group_score = [], []
    for g in range(n_groups):
        m = b[EPG * g:EPG * (g + 1)]
        total = None
        for k in range(EPG):
            rank = None
            for j in range(EPG):
                if j != k:
                    beats = jnp.where((m[j] >= m[k]) if j < k else (m[j] > m[k]), 1.0, 0.0)
                    rank = beats if rank is None else rank + beats
            top2.append(rank < 2.0)
            term = jnp.where(top2[-1], m[k], 0.0)
            total = term if total is None else total + term
        group_score.append(total)
    best, g_sel = group_score[0], jnp.zeros((1, TM), jnp.int32)
    for g in range(1, n_groups):
        better = group_score[g] > best
        g_sel = jnp.where(better, g, g_sel)
        best = jnp.where(better, group_score[g], best)
    sel = [top2[e] & (g_sel == e // EPG) for e in range(N_EXPERTS)]
    w_sum = None
    for e in range(N_EXPERTS):
        term = jnp.where(sel[e], s[e], 0.0)
        w_sum = term if w_sum is None else w_sum + term
    zero = jnp.zeros((1, TM), F32)
    w_lo, w_hi, bid = zero, zero, zero
    for g in range(n_groups):
        seen = None
        for k in range(EPG):
            e = EPG * g + k
            if seen is None:
                w_lo = w_lo + jnp.where(sel[e], s[e], 0.0)
                seen = sel[e]
            else:
                w_lo = w_lo + jnp.where(sel[e] & jnp.logical_not(seen), s[e], 0.0)
                w_hi = w_hi + jnp.where(sel[e] & seen, s[e], 0.0)
                seen = seen | sel[e]
            bid = bid + jnp.where(sel[e], float((1 << k) + 8 * g), 0.0)
    w_lo = w_lo / w_sum
    w_hi = w_hi / w_sum
    bucket = lax.broadcasted_iota(jnp.int32, (N_BUCKET_IDS, TM), 0).astype(F32)
    onehot = bucket == bid
    ones = jnp.where(onehot & valid, 1.0, 0.0)
    earlier = _dot(ones.astype(BF16), triu_ref[...])
    run = run_ref[...]
    run_t = jnp.concatenate([run] * (TM // LANES), axis=1)
    rank_row = jnp.sum(jnp.where(onehot, earlier + run_t, 0.0), axis=0, keepdims=True)
    new_run = run + jnp.sum(ones, axis=1, keepdims=True)
    run_ref[...] = new_run
    cnt_ref[...] = new_run
    meta_rows = jnp.concatenate([w_lo, w_hi, bid, rank_row, jnp.zeros((SUBLANES - 4, TM), F32)], axis=0)
    meta_ref[...] = meta_rows
    padded = jnp.concatenate([meta_rows, jnp.zeros((LANES - SUBLANES, TM), F32)], axis=0)
    xrow_ref[:, D:XROW_W] = padded.T


def _epilogue_specs(ntiles):
    in_specs = [
        _resident((1, D)),
        _resident((D, LANES)),
        _resident((N_EXPERTS, TM)),
        _resident((TM, TM)),
    ]
    out_specs = [
        pl.BlockSpec((TM, D), lambda i: (jnp.minimum(i, ntiles - 1), 0)),
        pl.BlockSpec((TM, XROW_W), lambda i: (jnp.maximum(i - 1, 0), 0)),
        pl.BlockSpec((SUBLANES, TM), lambda i: (0, jnp.maximum(i - 1, 0))),
        pl.BlockSpec((N_BUCKET_IDS, LANES), lambda i: (0, 0)),
    ]
    n = ntiles * TM
    out_shape = [
        jax.ShapeDtypeStruct((n, D), F32),
        jax.ShapeDtypeStruct((n, XROW_W), F32),
        jax.ShapeDtypeStruct((SUBLANES, n), F32),
        jax.ShapeDtypeStruct((N_BUCKET_IDS, LANES), F32),
    ]
    return in_specs, out_specs, out_shape


_EPILOGUE_SCRATCH = [pltpu.VMEM((N_BUCKET_IDS, LANES), F32), pltpu.VMEM((2, TM, D), F32)]


def _two_stage(residual_tile, mod_ref, g2_ref, rw_ref, rb_ref, triu_ref,
               h1_ref, xrow_ref, meta_ref, cnt_ref, run_ref, stash):
    i = pl.program_id(0)

    @pl.when(i == 0)
    def _():
        run_ref[...] = jnp.zeros_like(run_ref)
        stash[1] = jnp.zeros((TM, D), F32)

    _route_stage(stash[1 - i % 2], i >= 1, rw_ref, rb_ref, triu_ref, run_ref, xrow_ref, meta_ref, cnt_ref)
    h1 = residual_tile()
    h1_ref[...] = h1
    stash[i % 2] = _rms_mod(h1, g2_ref[...], mod_ref[0, 3:4, :], mod_ref[0, 4:5, :])


def _outproj0_kernel(a_ref, o_ref, aoc_ref, x_ref, ctx_ref, mod_ref, w_ref, *rest):
    def residual_tile():
        is_ctx = pl.program_id(0) >= NT_LAT
        a = jnp.where(is_ctx, aoc_ref[:, 0:FNET_W], a_ref[...].astype(BF16))
        o = jnp.where(is_ctx, aoc_ref[:, FNET_W:], o_ref[...])
        y = _dot(a, w_ref[0:FNET_W, :]) + _dot(o, w_ref[FNET_W:, :])
        return _stream_tile(x_ref, ctx_ref) + mod_ref[0, 2:3, :] * y

    _two_stage(residual_tile, mod_ref, *rest)


def _outproj0(a_lat, o_lat, ao_ctx, x, ctx, modv, w_out, g2, rw, rb, triu):
    e_in, e_out, e_shape = _epilogue_specs(NT)
    lat = lambda w: pl.BlockSpec((TM, w), lambda i: (jnp.minimum(i, NT_LAT - 1), 0))
    return pl.pallas_call(
        _outproj0_kernel,
        grid=(NT + 1,),
        in_specs=[lat(FNET_W), lat(NA_W), pl.BlockSpec((CTX, D), lambda i: (0, 0))] + _stream_specs() + [
            _mod_spec(),
            _resident((D, D)),
        ] + e_in,
        out_specs=e_out,
        out_shape=e_shape,
        scratch_shapes=_EPILOGUE_SCRATCH,
        compiler_params=_params(("arbitrary",)),
        name="outproj0",
    )(a_lat, o_lat, ao_ctx, x, ctx, modv, w_out, g2, rw, rb, triu)


def _bucket_tables():
    lo = np.zeros(N_BUCKET_IDS, np.int32)
    hi = np.zeros(N_BUCKET_IDS, np.int32)
    for g in range(N_EXPERTS // EPG):
        for a in range(EPG):
            for b in range(a + 1, EPG):
                i = 16 * g + (1 << a) + (1 << b)
                lo[i] = EPG * g + a
                hi[i] = EPG * g + b
    return lo, hi


def _slot_kernel(meta_ref, starts_ref, o_ref):
    n = meta_ref.shape[1]
    bucket = lax.broadcasted_iota(jnp.int32, (N_BUCKET_IDS, n), 0).astype(F32)
    first = jnp.sum(jnp.where(bucket == meta_ref[2:3, :], starts_ref[:, 0:1], 0.0), axis=0, keepdims=True)
    o_ref[...] = jnp.broadcast_to((first + meta_ref[3:4, :]).astype(jnp.int32), (SUBLANES, n))


def _token_slots(meta, first_row):
    n = meta.shape[1]
    out = pl.pallas_call(
        _slot_kernel,
        grid=(1,),
        in_specs=[pl.BlockSpec((SUBLANES, n), lambda i: (0, 0)),
                  pl.BlockSpec((N_BUCKET_IDS, LANES), lambda i: (0, 0))],
        out_specs=pl.BlockSpec((SUBLANES, n), lambda i: (0, 0)),
        out_shape=jax.ShapeDtypeStruct((SUBLANES, n), jnp.int32),
        compiler_params=_params(("arbitrary",)),
        name="token_slots",
    )(meta, first_row)
    return out[0]


def _route_plan(meta, cnt, n_tiles, layer):
    counts = cnt[:, 0].astype(jnp.int32)
    nt_b = (counts + TMM - 1) // TMM
    ends = jnp.cumsum(nt_b)
    starts = ends - nt_b
    first_row = jnp.broadcast_to((starts * TMM).astype(F32)[:, None], (N_BUCKET_IDS, LANES))
    pos = _token_slots(meta, first_row)
    n_used = ends[-1]
    tile = jnp.minimum(jnp.arange(n_tiles, dtype=jnp.int32), n_used - 1)
    tile_b = jnp.sum((ends[None, :] <= tile[:, None]).astype(jnp.int32), axis=1)
    lo, hi = _bucket_tables()
    e_lo = jnp.asarray(lo)[tile_b] + layer * N_EXPERTS
    e_hi = jnp.asarray(hi)[tile_b] + layer * N_EXPERTS
    return pos, e_lo.astype(jnp.int32), e_hi.astype(jnp.int32), n_used.reshape(1).astype(jnp.int32)


def _dispatch_kernel(pos_ref, x_ref, xs_in_ref, xs_ref, sem):
    del xs_in_ref
    base = pl.program_id(0) * TM

    def issue(j, carry):
        for u in range(SUBLANES):
            p = pos_ref[base + j * SUBLANES + u]
            pltpu.make_async_copy(x_ref.at[j, pl.ds(u, 1), :],
                                  xs_ref.at[p >> 3, pl.ds(p & (SUBLANES - 1), 1), :], sem).start(priority=u % 2)
        return carry

    lax.fori_loop(0, TM // SUBLANES, issue, 0)
    pltpu.make_async_copy(x_ref, xs_ref.at[pl.ds(0, TM // SUBLANES)], sem).wait()


def _dispatch(pos, xrow, n_tiles):
    ntok = xrow.shape[0]
    xs0 = jnp.zeros((n_tiles * TMM // SUBLANES, SUBLANES, XROW_W), F32)
    xs = pl.pallas_call(
        _dispatch_kernel,
        grid_spec=pltpu.PrefetchScalarGridSpec(
            num_scalar_prefetch=1,
            grid=(ntok // TM,),
            in_specs=[
                pl.BlockSpec((TM // SUBLANES, SUBLANES, XROW_W), lambda i, pos: (i, 0, 0)),
                pl.BlockSpec(memory_space=pl.ANY),
            ],
            out_specs=pl.BlockSpec(memory_space=pl.ANY),
            scratch_shapes=[pltpu.SemaphoreType.DMA(())],
        ),
        out_shape=jax.ShapeDtypeStruct(xs0.shape, F32),
        input_output_aliases={2: 0},
        compiler_params=_params(("arbitrary",)),
        name="moe_dispatch",
    )(pos, _by_sublane_tile(xrow), xs0)
    return xs.reshape(n_tiles * TMM, XROW_W)


def _moe_kernel(elo_ref, ehi_ref, nu_ref, xs_ref, w1a, w3a, w2a, w1b, w3b, w2b, ys_ref):
    j = pl.program_id(0)

    @pl.when(j < nu_ref[0])
    def _():
        x = xs_ref[:, 0:D].astype(BF16)
        g_lo = xs_ref[:, D:D + 1]
        g_hi = xs_ref[:, D + 1:D + 2]

        def expert(w1, w3, w2):
            a = _dot(x, w1[0].astype(BF16))
            hid = (a * jax.nn.sigmoid(a)) * _dot(x, w3[0].astype(BF16))
            return _dot(hid.astype(BF16), w2[0].astype(BF16))

        ys_ref[...] = g_lo * expert(w1a, w3a, w2a) + g_hi * expert(w1b, w3b, w2b)

    @pl.when(j >= nu_ref[0])
    def _():
        ys_ref[...] = jnp.zeros_like(ys_ref)


def _moe(e_lo, e_hi, n_used, xs, w1, w3, w2, n_tiles):
    def wspec(shape, which):
        if which == 0:
            return pl.BlockSpec(shape, lambda j, lo, hi, nu: (lo[j], 0, 0))
        return pl.BlockSpec(shape, lambda j, lo, hi, nu: (hi[j], 0, 0))

    s13 = (1, D, D_EXPERT)
    s2 = (1, D_EXPERT, D)
    return pl.pallas_call(
        _moe_kernel,
        grid_spec=pltpu.PrefetchScalarGridSpec(
            num_scalar_prefetch=3,
            grid=(n_tiles,),
            in_specs=[
                pl.BlockSpec((TMM, XROW_W), lambda j, lo, hi, nu: (j, 0)),
                wspec(s13, 0), wspec(s13, 0), wspec(s2, 0),
                wspec(s13, 1), wspec(s13, 1), wspec(s2, 1),
            ],
            out_specs=pl.BlockSpec((TMM, D), lambda j, lo, hi, nu: (j, 0)),
        ),
        out_shape=jax.ShapeDtypeStruct((n_tiles * TMM, D), F32),
        compiler_params=_params(("arbitrary",)),
        name="moe_experts",
    )(e_lo, e_hi, n_used, xs, w1, w3, w2, w1, w3, w2)


def _gather_issue(pos_ref, ys_hbm, buf, sem, tile, slot, inline, part=(0, TM // SUBLANES)):
    base = tile * TM

    def issue(j, carry):
        for u in range(SUBLANES):
            p = pos_ref[base + j * SUBLANES + u]
            pltpu.make_async_copy(ys_hbm.at[p >> 3, pl.ds(p & (SUBLANES - 1), 1), :],
                                  buf.at[slot, j, pl.ds(u, 1), :], sem.at[slot]).start(priority=u % 2)
        return carry

    if inline:
        for j in range(*part):
            issue(j, 0)
    else:
        lax.fori_loop(0, TM // SUBLANES, issue, 0)


def _gather_wait(ys_hbm, buf, sem, slot):
    pltpu.make_async_copy(ys_hbm.at[pl.ds(0, TM // SUBLANES)], buf.at[slot], sem.at[slot]).wait()


def _gathered_rows(pos_ref, ys_hbm, buf, sem, request_next):
    i = pl.program_id(0)
    slot = i % 2

    @pl.when(i == 0)
    def _():
        _gather_issue(pos_ref, ys_hbm, buf, sem, 0, 0, inline=False)

    if request_next:
        @pl.when(i + 1 < pl.num_programs(0))
        def _():
            _gather_issue(pos_ref, ys_hbm, buf, sem, i + 1, 1 - slot, inline=False)

    _gather_wait(ys_hbm, buf, sem, slot)
    return buf[slot].reshape(TM, D)


def _gather_drain(ys_hbm, buf, sem):
    i = pl.program_id(0)

    @pl.when(i + 1 == pl.num_programs(0))
    def _():
        _gather_wait(ys_hbm, buf, sem, 1 - i % 2)


_GATHER_SCRATCH = [pltpu.VMEM((2, TM // SUBLANES, SUBLANES, D), F32), pltpu.SemaphoreType.DMA((2,))]


def _by_sublane_tile(a):
    return a.reshape(a.shape[0] // SUBLANES, SUBLANES, a.shape[1])


def _rope_store(x, tabs, out_ref):
    cr, sr, cc, sc = tabs
    for ch in range(RET_QK // LANES):
        xc = x[:, ch * LANES:(ch + 1) * LANES]
        xr = pltpu.roll(xc, LANES // 2, axis=1)
        y = xc * cr + xr * sr if ch % 2 == 0 else xc * cc + xr * sc
        out_ref[:, ch * LANES:(ch + 1) * LANES] = y.astype(BF16)


def _inproj1_kernel(pos_ref, h_ref, ys_hbm, mod0_ref, mod1_ref, g_ref, w_ref, rt_ref, ct_ref,
                    h2_ref, q_ref, k_ref, v_ref, gg_ref, buf, sem):
    f = _gathered_rows(pos_ref, ys_hbm, buf, sem, request_next=False)
    h2 = h_ref[...] + mod0_ref[0, 5:6, :] * f
    h2_ref[...] = h2
    buf[2] = _rms_mod(h2, g_ref[...], mod1_ref[0, 0:1, :], mod1_ref[0, 1:2, :]).reshape(TM // SUBLANES, SUBLANES, D)
    i = pl.program_id(0)
    nxt = jnp.where(i + 1 < pl.num_programs(0), i + 1, 0)
    bounds = [round(k * (TM // SUBLANES) / 6) for k in range(7)]

    def xm_after_requests(part):
        _gather_issue(pos_ref, ys_hbm, buf, sem, nxt, 1 - i % 2, inline=True, part=(bounds[part], bounds[part + 1]))
        return buf[2].reshape(TM, D).astype(BF16)

    off = (pl.program_id(0) % (SUBLANES // ROPE_TILE_ROWS)) * ROPE_TILE_ROWS

    def row_table(which):
        return jnp.concatenate([jnp.broadcast_to(rt_ref[which, pl.ds(off + j, 1), :], (GRID_W, LANES))
                                for j in range(ROPE_TILE_ROWS)], axis=0)

    tabs = (row_table(0), row_table(1), ct_ref[0, 0], ct_ref[0, 1])
    _rope_store(_dot(xm_after_requests(0), w_ref[:, 0:RET_QK]), tabs, q_ref)
    _rope_store(_dot(xm_after_requests(1), w_ref[:, RET_QK:2 * RET_QK]) * (RET_DK ** -0.5), tabs, k_ref)
    half = RET_V // 2
    for part, (out_ref, col0) in enumerate(((v_ref, 2 * RET_QK), (gg_ref, 2 * RET_QK + RET_V))):
        for s in range(2):
            xm = xm_after_requests(2 + 2 * part + s)
            out_ref[:, s * half:(s + 1) * half] = _dot(
                xm, w_ref[:, col0 + s * half:col0 + (s + 1) * half]).astype(BF16)
    _gather_drain(ys_hbm, buf, sem)


def _inproj1(pos, h1, ys, modv0, modv1, g, w_in, tabs):
    tok = lambda w: pl.BlockSpec((TM, w), lambda i, pos: (i, 0))
    res = lambda shape: pl.BlockSpec(shape, lambda i, pos: (0,) * len(shape), pipeline_mode=pl.Buffered(1))
    mod = pl.BlockSpec((1, 8, D), lambda i, pos: (i // NT_LAT, 0, 0))
    row_tab = pl.BlockSpec((2, SUBLANES, LANES), lambda i, pos: (0, i // (SUBLANES // ROPE_TILE_ROWS), 0))
    col_tab = pl.BlockSpec((1, 2, TM, LANES), lambda i, pos: (i // NT_LAT, 0, 0, 0))
    return pl.pallas_call(
        _inproj1_kernel,
        grid_spec=pltpu.PrefetchScalarGridSpec(
            num_scalar_prefetch=1,
            grid=(NT,),
            in_specs=[tok(D), pl.BlockSpec(memory_space=pl.ANY), mod, mod, res((1, D)), res((D, 2 * RET_QK + 2 * RET_V)),
                      row_tab, col_tab],
            out_specs=[tok(D), tok(RET_QK), tok(RET_QK), tok(RET_V), tok(RET_V)],
            scratch_shapes=[pltpu.VMEM((3, TM // SUBLANES, SUBLANES, D), F32), pltpu.SemaphoreType.DMA((2,))],
        ),
        out_shape=[
            jax.ShapeDtypeStruct((T, D), F32),
            jax.ShapeDtypeStruct((T, RET_QK), BF16),
            jax.ShapeDtypeStruct((T, RET_QK), BF16),
            jax.ShapeDtypeStruct((T, RET_V), BF16),
            jax.ShapeDtypeStruct((T, RET_V), BF16),
        ],
        compiler_params=_params(("arbitrary",)),
        name="inproj1",
    )(pos, h1, _by_sublane_tile(ys), modv0, modv1, g, w_in, *tabs)


ROPE_TILE_ROWS = TM // GRID_W


def _rope_tables():
    n = LANES // 2
    inv_freq = ROPE_BASE ** (-np.arange(n, dtype=np.float64) / n)

    def lines(p):
        ang = p[:, None] * inv_freq[None, :]
        return np.stack([np.concatenate([np.cos(ang), np.cos(ang)], axis=1),
                         np.concatenate([-np.sin(ang), np.sin(ang)], axis=1)])

    ident = np.stack([np.ones((TM, LANES)), np.zeros((TM, LANES))])
    row_tab = np.concatenate([lines(np.arange(ROWS, dtype=np.float64)), ident[:, :SUBLANES]], axis=1)
    col_lat = lines(np.tile(np.arange(GRID_W, dtype=np.float64), ROPE_TILE_ROWS))
    col_tab = np.stack([col_lat, ident])
    return jnp.asarray(row_tab, dtype=F32), jnp.asarray(col_tab, dtype=F32)


RET_NCHUNK = T // RET_CHUNK
RET_CTX_CHUNKS = CTX // RET_CHUNK
RET_LAT_CHUNKS = SEQ // RET_CHUNK
RET_NHD = 2 * RET_HEADS


def _ret_kernel(dec_ref, qf_ref, kf_ref, vf_ref, qb_ref, kb_ref, vb_ref, of_ref, ob_ref,
                state, dm, qd, kd, cd):
    C = RET_CHUNK

    @pl.when(pl.program_id(0) == 0)
    def _():
        state[...] = jnp.zeros_like(state)
        n = lax.broadcasted_iota(jnp.int32, (C, C), 0).astype(F32)
        m = lax.broadcasted_iota(jnp.int32, (C, C), 1).astype(F32)
        for d in range(2):
            for h in range(RET_HEADS):
                idx = d * RET_HEADS + h
                lg = jnp.log1p(-jnp.exp(jnp.full((C, C), dec_ref[d, h], F32)))
                diff = n - m if d == 0 else m - n
                low = diff >= 0.0
                dm[idx] = jnp.where(low, jnp.exp(jnp.where(low, diff, 0.0) * lg), 0.0)
                qpow = n + 1.0 if d == 0 else C - n
                kpow = (C - 1.0) - n if d == 0 else n
                qcol = jnp.exp(qpow * lg)
                kcol = jnp.exp(kpow * lg)
                for c0 in range(0, RET_DV, C):
                    qd[idx, :, c0:c0 + C] = qcol
                for c0 in range(0, RET_DK, C):
                    kd[idx, :, c0:c0 + C] = kcol
                for c0 in range(0, RET_DV, LANES):
                    cd[idx, :, c0:c0 + LANES] = jnp.exp(C * lg[0:8, 0:LANES])

    for d, (q_ref, k_ref, v_ref, o_ref) in enumerate(((qf_ref, kf_ref, vf_ref, of_ref),
                                                       (qb_ref, kb_ref, vb_ref, ob_ref))):
        for h in range(RET_HEADS):
            idx = d * RET_HEADS + h
            q = q_ref[:, h * RET_DK:(h + 1) * RET_DK]
            k = k_ref[:, h * RET_DK:(h + 1) * RET_DK]
            v = v_ref[:, h * RET_DV:(h + 1) * RET_DV]
            st = state[idx]
            s = _dot_nt(q, k) * dm[idx]
            o = _dot(s.astype(BF16), v) + qd[idx] * _dot(q, st.astype(BF16))
            o_ref[:, h * RET_DV:(h + 1) * RET_DV] = o.astype(BF16)
            kdk = (k.astype(F32) * kd[idx]).astype(BF16)
            cdv = cd[idx, 0:1, :]
            state[idx] = st * cdv + _dot(kdk.T, v)


def _ret_scan(decay, q, k, v):
    def fwd(s):
        return jnp.where(s < RET_CTX_CHUNKS, RET_LAT_CHUNKS + s, s - RET_CTX_CHUNKS)

    def bwd(s):
        return RET_NCHUNK - 1 - s

    def spec(w, f):
        return pl.BlockSpec((RET_CHUNK, w), lambda s: (f(s), 0))

    return pl.pallas_call(
        _ret_kernel,
        grid=(RET_NCHUNK,),
        in_specs=[
            pl.BlockSpec(memory_space=pltpu.SMEM),
            spec(RET_QK, fwd), spec(RET_QK, fwd), spec(RET_V, fwd),
            spec(RET_QK, bwd), spec(RET_QK, bwd), spec(RET_V, bwd),
        ],
        out_specs=[spec(RET_V, fwd), spec(RET_V, bwd)],
        out_shape=[jax.ShapeDtypeStruct((T, RET_V), BF16), jax.ShapeDtypeStruct((T, RET_V), BF16)],
        scratch_shapes=[
            pltpu.VMEM((RET_NHD, RET_DK, RET_DV), F32),
            pltpu.VMEM((RET_NHD, RET_CHUNK, RET_CHUNK), F32),
            pltpu.VMEM((RET_NHD, RET_CHUNK, RET_DV), F32),
            pltpu.VMEM((RET_NHD, RET_CHUNK, RET_DK), F32),
            pltpu.VMEM((RET_NHD, 8, RET_DV), F32),
        ],
        compiler_params=_params(("arbitrary",)),
        name="retention_scan",
    )(decay, q, k, v, q, k, v)


def _outproj1_kernel(of_ref, ob_ref, gg_ref, h_ref, mod_ref, w_ref, *rest):
    def residual_tile():
        y = jnp.zeros((TM, D), F32)
        for hh in range(RET_HEADS):
            cols = slice(hh * RET_DV, (hh + 1) * RET_DV)
            o = of_ref[:, cols].astype(F32) + ob_ref[:, cols].astype(F32)
            on = o * lax.rsqrt(jnp.mean(o * o, axis=-1, keepdims=True) + RMS_EPS)
            g = gg_ref[:, cols].astype(F32)
            z = (g * jax.nn.sigmoid(g)) * on
            y = y + _dot(z.astype(BF16), w_ref[cols, :])
        return h_ref[...] + mod_ref[0, 2:3, :] * y

    _two_stage(residual_tile, mod_ref, *rest)


def _outproj1(o_f, o_b, gg, h, modv, w_out, g2, rw, rb, triu):
    e_in, e_out, e_shape = _epilogue_specs(NT_LAT)
    tok = lambda w: pl.BlockSpec((TM, w), lambda i: (jnp.minimum(i, NT_LAT - 1), 0))
    latent_mod = pl.BlockSpec((1, 8, D), lambda i: (0, 0, 0))
    return pl.pallas_call(
        _outproj1_kernel,
        grid=(NT_LAT + 1,),
        in_specs=[tok(RET_V), tok(RET_V), tok(RET_V), tok(D), latent_mod, _resident((RET_V, D))] + e_in,
        out_specs=e_out,
        out_shape=e_shape,
        scratch_shapes=_EPILOGUE_SCRATCH,
        compiler_params=_params(("arbitrary",)),
        name="outproj1",
    )(o_f, o_b, gg, h, modv, w_out, g2, rw, rb, triu)


def _final_kernel(pos_ref, h_ref, ys_hbm, mod_ref, g_ref, o_ref, buf, sem):
    f = _gathered_rows(pos_ref, ys_hbm, buf, sem, request_next=True)
    h = h_ref[...] + mod_ref[0, 5:6, :] * f
    o_ref[...] = h * lax.rsqrt(jnp.mean(h * h, axis=-1, keepdims=True) + RMS_EPS) * g_ref[...]


def _final(pos, h, ys, modv, g):
    tok = pl.BlockSpec((TM, D), lambda i, pos: (i, 0))
    return pl.pallas_call(
        _final_kernel,
        grid_spec=pltpu.PrefetchScalarGridSpec(
            num_scalar_prefetch=1,
            grid=(NT_LAT,),
            in_specs=[tok, pl.BlockSpec(memory_space=pl.ANY),
                      pl.BlockSpec((1, 8, D), lambda i, pos: (0, 0, 0)),
                      pl.BlockSpec((1, D), lambda i, pos: (0, 0))],
            out_specs=tok,
            scratch_shapes=_GATHER_SCRATCH,
        ),
        out_shape=jax.ShapeDtypeStruct((SEQ, D), F32),
        compiler_params=_params(("arbitrary",)),
        name="final_norm",
    )(pos, h, _by_sublane_tile(ys), modv, g)


def _dft_constants():
    c = np.arange(FNET_GD)
    ang = 2.0 * np.pi * np.outer(c, c) / FNET_GD
    d64 = np.zeros((FNET_W, 2 * FNET_W))
    for g in range(FNET_W // FNET_GD):
        sl = slice(g * FNET_GD, (g + 1) * FNET_GD)
        d64[sl, sl] = np.cos(ang)
        d64[sl, FNET_W + g * FNET_GD:FNET_W + (g + 1) * FNET_GD] = -np.sin(ang)
    k2 = np.arange(FFT_N2)[:, None]
    n2 = np.arange(FFT_N2)[None, :]
    ga = np.zeros((FFT_N1, 2 * FFT_N2, 2 * FFT_N2))
    for n1 in range(FFT_N1):
        th = 2.0 * np.pi * ((k2 * (n1 + FFT_N1 * n2)) % SEQ) / SEQ
        cs, sn = np.cos(th), np.sin(th)
        ga[n1] = np.block([[cs, sn], [-sn, cs]])
    k1 = np.arange(FFT_N1)
    ph = 2.0 * np.pi * np.outer(k1, k1) / FFT_N1
    csb = np.concatenate([np.cos(ph), np.sin(ph)], axis=1)
    p = np.arange(CTX)
    pc = 2.0 * np.pi * (np.outer(p, p) % CTX) / CTX
    csc = np.concatenate([np.cos(pc), np.sin(pc)], axis=1)
    as_bf16 = lambda a: jnp.asarray(a, dtype=F32).astype(BF16)
    return as_bf16(d64), as_bf16(ga), as_bf16(csb), as_bf16(csc)


def _mod_table(mod_l):
    t = mod_l.reshape(2, 6, D)
    return jnp.concatenate([t, jnp.zeros((2, 2, D), F32)], axis=1)


def kernel(x, c, ctx, c_ctx, ada_w, ada_b, norm_g, final_norm_g, mixab_w_in, mixab_w_out, na_rpb,
           ret_w_in, ret_w_out, ret_decay, router_w, router_b, moe_w1, moe_w3, moe_w2):
    d64, ga, csb, csc = _dft_constants()
    triu = jnp.asarray(np.triu(np.ones((TM, TM)), 1), dtype=BF16)
    rw = jnp.pad(router_w, ((0, 0), (0, LANES - N_EXPERTS)))
    rb = jnp.broadcast_to(router_b[:, None], (N_EXPERTS, TM))
    w1 = moe_w1.reshape(DEPTH * N_EXPERTS, D, D_EXPERT)
    w3 = moe_w3.reshape(DEPTH * N_EXPERTS, D, D_EXPERT)
    w2 = moe_w2.reshape(DEPTH * N_EXPERTS, D_EXPERT, D)

    mod = _mod_vectors(c, c_ctx, ada_w, ada_b)
    modv0, modv1 = _mod_table(mod[0]), _mod_table(mod[1])

    qkv, br, bi = _inproj0(x[0], ctx[0], modv0, norm_g[0, 0].reshape(1, D), mixab_w_in[0].astype(BF16), d64)
    a_lat = _fft_latent(br, bi, ga, csb)
    ao_ctx = _ctx_mixer(qkv, br, bi, csc)
    o_lat = _na_attention(qkv, _na_bias_rows(na_rpb[0]))
    h1, xrow, meta, cnt = _outproj0(a_lat, o_lat, ao_ctx, x[0], ctx[0], modv0, mixab_w_out[0].astype(BF16),
                                    norm_g[0, 1].reshape(1, D), rw, rb, triu)
    n_tiles0 = NT + N_REAL_BUCKETS
    pos0, e_lo, e_hi, n_used = _route_plan(meta, cnt, n_tiles0, 0)
    ys0 = _moe(e_lo, e_hi, n_used, _dispatch(pos0, xrow, n_tiles0), w1, w3, w2, n_tiles0)

    h2, q, k, v, gg = _inproj1(pos0, h1, ys0, modv0, modv1, norm_g[1, 0].reshape(1, D),
                               ret_w_in[0].astype(BF16), _rope_tables())
    o_f, o_b = _ret_scan(ret_decay[0].astype(F32), q, k, v)
    h3, xrow1, meta1, cnt1 = _outproj1(o_f, o_b, gg, h2, modv1, ret_w_out[0].astype(BF16),
                                       norm_g[1, 1].reshape(1, D), rw, rb, triu)
    n_tiles1 = NT_LAT + N_REAL_BUCKETS
    pos1, e_lo1, e_hi1, n_used1 = _route_plan(meta1, cnt1, n_tiles1, 1)
    ys1 = _moe(e_lo1, e_hi1, n_used1, _dispatch(pos1, xrow1, n_tiles1), w1, w3, w2, n_tiles1)
    out = _final(pos1, h3, ys1, modv1, final_norm_g.reshape(1, D))
    return out[None]
```

```python
import functools
import math

import numpy as np
import jax
import jax.numpy as jnp
from jax import lax
from jax.experimental import pallas as pl
from jax.experimental.pallas import tpu as pltpu

F32 = jnp.float32
BF16 = jnp.bfloat16

D = 1024
SEQ = 16384
CTX = 256
T = SEQ + CTX
DEPTH = 2
GRID_W = 64
ROWS = SEQ // GRID_W
RMS_EPS = 1e-6
ROPE_BASE = 10000.0

FNET_W = 256
FNET_GD = 64
NA_HEADS = 12
NA_HD = 64
NA_W = NA_HEADS * NA_HD
NA_KH = 8
NA_KW = 16
AB_IN = FNET_W + 3 * NA_W
QKV_W = 3 * NA_W

RET_HEADS = 4
RET_DK = 256
RET_DV = 512
RET_QK = RET_HEADS * RET_DK
RET_V = RET_HEADS * RET_DV
RET_CHUNK = 256

N_EXPERTS = 16
EPG = 4
D_EXPERT = 512

LANES = 128
TM = 256
NT = T // TM
NT_LAT = SEQ // TM
TMM = TM
N_BUCKET_IDS = 64
N_REAL_BUCKETS = 24
XROW_W = D + LANES
FFT_N1 = 128
FFT_N2 = 128
NEG_BIG = -1e30
SUBLANES = 8

VMEM_LIMIT = 56 * 1024 * 1024


def _params(sem, vmem=VMEM_LIMIT):
    return pltpu.CompilerParams(dimension_semantics=sem, vmem_limit_bytes=vmem)


def _dot(a, b):
    return jnp.dot(a, b, preferred_element_type=F32)


def _dot_nt(a, b):
    return lax.dot_general(a, b, (((1,), (1,)), ((), ())), preferred_element_type=F32)


def _dot_tn(a, b):
    return lax.dot_general(a, b, (((0,), (0,)), ((), ())), preferred_element_type=F32)


def _rms_mod(x, g, sh, sc):
    ms = jnp.mean(x * x, axis=-1, keepdims=True)
    y = x * lax.rsqrt(ms + RMS_EPS) * g
    return y * (1.0 + sc) + sh


def _resident(shape):
    nd = len(shape)
    return pl.BlockSpec(shape, lambda *_: (0,) * nd, pipeline_mode=pl.Buffered(1))


MOD_TN = 768


def _mod_kernel(cs_ref, w_ref, b_ref, o_ref):
    cs = cs_ref[...]
    s = cs * jax.nn.sigmoid(cs)
    w = w_ref[0]
    r0 = jnp.sum(s[:, 0:1] * w, axis=0, keepdims=True)
    r1 = jnp.sum(s[:, 1:2] * w, axis=0, keepdims=True)
    o_ref[0] = jnp.concatenate([r0, r1], axis=0) + b_ref[0]


def _mod_vectors(c, c_ctx, ada_w, ada_b):
    cs = jnp.stack([c[0], c_ctx], axis=1)
    n = 6 * D
    return pl.pallas_call(
        _mod_kernel,
        grid=(DEPTH, n // MOD_TN),
        in_specs=[
            pl.BlockSpec((D, 2), lambda l, j: (0, 0)),
            pl.BlockSpec((1, D, MOD_TN), lambda l, j: (l, 0, j)),
            pl.BlockSpec((1, 1, MOD_TN), lambda l, j: (l, 0, j)),
        ],
        out_specs=pl.BlockSpec((1, 2, MOD_TN), lambda l, j: (l, 0, j)),
        out_shape=jax.ShapeDtypeStruct((DEPTH, 2, n), F32),
        compiler_params=_params(("arbitrary", "arbitrary")),
        name="mod_vectors",
    )(cs, ada_w, ada_b.reshape(DEPTH, 1, n))


def _mod_spec():
    return pl.BlockSpec((1, 8, D), lambda i, *_: (i // NT_LAT, 0, 0))


def _stream_tile(x_ref, ctx_ref):
    return jnp.where(pl.program_id(0) >= NT_LAT, ctx_ref[...], x_ref[...])


def _stream_specs():
    return [pl.BlockSpec((TM, D), lambda i: (jnp.minimum(i, NT_LAT - 1), 0)),
            pl.BlockSpec((CTX, D), lambda i: (0, 0))]


def _inproj0_kernel(x_ref, ctx_ref, mod_ref, g_ref, w_ref, d64_ref, qkv_ref, br_ref, bi_ref):
    h = _stream_tile(x_ref, ctx_ref)
    xm = _rms_mod(h, g_ref[...], mod_ref[0, 0:1, :], mod_ref[0, 1:2, :]).astype(BF16)
    p = _dot(xm, w_ref[...])
    qkv_ref[...] = p[:, FNET_W:].astype(BF16)
    b = _dot(p[:, :FNET_W].astype(BF16), d64_ref[...])
    br_ref[...] = b[:, :FNET_W].astype(BF16)
    bi_ref[...] = b[:, FNET_W:].astype(BF16)


def _inproj0(x, ctx, modv, g, w_in, d64):
    return pl.pallas_call(
        _inproj0_kernel,
        grid=(NT,),
        in_specs=_stream_specs() + [
            _mod_spec(),
            _resident((1, D)),
            _resident((D, AB_IN)),
            _resident((FNET_W, 2 * FNET_W)),
        ],
        out_specs=[
            pl.BlockSpec((TM, QKV_W), lambda i: (i, 0)),
            pl.BlockSpec((TM, FNET_W), lambda i: (i, 0)),
            pl.BlockSpec((TM, FNET_W), lambda i: (i, 0)),
        ],
        out_shape=[
            jax.ShapeDtypeStruct((T, QKV_W), BF16),
            jax.ShapeDtypeStruct((T, FNET_W), BF16),
            jax.ShapeDtypeStruct((T, FNET_W), BF16),
        ],
        compiler_params=_params(("arbitrary",)),
        name="inproj0",
    )(x, ctx, modv, g, w_in, d64)


FFT_A_NB = 8
FFT_B_TN = 4096


def _fft_a_kernel(br_ref, bi_ref, ga_ref, v_ref):
    for u in range(FFT_A_NB):
        cols = slice(u * FNET_W, (u + 1) * FNET_W)
        z = _dot(ga_ref[u, :, 0:FFT_N2], br_ref[:, cols]) + _dot(ga_ref[u, :, FFT_N2:], bi_ref[:, cols])
        v_ref[0, u] = z[:FFT_N2].astype(BF16)
        v_ref[1, u] = z[FFT_N2:].astype(BF16)


def _fft_b_kernel(cs_ref, v_ref, o_ref):
    o_ref[...] = _dot(cs_ref[...], v_ref[...]) * (1.0 / math.sqrt(SEQ * FNET_GD))


def _fft_latent(br, bi, ga, csb):
    width = FFT_N1 * FNET_W
    br2 = br.reshape(T // FFT_N1, width)
    bi2 = bi.reshape(T // FFT_N1, width)
    v = pl.pallas_call(
        _fft_a_kernel,
        grid=(FFT_N1 // FFT_A_NB,),
        in_specs=[
            pl.BlockSpec((FFT_N2, FFT_A_NB * FNET_W), lambda j: (0, j)),
            pl.BlockSpec((FFT_N2, FFT_A_NB * FNET_W), lambda j: (0, j)),
            pl.BlockSpec((FFT_A_NB, 2 * FFT_N2, 2 * FFT_N2), lambda j: (j, 0, 0)),
        ],
        out_specs=pl.BlockSpec((2, FFT_A_NB, FFT_N2, FNET_W), lambda j: (0, j, 0, 0)),
        out_shape=jax.ShapeDtypeStruct((2, FFT_N1, FFT_N2, FNET_W), BF16),
        compiler_params=_params(("arbitrary",)),
        name="fft_stage_a",
    )(br2, bi2, ga)
    v2 = v.reshape(2 * FFT_N1, FFT_N2 * FNET_W)
    x = pl.pallas_call(
        _fft_b_kernel,
        grid=(FFT_N2 * FNET_W // FFT_B_TN,),
        in_specs=[
            _resident((FFT_N1, 2 * FFT_N1)),
            pl.BlockSpec((2 * FFT_N1, FFT_B_TN), lambda j: (0, j)),
        ],
        out_specs=pl.BlockSpec((FFT_N1, FFT_B_TN), lambda j: (0, j)),
        out_shape=jax.ShapeDtypeStruct((FFT_N1, FFT_N2 * FNET_W), F32),
        compiler_params=_params(("arbitrary",)),
        name="fft_stage_b",
    )(csb, v2)
    return x.reshape(SEQ, FNET_W)


def _head_pair_masks():
    lane = lax.broadcasted_iota(jnp.int32, (1, LANES), 1)
    return lane < NA_HD


def _ctx_kernel(qkv_ref, br_ref, bi_ref, cs_ref, o_ref):
    a = _dot(cs_ref[:, 0:CTX], br_ref[...]) + _dot(cs_ref[:, CTX:], bi_ref[...])
    o_ref[:, 0:FNET_W] = (a * (1.0 / math.sqrt(CTX * FNET_GD))).astype(BF16)
    m0 = _head_pair_masks()
    for hp in range(NA_HEADS // 2):
        q = qkv_ref[:, hp * LANES:(hp + 1) * LANES]
        k = qkv_ref[:, NA_W + hp * LANES:NA_W + (hp + 1) * LANES]
        v = qkv_ref[:, 2 * NA_W + hp * LANES:2 * NA_W + (hp + 1) * LANES]
        outs = []
        for a_ in range(2):
            qa = jnp.where(m0 if a_ == 0 else jnp.logical_not(m0), q, jnp.zeros_like(q))
            s = _dot_nt(qa, k) * (NA_HD ** -0.5)
            p = jnp.exp(s - jnp.max(s, axis=-1, keepdims=True))
            l = jnp.sum(p, axis=-1, keepdims=True)
            outs.append(_dot(p.astype(BF16), v) / l)
        o_ref[:, FNET_W + hp * LANES:FNET_W + (hp + 1) * LANES] = jnp.where(m0, outs[0], outs[1]).astype(BF16)


def _ctx_mixer(qkv, br, bi, csc):
    return pl.pallas_call(
        _ctx_kernel,
        grid=(1,),
        in_specs=[
            pl.BlockSpec((CTX, QKV_W), lambda i: (SEQ // CTX, 0)),
            pl.BlockSpec((CTX, FNET_W), lambda i: (SEQ // CTX, 0)),
            pl.BlockSpec((CTX, FNET_W), lambda i: (SEQ // CTX, 0)),
            pl.BlockSpec((CTX, 2 * CTX), lambda i: (0, 0)),
        ],
        out_specs=pl.BlockSpec((CTX, D), lambda i: (0, 0)),
        out_shape=jax.ShapeDtypeStruct((CTX, D), BF16),
        compiler_params=_params(("arbitrary",)),
        name="ctx_mixer",
    )(qkv, br, bi, csc)


NA_RB = 8
NA_WIN = NA_KH * GRID_W


NA_NK = NA_WIN + CTX
NA_SUB = 16


def _na_kernel(q_ref, k_ref, v_ref, kc_ref, vc_ref, bias_ref, o_ref, s_scr, p_scr, l_scr):
    b = pl.program_id(1)
    m0 = _head_pair_masks()
    nm0 = jnp.logical_not(m0)
    scale = NA_HD ** -0.5

    def window(i):
        r = b * NA_RB + i
        rs = jnp.clip(r - NA_KH // 2, 0, ROWS - NA_KH)
        return pl.multiple_of(rs * GRID_W, GRID_W), rs - r + (NA_KH - 1)

    def scores(i):
        start, _ = window(i)
        q = q_ref[i * GRID_W:(i + 1) * GRID_W, :] * scale
        qs = jnp.concatenate([jnp.where(m0, q, jnp.zeros_like(q)), jnp.where(nm0, q, jnp.zeros_like(q))], axis=0)
        s_scr[i, :, 0:NA_WIN] = _dot_nt(qs, k_ref[pl.ds(start, NA_WIN), :])
        s_scr[i, :, NA_WIN:NA_NK] = _dot_nt(qs, kc_ref[...])

    def softmax(i):
        _, e = window(i)
        for g in range(2 * GRID_W // NA_SUB):
            rows = slice(g * NA_SUB, (g + 1) * NA_SUB)
            a, c0 = divmod(g * NA_SUB, GRID_W)
            s1 = s_scr[i, rows, 0:NA_WIN] + bias_ref[a, e, c0:c0 + NA_SUB, :]
            s2 = s_scr[i, rows, NA_WIN:NA_NK]
            mx = jnp.maximum(jnp.max(s1, axis=-1, keepdims=True), jnp.max(s2, axis=-1, keepdims=True))
            p1 = jnp.exp(s1 - mx)
            p2 = jnp.exp(s2 - mx)
            l = jnp.sum(p1, axis=-1, keepdims=True) + jnp.sum(p2, axis=-1, keepdims=True)
            p_scr[i, rows, 0:NA_WIN] = p1.astype(BF16)
            p_scr[i, rows, NA_WIN:NA_NK] = p2.astype(BF16)
            l_scr[i, rows, :] = jnp.broadcast_to(l, (NA_SUB, LANES))

    def values(i):
        start, _ = window(i)
        o = _dot(p_scr[i, :, 0:NA_WIN], v_ref[pl.ds(start, NA_WIN), :]) + _dot(p_scr[i, :, NA_WIN:NA_NK], vc_ref[...])
        o = o / l_scr[i]
        o_ref[i * GRID_W:(i + 1) * GRID_W, :] = jnp.where(m0, o[0:GRID_W], o[GRID_W:]).astype(BF16)

    for i in range(NA_RB):
        scores(i)
    for i in range(NA_RB):
        softmax(i)
        values(i)


def _na_attention(qkv, bias):
    nq = NA_RB * GRID_W
    nhp = NA_HEADS // 2
    return pl.pallas_call(
        _na_kernel,
        grid=(nhp, ROWS // NA_RB),
        in_specs=[
            pl.BlockSpec((nq, LANES), lambda hp, b: (b, hp)),
            pl.BlockSpec((SEQ, LANES), lambda hp, b: (0, nhp + hp)),
            pl.BlockSpec((SEQ, LANES), lambda hp, b: (0, 2 * nhp + hp)),
            pl.BlockSpec((CTX, LANES), lambda hp, b: (SEQ // CTX, nhp + hp)),
            pl.BlockSpec((CTX, LANES), lambda hp, b: (SEQ // CTX, 2 * nhp + hp)),
            pl.BlockSpec((2, NA_KH, GRID_W, NA_WIN), lambda hp, b: (hp, 0, 0, 0)),
        ],
        out_specs=pl.BlockSpec((nq, LANES), lambda hp, b: (b, hp)),
        out_shape=jax.ShapeDtypeStruct((SEQ, NA_W), BF16),
        scratch_shapes=[
            pltpu.VMEM((NA_RB, 2 * GRID_W, NA_NK), F32),
            pltpu.VMEM((NA_RB, 2 * GRID_W, NA_NK), BF16),
            pltpu.VMEM((NA_RB, 2 * GRID_W, LANES), F32),
        ],
        compiler_params=_params(("arbitrary", "arbitrary")),
        name="na_attention",
    )(qkv, qkv, qkv, qkv, qkv, bias)


def _na_bias_rows(rpb):
    c = np.arange(GRID_W)[:, None]
    kc = np.arange(GRID_W)[None, :]
    ws = np.clip(c - NA_KW // 2, 0, GRID_W - NA_KW)
    col_ok = (kc >= ws) & (kc < ws + NA_KW)
    dc = np.clip(kc - c + NA_KW - 1, 0, 2 * NA_KW - 2)
    pick = (dc[None] == np.arange(2 * NA_KW - 1)[:, None, None]).astype(np.float32)
    full = jnp.einsum("hrd,dck->hrck", rpb.astype(F32), jnp.asarray(pick), precision=lax.Precision.HIGHEST)
    full = jnp.where(jnp.asarray(col_ok)[None, None], full, NEG_BIG)
    rows = [jnp.transpose(full[:, e:e + NA_KH], (0, 2, 1, 3)).reshape(NA_HEADS, GRID_W, NA_WIN)
            for e in range(NA_KH)]
    return jnp.stack(rows, axis=1)


def _route_stage(xm2, valid, rw_ref, rb_ref, triu_ref, run_ref, xrow_ref, meta_ref, cnt_ref):
    xrow_ref[:, 0:D] = xm2
    xh = xm2.astype(BF16)
    xl = (xm2 - xh.astype(F32)).astype(BF16)
    w = rw_ref[...]
    wh = w.astype(BF16)
    wl = (w - wh.astype(F32)).astype(BF16)
    logits = _dot(xh, wh) + (_dot(xl, wh) + _dot(xh, wl))
    scores = jax.nn.sigmoid(logits.T[0:N_EXPERTS, :])
    biased = scores + rb_ref[...]
    s = [scores[e:e + 1, :] for e in range(N_EXPERTS)]
    b = [biased[e:e + 1, :] for e in range(N_EXPERTS)]
    n_groups = N_EXPERTS // EPG
    top2, group_score = [], []
    for g in range(n_groups):
        m = b[EPG * g:EPG * (g + 1)]
        total = None
        for k in range(EPG):
            rank = None
            for j in range(EPG):
                if j != k:
                    beats = jnp.where((m[j] >= m[k]) if j < k else (m[j] > m[k]), 1.0, 0.0)
                    rank = beats if rank is None else rank + beats
            top2.append(rank < 2.0)
            term = jnp.where(top2[-1], m[k], 0.0)
            total = term if total is None else total + term
        group_score.append(total)
    best, g_sel = group_score[0], jnp.zeros((1, TM), jnp.int32)
    for g in range(1, n_groups):
        better = group_score[g] > best
        g_sel = jnp.where(better, g, g_sel)
        best = jnp.where(better, group_score[g], best)
    sel = [top2[e] & (g_sel == e // EPG) for e in range(N_EXPERTS)]
    w_sum = None
    for e in range(N_EXPERTS):
        term = jnp.where(sel[e], s[e], 0.0)
        w_sum = term if w_sum is None else w_sum + term
    zero = jnp.zeros((1, TM), F32)
    w_lo, w_hi, bid = zero, zero, zero
    for g in range(n_groups):
        seen = None
        for k in range(EPG):
            e = EPG * g + k
            if seen is None:
                w_lo = w_lo + jnp.where(sel[e], s[e], 0.0)
                seen = sel[e]
            else:
                w_lo = w_lo + jnp.where(sel[e] & jnp.logical_not(seen), s[e], 0.0)
                w_hi = w_hi + jnp.where(sel[e] & seen, s[e], 0.0)
                seen = seen | sel[e]
            bid = bid + jnp.where(sel[e], float((1 << k) + 8 * g), 0.0)
    w_lo = w_lo / w_sum
    w_hi = w_hi / w_sum
    bucket = lax.broadcasted_iota(jnp.int32, (N_BUCKET_IDS, TM), 0).astype(F32)
    onehot = bucket == bid
    ones = jnp.where(onehot & valid, 1.0, 0.0)
    earlier = _dot(ones.astype(BF16), triu_ref[...])
    run = run_ref[...]
    run_t = jnp.concatenate([run] * (TM // LANES), axis=1)
    rank_row = jnp.sum(jnp.where(onehot, earlier + run_t, 0.0), axis=0, keepdims=True)
    new_run = run + jnp.sum(ones, axis=1, keepdims=True)
    run_ref[...] = new_run
    cnt_ref[...] = new_run
    meta_rows = jnp.concatenate([w_lo, w_hi, bid, rank_row, jnp.zeros((SUBLANES - 4, TM), F32)], axis=0)
    meta_ref[...] = meta_rows
    padded = jnp.concatenate([meta_rows, jnp.zeros((LANES - SUBLANES, TM), F32)], axis=0)
    xrow_ref[:, D:XROW_W] = padded.T


def _epilogue_specs(ntiles):
    in_specs = [
        _resident((1, D)),
        _resident((D, LANES)),
        _resident((N_EXPERTS, TM)),
        _resident((TM, TM)),
    ]
    out_specs = [
        pl.BlockSpec((TM, D), lambda i: (jnp.minimum(i, ntiles - 1), 0)),
        pl.BlockSpec((TM, XROW_W), lambda i: (jnp.maximum(i - 1, 0), 0)),
        pl.BlockSpec((SUBLANES, TM), lambda i: (0, jnp.maximum(i - 1, 0))),
        pl.BlockSpec((N_BUCKET_IDS, LANES), lambda i: (0, 0)),
    ]
    n = ntiles * TM
    out_shape = [
        jax.ShapeDtypeStruct((n, D), F32),
        jax.ShapeDtypeStruct((n, XROW_W), F32),
        jax.ShapeDtypeStruct((SUBLANES, n), F32),
        jax.ShapeDtypeStruct((N_BUCKET_IDS, LANES), F32),
    ]
    return in_specs, out_specs, out_shape


_EPILOGUE_SCRATCH = [pltpu.VMEM((N_BUCKET_IDS, LANES), F32), pltpu.VMEM((2, TM, D), F32)]


def _two_stage(residual_tile, mod_ref, g2_ref, rw_ref, rb_ref, triu_ref,
               h1_ref, xrow_ref, meta_ref, cnt_ref, run_ref, stash):
    i = pl.program_id(0)

    @pl.when(i == 0)
    def _():
        run_ref[...] = jnp.zeros_like(run_ref)
        stash[1] = jnp.zeros((TM, D), F32)

    _route_stage(stash[1 - i % 2], i >= 1, rw_ref, rb_ref, triu_ref, run_ref, xrow_ref, meta_ref, cnt_ref)
    h1 = residual_tile()
    h1_ref[...] = h1
    stash[i % 2] = _rms_mod(h1, g2_ref[...], mod_ref[0, 3:4, :], mod_ref[0, 4:5, :])


def _outproj0_kernel(a_ref, o_ref, aoc_ref, x_ref, ctx_ref, mod_ref, w_ref, *rest):
    def residual_tile():
        is_ctx = pl.program_id(0) >= NT_LAT
        a = jnp.where(is_ctx, aoc_ref[:, 0:FNET_W], a_ref[...].astype(BF16))
        o = jnp.where(is_ctx, aoc_ref[:, FNET_W:], o_ref[...])
        y = _dot(a, w_ref[0:FNET_W, :]) + _dot(o, w_ref[FNET_W:, :])
        return _stream_tile(x_ref, ctx_ref) + mod_ref[0, 2:3, :] * y

    _two_stage(residual_tile, mod_ref, *rest)


def _outproj0(a_lat, o_lat, ao_ctx, x, ctx, modv, w_out, g2, rw, rb, triu):
    e_in, e_out, e_shape = _epilogue_specs(NT)
    lat = lambda w: pl.BlockSpec((TM, w), lambda i: (jnp.minimum(i, NT_LAT - 1), 0))
    return pl.pallas_call(
        _outproj0_kernel,
        grid=(NT + 1,),
        in_specs=[lat(FNET_W), lat(NA_W), pl.BlockSpec((CTX, D), lambda i: (0, 0))] + _stream_specs() + [
            _mod_spec(),
            _resident((D, D)),
        ] + e_in,
        out_specs=e_out,
        out_shape=e_shape,
        scratch_shapes=_EPILOGUE_SCRATCH,
        compiler_params=_params(("arbitrary",)),
        name="outproj0",
    )(a_lat, o_lat, ao_ctx, x, ctx, modv, w_out, g2, rw, rb, triu)


def _bucket_tables():
    lo = np.zeros(N_BUCKET_IDS, np.int32)
    hi = np.zeros(N_BUCKET_IDS, np.int32)
    for g in range(N_EXPERTS // EPG):
        for a in range(EPG):
            for b in range(a + 1, EPG):
                i = 16 * g + (1 << a) + (1 << b)
                lo[i] = EPG * g + a
                hi[i] = EPG * g + b
    return lo, hi


def _slot_kernel(meta_ref, starts_ref, o_ref):
    n = meta_ref.shape[1]
    bucket = lax.broadcasted_iota(jnp.int32, (N_BUCKET_IDS, n), 0).astype(F32)
    first = jnp.sum(jnp.where(bucket == meta_ref[2:3, :], starts_ref[:, 0:1], 0.0), axis=0, keepdims=True)
    o_ref[...] = jnp.broadcast_to((first + meta_ref[3:4, :]).astype(jnp.int32), (SUBLANES, n))


def _token_slots(meta, first_row):
    n = meta.shape[1]
    out = pl.pallas_call(
        _slot_kernel,
        grid=(1,),
        in_specs=[pl.BlockSpec((SUBLANES, n), lambda i: (0, 0)),
                  pl.BlockSpec((N_BUCKET_IDS, LANES), lambda i: (0, 0))],
        out_specs=pl.BlockSpec((SUBLANES, n), lambda i: (0, 0)),
        out_shape=jax.ShapeDtypeStruct((SUBLANES, n), jnp.int32),
        compiler_params=_params(("arbitrary",)),
        name="token_slots",
    )(meta, first_row)
    return out[0]


def _route_plan(meta, cnt, n_tiles, layer):
    counts = cnt[:, 0].astype(jnp.int32)
    nt_b = (counts + TMM - 1) // TMM
    ends = jnp.cumsum(nt_b)
    starts = ends - nt_b
    first_row = jnp.broadcast_to((starts * TMM).astype(F32)[:, None], (N_BUCKET_IDS, LANES))
    pos = _token_slots(meta, first_row)
    n_used = ends[-1]
    tile = jnp.minimum(jnp.arange(n_tiles, dtype=jnp.int32), n_used - 1)
    tile_b = jnp.sum((ends[None, :] <= tile[:, None]).astype(jnp.int32), axis=1)
    lo, hi = (jnp.asarray(t) + layer * N_EXPERTS for t in _bucket_tables())
    ids = jnp.arange(n_tiles, dtype=jnp.int32)
    first = (tile_b != jnp.concatenate([jnp.full((1,), -1, jnp.int32), tile_b[:-1]])) & (ids < n_used)
    wslot = (jnp.cumsum(first.astype(jnp.int32)) - 1) % 2
    next_tile = ends[tile_b]
    has_next = next_tile < n_used
    next_b = tile_b[jnp.minimum(next_tile, n_tiles - 1)]
    plan = jnp.stack([lo[tile_b], hi[tile_b], first.astype(jnp.int32), wslot,
                      jnp.where(has_next, lo[next_b], -1), jnp.where(has_next, hi[next_b], -1)])
    return pos, plan.astype(jnp.int32), n_used.reshape(1).astype(jnp.int32)


def _dispatch_kernel(pos_ref, x_ref, xs_in_ref, xs_ref, sem):
    del xs_in_ref
    base = pl.program_id(0) * TM

    def issue(j, carry):
        for u in range(SUBLANES):
            p = pos_ref[base + j * SUBLANES + u]
            pltpu.make_async_copy(x_ref.at[j, pl.ds(u, 1), :],
                                  xs_ref.at[p >> 3, pl.ds(p & (SUBLANES - 1), 1), :], sem).start()
        return carry

    lax.fori_loop(0, TM // SUBLANES, issue, 0)
    pltpu.make_async_copy(x_ref, xs_ref.at[pl.ds(0, TM // SUBLANES)], sem).wait()


def _dispatch(pos, xrow, n_tiles):
    ntok = xrow.shape[0]
    xs0 = jnp.zeros((n_tiles * TMM // SUBLANES, SUBLANES, XROW_W), F32)
    xs = pl.pallas_call(
        _dispatch_kernel,
        grid_spec=pltpu.PrefetchScalarGridSpec(
            num_scalar_prefetch=1,
            grid=(ntok // TM,),
            in_specs=[
                pl.BlockSpec((TM // SUBLANES, SUBLANES, XROW_W), lambda i, pos: (i, 0, 0)),
                pl.BlockSpec(memory_space=pl.ANY),
            ],
            out_specs=pl.BlockSpec(memory_space=pl.ANY),
            scratch_shapes=[pltpu.SemaphoreType.DMA(())],
        ),
        out_shape=jax.ShapeDtypeStruct(xs0.shape, F32),
        input_output_aliases={2: 0},
        compiler_params=_params(("arbitrary",)),
        name="moe_dispatch",
    )(pos, _by_sublane_tile(xrow), xs0)
    return xs.reshape(n_tiles * TMM, XROW_W)


def _moe_kernel(plan_ref, nu_ref, xs_ref, w1_hbm, w3_hbm, w2_hbm, ys_ref, wb1, wb3, wb2, sem):
    j = pl.program_id(0)
    slot = plan_ref[3, j]

    def weight_copies(e_lo, e_hi, s):
        out = []
        for k, e in enumerate((e_lo, e_hi)):
            out.append(pltpu.make_async_copy(w1_hbm.at[e], wb1.at[s, k], sem.at[s, 3 * k]))
            out.append(pltpu.make_async_copy(w3_hbm.at[e], wb3.at[s, k], sem.at[s, 3 * k + 1]))
            out.append(pltpu.make_async_copy(w2_hbm.at[e], wb2.at[s, k], sem.at[s, 3 * k + 2]))
        return out

    @pl.when(j == 0)
    def _():
        for c in weight_copies(plan_ref[0, 0], plan_ref[1, 0], 0):
            c.start()

    @pl.when(plan_ref[2, j] == 1)
    def _():
        for c in weight_copies(plan_ref[0, j], plan_ref[1, j], slot):
            c.wait()

        @pl.when(plan_ref[4, j] >= 0)
        def _():
            for c in weight_copies(plan_ref[4, j], plan_ref[5, j], 1 - slot):
                c.start()

    @pl.when(j < nu_ref[0])
    def _():
        x = xs_ref[:, 0:D].astype(BF16)
        g_lo = xs_ref[:, D:D + 1]
        g_hi = xs_ref[:, D + 1:D + 2]

        def expert(k):
            a = _dot(x, wb1[slot, k].astype(BF16))
            hid = (a * jax.nn.sigmoid(a)) * _dot(x, wb3[slot, k].astype(BF16))
            return _dot(hid.astype(BF16), wb2[slot, k].astype(BF16))

        ys_ref[...] = g_lo * expert(0) + g_hi * expert(1)

    @pl.when(j >= nu_ref[0])
    def _():
        ys_ref[...] = jnp.zeros_like(ys_ref)


def _moe(plan, n_used, xs, w1, w3, w2, n_tiles):
    hbm = pl.BlockSpec(memory_space=pl.ANY)
    return pl.pallas_call(
        _moe_kernel,
        grid_spec=pltpu.PrefetchScalarGridSpec(
            num_scalar_prefetch=2,
            grid=(n_tiles,),
            in_specs=[pl.BlockSpec((TMM, XROW_W), lambda j, plan, nu: (j, 0)), hbm, hbm, hbm],
            out_specs=pl.BlockSpec((TMM, D), lambda j, plan, nu: (j, 0)),
            scratch_shapes=[
                pltpu.VMEM((2, 2, D, D_EXPERT), F32),
                pltpu.VMEM((2, 2, D, D_EXPERT), F32),
                pltpu.VMEM((2, 2, D_EXPERT, D), F32),
                pltpu.SemaphoreType.DMA((2, 6)),
            ],
        ),
        out_shape=jax.ShapeDtypeStruct((n_tiles * TMM, D), F32),
        compiler_params=_params(("arbitrary",)),
        name="moe_experts",
    )(plan, n_used, xs, w1, w3, w2)


def _gather_issue(pos_ref, ys_hbm, buf, sem, tile, slot, inline, part=(0, TM // SUBLANES)):
    base = tile * TM

    def issue(j, carry):
        for u in range(SUBLANES):
            p = pos_ref[base + j * SUBLANES + u]
            pltpu.make_async_copy(ys_hbm.at[p >> 3, pl.ds(p & (SUBLANES - 1), 1), :],
                                  buf.at[slot, j, pl.ds(u, 1), :], sem.at[slot]).start()
        return carry

    if inline:
        for j in range(*part):
            issue(j, 0)
    else:
        lax.fori_loop(0, TM // SUBLANES, issue, 0)


def _gather_wait(ys_hbm, buf, sem, slot):
    pltpu.make_async_copy(ys_hbm.at[pl.ds(0, TM // SUBLANES)], buf.at[slot], sem.at[slot]).wait()


def _gathered_rows(pos_ref, ys_hbm, buf, sem, request_next):
    i = pl.program_id(0)
    slot = i % 2

    @pl.when(i == 0)
    def _():
        _gather_issue(pos_ref, ys_hbm, buf, sem, 0, 0, inline=False)

    if request_next:
        @pl.when(i + 1 < pl.num_programs(0))
        def _():
            _gather_issue(pos_ref, ys_hbm, buf, sem, i + 1, 1 - slot, inline=False)

    _gather_wait(ys_hbm, buf, sem, slot)
    return buf[slot].reshape(TM, D)


def _gather_drain(ys_hbm, buf, sem):
    i = pl.program_id(0)

    @pl.when(i + 1 == pl.num_programs(0))
    def _():
        _gather_wait(ys_hbm, buf, sem, 1 - i % 2)


_GATHER_SCRATCH = [pltpu.VMEM((2, TM // SUBLANES, SUBLANES, D), F32), pltpu.SemaphoreType.DMA((2,))]


def _by_sublane_tile(a):
    return a.reshape(a.shape[0] // SUBLANES, SUBLANES, a.shape[1])


def _rope_store(x, tabs, out_ref):
    cr, sr, cc, sc = tabs
    for ch in range(RET_QK // LANES):
        xc = x[:, ch * LANES:(ch + 1) * LANES]
        xr = pltpu.roll(xc, LANES // 2, axis=1)
        y = xc * cr + xr * sr if ch % 2 == 0 else xc * cc + xr * sc
        out_ref[:, ch * LANES:(ch + 1) * LANES] = y.astype(BF16)


def _inproj1_kernel(pos_ref, h_ref, ys_hbm, mod0_ref, mod1_ref, g_ref, w_ref, rt_ref, ct_ref,
                    h2_ref, q_ref, k_ref, v_ref, gg_ref, buf, sem):
    f = _gathered_rows(pos_ref, ys_hbm, buf, sem, request_next=False)
    h2 = h_ref[...] + mod0_ref[0, 5:6, :] * f
    h2_ref[...] = h2
    buf[2] = _rms_mod(h2, g_ref[...], mod1_ref[0, 0:1, :], mod1_ref[0, 1:2, :]).reshape(TM // SUBLANES, SUBLANES, D)
    i = pl.program_id(0)
    nxt = jnp.where(i + 1 < pl.num_programs(0), i + 1, 0)
    bounds = [round(k * (TM // SUBLANES) / 6) for k in range(7)]

    def xm_after_requests(part):
        _gather_issue(pos_ref, ys_hbm, buf, sem, nxt, 1 - i % 2, inline=True, part=(bounds[part], bounds[part + 1]))
        return buf[2].reshape(TM, D).astype(BF16)

    off = (pl.program_id(0) % (SUBLANES // ROPE_TILE_ROWS)) * ROPE_TILE_ROWS

    def row_table(which):
        return jnp.concatenate([jnp.broadcast_to(rt_ref[which, pl.ds(off + j, 1), :], (GRID_W, LANES))
                                for j in range(ROPE_TILE_ROWS)], axis=0)

    tabs = (row_table(0), row_table(1), ct_ref[0, 0], ct_ref[0, 1])
    _rope_store(_dot(xm_after_requests(0), w_ref[:, 0:RET_QK]), tabs, q_ref)
    _rope_store(_dot(xm_after_requests(1), w_ref[:, RET_QK:2 * RET_QK]) * (RET_DK ** -0.5), tabs, k_ref)
    half = RET_V // 2
    for part, (out_ref, col0) in enumerate(((v_ref, 2 * RET_QK), (gg_ref, 2 * RET_QK + RET_V))):
        for s in range(2):
            xm = xm_after_requests(2 + 2 * part + s)
            out_ref[:, s * half:(s + 1) * half] = _dot(
                xm, w_ref[:, col0 + s * half:col0 + (s + 1) * half]).astype(BF16)
    _gather_drain(ys_hbm, buf, sem)


def _inproj1(pos, h1, ys, modv0, modv1, g, w_in, tabs):
    tok = lambda w: pl.BlockSpec((TM, w), lambda i, pos: (i, 0))
    res = lambda shape: pl.BlockSpec(shape, lambda i, pos: (0,) * len(shape), pipeline_mode=pl.Buffered(1))
    mod = pl.BlockSpec((1, 8, D), lambda i, pos: (i // NT_LAT, 0, 0))
    row_tab = pl.BlockSpec((2, SUBLANES, LANES), lambda i, pos: (0, i // (SUBLANES // ROPE_TILE_ROWS), 0))
    col_tab = pl.BlockSpec((1, 2, TM, LANES), lambda i, pos: (i // NT_LAT, 0, 0, 0))
    return pl.pallas_call(
        _inproj1_kernel,
        grid_spec=pltpu.PrefetchScalarGridSpec(
            num_scalar_prefetch=1,
            grid=(NT,),
            in_specs=[tok(D), pl.BlockSpec(memory_space=pl.ANY), mod, mod, res((1, D)), res((D, 2 * RET_QK + 2 * RET_V)),
                      row_tab, col_tab],
            out_specs=[tok(D), tok(RET_QK), tok(RET_QK), tok(RET_V), tok(RET_V)],
            scratch_shapes=[pltpu.VMEM((3, TM // SUBLANES, SUBLANES, D), F32), pltpu.SemaphoreType.DMA((2,))],
        ),
        out_shape=[
            jax.ShapeDtypeStruct((T, D), F32),
            jax.ShapeDtypeStruct((T, RET_QK), BF16),
            jax.ShapeDtypeStruct((T, RET_QK), BF16),
            jax.ShapeDtypeStruct((T, RET_V), BF16),
            jax.ShapeDtypeStruct((T, RET_V), BF16),
        ],
        compiler_params=_params(("arbitrary",)),
        name="inproj1",
    )(pos, h1, _by_sublane_tile(ys), modv0, modv1, g, w_in, *tabs)


ROPE_TILE_ROWS = TM // GRID_W


def _rope_tables():
    n = LANES // 2
    inv_freq = ROPE_BASE ** (-np.arange(n, dtype=np.float64) / n)

    def lines(p):
        ang = p[:, None] * inv_freq[None, :]
        return np.stack([np.concatenate([np.cos(ang), np.cos(ang)], axis=1),
                         np.concatenate([-np.sin(ang), np.sin(ang)], axis=1)])

    ident = np.stack([np.ones((TM, LANES)), np.zeros((TM, LANES))])
    row_tab = np.concatenate([lines(np.arange(ROWS, dtype=np.float64)), ident[:, :SUBLANES]], axis=1)
    col_lat = lines(np.tile(np.arange(GRID_W, dtype=np.float64), ROPE_TILE_ROWS))
    col_tab = np.stack([col_lat, ident])
    return jnp.asarray(row_tab, dtype=F32), jnp.asarray(col_tab, dtype=F32)


RET_NCHUNK = T // RET_CHUNK
RET_CTX_CHUNKS = CTX // RET_CHUNK
RET_LAT_CHUNKS = SEQ // RET_CHUNK
RET_NHD = 2 * RET_HEADS


def _ret_kernel(dec_ref, qf_ref, kf_ref, vf_ref, qb_ref, kb_ref, vb_ref, of_ref, ob_ref,
                state, dm, qd, kd, cd):
    C = RET_CHUNK

    @pl.when(pl.program_id(0) == 0)
    def _():
        state[...] = jnp.zeros_like(state)
        n = lax.broadcasted_iota(jnp.int32, (C, C), 0).astype(F32)
        m = lax.broadcasted_iota(jnp.int32, (C, C), 1).astype(F32)
        for d in range(2):
            for h in range(RET_HEADS):
                idx = d * RET_HEADS + h
                lg = jnp.log1p(-jnp.exp(jnp.full((C, C), dec_ref[d, h], F32)))
                diff = n - m if d == 0 else m - n
                low = diff >= 0.0
                dm[idx] = jnp.where(low, jnp.exp(jnp.where(low, diff, 0.0) * lg), 0.0)
                qpow = n + 1.0 if d == 0 else C - n
                kpow = (C - 1.0) - n if d == 0 else n
                qcol = jnp.exp(qpow * lg)
                kcol = jnp.exp(kpow * lg)
                for c0 in range(0, RET_DV, C):
                    qd[idx, :, c0:c0 + C] = qcol
                for c0 in range(0, RET_DK, C):
                    kd[idx, :, c0:c0 + C] = kcol
                for c0 in range(0, RET_DV, LANES):
                    cd[idx, :, c0:c0 + LANES] = jnp.exp(C * lg[0:8, 0:LANES])

    for d, (q_ref, k_ref, v_ref, o_ref) in enumerate(((qf_ref, kf_ref, vf_ref, of_ref),
                                                       (qb_ref, kb_ref, vb_ref, ob_ref))):
        for h in range(RET_HEADS):
            idx = d * RET_HEADS + h
            q = q_ref[:, h * RET_DK:(h + 1) * RET_DK]
            k = k_ref[:, h * RET_DK:(h + 1) * RET_DK]
            v = v_ref[:, h * RET_DV:(h + 1) * RET_DV]
            st = state[idx]
            s = _dot_nt(q, k) * dm[idx]
            o = _dot(s.astype(BF16), v) + qd[idx] * _dot(q, st.astype(BF16))
            o_ref[:, h * RET_DV:(h + 1) * RET_DV] = o.astype(BF16)
            kdk = (k.astype(F32) * kd[idx]).astype(BF16)
            cdv = cd[idx, 0:1, :]
            state[idx] = st * cdv + _dot(kdk.T, v)


def _ret_scan(decay, q, k, v):
    def fwd(s):
        return jnp.where(s < RET_CTX_CHUNKS, RET_LAT_CHUNKS + s, s - RET_CTX_CHUNKS)

    def bwd(s):
        return RET_NCHUNK - 1 - s

    def spec(w, f):
        return pl.BlockSpec((RET_CHUNK, w), lambda s: (f(s), 0))

    return pl.pallas_call(
        _ret_kernel,
        grid=(RET_NCHUNK,),
        in_specs=[
            pl.BlockSpec(memory_space=pltpu.SMEM),
            spec(RET_QK, fwd), spec(RET_QK, fwd), spec(RET_V, fwd),
            spec(RET_QK, bwd), spec(RET_QK, bwd), spec(RET_V, bwd),
        ],
        out_specs=[spec(RET_V, fwd), spec(RET_V, bwd)],
        out_shape=[jax.ShapeDtypeStruct((T, RET_V), BF16), jax.ShapeDtypeStruct((T, RET_V), BF16)],
        scratch_shapes=[
            pltpu.VMEM((RET_NHD, RET_DK, RET_DV), F32),
            pltpu.VMEM((RET_NHD, RET_CHUNK, RET_CHUNK), F32),
            pltpu.VMEM((RET_NHD, RET_CHUNK, RET_DV), F32),
            pltpu.VMEM((RET_NHD, RET_CHUNK, RET_DK), F32),
            pltpu.VMEM((RET_NHD, 8, RET_DV), F32),
        ],
        compiler_params=_params(("arbitrary",)),
        name="retention_scan",
    )(decay, q, k, v, q, k, v)


def _outproj1_kernel(of_ref, ob_ref, gg_ref, h_ref, mod_ref, w_ref, *rest):
    def residual_tile():
        y = jnp.zeros((TM, D), F32)
        for hh in range(RET_HEADS):
            cols = slice(hh * RET_DV, (hh + 1) * RET_DV)
            o = of_ref[:, cols].astype(F32) + ob_ref[:, cols].astype(F32)
            on = o * lax.rsqrt(jnp.mean(o * o, axis=-1, keepdims=True) + RMS_EPS)
            g = gg_ref[:, cols].astype(F32)
            z = (g * jax.nn.sigmoid(g)) * on
            y = y + _dot(z.astype(BF16), w_ref[cols, :])
        return h_ref[...] + mod_ref[0, 2:3, :] * y

    _two_stage(residual_tile, mod_ref, *rest)


def _outproj1(o_f, o_b, gg, h, modv, w_out, g2, rw, rb, triu):
    e_in, e_out, e_shape = _epilogue_specs(NT_LAT)
    tok = lambda w: pl.BlockSpec((TM, w), lambda i: (jnp.minimum(i, NT_LAT - 1), 0))
    latent_mod = pl.BlockSpec((1, 8, D), lambda i: (0, 0, 0))
    return pl.pallas_call(
        _outproj1_kernel,
        grid=(NT_LAT + 1,),
        in_specs=[tok(RET_V), tok(RET_V), tok(RET_V), tok(D), latent_mod, _resident((RET_V, D))] + e_in,
        out_specs=e_out,
        out_shape=e_shape,
        scratch_shapes=_EPILOGUE_SCRATCH,
        compiler_params=_params(("arbitrary",)),
        name="outproj1",
    )(o_f, o_b, gg, h, modv, w_out, g2, rw, rb, triu)


def _final_kernel(pos_ref, h_ref, ys_hbm, mod_ref, g_ref, o_ref, buf, sem):
    f = _gathered_rows(pos_ref, ys_hbm, buf, sem, request_next=True)
    h = h_ref[...] + mod_ref[0, 5:6, :] * f
    o_ref[...] = h * lax.rsqrt(jnp.mean(h * h, axis=-1, keepdims=True) + RMS_EPS) * g_ref[...]


def _final(pos, h, ys, modv, g):
    tok = pl.BlockSpec((TM, D), lambda i, pos: (i, 0))
    return pl.pallas_call(
        _final_kernel,
        grid_spec=pltpu.PrefetchScalarGridSpec(
            num_scalar_prefetch=1,
            grid=(NT_LAT,),
            in_specs=[tok, pl.BlockSpec(memory_space=pl.ANY),
                      pl.BlockSpec((1, 8, D), lambda i, pos: (0, 0, 0)),
                      pl.BlockSpec((1, D), lambda i, pos: (0, 0))],
            out_specs=tok,
            scratch_shapes=_GATHER_SCRATCH,
        ),
        out_shape=jax.ShapeDtypeStruct((SEQ, D), F32),
        compiler_params=_params(("arbitrary",)),
        name="final_norm",
    )(pos, h, _by_sublane_tile(ys), modv, g)


def _dft_constants():
    c = np.arange(FNET_GD)
    ang = 2.0 * np.pi * np.outer(c, c) / FNET_GD
    d64 = np.zeros((FNET_W, 2 * FNET_W))
    for g in range(FNET_W // FNET_GD):
        sl = slice(g * FNET_GD, (g + 1) * FNET_GD)
        d64[sl, sl] = np.cos(ang)
        d64[sl, FNET_W + g * FNET_GD:FNET_W + (g + 1) * FNET_GD] = -np.sin(ang)
    k2 = np.arange(FFT_N2)[:, None]
    n2 = np.arange(FFT_N2)[None, :]
    ga = np.zeros((FFT_N1, 2 * FFT_N2, 2 * FFT_N2))
    for n1 in range(FFT_N1):
        th = 2.0 * np.pi * ((k2 * (n1 + FFT_N1 * n2)) % SEQ) / SEQ
        cs, sn = np.cos(th), np.sin(th)
        ga[n1] = np.block([[cs, sn], [-sn, cs]])
    k1 = np.arange(FFT_N1)
    ph = 2.0 * np.pi * np.outer(k1, k1) / FFT_N1
    csb = np.concatenate([np.cos(ph), np.sin(ph)], axis=1)
    p = np.arange(CTX)
    pc = 2.0 * np.pi * (np.outer(p, p) % CTX) / CTX
    csc = np.concatenate([np.cos(pc), np.sin(pc)], axis=1)
    as_bf16 = lambda a: jnp.asarray(a, dtype=F32).astype(BF16)
    return as_bf16(d64), as_bf16(ga), as_bf16(csb), as_bf16(csc)


def _mod_table(mod_l):
    t = mod_l.reshape(2, 6, D)
    return jnp.concatenate([t, jnp.zeros((2, 2, D), F32)], axis=1)


def kernel(x, c, ctx, c_ctx, ada_w, ada_b, norm_g, final_norm_g, mixab_w_in, mixab_w_out, na_rpb,
           ret_w_in, ret_w_out, ret_decay, router_w, router_b, moe_w1, moe_w3, moe_w2):
    d64, ga, csb, csc = _dft_constants()
    triu = jnp.asarray(np.triu(np.ones((TM, TM)), 1), dtype=BF16)
    rw = jnp.pad(router_w, ((0, 0), (0, LANES - N_EXPERTS)))
    rb = jnp.broadcast_to(router_b[:, None], (N_EXPERTS, TM))
    w1 = moe_w1.reshape(DEPTH * N_EXPERTS, D, D_EXPERT)
    w3 = moe_w3.reshape(DEPTH * N_EXPERTS, D, D_EXPERT)
    w2 = moe_w2.reshape(DEPTH * N_EXPERTS, D_EXPERT, D)

    mod = _mod_vectors(c, c_ctx, ada_w, ada_b)
    modv0, modv1 = _mod_table(mod[0]), _mod_table(mod[1])

    qkv, br, bi = _inproj0(x[0], ctx[0], modv0, norm_g[0, 0].reshape(1, D), mixab_w_in[0].astype(BF16), d64)
    a_lat = _fft_latent(br, bi, ga, csb)
    ao_ctx = _ctx_mixer(qkv, br, bi, csc)
    o_lat = _na_attention(qkv, _na_bias_rows(na_rpb[0]))
    h1, xrow, meta, cnt = _outproj0(a_lat, o_lat, ao_ctx, x[0], ctx[0], modv0, mixab_w_out[0].astype(BF16),
                                    norm_g[0, 1].reshape(1, D), rw, rb, triu)
    n_tiles0 = NT + N_REAL_BUCKETS
    pos0, plan0, n_used0 = _route_plan(meta, cnt, n_tiles0, 0)
    ys0 = _moe(plan0, n_used0, _dispatch(pos0, xrow, n_tiles0), w1, w3, w2, n_tiles0)

    h2, q, k, v, gg = _inproj1(pos0, h1, ys0, modv0, modv1, norm_g[1, 0].reshape(1, D),
                               ret_w_in[0].astype(BF16), _rope_tables())
    o_f, o_b = _ret_scan(ret_decay[0].astype(F32), q, k, v)
    h3, xrow1, meta1, cnt1 = _outproj1(o_f, o_b, gg, h2, modv1, ret_w_out[0].astype(BF16),
                                       norm_g[1, 1].reshape(1, D), rw, rb, triu)
    n_tiles1 = NT_LAT + N_REAL_BUCKETS
    pos1, plan1, n_used1 = _route_plan(meta1, cnt1, n_tiles1, 1)
    ys1 = _moe(plan1, n_used1, _dispatch(pos1, xrow1, n_tiles1), w1, w3, w2, n_tiles1)
    out = _final(pos1, h3, ys1, modv1, final_norm_g.reshape(1, D))
    return out[None]
```

```python
import functools
import math

import numpy as np
import jax
import jax.numpy as jnp
from jax import lax
from jax.experimental import pallas as pl
from jax.experimental.pallas import tpu as pltpu

F32 = jnp.float32
BF16 = jnp.bfloat16

D = 1024
SEQ = 16384
CTX = 256
T = SEQ + CTX
DEPTH = 2
GRID_W = 64
ROWS = SEQ // GRID_W
RMS_EPS = 1e-6
ROPE_BASE = 10000.0

FNET_W = 256
FNET_GD = 64
NA_HEADS = 12
NA_HD = 64
NA_W = NA_HEADS * NA_HD
NA_KH = 8
NA_KW = 16
AB_IN = FNET_W + 3 * NA_W
QKV_W = 3 * NA_W

RET_HEADS = 4
RET_DK = 256
RET_DV = 512
RET_QK = RET_HEADS * RET_DK
RET_V = RET_HEADS * RET_DV
RET_CHUNK = 256

N_EXPERTS = 16
EPG = 4
D_EXPERT = 512

LANES = 128
TM = 256
NT = T // TM
NT_LAT = SEQ // TM
TMM = TM
N_BUCKET_IDS = 64
N_REAL_BUCKETS = 24
XROW_W = D + LANES
FFT_N1 = 128
FFT_N2 = 128
NEG_BIG = -1e30
SUBLANES = 8

VMEM_LIMIT = 56 * 1024 * 1024


def _params(sem, vmem=VMEM_LIMIT):
    return pltpu.CompilerParams(dimension_semantics=sem, vmem_limit_bytes=vmem)


def _dot(a, b):
    return jnp.dot(a, b, preferred_element_type=F32)


def _dot_nt(a, b):
    return lax.dot_general(a, b, (((1,), (1,)), ((), ())), preferred_element_type=F32)


def _dot_tn(a, b):
    return lax.dot_general(a, b, (((0,), (0,)), ((), ())), preferred_element_type=F32)


def _rms_mod(x, g, sh, sc):
    ms = jnp.mean(x * x, axis=-1, keepdims=True)
    y = x * lax.rsqrt(ms + RMS_EPS) * g
    return y * (1.0 + sc) + sh


def _resident(shape):
    nd = len(shape)
    return pl.BlockSpec(shape, lambda *_: (0,) * nd, pipeline_mode=pl.Buffered(1))


MOD_TN = 768


def _mod_kernel(cs_ref, w_ref, b_ref, o_ref):
    cs = cs_ref[...]
    s = cs * jax.nn.sigmoid(cs)
    w = w_ref[0]
    r0 = jnp.sum(s[:, 0:1] * w, axis=0, keepdims=True)
    r1 = jnp.sum(s[:, 1:2] * w, axis=0, keepdims=True)
    o_ref[0] = jnp.concatenate([r0, r1], axis=0) + b_ref[0]


def _mod_vectors(c, c_ctx, ada_w, ada_b):
    cs = jnp.stack([c[0], c_ctx], axis=1)
    n = 6 * D
    return pl.pallas_call(
        _mod_kernel,
        grid=(DEPTH, n // MOD_TN),
        in_specs=[
            pl.BlockSpec((D, 2), lambda l, j: (0, 0)),
            pl.BlockSpec((1, D, MOD_TN), lambda l, j: (l, 0, j)),
            pl.BlockSpec((1, 1, MOD_TN), lambda l, j: (l, 0, j)),
        ],
        out_specs=pl.BlockSpec((1, 2, MOD_TN), lambda l, j: (l, 0, j)),
        out_shape=jax.ShapeDtypeStruct((DEPTH, 2, n), F32),
        compiler_params=_params(("arbitrary", "arbitrary")),
        name="mod_vectors",
    )(cs, ada_w, ada_b.reshape(DEPTH, 1, n))


def _mod_spec():
    return pl.BlockSpec((1, 8, D), lambda i, *_: (i // NT_LAT, 0, 0))


def _stream_tile(x_ref, ctx_ref):
    return jnp.where(pl.program_id(0) >= NT_LAT, ctx_ref[...], x_ref[...])


def _stream_specs():
    return [pl.BlockSpec((TM, D), lambda i: (jnp.minimum(i, NT_LAT - 1), 0)),
            pl.BlockSpec((CTX, D), lambda i: (0, 0))]


def _inproj0_kernel(x_ref, ctx_ref, mod_ref, g_ref, w_ref, d64_ref, qkv_ref, br_ref, bi_ref):
    h = _stream_tile(x_ref, ctx_ref)
    xm = _rms_mod(h, g_ref[...], mod_ref[0, 0:1, :], mod_ref[0, 1:2, :]).astype(BF16)
    p = _dot(xm, w_ref[...])
    qkv_ref[...] = p[:, FNET_W:].astype(BF16)
    b = _dot(p[:, :FNET_W].astype(BF16), d64_ref[...])
    br_ref[...] = b[:, :FNET_W].astype(BF16)
    bi_ref[...] = b[:, FNET_W:].astype(BF16)


def _inproj0(x, ctx, modv, g, w_in, d64):
    return pl.pallas_call(
        _inproj0_kernel,
        grid=(NT,),
        in_specs=_stream_specs() + [
            _mod_spec(),
            _resident((1, D)),
            _resident((D, AB_IN)),
            _resident((FNET_W, 2 * FNET_W)),
        ],
        out_specs=[
            pl.BlockSpec((TM, QKV_W), lambda i: (i, 0)),
            pl.BlockSpec((TM, FNET_W), lambda i: (i, 0)),
            pl.BlockSpec((TM, FNET_W), lambda i: (i, 0)),
        ],
        out_shape=[
            jax.ShapeDtypeStruct((T, QKV_W), BF16),
            jax.ShapeDtypeStruct((T, FNET_W), BF16),
            jax.ShapeDtypeStruct((T, FNET_W), BF16),
        ],
        compiler_params=_params(("arbitrary",)),
        name="inproj0",
    )(x, ctx, modv, g, w_in, d64)


FFT_A_NB = 8
FFT_B_TN = 4096


def _fft_a_kernel(br_ref, bi_ref, ga_ref, v_ref):
    for u in range(FFT_A_NB):
        cols = slice(u * FNET_W, (u + 1) * FNET_W)
        z = _dot(ga_ref[u, :, 0:FFT_N2], br_ref[:, cols]) + _dot(ga_ref[u, :, FFT_N2:], bi_ref[:, cols])
        v_ref[0, u] = z[:FFT_N2].astype(BF16)
        v_ref[1, u] = z[FFT_N2:].astype(BF16)


def _fft_b_kernel(cs_ref, v_ref, o_ref):
    o_ref[...] = _dot(cs_ref[...], v_ref[...]) * (1.0 / math.sqrt(SEQ * FNET_GD))


def _fft_latent(br, bi, ga, csb):
    width = FFT_N1 * FNET_W
    br2 = br.reshape(T // FFT_N1, width)
    bi2 = bi.reshape(T // FFT_N1, width)
    v = pl.pallas_call(
        _fft_a_kernel,
        grid=(FFT_N1 // FFT_A_NB,),
        in_specs=[
            pl.BlockSpec((FFT_N2, FFT_A_NB * FNET_W), lambda j: (0, j)),
            pl.BlockSpec((FFT_N2, FFT_A_NB * FNET_W), lambda j: (0, j)),
            pl.BlockSpec((FFT_A_NB, 2 * FFT_N2, 2 * FFT_N2), lambda j: (j, 0, 0)),
        ],
        out_specs=pl.BlockSpec((2, FFT_A_NB, FFT_N2, FNET_W), lambda j: (0, j, 0, 0)),
        out_shape=jax.ShapeDtypeStruct((2, FFT_N1, FFT_N2, FNET_W), BF16),
        compiler_params=_params(("arbitrary",)),
        name="fft_stage_a",
    )(br2, bi2, ga)
    v2 = v.reshape(2 * FFT_N1, FFT_N2 * FNET_W)
    x = pl.pallas_call(
        _fft_b_kernel,
        grid=(FFT_N2 * FNET_W // FFT_B_TN,),
        in_specs=[
            _resident((FFT_N1, 2 * FFT_N1)),
            pl.BlockSpec((2 * FFT_N1, FFT_B_TN), lambda j: (0, j)),
        ],
        out_specs=pl.BlockSpec((FFT_N1, FFT_B_TN), lambda j: (0, j)),
        out_shape=jax.ShapeDtypeStruct((FFT_N1, FFT_N2 * FNET_W), F32),
        compiler_params=_params(("arbitrary",)),
        name="fft_stage_b",
    )(csb, v2)
    return x.reshape(SEQ, FNET_W)


def _head_pair_masks():
    lane = lax.broadcasted_iota(jnp.int32, (1, LANES), 1)
    return lane < NA_HD


def _ctx_kernel(qkv_ref, br_ref, bi_ref, cs_ref, o_ref):
    a = _dot(cs_ref[:, 0:CTX], br_ref[...]) + _dot(cs_ref[:, CTX:], bi_ref[...])
    o_ref[:, 0:FNET_W] = (a * (1.0 / math.sqrt(CTX * FNET_GD))).astype(BF16)
    m0 = _head_pair_masks()
    for hp in range(NA_HEADS // 2):
        q = qkv_ref[:, hp * LANES:(hp + 1) * LANES]
        k = qkv_ref[:, NA_W + hp * LANES:NA_W + (hp + 1) * LANES]
        v = qkv_ref[:, 2 * NA_W + hp * LANES:2 * NA_W + (hp + 1) * LANES]
        outs = []
        for a_ in range(2):
            qa = jnp.where(m0 if a_ == 0 else jnp.logical_not(m0), q, jnp.zeros_like(q))
            s = _dot_nt(qa, k) * (NA_HD ** -0.5)
            p = jnp.exp(s - jnp.max(s, axis=-1, keepdims=True))
            l = jnp.sum(p, axis=-1, keepdims=True)
            outs.append(_dot(p.astype(BF16), v) / l)
        o_ref[:, FNET_W + hp * LANES:FNET_W + (hp + 1) * LANES] = jnp.where(m0, outs[0], outs[1]).astype(BF16)


def _ctx_mixer(qkv, br, bi, csc):
    return pl.pallas_call(
        _ctx_kernel,
        grid=(1,),
        in_specs=[
            pl.BlockSpec((CTX, QKV_W), lambda i: (SEQ // CTX, 0)),
            pl.BlockSpec((CTX, FNET_W), lambda i: (SEQ // CTX, 0)),
            pl.BlockSpec((CTX, FNET_W), lambda i: (SEQ // CTX, 0)),
            pl.BlockSpec((CTX, 2 * CTX), lambda i: (0, 0)),
        ],
        out_specs=pl.BlockSpec((CTX, D), lambda i: (0, 0)),
        out_shape=jax.ShapeDtypeStruct((CTX, D), BF16),
        compiler_params=_params(("arbitrary",)),
        name="ctx_mixer",
    )(qkv, br, bi, csc)


NA_RB = 8
NA_WIN = NA_KH * GRID_W


NA_NK = NA_WIN + CTX
NA_SUB = 16


def _na_kernel(q_ref, k_ref, v_ref, kc_ref, vc_ref, bias_ref, o_ref, s_scr, p_scr, l_scr):
    b = pl.program_id(1)
    m0 = _head_pair_masks()
    nm0 = jnp.logical_not(m0)
    scale = NA_HD ** -0.5

    def window(i):
        r = b * NA_RB + i
        rs = jnp.clip(r - NA_KH // 2, 0, ROWS - NA_KH)
        return pl.multiple_of(rs * GRID_W, GRID_W), rs - r + (NA_KH - 1)

    def scores(i):
        start, _ = window(i)
        q = q_ref[i * GRID_W:(i + 1) * GRID_W, :] * scale
        qs = jnp.concatenate([jnp.where(m0, q, jnp.zeros_like(q)), jnp.where(nm0, q, jnp.zeros_like(q))], axis=0)
        s_scr[i, :, 0:NA_WIN] = _dot_nt(qs, k_ref[pl.ds(start, NA_WIN), :])
        s_scr[i, :, NA_WIN:NA_NK] = _dot_nt(qs, kc_ref[...])

    def softmax(i):
        _, e = window(i)
        for g in range(2 * GRID_W // NA_SUB):
            rows = slice(g * NA_SUB, (g + 1) * NA_SUB)
            a, c0 = divmod(g * NA_SUB, GRID_W)
            s1 = s_scr[i, rows, 0:NA_WIN] + bias_ref[a, e, c0:c0 + NA_SUB, :]
            s2 = s_scr[i, rows, NA_WIN:NA_NK]
            mx = jnp.maximum(jnp.max(s1, axis=-1, keepdims=True), jnp.max(s2, axis=-1, keepdims=True))
            p1 = jnp.exp(s1 - mx)
            p2 = jnp.exp(s2 - mx)
            l = jnp.sum(p1, axis=-1, keepdims=True) + jnp.sum(p2, axis=-1, keepdims=True)
            p_scr[i, rows, 0:NA_WIN] = p1.astype(BF16)
            p_scr[i, rows, NA_WIN:NA_NK] = p2.astype(BF16)
            l_scr[i, rows, :] = jnp.broadcast_to(l, (NA_SUB, LANES))

    def values(i):
        start, _ = window(i)
        o = _dot(p_scr[i, :, 0:NA_WIN], v_ref[pl.ds(start, NA_WIN), :]) + _dot(p_scr[i, :, NA_WIN:NA_NK], vc_ref[...])
        o = o / l_scr[i]
        o_ref[i * GRID_W:(i + 1) * GRID_W, :] = jnp.where(m0, o[0:GRID_W], o[GRID_W:]).astype(BF16)

    for i in range(NA_RB):
        scores(i)
    for i in range(NA_RB):
        softmax(i)
        values(i)


def _na_attention(qkv, bias):
    nq = NA_RB * GRID_W
    nhp = NA_HEADS // 2
    return pl.pallas_call(
        _na_kernel,
        grid=(nhp, ROWS // NA_RB),
        in_specs=[
            pl.BlockSpec((nq, LANES), lambda hp, b: (b, hp)),
            pl.BlockSpec((SEQ, LANES), lambda hp, b: (0, nhp + hp)),
            pl.BlockSpec((SEQ, LANES), lambda hp, b: (0, 2 * nhp + hp)),
            pl.BlockSpec((CTX, LANES), lambda hp, b: (SEQ // CTX, nhp + hp)),
            pl.BlockSpec((CTX, LANES), lambda hp, b: (SEQ // CTX, 2 * nhp + hp)),
            pl.BlockSpec((2, NA_KH, GRID_W, NA_WIN), lambda hp, b: (hp, 0, 0, 0)),
        ],
        out_specs=pl.BlockSpec((nq, LANES), lambda hp, b: (b, hp)),
        out_shape=jax.ShapeDtypeStruct((SEQ, NA_W), BF16),
        scratch_shapes=[
            pltpu.VMEM((NA_RB, 2 * GRID_W, NA_NK), F32),
            pltpu.VMEM((NA_RB, 2 * GRID_W, NA_NK), BF16),
            pltpu.VMEM((NA_RB, 2 * GRID_W, LANES), F32),
        ],
        compiler_params=_params(("arbitrary", "arbitrary")),
        name="na_attention",
    )(qkv, qkv, qkv, qkv, qkv, bias)


def _na_bias_rows(rpb):
    c = np.arange(GRID_W)[:, None]
    kc = np.arange(GRID_W)[None, :]
    ws = np.clip(c - NA_KW // 2, 0, GRID_W - NA_KW)
    col_ok = (kc >= ws) & (kc < ws + NA_KW)
    dc = np.clip(kc - c + NA_KW - 1, 0, 2 * NA_KW - 2)
    pick = (dc[None] == np.arange(2 * NA_KW - 1)[:, None, None]).astype(np.float32)
    full = jnp.einsum("hrd,dck->hrck", rpb.astype(F32), jnp.asarray(pick), precision=lax.Precision.HIGHEST)
    full = jnp.where(jnp.asarray(col_ok)[None, None], full, NEG_BIG)
    rows = [jnp.transpose(full[:, e:e + NA_KH], (0, 2, 1, 3)).reshape(NA_HEADS, GRID_W, NA_WIN)
            for e in range(NA_KH)]
    return jnp.stack(rows, axis=1)


def _route_stage(xm2, valid, rw_ref, rb_ref, triu_ref, run_ref, xrow_ref, meta_ref, cnt_ref):
    xrow_ref[:, 0:D] = xm2
    xh = xm2.astype(BF16)
    xl = (xm2 - xh.astype(F32)).astype(BF16)
    w = rw_ref[...]
    wh = w.astype(BF16)
    wl = (w - wh.astype(F32)).astype(BF16)
    logits = _dot(xh, wh) + (_dot(xl, wh) + _dot(xh, wl))
    scores = jax.nn.sigmoid(logits.T[0:N_EXPERTS, :])
    biased = scores + rb_ref[...]
    s = [scores[e:e + 1, :] for e in range(N_EXPERTS)]
    b = [biased[e:e + 1, :] for e in range(N_EXPERTS)]
    n_groups = N_EXPERTS // EPG
    top2, group_score = [], []
    for g in range(n_groups):
        m = b[EPG * g:EPG * (g + 1)]
        total = None
        for k in range(EPG):
            rank = None
            for j in range(EPG):
                if j != k:
                    beats = jnp.where((m[j] >= m[k]) if j < k else (m[j] > m[k]), 1.0, 0.0)
                    rank = beats if rank is None else rank + beats
            top2.append(rank < 2.0)
            term = jnp.where(top2[-1], m[k], 0.0)
            total = term if total is None else total + term
        group_score.append(total)
    best, g_sel = group_score[0], jnp.zeros((1, TM), jnp.int32)
    for g in range(1, n_groups):
        better = group_score[g] > best
        g_sel = jnp.where(better, g, g_sel)
        best = jnp.where(better, group_score[g], best)
    sel = [top2[e] & (g_sel == e // EPG) for e in range(N_EXPERTS)]
    w_sum = None
    for e in range(N_EXPERTS):
        term = jnp.where(sel[e], s[e], 0.0)
        w_sum = term if w_sum is None else w_sum + term
    zero = jnp.zeros((1, TM), F32)
    w_lo, w_hi, bid = zero, zero, zero
    for g in range(n_groups):
        seen = None
        for k in range(EPG):
            e = EPG * g + k
            if seen is None:
                w_lo = w_lo + jnp.where(sel[e], s[e], 0.0)
                seen = sel[e]
            else:
                w_lo = w_lo + jnp.where(sel[e] & jnp.logical_not(seen), s[e], 0.0)
                w_hi = w_hi + jnp.where(sel[e] & seen, s[e], 0.0)
                seen = seen | sel[e]
            bid = bid + jnp.where(sel[e], float((1 << k) + 8 * g), 0.0)
    w_lo = w_lo / w_sum
    w_hi = w_hi / w_sum
    bucket = lax.broadcasted_iota(jnp.int32, (N_BUCKET_IDS, TM), 0).astype(F32)
    onehot = bucket == bid
    ones = jnp.where(onehot & valid, 1.0, 0.0)
    earlier = _dot(ones.astype(BF16), triu_ref[...])
    run = run_ref[...]
    run_t = jnp.concatenate([run] * (TM // LANES), axis=1)
    rank_row = jnp.sum(jnp.where(onehot, earlier + run_t, 0.0), axis=0, keepdims=True)
    new_run = run + jnp.sum(ones, axis=1, keepdims=True)
    run_ref[...] = new_run
    cnt_ref[...] = new_run
    meta_rows = jnp.concatenate([w_lo, w_hi, bid, rank_row, jnp.zeros((SUBLANES - 4, TM), F32)], axis=0)
    meta_ref[...] = meta_rows
    padded = jnp.concatenate([meta_rows, jnp.zeros((LANES - SUBLANES, TM), F32)], axis=0)
    xrow_ref[:, D:XROW_W] = padded.T


def _epilogue_specs(ntiles):
    in_specs = [
        _resident((1, D)),
        _resident((D, LANES)),
        _resident((N_EXPERTS, TM)),
        _resident((TM, TM)),
    ]
    out_specs = [
        pl.BlockSpec((TM, D), lambda i: (jnp.minimum(i, ntiles - 1), 0)),
        pl.BlockSpec((TM, XROW_W), lambda i: (jnp.maximum(i - 1, 0), 0)),
        pl.BlockSpec((SUBLANES, TM), lambda i: (0, jnp.maximum(i - 1, 0))),
        pl.BlockSpec((N_BUCKET_IDS, LANES), lambda i: (0, 0)),
    ]
    n = ntiles * TM
    out_shape = [
        jax.ShapeDtypeStruct((n, D), F32),
        jax.ShapeDtypeStruct((n, XROW_W), F32),
        jax.ShapeDtypeStruct((SUBLANES, n), F32),
        jax.ShapeDtypeStruct((N_BUCKET_IDS, LANES), F32),
    ]
    return in_specs, out_specs, out_shape


_EPILOGUE_SCRATCH = [pltpu.VMEM((N_BUCKET_IDS, LANES), F32), pltpu.VMEM((2, TM, D), F32)]


def _two_stage(residual_tile, mod_ref, g2_ref, rw_ref, rb_ref, triu_ref,
               h1_ref, xrow_ref, meta_ref, cnt_ref, run_ref, stash):
    i = pl.program_id(0)

    @pl.when(i == 0)
    def _():
        run_ref[...] = jnp.zeros_like(run_ref)
        stash[1] = jnp.zeros((TM, D), F32)

    _route_stage(stash[1 - i % 2], i >= 1, rw_ref, rb_ref, triu_ref, run_ref, xrow_ref, meta_ref, cnt_ref)
    h1 = residual_tile()
    h1_ref[...] = h1
    stash[i % 2] = _rms_mod(h1, g2_ref[...], mod_ref[0, 3:4, :], mod_ref[0, 4:5, :])


def _outproj0_kernel(a_ref, o_ref, aoc_ref, x_ref, ctx_ref, mod_ref, w_ref, *rest):
    def residual_tile():
        is_ctx = pl.program_id(0) >= NT_LAT
        a = jnp.where(is_ctx, aoc_ref[:, 0:FNET_W], a_ref[...].astype(BF16))
        o = jnp.where(is_ctx, aoc_ref[:, FNET_W:], o_ref[...])
        y = _dot(a, w_ref[0:FNET_W, :]) + _dot(o, w_ref[FNET_W:, :])
        return _stream_tile(x_ref, ctx_ref) + mod_ref[0, 2:3, :] * y

    _two_stage(residual_tile, mod_ref, *rest)


def _outproj0(a_lat, o_lat, ao_ctx, x, ctx, modv, w_out, g2, rw, rb, triu):
    e_in, e_out, e_shape = _epilogue_specs(NT)
    lat = lambda w: pl.BlockSpec((TM, w), lambda i: (jnp.minimum(i, NT_LAT - 1), 0))
    return pl.pallas_call(
        _outproj0_kernel,
        grid=(NT + 1,),
        in_specs=[lat(FNET_W), lat(NA_W), pl.BlockSpec((CTX, D), lambda i: (0, 0))] + _stream_specs() + [
            _mod_spec(),
            _resident((D, D)),
        ] + e_in,
        out_specs=e_out,
        out_shape=e_shape,
        scratch_shapes=_EPILOGUE_SCRATCH,
        compiler_params=_params(("arbitrary",)),
        name="outproj0",
    )(a_lat, o_lat, ao_ctx, x, ctx, modv, w_out, g2, rw, rb, triu)


def _bucket_tables():
    lo = np.zeros(N_BUCKET_IDS, np.int32)
    hi = np.zeros(N_BUCKET_IDS, np.int32)
    for g in range(N_EXPERTS // EPG):
        for a in range(EPG):
            for b in range(a + 1, EPG):
                i = 16 * g + (1 << a) + (1 << b)
                lo[i] = EPG * g + a
                hi[i] = EPG * g + b
    return lo, hi


def _slot_kernel(meta_ref, starts_ref, o_ref):
    n = meta_ref.shape[1]
    bucket = lax.broadcasted_iota(jnp.int32, (N_BUCKET_IDS, n), 0).astype(F32)
    first = jnp.sum(jnp.where(bucket == meta_ref[2:3, :], starts_ref[:, 0:1], 0.0), axis=0, keepdims=True)
    o_ref[...] = jnp.broadcast_to((first + meta_ref[3:4, :]).astype(jnp.int32), (SUBLANES, n))


def _token_slots(meta, first_row):
    n = meta.shape[1]
    out = pl.pallas_call(
        _slot_kernel,
        grid=(1,),
        in_specs=[pl.BlockSpec((SUBLANES, n), lambda i: (0, 0)),
                  pl.BlockSpec((N_BUCKET_IDS, LANES), lambda i: (0, 0))],
        out_specs=pl.BlockSpec((SUBLANES, n), lambda i: (0, 0)),
        out_shape=jax.ShapeDtypeStruct((SUBLANES, n), jnp.int32),
        compiler_params=_params(("arbitrary",)),
        name="token_slots",
    )(meta, first_row)
    return out[0]


def _route_plan(meta, cnt, n_tiles, layer):
    counts = cnt[:, 0].astype(jnp.int32)
    nt_b = (counts + TMM - 1) // TMM
    ends = jnp.cumsum(nt_b)
    starts = ends - nt_b
    first_row = jnp.broadcast_to((starts * TMM).astype(F32)[:, None], (N_BUCKET_IDS, LANES))
    pos = _token_slots(meta, first_row)
    n_used = ends[-1]
    tile = jnp.minimum(jnp.arange(n_tiles, dtype=jnp.int32), n_used - 1)
    tile_b = jnp.sum((ends[None, :] <= tile[:, None]).astype(jnp.int32), axis=1)
    lo, hi = (jnp.asarray(t) + layer * N_EXPERTS for t in _bucket_tables())
    ids = jnp.arange(n_tiles, dtype=jnp.int32)
    first = (tile_b != jnp.concatenate([jnp.full((1,), -1, jnp.int32), tile_b[:-1]])) & (ids < n_used)
    wslot = (jnp.cumsum(first.astype(jnp.int32)) - 1) % 2
    next_tile = ends[tile_b]
    has_next = next_tile < n_used
    next_b = tile_b[jnp.minimum(next_tile, n_tiles - 1)]
    plan = jnp.stack([lo[tile_b], hi[tile_b], first.astype(jnp.int32), wslot,
                      jnp.where(has_next, lo[next_b], -1), jnp.where(has_next, hi[next_b], -1)])
    fill = ((ids >= n_used) | (ids == next_tile - 1)).astype(jnp.int32)
    return pos, fill, plan.astype(jnp.int32), n_used.reshape(1).astype(jnp.int32)


def _dispatch_kernel(pos_ref, fill_ref, x_ref, xs_ref, sem, zeros, zsem):
    base = pl.program_id(0) * TM
    tile_rows = TMM // SUBLANES

    @pl.when(pl.program_id(0) == 0)
    def _():
        zeros[...] = jnp.zeros_like(zeros)

        def fill(t):
            return pltpu.make_async_copy(zeros, xs_ref.at[pl.ds(t * tile_rows, tile_rows)], zsem)

        def start(t, carry):
            @pl.when(fill_ref[t] == 1)
            def _():
                fill(t).start()
            return carry

        def wait(t, carry):
            @pl.when(fill_ref[t] == 1)
            def _():
                fill(t).wait()
            return carry

        lax.fori_loop(0, fill_ref.shape[0], start, 0)
        lax.fori_loop(0, fill_ref.shape[0], wait, 0)

    for j in range(TM // SUBLANES):
        for u in range(SUBLANES):
            p = pos_ref[base + j * SUBLANES + u]
            pltpu.make_async_copy(x_ref.at[j, pl.ds(u, 1), :],
                                  xs_ref.at[p >> 3, pl.ds(p & (SUBLANES - 1), 1), :], sem).start()
    pltpu.make_async_copy(x_ref, xs_ref.at[pl.ds(0, TM // SUBLANES)], sem).wait()


def _dispatch(pos, fill, xrow, n_tiles):
    ntok = xrow.shape[0]
    xs = pl.pallas_call(
        _dispatch_kernel,
        grid_spec=pltpu.PrefetchScalarGridSpec(
            num_scalar_prefetch=2,
            grid=(ntok // TM,),
            in_specs=[pl.BlockSpec((TM // SUBLANES, SUBLANES, XROW_W), lambda i, pos, fill: (i, 0, 0))],
            out_specs=pl.BlockSpec(memory_space=pl.ANY),
            scratch_shapes=[
                pltpu.SemaphoreType.DMA(()),
                pltpu.VMEM((TMM // SUBLANES, SUBLANES, XROW_W), F32),
                pltpu.SemaphoreType.DMA(()),
            ],
        ),
        out_shape=jax.ShapeDtypeStruct((n_tiles * TMM // SUBLANES, SUBLANES, XROW_W), F32),
        compiler_params=_params(("arbitrary",)),
        name="moe_dispatch",
    )(pos, fill, _by_sublane_tile(xrow))
    return xs.reshape(n_tiles * TMM, XROW_W)


def _moe_kernel(plan_ref, nu_ref, xs_ref, w1_hbm, w3_hbm, w2_hbm, ys_ref, wb1, wb3, wb2, sem):
    j = pl.program_id(0)
    slot = plan_ref[3, j]

    def weight_copies(e_lo, e_hi, s):
        out = []
        for k, e in enumerate((e_lo, e_hi)):
            out.append(pltpu.make_async_copy(w1_hbm.at[e], wb1.at[s, k], sem.at[s, 3 * k]))
            out.append(pltpu.make_async_copy(w3_hbm.at[e], wb3.at[s, k], sem.at[s, 3 * k + 1]))
            out.append(pltpu.make_async_copy(w2_hbm.at[e], wb2.at[s, k], sem.at[s, 3 * k + 2]))
        return out

    @pl.when(j == 0)
    def _():
        for c in weight_copies(plan_ref[0, 0], plan_ref[1, 0], 0):
            c.start()

    @pl.when(plan_ref[2, j] == 1)
    def _():
        for c in weight_copies(plan_ref[0, j], plan_ref[1, j], slot):
            c.wait()

        @pl.when(plan_ref[4, j] >= 0)
        def _():
            for c in weight_copies(plan_ref[4, j], plan_ref[5, j], 1 - slot):
                c.start()

    @pl.when(j < nu_ref[0])
    def _():
        x = xs_ref[:, 0:D].astype(BF16)
        g_lo = xs_ref[:, D:D + 1]
        g_hi = xs_ref[:, D + 1:D + 2]

        def expert(k):
            a = _dot(x, wb1[slot, k].astype(BF16))
            hid = (a * jax.nn.sigmoid(a)) * _dot(x, wb3[slot, k].astype(BF16))
            return _dot(hid.astype(BF16), wb2[slot, k].astype(BF16))

        ys_ref[...] = g_lo * expert(0) + g_hi * expert(1)

    @pl.when(j >= nu_ref[0])
    def _():
        ys_ref[...] = jnp.zeros_like(ys_ref)


def _moe(plan, n_used, xs, w1, w3, w2, n_tiles):
    hbm = pl.BlockSpec(memory_space=pl.ANY)
    return pl.pallas_call(
        _moe_kernel,
        grid_spec=pltpu.PrefetchScalarGridSpec(
            num_scalar_prefetch=2,
            grid=(n_tiles,),
            in_specs=[pl.BlockSpec((TMM, XROW_W), lambda j, plan, nu: (j, 0)), hbm, hbm, hbm],
            out_specs=pl.BlockSpec((TMM, D), lambda j, plan, nu: (j, 0)),
            scratch_shapes=[
                pltpu.VMEM((2, 2, D, D_EXPERT), F32),
                pltpu.VMEM((2, 2, D, D_EXPERT), F32),
                pltpu.VMEM((2, 2, D_EXPERT, D), F32),
                pltpu.SemaphoreType.DMA((2, 6)),
            ],
        ),
        out_shape=jax.ShapeDtypeStruct((n_tiles * TMM, D), F32),
        compiler_params=_params(("arbitrary",)),
        name="moe_experts",
    )(plan, n_used, xs, w1, w3, w2)


def _gather_issue(pos_ref, ys_hbm, buf, sem, tile, slot, inline, part=(0, TM // SUBLANES)):
    base = tile * TM

    def issue(j, carry):
        for u in range(SUBLANES):
            p = pos_ref[base + j * SUBLANES + u]
            pltpu.make_async_copy(ys_hbm.at[p >> 3, pl.ds(p & (SUBLANES - 1), 1), :],
                                  buf.at[slot, j, pl.ds(u, 1), :], sem.at[slot]).start()
        return carry

    if inline:
        for j in range(*part):
            issue(j, 0)
    else:
        lax.fori_loop(0, TM // SUBLANES, issue, 0)


def _gather_wait(ys_hbm, buf, sem, slot):
    pltpu.make_async_copy(ys_hbm.at[pl.ds(0, TM // SUBLANES)], buf.at[slot], sem.at[slot]).wait()


def _gathered_rows(pos_ref, ys_hbm, buf, sem, request_next):
    i = pl.program_id(0)
    slot = i % 2

    @pl.when(i == 0)
    def _():
        _gather_issue(pos_ref, ys_hbm, buf, sem, 0, 0, inline=False)

    if request_next:
        @pl.when(i + 1 < pl.num_programs(0))
        def _():
            _gather_issue(pos_ref, ys_hbm, buf, sem, i + 1, 1 - slot, inline=False)

    _gather_wait(ys_hbm, buf, sem, slot)
    return buf[slot].reshape(TM, D)


def _gather_drain(ys_hbm, buf, sem):
    i = pl.program_id(0)

    @pl.when(i + 1 == pl.num_programs(0))
    def _():
        _gather_wait(ys_hbm, buf, sem, 1 - i % 2)


_GATHER_SCRATCH = [pltpu.VMEM((2, TM // SUBLANES, SUBLANES, D), F32), pltpu.SemaphoreType.DMA((2,))]


def _by_sublane_tile(a):
    return a.reshape(a.shape[0] // SUBLANES, SUBLANES, a.shape[1])


def _rope_store(x, tabs, out_ref):
    cr, sr, cc, sc = tabs
    for ch in range(RET_QK // LANES):
        xc = x[:, ch * LANES:(ch + 1) * LANES]
        xr = pltpu.roll(xc, LANES // 2, axis=1)
        y = xc * cr + xr * sr if ch % 2 == 0 else xc * cc + xr * sc
        out_ref[:, ch * LANES:(ch + 1) * LANES] = y.astype(BF16)


def _inproj1_kernel(pos_ref, h_ref, ys_hbm, mod0_ref, mod1_ref, g_ref, w_ref, rt_ref, ct_ref,
                    h2_ref, q_ref, k_ref, v_ref, gg_ref, buf, sem, xm_scr):
    f = _gathered_rows(pos_ref, ys_hbm, buf, sem, request_next=False)
    h2 = h_ref[...] + mod0_ref[0, 5:6, :] * f
    h2_ref[...] = h2
    xm_scr[...] = _rms_mod(h2, g_ref[...], mod1_ref[0, 0:1, :], mod1_ref[0, 1:2, :]).astype(BF16)
    i = pl.program_id(0)
    _gather_issue(pos_ref, ys_hbm, buf, sem, jnp.where(i + 1 < pl.num_programs(0), i + 1, 0), 1 - i % 2, inline=True)
    xm = xm_scr[...]

    off = (pl.program_id(0) % (SUBLANES // ROPE_TILE_ROWS)) * ROPE_TILE_ROWS

    def row_table(which):
        return jnp.concatenate([jnp.broadcast_to(rt_ref[which, pl.ds(off + j, 1), :], (GRID_W, LANES))
                                for j in range(ROPE_TILE_ROWS)], axis=0)

    tabs = (row_table(0), row_table(1), ct_ref[0, 0], ct_ref[0, 1])
    _rope_store(_dot(xm, w_ref[:, 0:RET_QK]), tabs, q_ref)
    _rope_store(_dot(xm, w_ref[:, RET_QK:2 * RET_QK]) * (RET_DK ** -0.5), tabs, k_ref)
    v_ref[...] = _dot(xm, w_ref[:, 2 * RET_QK:2 * RET_QK + RET_V]).astype(BF16)
    gg_ref[...] = _dot(xm, w_ref[:, 2 * RET_QK + RET_V:]).astype(BF16)
    _gather_drain(ys_hbm, buf, sem)


def _inproj1(pos, h1, ys, modv0, modv1, g, w_in, tabs):
    tok = lambda w: pl.BlockSpec((TM, w), lambda i, pos: (i, 0))
    res = lambda shape: pl.BlockSpec(shape, lambda i, pos: (0,) * len(shape), pipeline_mode=pl.Buffered(1))
    mod = pl.BlockSpec((1, 8, D), lambda i, pos: (i // NT_LAT, 0, 0))
    row_tab = pl.BlockSpec((2, SUBLANES, LANES), lambda i, pos: (0, i // (SUBLANES // ROPE_TILE_ROWS), 0))
    col_tab = pl.BlockSpec((1, 2, TM, LANES), lambda i, pos: (i // NT_LAT, 0, 0, 0))
    return pl.pallas_call(
        _inproj1_kernel,
        grid_spec=pltpu.PrefetchScalarGridSpec(
            num_scalar_prefetch=1,
            grid=(NT,),
            in_specs=[tok(D), pl.BlockSpec(memory_space=pl.ANY), mod, mod, res((1, D)), res((D, 2 * RET_QK + 2 * RET_V)),
                      row_tab, col_tab],
            out_specs=[tok(D), tok(RET_QK), tok(RET_QK), tok(RET_V), tok(RET_V)],
            scratch_shapes=_GATHER_SCRATCH + [pltpu.VMEM((TM, D), BF16)],
        ),
        out_shape=[
            jax.ShapeDtypeStruct((T, D), F32),
            jax.ShapeDtypeStruct((T, RET_QK), BF16),
            jax.ShapeDtypeStruct((T, RET_QK), BF16),
            jax.ShapeDtypeStruct((T, RET_V), BF16),
            jax.ShapeDtypeStruct((T, RET_V), BF16),
        ],
        compiler_params=_params(("arbitrary",)),
        name="inproj1",
    )(pos, h1, _by_sublane_tile(ys), modv0, modv1, g, w_in, *tabs)


ROPE_TILE_ROWS = TM // GRID_W


def _rope_tables():
    n = LANES // 2
    inv_freq = ROPE_BASE ** (-np.arange(n, dtype=np.float64) / n)

    def lines(p):
        ang = p[:, None] * inv_freq[None, :]
        return np.stack([np.concatenate([np.cos(ang), np.cos(ang)], axis=1),
                         np.concatenate([-np.sin(ang), np.sin(ang)], axis=1)])

    ident = np.stack([np.ones((TM, LANES)), np.zeros((TM, LANES))])
    row_tab = np.concatenate([lines(np.arange(ROWS, dtype=np.float64)), ident[:, :SUBLANES]], axis=1)
    col_lat = lines(np.tile(np.arange(GRID_W, dtype=np.float64), ROPE_TILE_ROWS))
    col_tab = np.stack([col_lat, ident])
    return jnp.asarray(row_tab, dtype=F32), jnp.asarray(col_tab, dtype=F32)


RET_NCHUNK = T // RET_CHUNK
RET_CTX_CHUNKS = CTX // RET_CHUNK
RET_LAT_CHUNKS = SEQ // RET_CHUNK
RET_NHD = 2 * RET_HEADS


def _ret_kernel(dec_ref, qf_ref, kf_ref, vf_ref, qb_ref, kb_ref, vb_ref, of_ref, ob_ref,
                state, dm, qd, kd, cd):
    C = RET_CHUNK

    @pl.when(pl.program_id(0) == 0)
    def _():
        state[...] = jnp.zeros_like(state)
        n = lax.broadcasted_iota(jnp.int32, (C, C), 0).astype(F32)
        m = lax.broadcasted_iota(jnp.int32, (C, C), 1).astype(F32)
        for d in range(2):
            for h in range(RET_HEADS):
                idx = d * RET_HEADS + h
                lg = jnp.log1p(-jnp.exp(jnp.full((C, C), dec_ref[d, h], F32)))
                diff = n - m if d == 0 else m - n
                low = diff >= 0.0
                dm[idx] = jnp.where(low, jnp.exp(jnp.where(low, diff, 0.0) * lg), 0.0)
                qpow = n + 1.0 if d == 0 else C - n
                kpow = (C - 1.0) - n if d == 0 else n
                qcol = jnp.exp(qpow * lg)
                kcol = jnp.exp(kpow * lg)
                for c0 in range(0, RET_DV, C):
                    qd[idx, :, c0:c0 + C] = qcol
                for c0 in range(0, RET_DK, C):
                    kd[idx, :, c0:c0 + C] = kcol
                for c0 in range(0, RET_DV, LANES):
                    cd[idx, :, c0:c0 + LANES] = jnp.exp(C * lg[0:8, 0:LANES])

    for d, (q_ref, k_ref, v_ref, o_ref) in enumerate(((qf_ref, kf_ref, vf_ref, of_ref),
                                                       (qb_ref, kb_ref, vb_ref, ob_ref))):
        for h in range(RET_HEADS):
            idx = d * RET_HEADS + h
            q = q_ref[:, h * RET_DK:(h + 1) * RET_DK]
            k = k_ref[:, h * RET_DK:(h + 1) * RET_DK]
            v = v_ref[:, h * RET_DV:(h + 1) * RET_DV]
            st = state[idx]
            s = _dot_nt(q, k) * dm[idx]
            o = _dot(s.astype(BF16), v) + qd[idx] * _dot(q, st.astype(BF16))
            o_ref[:, h * RET_DV:(h + 1) * RET_DV] = o.astype(BF16)
            kdk = (k.astype(F32) * kd[idx]).astype(BF16)
            cdv = cd[idx, 0:1, :]
            state[idx] = st * cdv + _dot(kdk.T, v)


def _ret_scan(decay, q, k, v):
    def fwd(s):
        return jnp.where(s < RET_CTX_CHUNKS, RET_LAT_CHUNKS + s, s - RET_CTX_CHUNKS)

    def bwd(s):
        return RET_NCHUNK - 1 - s

    def spec(w, f):
        return pl.BlockSpec((RET_CHUNK, w), lambda s: (f(s), 0))

    return pl.pallas_call(
        _ret_kernel,
        grid=(RET_NCHUNK,),
        in_specs=[
            pl.BlockSpec(memory_space=pltpu.SMEM),
            spec(RET_QK, fwd), spec(RET_QK, fwd), spec(RET_V, fwd),
            spec(RET_QK, bwd), spec(RET_QK, bwd), spec(RET_V, bwd),
        ],
        out_specs=[spec(RET_V, fwd), spec(RET_V, bwd)],
        out_shape=[jax.ShapeDtypeStruct((T, RET_V), BF16), jax.ShapeDtypeStruct((T, RET_V), BF16)],
        scratch_shapes=[
            pltpu.VMEM((RET_NHD, RET_DK, RET_DV), F32),
            pltpu.VMEM((RET_NHD, RET_CHUNK, RET_CHUNK), F32),
            pltpu.VMEM((RET_NHD, RET_CHUNK, RET_DV), F32),
            pltpu.VMEM((RET_NHD, RET_CHUNK, RET_DK), F32),
            pltpu.VMEM((RET_NHD, 8, RET_DV), F32),
        ],
        compiler_params=_params(("arbitrary",)),
        name="retention_scan",
    )(decay, q, k, v, q, k, v)


def _outproj1_kernel(of_ref, ob_ref, gg_ref, h_ref, mod_ref, w_ref, *rest):
    def residual_tile():
        y = jnp.zeros((TM, D), F32)
        for hh in range(RET_HEADS):
            cols = slice(hh * RET_DV, (hh + 1) * RET_DV)
            o = of_ref[:, cols].astype(F32) + ob_ref[:, cols].astype(F32)
            on = o * lax.rsqrt(jnp.mean(o * o, axis=-1, keepdims=True) + RMS_EPS)
            g = gg_ref[:, cols].astype(F32)
            z = (g * jax.nn.sigmoid(g)) * on
            y = y + _dot(z.astype(BF16), w_ref[cols, :])
        return h_ref[...] + mod_ref[0, 2:3, :] * y

    _two_stage(residual_tile, mod_ref, *rest)


def _outproj1(o_f, o_b, gg, h, modv, w_out, g2, rw, rb, triu):
    e_in, e_out, e_shape = _epilogue_specs(NT_LAT)
    tok = lambda w: pl.BlockSpec((TM, w), lambda i: (jnp.minimum(i, NT_LAT - 1), 0))
    latent_mod = pl.BlockSpec((1, 8, D), lambda i: (0, 0, 0))
    return pl.pallas_call(
        _outproj1_kernel,
        grid=(NT_LAT + 1,),
        in_specs=[tok(RET_V), tok(RET_V), tok(RET_V), tok(D), latent_mod, _resident((RET_V, D))] + e_in,
        out_specs=e_out,
        out_shape=e_shape,
        scratch_shapes=_EPILOGUE_SCRATCH,
        compiler_params=_params(("arbitrary",)),
        name="outproj1",
    )(o_f, o_b, gg, h, modv, w_out, g2, rw, rb, triu)


def _final_kernel(pos_ref, h_ref, ys_hbm, mod_ref, g_ref, o_ref, buf, sem):
    f = _gathered_rows(pos_ref, ys_hbm, buf, sem, request_next=True)
    h = h_ref[...] + mod_ref[0, 5:6, :] * f
    o_ref[...] = h * lax.rsqrt(jnp.mean(h * h, axis=-1, keepdims=True) + RMS_EPS) * g_ref[...]


def _final(pos, h, ys, modv, g):
    tok = pl.BlockSpec((TM, D), lambda i, pos: (i, 0))
    return pl.pallas_call(
        _final_kernel,
        grid_spec=pltpu.PrefetchScalarGridSpec(
            num_scalar_prefetch=1,
            grid=(NT_LAT,),
            in_specs=[tok, pl.BlockSpec(memory_space=pl.ANY),
                      pl.BlockSpec((1, 8, D), lambda i, pos: (0, 0, 0)),
                      pl.BlockSpec((1, D), lambda i, pos: (0, 0))],
            out_specs=tok,
            scratch_shapes=_GATHER_SCRATCH,
        ),
        out_shape=jax.ShapeDtypeStruct((SEQ, D), F32),
        compiler_params=_params(("arbitrary",)),
        name="final_norm",
    )(pos, h, _by_sublane_tile(ys), modv, g)


def _dft_constants():
    c = np.arange(FNET_GD)
    ang = 2.0 * np.pi * np.outer(c, c) / FNET_GD
    d64 = np.zeros((FNET_W, 2 * FNET_W))
    for g in range(FNET_W // FNET_GD):
        sl = slice(g * FNET_GD, (g + 1) * FNET_GD)
        d64[sl, sl] = np.cos(ang)
        d64[sl, FNET_W + g * FNET_GD:FNET_W + (g + 1) * FNET_GD] = -np.sin(ang)
    k2 = np.arange(FFT_N2)[:, None]
    n2 = np.arange(FFT_N2)[None, :]
    ga = np.zeros((FFT_N1, 2 * FFT_N2, 2 * FFT_N2))
    for n1 in range(FFT_N1):
        th = 2.0 * np.pi * ((k2 * (n1 + FFT_N1 * n2)) % SEQ) / SEQ
        cs, sn = np.cos(th), np.sin(th)
        ga[n1] = np.block([[cs, sn], [-sn, cs]])
    k1 = np.arange(FFT_N1)
    ph = 2.0 * np.pi * np.outer(k1, k1) / FFT_N1
    csb = np.concatenate([np.cos(ph), np.sin(ph)], axis=1)
    p = np.arange(CTX)
    pc = 2.0 * np.pi * (np.outer(p, p) % CTX) / CTX
    csc = np.concatenate([np.cos(pc), np.sin(pc)], axis=1)
    as_bf16 = lambda a: jnp.asarray(a, dtype=F32).astype(BF16)
    return as_bf16(d64), as_bf16(ga), as_bf16(csb), as_bf16(csc)


def _mod_table(mod_l):
    t = mod_l.reshape(2, 6, D)
    return jnp.concatenate([t, jnp.zeros((2, 2, D), F32)], axis=1)


def kernel(x, c, ctx, c_ctx, ada_w, ada_b, norm_g, final_norm_g, mixab_w_in, mixab_w_out, na_rpb,
           ret_w_in, ret_w_out, ret_decay, router_w, router_b, moe_w1, moe_w3, moe_w2):
    d64, ga, csb, csc = _dft_constants()
    triu = jnp.asarray(np.triu(np.ones((TM, TM)), 1), dtype=BF16)
    rw = jnp.pad(router_w, ((0, 0), (0, LANES - N_EXPERTS)))
    rb = jnp.broadcast_to(router_b[:, None], (N_EXPERTS, TM))
    w1 = moe_w1.reshape(DEPTH * N_EXPERTS, D, D_EXPERT)
    w3 = moe_w3.reshape(DEPTH * N_EXPERTS, D, D_EXPERT)
    w2 = moe_w2.reshape(DEPTH * N_EXPERTS, D_EXPERT, D)

    mod = _mod_vectors(c, c_ctx, ada_w, ada_b)
    modv0, modv1 = _mod_table(mod[0]), _mod_table(mod[1])

    qkv, br, bi = _inproj0(x[0], ctx[0], modv0, norm_g[0, 0].reshape(1, D), mixab_w_in[0].astype(BF16), d64)
    a_lat = _fft_latent(br, bi, ga, csb)
    ao_ctx = _ctx_mixer(qkv, br, bi, csc)
    o_lat = _na_attention(qkv, _na_bias_rows(na_rpb[0]))
    h1, xrow, meta, cnt = _outproj0(a_lat, o_lat, ao_ctx, x[0], ctx[0], modv0, mixab_w_out[0].astype(BF16),
                                    norm_g[0, 1].reshape(1, D), rw, rb, triu)
    n_tiles0 = NT + N_REAL_BUCKETS
    pos0, fill0, plan0, n_used0 = _route_plan(meta, cnt, n_tiles0, 0)
    ys0 = _moe(plan0, n_used0, _dispatch(pos0, fill0, xrow, n_tiles0), w1, w3, w2, n_tiles0)

    h2, q, k, v, gg = _inproj1(pos0, h1, ys0, modv0, modv1, norm_g[1, 0].reshape(1, D),
                               ret_w_in[0].astype(BF16), _rope_tables())
    o_f, o_b = _ret_scan(ret_decay[0].astype(F32), q, k, v)
    h3, xrow1, meta1, cnt1 = _outproj1(o_f, o_b, gg, h2, modv1, ret_w_out[0].astype(BF16),
                                       norm_g[1, 1].reshape(1, D), rw, rb, triu)
    n_tiles1 = NT_LAT + N_REAL_BUCKETS
    pos1, fill1, plan1, n_used1 = _route_plan(meta1, cnt1, n_tiles1, 1)
    ys1 = _moe(plan1, n_used1, _dispatch(pos1, fill1, xrow1, n_tiles1), w1, w3, w2, n_tiles1)
    out = _final(pos1, h3, ys1, modv1, final_norm_g.reshape(1, D))
    return out[None]
```

```python
import functools
import math

import numpy as np
import jax
import jax.numpy as jnp
from jax import lax
from jax.experimental import pallas as pl
from jax.experimental.pallas import tpu as pltpu

F32 = jnp.float32
BF16 = jnp.bfloat16

D = 1024
SEQ = 16384
CTX = 256
T = SEQ + CTX
DEPTH = 2
GRID_W = 64
ROWS = SEQ // GRID_W
RMS_EPS = 1e-6
ROPE_BASE = 10000.0

FNET_W = 256
FNET_GD = 64
NA_HEADS = 12
NA_HD = 64
NA_W = NA_HEADS * NA_HD
NA_KH = 8
NA_KW = 16
AB_IN = FNET_W + 3 * NA_W
QKV_W = 3 * NA_W

RET_HEADS = 4
RET_DK = 256
RET_DV = 512
RET_QK = RET_HEADS * RET_DK
RET_V = RET_HEADS * RET_DV
RET_CHUNK = 256

N_EXPERTS = 16
EPG = 4
D_EXPERT = 512

LANES = 128
TM = 256
NT = T // TM
NT_LAT = SEQ // TM
TMM = TM
N_BUCKET_IDS = 64
N_REAL_BUCKETS = 24
XROW_W = D + LANES
FFT_N1 = 128
FFT_N2 = 128
NEG_BIG = -1e30
SUBLANES = 8

VMEM_LIMIT = 56 * 1024 * 1024


def _params(sem, vmem=VMEM_LIMIT):
    return pltpu.CompilerParams(dimension_semantics=sem, vmem_limit_bytes=vmem)


def _dot(a, b):
    return jnp.dot(a, b, preferred_element_type=F32)


def _dot_nt(a, b):
    return lax.dot_general(a, b, (((1,), (1,)), ((), ())), preferred_element_type=F32)


def _dot_tn(a, b):
    return lax.dot_general(a, b, (((0,), (0,)), ((), ())), preferred_element_type=F32)


def _rms_mod(x, g, sh, sc):
    ms = jnp.mean(x * x, axis=-1, keepdims=True)
    y = x * lax.rsqrt(ms + RMS_EPS) * g
    return y * (1.0 + sc) + sh


def _resident(shape):
    nd = len(shape)
    return pl.BlockSpec(shape, lambda *_: (0,) * nd, pipeline_mode=pl.Buffered(1))


MOD_TN = 768


def _mod_kernel(cs_ref, w_ref, b_ref, o_ref):
    cs = cs_ref[...]
    s = cs * jax.nn.sigmoid(cs)
    w = w_ref[0]
    r0 = jnp.sum(s[:, 0:1] * w, axis=0, keepdims=True)
    r1 = jnp.sum(s[:, 1:2] * w, axis=0, keepdims=True)
    o_ref[0] = jnp.concatenate([r0, r1], axis=0) + b_ref[0]


def _mod_vectors(c, c_ctx, ada_w, ada_b):
    cs = jnp.stack([c[0], c_ctx], axis=1)
    n = 6 * D
    return pl.pallas_call(
        _mod_kernel,
        grid=(DEPTH, n // MOD_TN),
        in_specs=[
            pl.BlockSpec((D, 2), lambda l, j: (0, 0)),
            pl.BlockSpec((1, D, MOD_TN), lambda l, j: (l, 0, j)),
            pl.BlockSpec((1, 1, MOD_TN), lambda l, j: (l, 0, j)),
        ],
        out_specs=pl.BlockSpec((1, 2, MOD_TN), lambda l, j: (l, 0, j)),
        out_shape=jax.ShapeDtypeStruct((DEPTH, 2, n), F32),
        compiler_params=_params(("arbitrary", "arbitrary")),
        name="mod_vectors",
    )(cs, ada_w, ada_b.reshape(DEPTH, 1, n))


def _mod_spec():
    return pl.BlockSpec((1, 8, D), lambda i, *_: (i // NT_LAT, 0, 0))


def _stream_tile(x_ref, ctx_ref):
    return jnp.where(pl.program_id(0) >= NT_LAT, ctx_ref[...], x_ref[...])


def _stream_specs():
    return [pl.BlockSpec((TM, D), lambda i: (jnp.minimum(i, NT_LAT - 1), 0)),
            pl.BlockSpec((CTX, D), lambda i: (0, 0))]


def _inproj0_kernel(x_ref, ctx_ref, mod_ref, g_ref, w_ref, d64_ref, qkv_ref, br_ref, bi_ref):
    h = _stream_tile(x_ref, ctx_ref)
    xm = _rms_mod(h, g_ref[...], mod_ref[0, 0:1, :], mod_ref[0, 1:2, :]).astype(BF16)
    p = _dot(xm, w_ref[...])
    qkv_ref[...] = p[:, FNET_W:].astype(BF16)
    b = _dot(p[:, :FNET_W].astype(BF16), d64_ref[...])
    br_ref[...] = b[:, :FNET_W].astype(BF16)
    bi_ref[...] = b[:, FNET_W:].astype(BF16)


def _inproj0(x, ctx, modv, g, w_in, d64):
    return pl.pallas_call(
        _inproj0_kernel,
        grid=(NT,),
        in_specs=_stream_specs() + [
            _mod_spec(),
            _resident((1, D)),
            _resident((D, AB_IN)),
            _resident((FNET_W, 2 * FNET_W)),
        ],
        out_specs=[
            pl.BlockSpec((TM, QKV_W), lambda i: (i, 0)),
            pl.BlockSpec((TM, FNET_W), lambda i: (i, 0)),
            pl.BlockSpec((TM, FNET_W), lambda i: (i, 0)),
        ],
        out_shape=[
            jax.ShapeDtypeStruct((T, QKV_W), BF16),
            jax.ShapeDtypeStruct((T, FNET_W), BF16),
            jax.ShapeDtypeStruct((T, FNET_W), BF16),
        ],
        compiler_params=_params(("arbitrary",)),
        name="inproj0",
    )(x, ctx, modv, g, w_in, d64)


FFT_A_NB = 8
FFT_B_TN = 4096


def _fft_a_kernel(br_ref, bi_ref, ga_ref, v_ref):
    for u in range(FFT_A_NB):
        cols = slice(u * FNET_W, (u + 1) * FNET_W)
        z = _dot(ga_ref[u, :, 0:FFT_N2], br_ref[:, cols]) + _dot(ga_ref[u, :, FFT_N2:], bi_ref[:, cols])
        v_ref[0, u] = z[:FFT_N2].astype(BF16)
        v_ref[1, u] = z[FFT_N2:].astype(BF16)


def _fft_b_kernel(cs_ref, v_ref, o_ref):
    o_ref[...] = _dot(cs_ref[...], v_ref[...]) * (1.0 / math.sqrt(SEQ * FNET_GD))


def _fft_latent(br, bi, ga, csb):
    width = FFT_N1 * FNET_W
    br2 = br.reshape(T // FFT_N1, width)
    bi2 = bi.reshape(T // FFT_N1, width)
    v = pl.pallas_call(
        _fft_a_kernel,
        grid=(FFT_N1 // FFT_A_NB,),
        in_specs=[
            pl.BlockSpec((FFT_N2, FFT_A_NB * FNET_W), lambda j: (0, j)),
            pl.BlockSpec((FFT_N2, FFT_A_NB * FNET_W), lambda j: (0, j)),
            pl.BlockSpec((FFT_A_NB, 2 * FFT_N2, 2 * FFT_N2), lambda j: (j, 0, 0)),
        ],
        out_specs=pl.BlockSpec((2, FFT_A_NB, FFT_N2, FNET_W), lambda j: (0, j, 0, 0)),
        out_shape=jax.ShapeDtypeStruct((2, FFT_N1, FFT_N2, FNET_W), BF16),
        compiler_params=_params(("arbitrary",)),
        name="fft_stage_a",
    )(br2, bi2, ga)
    v2 = v.reshape(2 * FFT_N1, FFT_N2 * FNET_W)
    x = pl.pallas_call(
        _fft_b_kernel,
        grid=(FFT_N2 * FNET_W // FFT_B_TN,),
        in_specs=[
            _resident((FFT_N1, 2 * FFT_N1)),
            pl.BlockSpec((2 * FFT_N1, FFT_B_TN), lambda j: (0, j)),
        ],
        out_specs=pl.BlockSpec((FFT_N1, FFT_B_TN), lambda j: (0, j)),
        out_shape=jax.ShapeDtypeStruct((FFT_N1, FFT_N2 * FNET_W), F32),
        compiler_params=_params(("arbitrary",)),
        name="fft_stage_b",
    )(csb, v2)
    return x.reshape(SEQ, FNET_W)


def _head_pair_masks():
    lane = lax.broadcasted_iota(jnp.int32, (1, LANES), 1)
    return lane < NA_HD


def _ctx_kernel(qkv_ref, br_ref, bi_ref, cs_ref, o_ref):
    a = _dot(cs_ref[:, 0:CTX], br_ref[...]) + _dot(cs_ref[:, CTX:], bi_ref[...])
    o_ref[:, 0:FNET_W] = (a * (1.0 / math.sqrt(CTX * FNET_GD))).astype(BF16)
    m0 = _head_pair_masks()
    for hp in range(NA_HEADS // 2):
        q = qkv_ref[:, hp * LANES:(hp + 1) * LANES]
        k = qkv_ref[:, NA_W + hp * LANES:NA_W + (hp + 1) * LANES]
        v = qkv_ref[:, 2 * NA_W + hp * LANES:2 * NA_W + (hp + 1) * LANES]
        outs = []
        for a_ in range(2):
            qa = jnp.where(m0 if a_ == 0 else jnp.logical_not(m0), q, jnp.zeros_like(q))
            s = _dot_nt(qa, k) * (NA_HD ** -0.5)
            p = jnp.exp(s - jnp.max(s, axis=-1, keepdims=True))
            l = jnp.sum(p, axis=-1, keepdims=True)
            outs.append(_dot(p.astype(BF16), v) / l)
        o_ref[:, FNET_W + hp * LANES:FNET_W + (hp + 1) * LANES] = jnp.where(m0, outs[0], outs[1]).astype(BF16)


def _ctx_mixer(qkv, br, bi, csc):
    return pl.pallas_call(
        _ctx_kernel,
        grid=(1,),
        in_specs=[
            pl.BlockSpec((CTX, QKV_W), lambda i: (SEQ // CTX, 0)),
            pl.BlockSpec((CTX, FNET_W), lambda i: (SEQ // CTX, 0)),
            pl.BlockSpec((CTX, FNET_W), lambda i: (SEQ // CTX, 0)),
            pl.BlockSpec((CTX, 2 * CTX), lambda i: (0, 0)),
        ],
        out_specs=pl.BlockSpec((CTX, D), lambda i: (0, 0)),
        out_shape=jax.ShapeDtypeStruct((CTX, D), BF16),
        compiler_params=_params(("arbitrary",)),
        name="ctx_mixer",
    )(qkv, br, bi, csc)


NA_RB = 8
NA_WIN = NA_KH * GRID_W


NA_NK = NA_WIN + CTX
NA_SUB = 16


def _na_kernel(q_ref, k_ref, v_ref, kc_ref, vc_ref, bias_ref, o_ref, s_scr, p_scr, l_scr):
    b = pl.program_id(1)
    m0 = _head_pair_masks()
    nm0 = jnp.logical_not(m0)
    scale = NA_HD ** -0.5

    def window(i):
        r = b * NA_RB + i
        rs = jnp.clip(r - NA_KH // 2, 0, ROWS - NA_KH)
        return pl.multiple_of(rs * GRID_W, GRID_W), rs - r + (NA_KH - 1)

    def scores(i):
        start, _ = window(i)
        q = q_ref[i * GRID_W:(i + 1) * GRID_W, :] * scale
        qs = jnp.concatenate([jnp.where(m0, q, jnp.zeros_like(q)), jnp.where(nm0, q, jnp.zeros_like(q))], axis=0)
        s_scr[i, :, 0:NA_WIN] = _dot_nt(qs, k_ref[pl.ds(start, NA_WIN), :])
        s_scr[i, :, NA_WIN:NA_NK] = _dot_nt(qs, kc_ref[...])

    def softmax(i):
        _, e = window(i)
        for g in range(2 * GRID_W // NA_SUB):
            rows = slice(g * NA_SUB, (g + 1) * NA_SUB)
            a, c0 = divmod(g * NA_SUB, GRID_W)
            s1 = s_scr[i, rows, 0:NA_WIN] + bias_ref[a, e, c0:c0 + NA_SUB, :]
            s2 = s_scr[i, rows, NA_WIN:NA_NK]
            mx = jnp.maximum(jnp.max(s1, axis=-1, keepdims=True), jnp.max(s2, axis=-1, keepdims=True))
            p1 = jnp.exp(s1 - mx)
            p2 = jnp.exp(s2 - mx)
            l = jnp.sum(p1, axis=-1, keepdims=True) + jnp.sum(p2, axis=-1, keepdims=True)
            p_scr[i, rows, 0:NA_WIN] = p1.astype(BF16)
            p_scr[i, rows, NA_WIN:NA_NK] = p2.astype(BF16)
            l_scr[i, rows, :] = jnp.broadcast_to(l, (NA_SUB, LANES))

    def values(i):
        start, _ = window(i)
        o = _dot(p_scr[i, :, 0:NA_WIN], v_ref[pl.ds(start, NA_WIN), :]) + _dot(p_scr[i, :, NA_WIN:NA_NK], vc_ref[...])
        o = o / l_scr[i]
        o_ref[i * GRID_W:(i + 1) * GRID_W, :] = jnp.where(m0, o[0:GRID_W], o[GRID_W:]).astype(BF16)

    for i in range(NA_RB):
        scores(i)
    for i in range(NA_RB):
        softmax(i)
        values(i)


def _na_attention(qkv, bias):
    nq = NA_RB * GRID_W
    nhp = NA_HEADS // 2
    return pl.pallas_call(
        _na_kernel,
        grid=(nhp, ROWS // NA_RB),
        in_specs=[
            pl.BlockSpec((nq, LANES), lambda hp, b: (b, hp)),
            pl.BlockSpec((SEQ, LANES), lambda hp, b: (0, nhp + hp)),
            pl.BlockSpec((SEQ, LANES), lambda hp, b: (0, 2 * nhp + hp)),
            pl.BlockSpec((CTX, LANES), lambda hp, b: (SEQ // CTX, nhp + hp)),
            pl.BlockSpec((CTX, LANES), lambda hp, b: (SEQ // CTX, 2 * nhp + hp)),
            pl.BlockSpec((2, NA_KH, GRID_W, NA_WIN), lambda hp, b: (hp, 0, 0, 0)),
        ],
        out_specs=pl.BlockSpec((nq, LANES), lambda hp, b: (b, hp)),
        out_shape=jax.ShapeDtypeStruct((SEQ, NA_W), BF16),
        scratch_shapes=[
            pltpu.VMEM((NA_RB, 2 * GRID_W, NA_NK), F32),
            pltpu.VMEM((NA_RB, 2 * GRID_W, NA_NK), BF16),
            pltpu.VMEM((NA_RB, 2 * GRID_W, LANES), F32),
        ],
        compiler_params=_params(("arbitrary", "arbitrary")),
        name="na_attention",
    )(qkv, qkv, qkv, qkv, qkv, bias)


def _na_bias_rows(rpb):
    c = np.arange(GRID_W)[:, None]
    kc = np.arange(GRID_W)[None, :]
    ws = np.clip(c - NA_KW // 2, 0, GRID_W - NA_KW)
    col_ok = (kc >= ws) & (kc < ws + NA_KW)
    dc = np.clip(kc - c + NA_KW - 1, 0, 2 * NA_KW - 2)
    pick = (dc[None] == np.arange(2 * NA_KW - 1)[:, None, None]).astype(np.float32)
    full = jnp.einsum("hrd,dck->hrck", rpb.astype(F32), jnp.asarray(pick), precision=lax.Precision.HIGHEST)
    full = jnp.where(jnp.asarray(col_ok)[None, None], full, NEG_BIG)
    rows = [jnp.transpose(full[:, e:e + NA_KH], (0, 2, 1, 3)).reshape(NA_HEADS, GRID_W, NA_WIN)
            for e in range(NA_KH)]
    return jnp.stack(rows, axis=1)


def _route_stage(xm2, valid, rw_ref, rb_ref, triu_ref, run_ref, xrow_ref, meta_ref, cnt_ref):
    xrow_ref[:, 0:D] = xm2
    xh = xm2.astype(BF16)
    xl = (xm2 - xh.astype(F32)).astype(BF16)
    w = rw_ref[...]
    wh = w.astype(BF16)
    wl = (w - wh.astype(F32)).astype(BF16)
    logits = _dot(xh, wh) + (_dot(xl, wh) + _dot(xh, wl))
    scores = jax.nn.sigmoid(logits.T[0:N_EXPERTS, :])
    biased = scores + rb_ref[...]
    s = [scores[e:e + 1, :] for e in range(N_EXPERTS)]
    b = [biased[e:e + 1, :] for e in range(N_EXPERTS)]
    n_groups = N_EXPERTS // EPG
    top2, group_score = [], []
    for g in range(n_groups):
        m = b[EPG * g:EPG * (g + 1)]
        total = None
        for k in range(EPG):
            rank = None
            for j in range(EPG):
                if j != k:
                    beats = jnp.where((m[j] >= m[k]) if j < k else (m[j] > m[k]), 1.0, 0.0)
                    rank = beats if rank is None else rank + beats
            top2.append(rank < 2.0)
            term = jnp.where(top2[-1], m[k], 0.0)
            total = term if total is None else total + term
        group_score.append(total)
    best, g_sel = group_score[0], jnp.zeros((1, TM), jnp.int32)
    for g in range(1, n_groups):
        better = group_score[g] > best
        g_sel = jnp.where(better, g, g_sel)
        best = jnp.where(better, group_score[g], best)
    sel = [top2[e] & (g_sel == e // EPG) for e in range(N_EXPERTS)]
    w_sum = None
    for e in range(N_EXPERTS):
        term = jnp.where(sel[e], s[e], 0.0)
        w_sum = term if w_sum is None else w_sum + term
    zero = jnp.zeros((1, TM), F32)
    w_lo, w_hi, bid = zero, zero, zero
    for g in range(n_groups):
        seen = None
        for k in range(EPG):
            e = EPG * g + k
            if seen is None:
                w_lo = w_lo + jnp.where(sel[e], s[e], 0.0)
                seen = sel[e]
            else:
                w_lo = w_lo + jnp.where(sel[e] & jnp.logical_not(seen), s[e], 0.0)
                w_hi = w_hi + jnp.where(sel[e] & seen, s[e], 0.0)
                seen = seen | sel[e]
            bid = bid + jnp.where(sel[e], float((1 << k) + 8 * g), 0.0)
    w_lo = w_lo / w_sum
    w_hi = w_hi / w_sum
    bucket = lax.broadcasted_iota(jnp.int32, (N_BUCKET_IDS, TM), 0).astype(F32)
    onehot = bucket == bid
    ones = jnp.where(onehot & valid, 1.0, 0.0)
    earlier = _dot(ones.astype(BF16), triu_ref[...])
    run = run_ref[...]
    run_t = jnp.concatenate([run] * (TM // LANES), axis=1)
    rank_row = jnp.sum(jnp.where(onehot, earlier + run_t, 0.0), axis=0, keepdims=True)
    new_run = run + jnp.sum(ones, axis=1, keepdims=True)
    run_ref[...] = new_run
    cnt_ref[...] = new_run
    meta_rows = jnp.concatenate([w_lo, w_hi, bid, rank_row, jnp.zeros((SUBLANES - 4, TM), F32)], axis=0)
    meta_ref[...] = meta_rows
    padded = jnp.concatenate([meta_rows, jnp.zeros((LANES - SUBLANES, TM), F32)], axis=0)
    xrow_ref[:, D:XROW_W] = padded.T


def _epilogue_specs(ntiles):
    in_specs = [
        _resident((1, D)),
        _resident((D, LANES)),
        _resident((N_EXPERTS, TM)),
        _resident((TM, TM)),
    ]
    out_specs = [
        pl.BlockSpec((TM, D), lambda i: (jnp.minimum(i, ntiles - 1), 0)),
        pl.BlockSpec((TM, XROW_W), lambda i: (jnp.maximum(i - 1, 0), 0)),
        pl.BlockSpec((SUBLANES, TM), lambda i: (0, jnp.maximum(i - 1, 0))),
        pl.BlockSpec((N_BUCKET_IDS, LANES), lambda i: (0, 0)),
    ]
    n = ntiles * TM
    out_shape = [
        jax.ShapeDtypeStruct((n, D), F32),
        jax.ShapeDtypeStruct((n, XROW_W), F32),
        jax.ShapeDtypeStruct((SUBLANES, n), F32),
        jax.ShapeDtypeStruct((N_BUCKET_IDS, LANES), F32),
    ]
    return in_specs, out_specs, out_shape


_EPILOGUE_SCRATCH = [pltpu.VMEM((N_BUCKET_IDS, LANES), F32), pltpu.VMEM((2, TM, D), F32)]


def _two_stage(residual_tile, mod_ref, g2_ref, rw_ref, rb_ref, triu_ref,
               h1_ref, xrow_ref, meta_ref, cnt_ref, run_ref, stash):
    i = pl.program_id(0)

    @pl.when(i == 0)
    def _():
        run_ref[...] = jnp.zeros_like(run_ref)
        stash[1] = jnp.zeros((TM, D), F32)

    _route_stage(stash[1 - i % 2], i >= 1, rw_ref, rb_ref, triu_ref, run_ref, xrow_ref, meta_ref, cnt_ref)
    h1 = residual_tile()
    h1_ref[...] = h1
    stash[i % 2] = _rms_mod(h1, g2_ref[...], mod_ref[0, 3:4, :], mod_ref[0, 4:5, :])


def _outproj0_kernel(a_ref, o_ref, aoc_ref, x_ref, ctx_ref, mod_ref, w_ref, *rest):
    def residual_tile():
        is_ctx = pl.program_id(0) >= NT_LAT
        a = jnp.where(is_ctx, aoc_ref[:, 0:FNET_W], a_ref[...].astype(BF16))
        o = jnp.where(is_ctx, aoc_ref[:, FNET_W:], o_ref[...])
        y = _dot(a, w_ref[0:FNET_W, :]) + _dot(o, w_ref[FNET_W:, :])
        return _stream_tile(x_ref, ctx_ref) + mod_ref[0, 2:3, :] * y

    _two_stage(residual_tile, mod_ref, *rest)


def _outproj0(a_lat, o_lat, ao_ctx, x, ctx, modv, w_out, g2, rw, rb, triu):
    e_in, e_out, e_shape = _epilogue_specs(NT)
    lat = lambda w: pl.BlockSpec((TM, w), lambda i: (jnp.minimum(i, NT_LAT - 1), 0))
    return pl.pallas_call(
        _outproj0_kernel,
        grid=(NT + 1,),
        in_specs=[lat(FNET_W), lat(NA_W), pl.BlockSpec((CTX, D), lambda i: (0, 0))] + _stream_specs() + [
            _mod_spec(),
            _resident((D, D)),
        ] + e_in,
        out_specs=e_out,
        out_shape=e_shape,
        scratch_shapes=_EPILOGUE_SCRATCH,
        compiler_params=_params(("arbitrary",)),
        name="outproj0",
    )(a_lat, o_lat, ao_ctx, x, ctx, modv, w_out, g2, rw, rb, triu)


def _bucket_tables():
    lo = np.zeros(N_BUCKET_IDS, np.int32)
    hi = np.zeros(N_BUCKET_IDS, np.int32)
    for g in range(N_EXPERTS // EPG):
        for a in range(EPG):
            for b in range(a + 1, EPG):
                i = 16 * g + (1 << a) + (1 << b)
                lo[i] = EPG * g + a
                hi[i] = EPG * g + b
    return lo, hi


def _slot_kernel(meta_ref, starts_ref, o_ref):
    n = meta_ref.shape[1]
    bucket = lax.broadcasted_iota(jnp.int32, (N_BUCKET_IDS, n), 0).astype(F32)
    first = jnp.sum(jnp.where(bucket == meta_ref[2:3, :], starts_ref[:, 0:1], 0.0), axis=0, keepdims=True)
    o_ref[...] = jnp.broadcast_to((first + meta_ref[3:4, :]).astype(jnp.int32), (SUBLANES, n))


def _token_slots(meta, first_row):
    n = meta.shape[1]
    out = pl.pallas_call(
        _slot_kernel,
        grid=(1,),
        in_specs=[pl.BlockSpec((SUBLANES, n), lambda i: (0, 0)),
                  pl.BlockSpec((N_BUCKET_IDS, LANES), lambda i: (0, 0))],
        out_specs=pl.BlockSpec((SUBLANES, n), lambda i: (0, 0)),
        out_shape=jax.ShapeDtypeStruct((SUBLANES, n), jnp.int32),
        compiler_params=_params(("arbitrary",)),
        name="token_slots",
    )(meta, first_row)
    return out[0]


def _route_plan(meta, cnt, n_tiles, layer):
    counts = cnt[:, 0].astype(jnp.int32)
    nt_b = (counts + TMM - 1) // TMM
    ends = jnp.cumsum(nt_b)
    starts = ends - nt_b
    first_row = jnp.broadcast_to((starts * TMM).astype(F32)[:, None], (N_BUCKET_IDS, LANES))
    pos = _token_slots(meta, first_row)
    n_used = ends[-1]
    tile = jnp.minimum(jnp.arange(n_tiles, dtype=jnp.int32), n_used - 1)
    tile_b = jnp.sum((ends[None, :] <= tile[:, None]).astype(jnp.int32), axis=1)
    lo, hi = (jnp.asarray(t) + layer * N_EXPERTS for t in _bucket_tables())
    ids = jnp.arange(n_tiles, dtype=jnp.int32)
    first = (tile_b != jnp.concatenate([jnp.full((1,), -1, jnp.int32), tile_b[:-1]])) & (ids < n_used)
    wslot = (jnp.cumsum(first.astype(jnp.int32)) - 1) % 2
    next_tile = ends[tile_b]
    has_next = next_tile < n_used
    next_b = tile_b[jnp.minimum(next_tile, n_tiles - 1)]
    plan = jnp.stack([lo[tile_b], hi[tile_b], first.astype(jnp.int32), wslot,
                      jnp.where(has_next, lo[next_b], -1), jnp.where(has_next, hi[next_b], -1)])
    fill = ((ids >= n_used) | (ids == next_tile - 1)).astype(jnp.int32)
    return pos, fill, plan.astype(jnp.int32), n_used.reshape(1).astype(jnp.int32)


def _dispatch_kernel(pos_ref, fill_ref, x_ref, xs_ref, sem, zeros, zsem):
    base = pl.program_id(0) * TM
    tile_rows = TMM // SUBLANES

    @pl.when(pl.program_id(0) == 0)
    def _():
        zeros[...] = jnp.zeros_like(zeros)

        def fill(t):
            return pltpu.make_async_copy(zeros, xs_ref.at[pl.ds(t * tile_rows, tile_rows)], zsem)

        def start(t, carry):
            @pl.when(fill_ref[t] == 1)
            def _():
                fill(t).start()
            return carry

        def wait(t, carry):
            @pl.when(fill_ref[t] == 1)
            def _():
                fill(t).wait()
            return carry

        lax.fori_loop(0, fill_ref.shape[0], start, 0)
        lax.fori_loop(0, fill_ref.shape[0], wait, 0)

    for j in range(TM // SUBLANES):
        for u in range(SUBLANES):
            p = pos_ref[base + j * SUBLANES + u]
            pltpu.make_async_copy(x_ref.at[j, pl.ds(u, 1), :],
                                  xs_ref.at[p >> 3, pl.ds(p & (SUBLANES - 1), 1), :], sem).start()
    pltpu.make_async_copy(x_ref, xs_ref.at[pl.ds(0, TM // SUBLANES)], sem).wait()


def _dispatch(pos, fill, xrow, n_tiles):
    ntok = xrow.shape[0]
    xs = pl.pallas_call(
        _dispatch_kernel,
        grid_spec=pltpu.PrefetchScalarGridSpec(
            num_scalar_prefetch=2,
            grid=(ntok // TM,),
            in_specs=[pl.BlockSpec((TM // SUBLANES, SUBLANES, XROW_W), lambda i, pos, fill: (i, 0, 0))],
            out_specs=pl.BlockSpec(memory_space=pl.ANY),
            scratch_shapes=[
                pltpu.SemaphoreType.DMA(()),
                pltpu.VMEM((TMM // SUBLANES, SUBLANES, XROW_W), F32),
                pltpu.SemaphoreType.DMA(()),
            ],
        ),
        out_shape=jax.ShapeDtypeStruct((n_tiles * TMM // SUBLANES, SUBLANES, XROW_W), F32),
        compiler_params=_params(("arbitrary",)),
        name="moe_dispatch",
    )(pos, fill, _by_sublane_tile(xrow))
    return xs.reshape(n_tiles * TMM, XROW_W)


def _moe_kernel(plan_ref, nu_ref, xs_ref, w1_hbm, w3_hbm, w2_hbm, ys_ref, wb1, wb3, wb2, sem):
    j = pl.program_id(0)
    slot = plan_ref[3, j]

    def weight_copies(e_lo, e_hi, s):
        out = []
        for k, e in enumerate((e_lo, e_hi)):
            out.append(pltpu.make_async_copy(w1_hbm.at[e], wb1.at[s, k], sem.at[s, 3 * k]))
            out.append(pltpu.make_async_copy(w3_hbm.at[e], wb3.at[s, k], sem.at[s, 3 * k + 1]))
            out.append(pltpu.make_async_copy(w2_hbm.at[e], wb2.at[s, k], sem.at[s, 3 * k + 2]))
        return out

    @pl.when(j == 0)
    def _():
        for c in weight_copies(plan_ref[0, 0], plan_ref[1, 0], 0):
            c.start()

    @pl.when(plan_ref[2, j] == 1)
    def _():
        for c in weight_copies(plan_ref[0, j], plan_ref[1, j], slot):
            c.wait()

        @pl.when(plan_ref[4, j] >= 0)
        def _():
            for c in weight_copies(plan_ref[4, j], plan_ref[5, j], 1 - slot):
                c.start()

    @pl.when(j < nu_ref[0])
    def _():
        x = xs_ref[:, 0:D].astype(BF16)
        g_lo = xs_ref[:, D:D + 1]
        g_hi = xs_ref[:, D + 1:D + 2]

        def expert(k):
            a = _dot(x, wb1[slot, k].astype(BF16))
            hid = (a * jax.nn.sigmoid(a)) * _dot(x, wb3[slot, k].astype(BF16))
            return _dot(hid.astype(BF16), wb2[slot, k].astype(BF16))

        ys_ref[...] = g_lo * expert(0) + g_hi * expert(1)

    @pl.when(j >= nu_ref[0])
    def _():
        ys_ref[...] = jnp.zeros_like(ys_ref)


def _moe(plan, n_used, xs, w1, w3, w2, n_tiles):
    hbm = pl.BlockSpec(memory_space=pl.ANY)
    return pl.pallas_call(
        _moe_kernel,
        grid_spec=pltpu.PrefetchScalarGridSpec(
            num_scalar_prefetch=2,
            grid=(n_tiles,),
            in_specs=[pl.BlockSpec((TMM, XROW_W), lambda j, plan, nu: (j, 0)), hbm, hbm, hbm],
            out_specs=pl.BlockSpec((TMM, D), lambda j, plan, nu: (j, 0)),
            scratch_shapes=[
                pltpu.VMEM((2, 2, D, D_EXPERT), F32),
                pltpu.VMEM((2, 2, D, D_EXPERT), F32),
                pltpu.VMEM((2, 2, D_EXPERT, D), F32),
                pltpu.SemaphoreType.DMA((2, 6)),
            ],
        ),
        out_shape=jax.ShapeDtypeStruct((n_tiles * TMM, D), F32),
        compiler_params=_params(("arbitrary",)),
        name="moe_experts",
    )(plan, n_used, xs, w1, w3, w2)


def _gather_issue(pos_ref, ys_hbm, buf, sem, tile, slot, inline, part=(0, TM // SUBLANES)):
    base = tile * TM

    def issue(j, carry):
        for u in range(SUBLANES):
            p = pos_ref[base + j * SUBLANES + u]
            pltpu.make_async_copy(ys_hbm.at[p >> 3, pl.ds(p & (SUBLANES - 1), 1), :],
                                  buf.at[slot, j, pl.ds(u, 1), :], sem.at[slot]).start()
        return carry

    if inline:
        for j in range(*part):
            issue(j, 0)
    else:
        lax.fori_loop(0, TM // SUBLANES, issue, 0)


def _gather_wait(ys_hbm, buf, sem, slot):
    pltpu.make_async_copy(ys_hbm.at[pl.ds(0, TM // SUBLANES)], buf.at[slot], sem.at[slot]).wait()


def _gathered_rows(pos_ref, ys_hbm, buf, sem, request_next):
    i = pl.program_id(0)
    n_slots = buf.shape[0]
    slot = i % n_slots

    @pl.when(i == 0)
    def _():
        for t in range(1 if request_next else n_slots - 1):
            _gather_issue(pos_ref, ys_hbm, buf, sem, t, t, inline=False)

    if request_next:
        @pl.when(i + 1 < pl.num_programs(0))
        def _():
            _gather_issue(pos_ref, ys_hbm, buf, sem, i + 1, 1 - slot, inline=True)

    _gather_wait(ys_hbm, buf, sem, slot)
    return buf[slot].reshape(TM, D)


def _gather_request_ahead(pos_ref, ys_hbm, buf, sem):
    i = pl.program_id(0)
    n = pl.num_programs(0)
    ahead = buf.shape[0] - 1
    tile = i + ahead
    _gather_issue(pos_ref, ys_hbm, buf, sem, jnp.where(tile < n, tile, tile - n), tile % buf.shape[0], inline=True)


def _gather_drain(ys_hbm, buf, sem):
    i = pl.program_id(0)

    @pl.when(i + 1 == pl.num_programs(0))
    def _():
        for k in range(1, buf.shape[0]):
            _gather_wait(ys_hbm, buf, sem, (i + k) % buf.shape[0])


_GATHER_SCRATCH = [pltpu.VMEM((2, TM // SUBLANES, SUBLANES, D), F32), pltpu.SemaphoreType.DMA((2,))]


def _by_sublane_tile(a):
    return a.reshape(a.shape[0] // SUBLANES, SUBLANES, a.shape[1])


def _rope_store(x, tabs, out_ref):
    cr, sr, cc, sc = tabs
    for ch in range(RET_QK // LANES):
        xc = x[:, ch * LANES:(ch + 1) * LANES]
        xr = pltpu.roll(xc, LANES // 2, axis=1)
        y = xc * cr + xr * sr if ch % 2 == 0 else xc * cc + xr * sc
        out_ref[:, ch * LANES:(ch + 1) * LANES] = y.astype(BF16)


def _inproj1_kernel(pos_ref, h_ref, ys_hbm, mod0_ref, mod1_ref, g_ref, w_ref, rt_ref, ct_ref,
                    h2_ref, q_ref, k_ref, v_ref, gg_ref, buf, sem, xm_scr):
    f = _gathered_rows(pos_ref, ys_hbm, buf, sem, request_next=False)
    h2 = h_ref[...] + mod0_ref[0, 5:6, :] * f
    h2_ref[...] = h2
    xm_scr[...] = _rms_mod(h2, g_ref[...], mod1_ref[0, 0:1, :], mod1_ref[0, 1:2, :]).astype(BF16)
    _gather_request_ahead(pos_ref, ys_hbm, buf, sem)
    xm = xm_scr[...]

    off = (pl.program_id(0) % (SUBLANES // ROPE_TILE_ROWS)) * ROPE_TILE_ROWS

    def row_table(which):
        return jnp.concatenate([jnp.broadcast_to(rt_ref[which, pl.ds(off + j, 1), :], (GRID_W, LANES))
                                for j in range(ROPE_TILE_ROWS)], axis=0)

    tabs = (row_table(0), row_table(1), ct_ref[0, 0], ct_ref[0, 1])
    _rope_store(_dot(xm, w_ref[:, 0:RET_QK]), tabs, q_ref)
    _rope_store(_dot(xm, w_ref[:, RET_QK:2 * RET_QK]) * (RET_DK ** -0.5), tabs, k_ref)
    v_ref[...] = _dot(xm, w_ref[:, 2 * RET_QK:2 * RET_QK + RET_V]).astype(BF16)
    gg_ref[...] = _dot(xm, w_ref[:, 2 * RET_QK + RET_V:]).astype(BF16)
    _gather_drain(ys_hbm, buf, sem)


def _inproj1(pos, h1, ys, modv0, modv1, g, w_in, tabs):
    tok = lambda w: pl.BlockSpec((TM, w), lambda i, pos: (i, 0))
    res = lambda shape: pl.BlockSpec(shape, lambda i, pos: (0,) * len(shape), pipeline_mode=pl.Buffered(1))
    mod = pl.BlockSpec((1, 8, D), lambda i, pos: (i // NT_LAT, 0, 0))
    row_tab = pl.BlockSpec((2, SUBLANES, LANES), lambda i, pos: (0, i // (SUBLANES // ROPE_TILE_ROWS), 0))
    col_tab = pl.BlockSpec((1, 2, TM, LANES), lambda i, pos: (i // NT_LAT, 0, 0, 0))
    return pl.pallas_call(
        _inproj1_kernel,
        grid_spec=pltpu.PrefetchScalarGridSpec(
            num_scalar_prefetch=1,
            grid=(NT,),
            in_specs=[tok(D), pl.BlockSpec(memory_space=pl.ANY), mod, mod, res((1, D)), res((D, 2 * RET_QK + 2 * RET_V)),
                      row_tab, col_tab],
            out_specs=[tok(D), tok(RET_QK), tok(RET_QK), tok(RET_V), tok(RET_V)],
            scratch_shapes=[pltpu.VMEM((3, TM // SUBLANES, SUBLANES, D), F32), pltpu.SemaphoreType.DMA((3,)),
                            pltpu.VMEM((TM, D), BF16)],
        ),
        out_shape=[
            jax.ShapeDtypeStruct((T, D), F32),
            jax.ShapeDtypeStruct((T, RET_QK), BF16),
            jax.ShapeDtypeStruct((T, RET_QK), BF16),
            jax.ShapeDtypeStruct((T, RET_V), BF16),
            jax.ShapeDtypeStruct((T, RET_V), BF16),
        ],
        compiler_params=_params(("arbitrary",)),
        name="inproj1",
    )(pos, h1, _by_sublane_tile(ys), modv0, modv1, g, w_in, *tabs)


ROPE_TILE_ROWS = TM // GRID_W


def _rope_tables():
    n = LANES // 2
    inv_freq = ROPE_BASE ** (-np.arange(n, dtype=np.float64) / n)

    def lines(p):
        ang = p[:, None] * inv_freq[None, :]
        return np.stack([np.concatenate([np.cos(ang), np.cos(ang)], axis=1),
                         np.concatenate([-np.sin(ang), np.sin(ang)], axis=1)])

    ident = np.stack([np.ones((TM, LANES)), np.zeros((TM, LANES))])
    row_tab = np.concatenate([lines(np.arange(ROWS, dtype=np.float64)), ident[:, :SUBLANES]], axis=1)
    col_lat = lines(np.tile(np.arange(GRID_W, dtype=np.float64), ROPE_TILE_ROWS))
    col_tab = np.stack([col_lat, ident])
    return jnp.asarray(row_tab, dtype=F32), jnp.asarray(col_tab, dtype=F32)


RET_NCHUNK = T // RET_CHUNK
RET_CTX_CHUNKS = CTX // RET_CHUNK
RET_LAT_CHUNKS = SEQ // RET_CHUNK
RET_NHD = 2 * RET_HEADS


def _ret_kernel(dec_ref, qf_ref, kf_ref, vf_ref, qb_ref, kb_ref, vb_ref, of_ref, ob_ref,
                state, dm, qd, kd, cd):
    C = RET_CHUNK

    @pl.when(pl.program_id(0) == 0)
    def _():
        state[...] = jnp.zeros_like(state)
        n = lax.broadcasted_iota(jnp.int32, (C, C), 0).astype(F32)
        m = lax.broadcasted_iota(jnp.int32, (C, C), 1).astype(F32)
        for d in range(2):
            for h in range(RET_HEADS):
                idx = d * RET_HEADS + h
                lg = jnp.log1p(-jnp.exp(jnp.full((C, C), dec_ref[d, h], F32)))
                diff = n - m if d == 0 else m - n
                low = diff >= 0.0
                dm[idx] = jnp.where(low, jnp.exp(jnp.where(low, diff, 0.0) * lg), 0.0)
                qpow = n + 1.0 if d == 0 else C - n
                kpow = (C - 1.0) - n if d == 0 else n
                qcol = jnp.exp(qpow * lg)
                kcol = jnp.exp(kpow * lg)
                for c0 in range(0, RET_DV, C):
                    qd[idx, :, c0:c0 + C] = qcol
                for c0 in range(0, RET_DK, C):
                    kd[idx, :, c0:c0 + C] = kcol
                for c0 in range(0, RET_DV, LANES):
                    cd[idx, :, c0:c0 + LANES] = jnp.exp(C * lg[0:8, 0:LANES])

    for d, (q_ref, k_ref, v_ref, o_ref) in enumerate(((qf_ref, kf_ref, vf_ref, of_ref),
                                                       (qb_ref, kb_ref, vb_ref, ob_ref))):
        for h in range(RET_HEADS):
            idx = d * RET_HEADS + h
            q = q_ref[:, h * RET_DK:(h + 1) * RET_DK]
            k = k_ref[:, h * RET_DK:(h + 1) * RET_DK]
            v = v_ref[:, h * RET_DV:(h + 1) * RET_DV]
            st = state[idx]
            s = _dot_nt(q, k) * dm[idx]
            o = _dot(s.astype(BF16), v) + qd[idx] * _dot(q, st.astype(BF16))
            o_ref[:, h * RET_DV:(h + 1) * RET_DV] = o.astype(BF16)
            kdk = (k.astype(F32) * kd[idx]).astype(BF16)
            cdv = cd[idx, 0:1, :]
            state[idx] = st * cdv + _dot(kdk.T, v)


def _ret_scan(decay, q, k, v):
    def fwd(s):
        return jnp.where(s < RET_CTX_CHUNKS, RET_LAT_CHUNKS + s, s - RET_CTX_CHUNKS)

    def bwd(s):
        return RET_NCHUNK - 1 - s

    def spec(w, f):
        return pl.BlockSpec((RET_CHUNK, w), lambda s: (f(s), 0))

    return pl.pallas_call(
        _ret_kernel,
        grid=(RET_NCHUNK,),
        in_specs=[
            pl.BlockSpec(memory_space=pltpu.SMEM),
            spec(RET_QK, fwd), spec(RET_QK, fwd), spec(RET_V, fwd),
            spec(RET_QK, bwd), spec(RET_QK, bwd), spec(RET_V, bwd),
        ],
        out_specs=[spec(RET_V, fwd), spec(RET_V, bwd)],
        out_shape=[jax.ShapeDtypeStruct((T, RET_V), BF16), jax.ShapeDtypeStruct((T, RET_V), BF16)],
        scratch_shapes=[
            pltpu.VMEM((RET_NHD, RET_DK, RET_DV), F32),
            pltpu.VMEM((RET_NHD, RET_CHUNK, RET_CHUNK), F32),
            pltpu.VMEM((RET_NHD, RET_CHUNK, RET_DV), F32),
            pltpu.VMEM((RET_NHD, RET_CHUNK, RET_DK), F32),
            pltpu.VMEM((RET_NHD, 8, RET_DV), F32),
        ],
        compiler_params=_params(("arbitrary",)),
        name="retention_scan",
    )(decay, q, k, v, q, k, v)


def _outproj1_kernel(of_ref, ob_ref, gg_ref, h_ref, mod_ref, w_ref, *rest):
    def residual_tile():
        y = jnp.zeros((TM, D), F32)
        for hh in range(RET_HEADS):
            cols = slice(hh * RET_DV, (hh + 1) * RET_DV)
            o = of_ref[:, cols].astype(F32) + ob_ref[:, cols].astype(F32)
            on = o * lax.rsqrt(jnp.mean(o * o, axis=-1, keepdims=True) + RMS_EPS)
            g = gg_ref[:, cols].astype(F32)
            z = (g * jax.nn.sigmoid(g)) * on
            y = y + _dot(z.astype(BF16), w_ref[cols, :])
        return h_ref[...] + mod_ref[0, 2:3, :] * y

    _two_stage(residual_tile, mod_ref, *rest)


def _outproj1(o_f, o_b, gg, h, modv, w_out, g2, rw, rb, triu):
    e_in, e_out, e_shape = _epilogue_specs(NT_LAT)
    tok = lambda w: pl.BlockSpec((TM, w), lambda i: (jnp.minimum(i, NT_LAT - 1), 0))
    latent_mod = pl.BlockSpec((1, 8, D), lambda i: (0, 0, 0))
    return pl.pallas_call(
        _outproj1_kernel,
        grid=(NT_LAT + 1,),
        in_specs=[tok(RET_V), tok(RET_V), tok(RET_V), tok(D), latent_mod, _resident((RET_V, D))] + e_in,
        out_specs=e_out,
        out_shape=e_shape,
        scratch_shapes=_EPILOGUE_SCRATCH,
        compiler_params=_params(("arbitrary",)),
        name="outproj1",
    )(o_f, o_b, gg, h, modv, w_out, g2, rw, rb, triu)


def _final_kernel(pos_ref, h_ref, ys_hbm, mod_ref, g_ref, o_ref, buf, sem):
    f = _gathered_rows(pos_ref, ys_hbm, buf, sem, request_next=True)
    h = h_ref[...] + mod_ref[0, 5:6, :] * f
    o_ref[...] = h * lax.rsqrt(jnp.mean(h * h, axis=-1, keepdims=True) + RMS_EPS) * g_ref[...]


def _final(pos, h, ys, modv, g):
    tok = pl.BlockSpec((TM, D), lambda i, pos: (i, 0))
    return pl.pallas_call(
        _final_kernel,
        grid_spec=pltpu.PrefetchScalarGridSpec(
            num_scalar_prefetch=1,
            grid=(NT_LAT,),
            in_specs=[tok, pl.BlockSpec(memory_space=pl.ANY),
                      pl.BlockSpec((1, 8, D), lambda i, pos: (0, 0, 0)),
                      pl.BlockSpec((1, D), lambda i, pos: (0, 0))],
            out_specs=tok,
            scratch_shapes=_GATHER_SCRATCH,
        ),
        out_shape=jax.ShapeDtypeStruct((SEQ, D), F32),
        compiler_params=_params(("arbitrary",)),
        name="final_norm",
    )(pos, h, _by_sublane_tile(ys), modv, g)


def _dft_constants():
    c = np.arange(FNET_GD)
    ang = 2.0 * np.pi * np.outer(c, c) / FNET_GD
    d64 = np.zeros((FNET_W, 2 * FNET_W))
    for g in range(FNET_W // FNET_GD):
        sl = slice(g * FNET_GD, (g + 1) * FNET_GD)
        d64[sl, sl] = np.cos(ang)
        d64[sl, FNET_W + g * FNET_GD:FNET_W + (g + 1) * FNET_GD] = -np.sin(ang)
    k2 = np.arange(FFT_N2)[:, None]
    n2 = np.arange(FFT_N2)[None, :]
    ga = np.zeros((FFT_N1, 2 * FFT_N2, 2 * FFT_N2))
    for n1 in range(FFT_N1):
        th = 2.0 * np.pi * ((k2 * (n1 + FFT_N1 * n2)) % SEQ) / SEQ
        cs, sn = np.cos(th), np.sin(th)
        ga[n1] = np.block([[cs, sn], [-sn, cs]])
    k1 = np.arange(FFT_N1)
    ph = 2.0 * np.pi * np.outer(k1, k1) / FFT_N1
    csb = np.concatenate([np.cos(ph), np.sin(ph)], axis=1)
    p = np.arange(CTX)
    pc = 2.0 * np.pi * (np.outer(p, p) % CTX) / CTX
    csc = np.concatenate([np.cos(pc), np.sin(pc)], axis=1)
    as_bf16 = lambda a: jnp.asarray(a, dtype=F32).astype(BF16)
    return as_bf16(d64), as_bf16(ga), as_bf16(csb), as_bf16(csc)


def _mod_table(mod_l):
    t = mod_l.reshape(2, 6, D)
    return jnp.concatenate([t, jnp.zeros((2, 2, D), F32)], axis=1)


def kernel(x, c, ctx, c_ctx, ada_w, ada_b, norm_g, final_norm_g, mixab_w_in, mixab_w_out, na_rpb,
           ret_w_in, ret_w_out, ret_decay, router_w, router_b, moe_w1, moe_w3, moe_w2):
    d64, ga, csb, csc = _dft_constants()
    triu = jnp.asarray(np.triu(np.ones((TM, TM)), 1), dtype=BF16)
    rw = jnp.pad(router_w, ((0, 0), (0, LANES - N_EXPERTS)))
    rb = jnp.broadcast_to(router_b[:, None], (N_EXPERTS, TM))
    w1 = moe_w1.reshape(DEPTH * N_EXPERTS, D, D_EXPERT)
    w3 = moe_w3.reshape(DEPTH * N_EXPERTS, D, D_EXPERT)
    w2 = moe_w2.reshape(DEPTH * N_EXPERTS, D_EXPERT, D)

    mod = _mod_vectors(c, c_ctx, ada_w, ada_b)
    modv0, modv1 = _mod_table(mod[0]), _mod_table(mod[1])

    qkv, br, bi = _inproj0(x[0], ctx[0], modv0, norm_g[0, 0].reshape(1, D), mixab_w_in[0].astype(BF16), d64)
    a_lat = _fft_latent(br, bi, ga, csb)
    ao_ctx = _ctx_mixer(qkv, br, bi, csc)
    o_lat = _na_attention(qkv, _na_bias_rows(na_rpb[0]))
    h1, xrow, meta, cnt = _outproj0(a_lat, o_lat, ao_ctx, x[0], ctx[0], modv0, mixab_w_out[0].astype(BF16),
                                    norm_g[0, 1].reshape(1, D), rw, rb, triu)
    n_tiles0 = NT + N_REAL_BUCKETS
    pos0, fill0, plan0, n_used0 = _route_plan(meta, cnt, n_tiles0, 0)
    ys0 = _moe(plan0, n_used0, _dispatch(pos0, fill0, xrow, n_tiles0), w1, w3, w2, n_tiles0)

    h2, q, k, v, gg = _inproj1(pos0, h1, ys0, modv0, modv1, norm_g[1, 0].reshape(1, D),
                               ret_w_in[0].astype(BF16), _rope_tables())
    o_f, o_b = _ret_scan(ret_decay[0].astype(F32), q, k, v)
    h3, xrow1, meta1, cnt1 = _outproj1(o_f, o_b, gg, h2, modv1, ret_w_out[0].astype(BF16),
                                       norm_g[1, 1].reshape(1, D), rw, rb, triu)
    n_tiles1 = NT_LAT + N_REAL_BUCKETS
    pos1, fill1, plan1, n_used1 = _route_plan(meta1, cnt1, n_tiles1, 1)
    ys1 = _moe(plan1, n_used1, _dispatch(pos1, fill1, xrow1, n_tiles1), w1, w3, w2, n_tiles1)
    out = _final(pos1, h3, ys1, modv1, final_norm_g.reshape(1, D))
    return out[None]
```

```python
import functools
import math

import numpy as np
import jax
import jax.numpy as jnp
from jax import lax
from jax.experimental import pallas as pl
from jax.experimental.pallas import tpu as pltpu

F32 = jnp.float32
BF16 = jnp.bfloat16

D = 1024
SEQ = 16384
CTX = 256
T = SEQ + CTX
DEPTH = 2
GRID_W = 64
ROWS = SEQ // GRID_W
RMS_EPS = 1e-6
ROPE_BASE = 10000.0

FNET_W = 256
FNET_GD = 64
NA_HEADS = 12
NA_HD = 64
NA_W = NA_HEADS * NA_HD
NA_KH = 8
NA_KW = 16
AB_IN = FNET_W + 3 * NA_W
QKV_W = 3 * NA_W

RET_HEADS = 4
RET_DK = 256
RET_DV = 512
RET_QK = RET_HEADS * RET_DK
RET_V = RET_HEADS * RET_DV
RET_CHUNK = 256

N_EXPERTS = 16
EPG = 4
D_EXPERT = 512

LANES = 128
TM = 256
NT = T // TM
NT_LAT = SEQ // TM
TMM = TM
N_BUCKET_IDS = 64
N_REAL_BUCKETS = 24
XROW_W = D + LANES
FFT_N1 = 128
FFT_N2 = 128
NEG_BIG = -1e30
SUBLANES = 8

VMEM_LIMIT = 56 * 1024 * 1024


def _params(sem, vmem=VMEM_LIMIT):
    return pltpu.CompilerParams(dimension_semantics=sem, vmem_limit_bytes=vmem)


def _dot(a, b):
    return jnp.dot(a, b, preferred_element_type=F32)


def _dot_nt(a, b):
    return lax.dot_general(a, b, (((1,), (1,)), ((), ())), preferred_element_type=F32)


def _dot_tn(a, b):
    return lax.dot_general(a, b, (((0,), (0,)), ((), ())), preferred_element_type=F32)


def _rms_mod(x, g, sh, sc):
    ms = jnp.mean(x * x, axis=-1, keepdims=True)
    y = x * lax.rsqrt(ms + RMS_EPS) * g
    return y * (1.0 + sc) + sh


def _resident(shape):
    nd = len(shape)
    return pl.BlockSpec(shape, lambda *_: (0,) * nd, pipeline_mode=pl.Buffered(1))


MOD_TN = 768


def _mod_kernel(cs_ref, w_ref, b_ref, o_ref):
    cs = cs_ref[...]
    s = cs * jax.nn.sigmoid(cs)
    w = w_ref[0]
    r0 = jnp.sum(s[:, 0:1] * w, axis=0, keepdims=True)
    r1 = jnp.sum(s[:, 1:2] * w, axis=0, keepdims=True)
    o_ref[0] = jnp.concatenate([r0, r1], axis=0) + b_ref[0]


def _mod_vectors(c, c_ctx, ada_w, ada_b):
    cs = jnp.stack([c[0], c_ctx], axis=1)
    n = 6 * D
    return pl.pallas_call(
        _mod_kernel,
        grid=(DEPTH, n // MOD_TN),
        in_specs=[
            pl.BlockSpec((D, 2), lambda l, j: (0, 0)),
            pl.BlockSpec((1, D, MOD_TN), lambda l, j: (l, 0, j)),
            pl.BlockSpec((1, 1, MOD_TN), lambda l, j: (l, 0, j)),
        ],
        out_specs=pl.BlockSpec((1, 2, MOD_TN), lambda l, j: (l, 0, j)),
        out_shape=jax.ShapeDtypeStruct((DEPTH, 2, n), F32),
        compiler_params=_params(("arbitrary", "arbitrary")),
        name="mod_vectors",
    )(cs, ada_w, ada_b.reshape(DEPTH, 1, n))


def _mod_spec():
    return pl.BlockSpec((1, 8, D), lambda i, *_: (i // NT_LAT, 0, 0))


def _stream_tile(x_ref, ctx_ref):
    return jnp.where(pl.program_id(0) >= NT_LAT, ctx_ref[...], x_ref[...])


def _stream_specs():
    return [pl.BlockSpec((TM, D), lambda i: (jnp.minimum(i, NT_LAT - 1), 0)),
            pl.BlockSpec((CTX, D), lambda i: (0, 0))]


def _inproj0_kernel(x_ref, ctx_ref, mod_ref, g_ref, w_ref, d64_ref, qkv_ref, br_ref, bi_ref):
    h = _stream_tile(x_ref, ctx_ref)
    xm = _rms_mod(h, g_ref[...], mod_ref[0, 0:1, :], mod_ref[0, 1:2, :]).astype(BF16)
    p = _dot(xm, w_ref[...])
    qkv_ref[...] = p[:, FNET_W:].astype(BF16)
    b = _dot(p[:, :FNET_W].astype(BF16), d64_ref[...])
    br_ref[...] = b[:, :FNET_W]
    bi_ref[...] = b[:, FNET_W:]


def _inproj0(x, ctx, modv, g, w_in, d64):
    return pl.pallas_call(
        _inproj0_kernel,
        grid=(NT,),
        in_specs=_stream_specs() + [
            _mod_spec(),
            _resident((1, D)),
            _resident((D, AB_IN)),
            _resident((FNET_W, 2 * FNET_W)),
        ],
        out_specs=[
            pl.BlockSpec((TM, QKV_W), lambda i: (i, 0)),
            pl.BlockSpec((TM, FNET_W), lambda i: (i, 0)),
            pl.BlockSpec((TM, FNET_W), lambda i: (i, 0)),
        ],
        out_shape=[
            jax.ShapeDtypeStruct((T, QKV_W), BF16),
            jax.ShapeDtypeStruct((T, FNET_W), F32),
            jax.ShapeDtypeStruct((T, FNET_W), F32),
        ],
        compiler_params=_params(("arbitrary",)),
        name="inproj0",
    )(x, ctx, modv, g, w_in, d64)


FFT_A_NB = 8
FFT_B_TN = 4096


def _fft_a_kernel(br_ref, bi_ref, ga_ref, v_ref):
    for u in range(FFT_A_NB):
        z = (_dot(ga_ref[u, :, 0:FFT_N2], br_ref[:, u, :].astype(BF16))
             + _dot(ga_ref[u, :, FFT_N2:], bi_ref[:, u, :].astype(BF16)))
        v_ref[0, u] = z[:FFT_N2].astype(BF16)
        v_ref[1, u] = z[FFT_N2:].astype(BF16)


def _fft_b_kernel(cs_ref, v_ref, o_ref):
    o_ref[...] = _dot(cs_ref[...], v_ref[...]) * (1.0 / math.sqrt(SEQ * FNET_GD))


def _fft_latent(br, bi, ga, csb):
    br2 = br.reshape(T // FFT_N1, FFT_N1, FNET_W)
    bi2 = bi.reshape(T // FFT_N1, FFT_N1, FNET_W)
    v = pl.pallas_call(
        _fft_a_kernel,
        grid=(FFT_N1 // FFT_A_NB,),
        in_specs=[
            pl.BlockSpec((FFT_N2, FFT_A_NB, FNET_W), lambda j: (0, j, 0)),
            pl.BlockSpec((FFT_N2, FFT_A_NB, FNET_W), lambda j: (0, j, 0)),
            pl.BlockSpec((FFT_A_NB, 2 * FFT_N2, 2 * FFT_N2), lambda j: (j, 0, 0)),
        ],
        out_specs=pl.BlockSpec((2, FFT_A_NB, FFT_N2, FNET_W), lambda j: (0, j, 0, 0)),
        out_shape=jax.ShapeDtypeStruct((2, FFT_N1, FFT_N2, FNET_W), BF16),
        compiler_params=_params(("arbitrary",)),
        name="fft_stage_a",
    )(br2, bi2, ga)
    v2 = v.reshape(2 * FFT_N1, FFT_N2 * FNET_W)
    x = pl.pallas_call(
        _fft_b_kernel,
        grid=(FFT_N2 * FNET_W // FFT_B_TN,),
        in_specs=[
            _resident((FFT_N1, 2 * FFT_N1)),
            pl.BlockSpec((2 * FFT_N1, FFT_B_TN), lambda j: (0, j)),
        ],
        out_specs=pl.BlockSpec((FFT_N1, FFT_B_TN), lambda j: (0, j)),
        out_shape=jax.ShapeDtypeStruct((FFT_N1, FFT_N2 * FNET_W), F32),
        compiler_params=_params(("arbitrary",)),
        name="fft_stage_b",
    )(csb, v2)
    return x.reshape(SEQ, FNET_W)


def _head_pair_masks():
    lane = lax.broadcasted_iota(jnp.int32, (1, LANES), 1)
    return lane < NA_HD


def _ctx_kernel(qkv_ref, br_ref, bi_ref, cs_ref, o_ref):
    a = _dot(cs_ref[:, 0:CTX], br_ref[...].astype(BF16)) + _dot(cs_ref[:, CTX:], bi_ref[...].astype(BF16))
    o_ref[:, 0:FNET_W] = (a * (1.0 / math.sqrt(CTX * FNET_GD))).astype(BF16)
    m0 = _head_pair_masks()
    for hp in range(NA_HEADS // 2):
        q = qkv_ref[:, hp * LANES:(hp + 1) * LANES]
        k = qkv_ref[:, NA_W + hp * LANES:NA_W + (hp + 1) * LANES]
        v = qkv_ref[:, 2 * NA_W + hp * LANES:2 * NA_W + (hp + 1) * LANES]
        outs = []
        for a_ in range(2):
            qa = jnp.where(m0 if a_ == 0 else jnp.logical_not(m0), q, jnp.zeros_like(q))
            s = _dot_nt(qa, k) * (NA_HD ** -0.5)
            p = jnp.exp(s - jnp.max(s, axis=-1, keepdims=True))
            l = jnp.sum(p, axis=-1, keepdims=True)
            outs.append(_dot(p.astype(BF16), v) / l)
        o_ref[:, FNET_W + hp * LANES:FNET_W + (hp + 1) * LANES] = jnp.where(m0, outs[0], outs[1]).astype(BF16)


def _ctx_mixer(qkv, br, bi, csc):
    return pl.pallas_call(
        _ctx_kernel,
        grid=(1,),
        in_specs=[
            pl.BlockSpec((CTX, QKV_W), lambda i: (SEQ // CTX, 0)),
            pl.BlockSpec((CTX, FNET_W), lambda i: (SEQ // CTX, 0)),
            pl.BlockSpec((CTX, FNET_W), lambda i: (SEQ // CTX, 0)),
            pl.BlockSpec((CTX, 2 * CTX), lambda i: (0, 0)),
        ],
        out_specs=pl.BlockSpec((CTX, D), lambda i: (0, 0)),
        out_shape=jax.ShapeDtypeStruct((CTX, D), BF16),
        compiler_params=_params(("arbitrary",)),
        name="ctx_mixer",
    )(qkv, br, bi, csc)


NA_RB = 16
NA_WIN = NA_KH * GRID_W


NA_NK = NA_WIN + CTX
NA_SUB = 16


def _na_kernel(q_ref, k_ref, v_ref, kc_ref, vc_ref, bias_ref, o_ref, s_scr, p_scr, l_scr):
    b = pl.program_id(1)
    m0 = _head_pair_masks()
    nm0 = jnp.logical_not(m0)
    scale = NA_HD ** -0.5

    def window(i):
        r = b * NA_RB + i
        rs = jnp.clip(r - NA_KH // 2, 0, ROWS - NA_KH)
        return pl.multiple_of(rs * GRID_W, GRID_W), rs - r + (NA_KH - 1)

    def scores(i):
        start, _ = window(i)
        q = q_ref[i * GRID_W:(i + 1) * GRID_W, :] * scale
        qs = jnp.concatenate([jnp.where(m0, q, jnp.zeros_like(q)), jnp.where(nm0, q, jnp.zeros_like(q))], axis=0)
        s_scr[i, :, 0:NA_WIN] = _dot_nt(qs, k_ref[pl.ds(start, NA_WIN), :])
        s_scr[i, :, NA_WIN:NA_NK] = _dot_nt(qs, kc_ref[...])

    def softmax(i):
        _, e = window(i)
        for g in range(2 * GRID_W // NA_SUB):
            rows = slice(g * NA_SUB, (g + 1) * NA_SUB)
            a, c0 = divmod(g * NA_SUB, GRID_W)
            s1 = s_scr[i, rows, 0:NA_WIN] + bias_ref[a, e, c0:c0 + NA_SUB, :]
            s2 = s_scr[i, rows, NA_WIN:NA_NK]
            mx = jnp.maximum(jnp.max(s1, axis=-1, keepdims=True), jnp.max(s2, axis=-1, keepdims=True))
            p1 = jnp.exp(s1 - mx)
            p2 = jnp.exp(s2 - mx)
            l = jnp.sum(p1, axis=-1, keepdims=True) + jnp.sum(p2, axis=-1, keepdims=True)
            p_scr[i, rows, 0:NA_WIN] = p1.astype(BF16)
            p_scr[i, rows, NA_WIN:NA_NK] = p2.astype(BF16)
            l_scr[i, rows, :] = jnp.broadcast_to(l, (NA_SUB, LANES))

    def values(i):
        start, _ = window(i)
        o = _dot(p_scr[i, :, 0:NA_WIN], v_ref[pl.ds(start, NA_WIN), :]) + _dot(p_scr[i, :, NA_WIN:NA_NK], vc_ref[...])
        o = o / l_scr[i]
        o_ref[i * GRID_W:(i + 1) * GRID_W, :] = jnp.where(m0, o[0:GRID_W], o[GRID_W:]).astype(BF16)

    for i in range(NA_RB):
        scores(i)
    for i in range(NA_RB):
        softmax(i)
        values(i)


def _na_attention(qkv, bias):
    nq = NA_RB * GRID_W
    nhp = NA_HEADS // 2
    return pl.pallas_call(
        _na_kernel,
        grid=(nhp, ROWS // NA_RB),
        in_specs=[
            pl.BlockSpec((nq, LANES), lambda hp, b: (b, hp)),
            pl.BlockSpec((SEQ, LANES), lambda hp, b: (0, nhp + hp)),
            pl.BlockSpec((SEQ, LANES), lambda hp, b: (0, 2 * nhp + hp)),
            pl.BlockSpec((CTX, LANES), lambda hp, b: (SEQ // CTX, nhp + hp)),
            pl.BlockSpec((CTX, LANES), lambda hp, b: (SEQ // CTX, 2 * nhp + hp)),
            pl.BlockSpec((2, NA_KH, GRID_W, NA_WIN), lambda hp, b: (hp, 0, 0, 0)),
        ],
        out_specs=pl.BlockSpec((nq, LANES), lambda hp, b: (b, hp)),
        out_shape=jax.ShapeDtypeStruct((SEQ, NA_W), BF16),
        scratch_shapes=[
            pltpu.VMEM((NA_RB, 2 * GRID_W, NA_NK), F32),
            pltpu.VMEM((NA_RB, 2 * GRID_W, NA_NK), BF16),
            pltpu.VMEM((NA_RB, 2 * GRID_W, LANES), F32),
        ],
        compiler_params=_params(("arbitrary", "arbitrary")),
        name="na_attention",
    )(qkv, qkv, qkv, qkv, qkv, bias)


def _na_bias_rows(rpb):
    c = np.arange(GRID_W)[:, None]
    kc = np.arange(GRID_W)[None, :]
    ws = np.clip(c - NA_KW // 2, 0, GRID_W - NA_KW)
    col_ok = (kc >= ws) & (kc < ws + NA_KW)
    dc = np.clip(kc - c + NA_KW - 1, 0, 2 * NA_KW - 2)
    pick = (dc[None] == np.arange(2 * NA_KW - 1)[:, None, None]).astype(np.float32)
    full = jnp.einsum("hrd,dck->hrck", rpb.astype(F32), jnp.asarray(pick), precision=lax.Precision.HIGHEST)
    full = jnp.where(jnp.asarray(col_ok)[None, None], full, NEG_BIG)
    rows = [jnp.transpose(full[:, e:e + NA_KH], (0, 2, 1, 3)).reshape(NA_HEADS, GRID_W, NA_WIN)
            for e in range(NA_KH)]
    return jnp.stack(rows, axis=1)


def _route_stage(xm2, valid, rw_ref, rb_ref, triu_ref, run_ref, xrow_ref, meta_ref, cnt_ref):
    xrow_ref[:, 0:D] = xm2
    xh = xm2.astype(BF16)
    xl = (xm2 - xh.astype(F32)).astype(BF16)
    w = rw_ref[...]
    wh = w.astype(BF16)
    wl = (w - wh.astype(F32)).astype(BF16)
    logits = _dot(xh, wh) + (_dot(xl, wh) + _dot(xh, wl))
    scores = jax.nn.sigmoid(logits.T[0:N_EXPERTS, :])
    biased = scores + rb_ref[...]
    s = [scores[e:e + 1, :] for e in range(N_EXPERTS)]
    b = [biased[e:e + 1, :] for e in range(N_EXPERTS)]
    n_groups = N_EXPERTS // EPG
    top2, group_score = [], []
    for g in range(n_groups):
        m = b[EPG * g:EPG * (g + 1)]
        total = None
        for k in range(EPG):
            rank = None
            for j in range(EPG):
                if j != k:
                    beats = jnp.where((m[j] >= m[k]) if j < k else (m[j] > m[k]), 1.0, 0.0)
                    rank = beats if rank is None else rank + beats
            top2.append(rank < 2.0)
            term = jnp.where(top2[-1], m[k], 0.0)
            total = term if total is None else total + term
        group_score.append(total)
    best, g_sel = group_score[0], jnp.zeros((1, TM), jnp.int32)
    for g in range(1, n_groups):
        better = group_score[g] > best
        g_sel = jnp.where(better, g, g_sel)
        best = jnp.where(better, group_score[g], best)
    sel = [top2[e] & (g_sel == e // EPG) for e in range(N_EXPERTS)]
    w_sum = None
    for e in range(N_EXPERTS):
        term = jnp.where(sel[e], s[e], 0.0)
        w_sum = term if w_sum is None else w_sum + term
    zero = jnp.zeros((1, TM), F32)
    w_lo, w_hi, bid = zero, zero, zero
    for g in range(n_groups):
        seen = None
        for k in range(EPG):
            e = EPG * g + k
            if seen is None:
                w_lo = w_lo + jnp.where(sel[e], s[e], 0.0)
                seen = sel[e]
            else:
                w_lo = w_lo + jnp.where(sel[e] & jnp.logical_not(seen), s[e], 0.0)
                w_hi = w_hi + jnp.where(sel[e] & seen, s[e], 0.0)
                seen = seen | sel[e]
            bid = bid + jnp.where(sel[e], float((1 << k) + 8 * g), 0.0)
    w_lo = w_lo / w_sum
    w_hi = w_hi / w_sum
    bucket = lax.broadcasted_iota(jnp.int32, (N_BUCKET_IDS, TM), 0).astype(F32)
    onehot = bucket == bid
    ones = jnp.where(onehot & valid, 1.0, 0.0)
    earlier = _dot(ones.astype(BF16), triu_ref[...])
    run = run_ref[...]
    run_t = jnp.concatenate([run] * (TM // LANES), axis=1)
    rank_row = jnp.sum(jnp.where(onehot, earlier + run_t, 0.0), axis=0, keepdims=True)
    new_run = run + jnp.sum(ones, axis=1, keepdims=True)
    run_ref[...] = new_run
    cnt_ref[...] = new_run
    meta_rows = jnp.concatenate([w_lo, w_hi, bid, rank_row, jnp.zeros((SUBLANES - 4, TM), F32)], axis=0)
    meta_ref[...] = meta_rows
    padded = jnp.concatenate([meta_rows, jnp.zeros((LANES - SUBLANES, TM), F32)], axis=0)
    xrow_ref[:, D:XROW_W] = padded.T


def _epilogue_specs(ntiles):
    in_specs = [
        _resident((1, D)),
        _resident((D, LANES)),
        _resident((N_EXPERTS, TM)),
        _resident((TM, TM)),
    ]
    out_specs = [
        pl.BlockSpec((TM, D), lambda i: (jnp.minimum(i, ntiles - 1), 0)),
        pl.BlockSpec((TM, XROW_W), lambda i: (jnp.maximum(i - 1, 0), 0)),
        pl.BlockSpec((SUBLANES, TM), lambda i: (0, jnp.maximum(i - 1, 0))),
        pl.BlockSpec((N_BUCKET_IDS, LANES), lambda i: (0, 0)),
    ]
    n = ntiles * TM
    out_shape = [
        jax.ShapeDtypeStruct((n, D), F32),
        jax.ShapeDtypeStruct((n, XROW_W), F32),
        jax.ShapeDtypeStruct((SUBLANES, n), F32),
        jax.ShapeDtypeStruct((N_BUCKET_IDS, LANES), F32),
    ]
    return in_specs, out_specs, out_shape


_EPILOGUE_SCRATCH = [pltpu.VMEM((N_BUCKET_IDS, LANES), F32), pltpu.VMEM((2, TM, D), F32)]


def _two_stage(residual_tile, mod_ref, g2_ref, rw_ref, rb_ref, triu_ref,
               h1_ref, xrow_ref, meta_ref, cnt_ref, run_ref, stash):
    i = pl.program_id(0)

    @pl.when(i == 0)
    def _():
        run_ref[...] = jnp.zeros_like(run_ref)
        stash[1] = jnp.zeros((TM, D), F32)

    _route_stage(stash[1 - i % 2], i >= 1, rw_ref, rb_ref, triu_ref, run_ref, xrow_ref, meta_ref, cnt_ref)
    h1 = residual_tile()
    h1_ref[...] = h1
    stash[i % 2] = _rms_mod(h1, g2_ref[...], mod_ref[0, 3:4, :], mod_ref[0, 4:5, :])


def _outproj0_kernel(a_ref, o_ref, aoc_ref, x_ref, ctx_ref, mod_ref, w_ref, *rest):
    def residual_tile():
        is_ctx = pl.program_id(0) >= NT_LAT
        a = jnp.where(is_ctx, aoc_ref[:, 0:FNET_W], a_ref[...].astype(BF16))
        o = jnp.where(is_ctx, aoc_ref[:, FNET_W:], o_ref[...])
        y = _dot(a, w_ref[0:FNET_W, :]) + _dot(o, w_ref[FNET_W:, :])
        return _stream_tile(x_ref, ctx_ref) + mod_ref[0, 2:3, :] * y

    _two_stage(residual_tile, mod_ref, *rest)


def _outproj0(a_lat, o_lat, ao_ctx, x, ctx, modv, w_out, g2, rw, rb, triu):
    e_in, e_out, e_shape = _epilogue_specs(NT)
    lat = lambda w: pl.BlockSpec((TM, w), lambda i: (jnp.minimum(i, NT_LAT - 1), 0))
    return pl.pallas_call(
        _outproj0_kernel,
        grid=(NT + 1,),
        in_specs=[lat(FNET_W), lat(NA_W), pl.BlockSpec((CTX, D), lambda i: (0, 0))] + _stream_specs() + [
            _mod_spec(),
            _resident((D, D)),
        ] + e_in,
        out_specs=e_out,
        out_shape=e_shape,
        scratch_shapes=_EPILOGUE_SCRATCH,
        compiler_params=_params(("arbitrary",)),
        name="outproj0",
    )(a_lat, o_lat, ao_ctx, x, ctx, modv, w_out, g2, rw, rb, triu)


def _bucket_tables():
    lo = np.zeros(N_BUCKET_IDS, np.int32)
    hi = np.zeros(N_BUCKET_IDS, np.int32)
    for g in range(N_EXPERTS // EPG):
        for a in range(EPG):
            for b in range(a + 1, EPG):
                i = 16 * g + (1 << a) + (1 << b)
                lo[i] = EPG * g + a
                hi[i] = EPG * g + b
    return lo, hi


def _slot_kernel(meta_ref, starts_ref, o_ref):
    n = meta_ref.shape[1]
    bucket = lax.broadcasted_iota(jnp.int32, (N_BUCKET_IDS, n), 0).astype(F32)
    first = jnp.sum(jnp.where(bucket == meta_ref[2:3, :], starts_ref[:, 0:1], 0.0), axis=0, keepdims=True)
    o_ref[...] = jnp.broadcast_to((first + meta_ref[3:4, :]).astype(jnp.int32), (SUBLANES, n))


def _token_slots(meta, first_row):
    n = meta.shape[1]
    out = pl.pallas_call(
        _slot_kernel,
        grid=(1,),
        in_specs=[pl.BlockSpec((SUBLANES, n), lambda i: (0, 0)),
                  pl.BlockSpec((N_BUCKET_IDS, LANES), lambda i: (0, 0))],
        out_specs=pl.BlockSpec((SUBLANES, n), lambda i: (0, 0)),
        out_shape=jax.ShapeDtypeStruct((SUBLANES, n), jnp.int32),
        compiler_params=_params(("arbitrary",)),
        name="token_slots",
    )(meta, first_row)
    return out[0]


def _route_plan(meta, cnt, n_tiles, layer):
    counts = cnt[:, 0].astype(jnp.int32)
    nt_b = (counts + TMM - 1) // TMM
    ends = jnp.cumsum(nt_b)
    starts = ends - nt_b
    first_row = jnp.broadcast_to((starts * TMM).astype(F32)[:, None], (N_BUCKET_IDS, LANES))
    pos = _token_slots(meta, first_row)
    n_used = ends[-1]
    tile = jnp.minimum(jnp.arange(n_tiles, dtype=jnp.int32), n_used - 1)
    tile_b = jnp.sum((ends[None, :] <= tile[:, None]).astype(jnp.int32), axis=1)
    lo, hi = (jnp.asarray(t) + layer * N_EXPERTS for t in _bucket_tables())
    ids = jnp.arange(n_tiles, dtype=jnp.int32)
    first = (tile_b != jnp.concatenate([jnp.full((1,), -1, jnp.int32), tile_b[:-1]])) & (ids < n_used)
    wslot = (jnp.cumsum(first.astype(jnp.int32)) - 1) % 2
    next_tile = ends[tile_b]
    has_next = next_tile < n_used
    next_b = tile_b[jnp.minimum(next_tile, n_tiles - 1)]
    plan = jnp.stack([lo[tile_b], hi[tile_b], first.astype(jnp.int32), wslot,
                      jnp.where(has_next, lo[next_b], -1), jnp.where(has_next, hi[next_b], -1)])
    fill = ((ids >= n_used) | (ids == next_tile - 1)).astype(jnp.int32)
    return pos, fill, plan.astype(jnp.int32), n_used.reshape(1).astype(jnp.int32)


def _dispatch_kernel(pos_ref, fill_ref, x_ref, xs_ref, sem, zeros, zsem):
    base = pl.program_id(0) * TM
    tile_rows = TMM // SUBLANES

    @pl.when(pl.program_id(0) == 0)
    def _():
        zeros[...] = jnp.zeros_like(zeros)

        def fill(t):
            return pltpu.make_async_copy(zeros, xs_ref.at[pl.ds(t * tile_rows, tile_rows)], zsem)

        def start(t, carry):
            @pl.when(fill_ref[t] == 1)
            def _():
                fill(t).start()
            return carry

        def wait(t, carry):
            @pl.when(fill_ref[t] == 1)
            def _():
                fill(t).wait()
            return carry

        lax.fori_loop(0, fill_ref.shape[0], start, 0)
        lax.fori_loop(0, fill_ref.shape[0], wait, 0)

    for j in range(TM // SUBLANES):
        for u in range(SUBLANES):
            p = pos_ref[base + j * SUBLANES + u]
            pltpu.make_async_copy(x_ref.at[j, pl.ds(u, 1), :],
                                  xs_ref.at[p >> 3, pl.ds(p & (SUBLANES - 1), 1), :], sem).start()
    pltpu.make_async_copy(x_ref, xs_ref.at[pl.ds(0, TM // SUBLANES)], sem).wait()


def _dispatch(pos, fill, xrow, n_tiles):
    ntok = xrow.shape[0]
    xs = pl.pallas_call(
        _dispatch_kernel,
        grid_spec=pltpu.PrefetchScalarGridSpec(
            num_scalar_prefetch=2,
            grid=(ntok // TM,),
            in_specs=[pl.BlockSpec((TM // SUBLANES, SUBLANES, XROW_W), lambda i, pos, fill: (i, 0, 0))],
            out_specs=pl.BlockSpec(memory_space=pl.ANY),
            scratch_shapes=[
                pltpu.SemaphoreType.DMA(()),
                pltpu.VMEM((TMM // SUBLANES, SUBLANES, XROW_W), F32),
                pltpu.SemaphoreType.DMA(()),
            ],
        ),
        out_shape=jax.ShapeDtypeStruct((n_tiles * TMM // SUBLANES, SUBLANES, XROW_W), F32),
        compiler_params=_params(("arbitrary",)),
        name="moe_dispatch",
    )(pos, fill, _by_sublane_tile(xrow))
    return xs.reshape(n_tiles * TMM, XROW_W)


def _moe_kernel(plan_ref, nu_ref, xs_ref, w1_hbm, w3_hbm, w2_hbm, ys_ref, wb1, wb3, wb2, sem):
    j = pl.program_id(0)
    slot = plan_ref[3, j]

    def weight_copies(e_lo, e_hi, s):
        out = []
        for k, e in enumerate((e_lo, e_hi)):
            out.append(pltpu.make_async_copy(w1_hbm.at[e], wb1.at[s, k], sem.at[s, 3 * k]))
            out.append(pltpu.make_async_copy(w3_hbm.at[e], wb3.at[s, k], sem.at[s, 3 * k + 1]))
            out.append(pltpu.make_async_copy(w2_hbm.at[e], wb2.at[s, k], sem.at[s, 3 * k + 2]))
        return out

    @pl.when(j == 0)
    def _():
        for c in weight_copies(plan_ref[0, 0], plan_ref[1, 0], 0):
            c.start()

    @pl.when(plan_ref[2, j] == 1)
    def _():
        for c in weight_copies(plan_ref[0, j], plan_ref[1, j], slot):
            c.wait()

        @pl.when(plan_ref[4, j] >= 0)
        def _():
            for c in weight_copies(plan_ref[4, j], plan_ref[5, j], 1 - slot):
                c.start()

    @pl.when(j < nu_ref[0])
    def _():
        x = xs_ref[:, 0:D].astype(BF16)
        g_lo = xs_ref[:, D:D + 1]
        g_hi = xs_ref[:, D + 1:D + 2]

        def expert(k):
            a = _dot(x, wb1[slot, k].astype(BF16))
            hid = (a * jax.nn.sigmoid(a)) * _dot(x, wb3[slot, k].astype(BF16))
            return _dot(hid.astype(BF16), wb2[slot, k].astype(BF16))

        ys_ref[...] = g_lo * expert(0) + g_hi * expert(1)

    @pl.when(j >= nu_ref[0])
    def _():
        ys_ref[...] = jnp.zeros_like(ys_ref)


def _moe(plan, n_used, xs, w1, w3, w2, n_tiles):
    hbm = pl.BlockSpec(memory_space=pl.ANY)
    return pl.pallas_call(
        _moe_kernel,
        grid_spec=pltpu.PrefetchScalarGridSpec(
            num_scalar_prefetch=2,
            grid=(n_tiles,),
            in_specs=[pl.BlockSpec((TMM, XROW_W), lambda j, plan, nu: (j, 0)), hbm, hbm, hbm],
            out_specs=pl.BlockSpec((TMM, D), lambda j, plan, nu: (j, 0)),
            scratch_shapes=[
                pltpu.VMEM((2, 2, D, D_EXPERT), F32),
                pltpu.VMEM((2, 2, D, D_EXPERT), F32),
                pltpu.VMEM((2, 2, D_EXPERT, D), F32),
                pltpu.SemaphoreType.DMA((2, 6)),
            ],
        ),
        out_shape=jax.ShapeDtypeStruct((n_tiles * TMM, D), F32),
        compiler_params=_params(("arbitrary",)),
        name="moe_experts",
    )(plan, n_used, xs, w1, w3, w2)


def _gather_issue(pos_ref, ys_hbm, buf, sem, tile, slot, inline, part=(0, TM // SUBLANES)):
    base = tile * TM

    def issue(j, carry):
        for u in range(SUBLANES):
            p = pos_ref[base + j * SUBLANES + u]
            pltpu.make_async_copy(ys_hbm.at[p >> 3, pl.ds(p & (SUBLANES - 1), 1), :],
                                  buf.at[slot, j, pl.ds(u, 1), :], sem.at[slot]).start()
        return carry

    if inline:
        for j in range(*part):
            issue(j, 0)
    else:
        lax.fori_loop(0, TM // SUBLANES, issue, 0)


def _gather_wait(ys_hbm, buf, sem, slot):
    pltpu.make_async_copy(ys_hbm.at[pl.ds(0, TM // SUBLANES)], buf.at[slot], sem.at[slot]).wait()


def _gathered_rows(pos_ref, ys_hbm, buf, sem, request_next):
    i = pl.program_id(0)
    n_slots = buf.shape[0]
    slot = i % n_slots

    @pl.when(i == 0)
    def _():
        for t in range(1 if request_next else n_slots - 1):
            _gather_issue(pos_ref, ys_hbm, buf, sem, t, t, inline=False)

    if request_next:
        @pl.when(i + 1 < pl.num_programs(0))
        def _():
            _gather_issue(pos_ref, ys_hbm, buf, sem, i + 1, 1 - slot, inline=True)

    _gather_wait(ys_hbm, buf, sem, slot)
    return buf[slot].reshape(TM, D)


def _gather_request_ahead(pos_ref, ys_hbm, buf, sem):
    i = pl.program_id(0)
    n = pl.num_programs(0)
    ahead = buf.shape[0] - 1
    tile = i + ahead
    _gather_issue(pos_ref, ys_hbm, buf, sem, jnp.where(tile < n, tile, tile - n), tile % buf.shape[0], inline=True)


def _gather_drain(ys_hbm, buf, sem):
    i = pl.program_id(0)

    @pl.when(i + 1 == pl.num_programs(0))
    def _():
        for k in range(1, buf.shape[0]):
            _gather_wait(ys_hbm, buf, sem, (i + k) % buf.shape[0])


_GATHER_SCRATCH = [pltpu.VMEM((2, TM // SUBLANES, SUBLANES, D), F32), pltpu.SemaphoreType.DMA((2,))]


def _by_sublane_tile(a):
    return a.reshape(a.shape[0] // SUBLANES, SUBLANES, a.shape[1])


def _rope_store(x, tabs, out_ref):
    cr, sr, cc, sc = tabs
    for ch in range(RET_QK // LANES):
        xc = x[:, ch * LANES:(ch + 1) * LANES]
        xr = pltpu.roll(xc, LANES // 2, axis=1)
        y = xc * cr + xr * sr if ch % 2 == 0 else xc * cc + xr * sc
        out_ref[:, ch * LANES:(ch + 1) * LANES] = y.astype(BF16)


def _inproj1_kernel(pos_ref, h_ref, ys_hbm, mod0_ref, mod1_ref, g_ref, w_ref, rt_ref, ct_ref,
                    h2_ref, q_ref, k_ref, v_ref, gg_ref, buf, sem, xm_scr):
    f = _gathered_rows(pos_ref, ys_hbm, buf, sem, request_next=False)
    h2 = h_ref[...] + mod0_ref[0, 5:6, :] * f
    h2_ref[...] = h2
    xm_scr[...] = _rms_mod(h2, g_ref[...], mod1_ref[0, 0:1, :], mod1_ref[0, 1:2, :]).astype(BF16)
    _gather_request_ahead(pos_ref, ys_hbm, buf, sem)
    xm = xm_scr[...]

    off = (pl.program_id(0) % (SUBLANES // ROPE_TILE_ROWS)) * ROPE_TILE_ROWS

    def row_table(which):
        return jnp.concatenate([jnp.broadcast_to(rt_ref[which, pl.ds(off + j, 1), :], (GRID_W, LANES))
                                for j in range(ROPE_TILE_ROWS)], axis=0)

    tabs = (row_table(0), row_table(1), ct_ref[0, 0], ct_ref[0, 1])
    _rope_store(_dot(xm, w_ref[:, 0:RET_QK]), tabs, q_ref)
    _rope_store(_dot(xm, w_ref[:, RET_QK:2 * RET_QK]) * (RET_DK ** -0.5), tabs, k_ref)
    v_ref[...] = _dot(xm, w_ref[:, 2 * RET_QK:2 * RET_QK + RET_V]).astype(BF16)
    gg_ref[...] = _dot(xm, w_ref[:, 2 * RET_QK + RET_V:]).astype(BF16)
    _gather_drain(ys_hbm, buf, sem)


def _inproj1(pos, h1, ys, modv0, modv1, g, w_in, tabs):
    tok = lambda w: pl.BlockSpec((TM, w), lambda i, pos: (i, 0))
    res = lambda shape: pl.BlockSpec(shape, lambda i, pos: (0,) * len(shape), pipeline_mode=pl.Buffered(1))
    mod = pl.BlockSpec((1, 8, D), lambda i, pos: (i // NT_LAT, 0, 0))
    row_tab = pl.BlockSpec((2, SUBLANES, LANES), lambda i, pos: (0, i // (SUBLANES // ROPE_TILE_ROWS), 0))
    col_tab = pl.BlockSpec((1, 2, TM, LANES), lambda i, pos: (i // NT_LAT, 0, 0, 0))
    return pl.pallas_call(
        _inproj1_kernel,
        grid_spec=pltpu.PrefetchScalarGridSpec(
            num_scalar_prefetch=1,
            grid=(NT,),
            in_specs=[tok(D), pl.BlockSpec(memory_space=pl.ANY), mod, mod, res((1, D)), res((D, 2 * RET_QK + 2 * RET_V)),
                      row_tab, col_tab],
            out_specs=[tok(D), tok(RET_QK), tok(RET_QK), tok(RET_V), tok(RET_V)],
            scratch_shapes=[pltpu.VMEM((3, TM // SUBLANES, SUBLANES, D), F32), pltpu.SemaphoreType.DMA((3,)),
                            pltpu.VMEM((TM, D), BF16)],
        ),
        out_shape=[
            jax.ShapeDtypeStruct((T, D), F32),
            jax.ShapeDtypeStruct((T, RET_QK), BF16),
            jax.ShapeDtypeStruct((T, RET_QK), BF16),
            jax.ShapeDtypeStruct((T, RET_V), BF16),
            jax.ShapeDtypeStruct((T, RET_V), BF16),
        ],
        compiler_params=_params(("arbitrary",)),
        name="inproj1",
    )(pos, h1, _by_sublane_tile(ys), modv0, modv1, g, w_in, *tabs)


ROPE_TILE_ROWS = TM // GRID_W


def _rope_tables():
    n = LANES // 2
    inv_freq = ROPE_BASE ** (-np.arange(n, dtype=np.float64) / n)

    def lines(p):
        ang = p[:, None] * inv_freq[None, :]
        return np.stack([np.concatenate([np.cos(ang), np.cos(ang)], axis=1),
                         np.concatenate([-np.sin(ang), np.sin(ang)], axis=1)])

    ident = np.stack([np.ones((TM, LANES)), np.zeros((TM, LANES))])
    row_tab = np.concatenate([lines(np.arange(ROWS, dtype=np.float64)), ident[:, :SUBLANES]], axis=1)
    col_lat = lines(np.tile(np.arange(GRID_W, dtype=np.float64), ROPE_TILE_ROWS))
    col_tab = np.stack([col_lat, ident])
    return jnp.asarray(row_tab, dtype=F32), jnp.asarray(col_tab, dtype=F32)


RET_NCHUNK = T // RET_CHUNK
RET_CTX_CHUNKS = CTX // RET_CHUNK
RET_LAT_CHUNKS = SEQ // RET_CHUNK
RET_NHD = 2 * RET_HEADS


def _ret_kernel(dec_ref, qf_ref, kf_ref, vf_ref, qb_ref, kb_ref, vb_ref, of_ref, ob_ref,
                state, dm, qd, kd, cd):
    C = RET_CHUNK

    @pl.when(pl.program_id(0) == 0)
    def _():
        state[...] = jnp.zeros_like(state)
        n = lax.broadcasted_iota(jnp.int32, (C, C), 0).astype(F32)
        m = lax.broadcasted_iota(jnp.int32, (C, C), 1).astype(F32)
        for d in range(2):
            for h in range(RET_HEADS):
                idx = d * RET_HEADS + h
                lg = jnp.log1p(-jnp.exp(jnp.full((C, C), dec_ref[d, h], F32)))
                diff = n - m if d == 0 else m - n
                low = diff >= 0.0
                dm[idx] = jnp.where(low, jnp.exp(jnp.where(low, diff, 0.0) * lg), 0.0)
                qpow = n + 1.0 if d == 0 else C - n
                kpow = (C - 1.0) - n if d == 0 else n
                qcol = jnp.exp(qpow * lg)
                kcol = jnp.exp(kpow * lg)
                for c0 in range(0, RET_DV, C):
                    qd[idx, :, c0:c0 + C] = qcol
                for c0 in range(0, RET_DK, C):
                    kd[idx, :, c0:c0 + C] = kcol
                for c0 in range(0, RET_DV, LANES):
                    cd[idx, :, c0:c0 + LANES] = jnp.exp(C * lg[0:8, 0:LANES])

    for d, (q_ref, k_ref, v_ref, o_ref) in enumerate(((qf_ref, kf_ref, vf_ref, of_ref),
                                                       (qb_ref, kb_ref, vb_ref, ob_ref))):
        for h in range(RET_HEADS):
            idx = d * RET_HEADS + h
            q = q_ref[:, h * RET_DK:(h + 1) * RET_DK]
            k = k_ref[:, h * RET_DK:(h + 1) * RET_DK]
            v = v_ref[:, h * RET_DV:(h + 1) * RET_DV]
            st = state[idx]
            s = _dot_nt(q, k) * dm[idx]
            o = _dot(s.astype(BF16), v) + qd[idx] * _dot(q, st.astype(BF16))
            o_ref[:, h * RET_DV:(h + 1) * RET_DV] = o.astype(BF16)
            kdk = (k.astype(F32) * kd[idx]).astype(BF16)
            cdv = cd[idx, 0:1, :]
            state[idx] = st * cdv + _dot(kdk.T, v)


def _ret_scan(decay, q, k, v):
    def fwd(s):
        return jnp.where(s < RET_CTX_CHUNKS, RET_LAT_CHUNKS + s, s - RET_CTX_CHUNKS)

    def bwd(s):
        return RET_NCHUNK - 1 - s

    def spec(w, f):
        return pl.BlockSpec((RET_CHUNK, w), lambda s: (f(s), 0))

    return pl.pallas_call(
        _ret_kernel,
        grid=(RET_NCHUNK,),
        in_specs=[
            pl.BlockSpec(memory_space=pltpu.SMEM),
            spec(RET_QK, fwd), spec(RET_QK, fwd), spec(RET_V, fwd),
            spec(RET_QK, bwd), spec(RET_QK, bwd), spec(RET_V, bwd),
        ],
        out_specs=[spec(RET_V, fwd), spec(RET_V, bwd)],
        out_shape=[jax.ShapeDtypeStruct((T, RET_V), BF16), jax.ShapeDtypeStruct((T, RET_V), BF16)],
        scratch_shapes=[
            pltpu.VMEM((RET_NHD, RET_DK, RET_DV), F32),
            pltpu.VMEM((RET_NHD, RET_CHUNK, RET_CHUNK), F32),
            pltpu.VMEM((RET_NHD, RET_CHUNK, RET_DV), F32),
            pltpu.VMEM((RET_NHD, RET_CHUNK, RET_DK), F32),
            pltpu.VMEM((RET_NHD, 8, RET_DV), F32),
        ],
        compiler_params=_params(("arbitrary",)),
        name="retention_scan",
    )(decay, q, k, v, q, k, v)


def _outproj1_kernel(of_ref, ob_ref, gg_ref, h_ref, mod_ref, w_ref, *rest):
    def residual_tile():
        y = jnp.zeros((TM, D), F32)
        for hh in range(RET_HEADS):
            cols = slice(hh * RET_DV, (hh + 1) * RET_DV)
            o = of_ref[:, cols].astype(F32) + ob_ref[:, cols].astype(F32)
            on = o * lax.rsqrt(jnp.mean(o * o, axis=-1, keepdims=True) + RMS_EPS)
            g = gg_ref[:, cols].astype(F32)
            z = (g * jax.nn.sigmoid(g)) * on
            y = y + _dot(z.astype(BF16), w_ref[cols, :])
        return h_ref[...] + mod_ref[0, 2:3, :] * y

    _two_stage(residual_tile, mod_ref, *rest)


def _outproj1(o_f, o_b, gg, h, modv, w_out, g2, rw, rb, triu):
    e_in, e_out, e_shape = _epilogue_specs(NT_LAT)
    tok = lambda w: pl.BlockSpec((TM, w), lambda i: (jnp.minimum(i, NT_LAT - 1), 0))
    latent_mod = pl.BlockSpec((1, 8, D), lambda i: (0, 0, 0))
    return pl.pallas_call(
        _outproj1_kernel,
        grid=(NT_LAT + 1,),
        in_specs=[tok(RET_V), tok(RET_V), tok(RET_V), tok(D), latent_mod, _resident((RET_V, D))] + e_in,
        out_specs=e_out,
        out_shape=e_shape,
        scratch_shapes=_EPILOGUE_SCRATCH,
        compiler_params=_params(("arbitrary",)),
        name="outproj1",
    )(o_f, o_b, gg, h, modv, w_out, g2, rw, rb, triu)


def _final_kernel(pos_ref, h_ref, ys_hbm, mod_ref, g_ref, o_ref, buf, sem):
    f = _gathered_rows(pos_ref, ys_hbm, buf, sem, request_next=True)
    h = h_ref[...] + mod_ref[0, 5:6, :] * f
    o_ref[...] = h * lax.rsqrt(jnp.mean(h * h, axis=-1, keepdims=True) + RMS_EPS) * g_ref[...]


def _final(pos, h, ys, modv, g):
    tok = pl.BlockSpec((TM, D), lambda i, pos: (i, 0))
    return pl.pallas_call(
        _final_kernel,
        grid_spec=pltpu.PrefetchScalarGridSpec(
            num_scalar_prefetch=1,
            grid=(NT_LAT,),
            in_specs=[tok, pl.BlockSpec(memory_space=pl.ANY),
                      pl.BlockSpec((1, 8, D), lambda i, pos: (0, 0, 0)),
                      pl.BlockSpec((1, D), lambda i, pos: (0, 0))],
            out_specs=tok,
            scratch_shapes=_GATHER_SCRATCH,
        ),
        out_shape=jax.ShapeDtypeStruct((SEQ, D), F32),
        compiler_params=_params(("arbitrary",)),
        name="final_norm",
    )(pos, h, _by_sublane_tile(ys), modv, g)


def _dft_constants():
    c = np.arange(FNET_GD)
    ang = 2.0 * np.pi * np.outer(c, c) / FNET_GD
    d64 = np.zeros((FNET_W, 2 * FNET_W))
    for g in range(FNET_W // FNET_GD):
        sl = slice(g * FNET_GD, (g + 1) * FNET_GD)
        d64[sl, sl] = np.cos(ang)
        d64[sl, FNET_W + g * FNET_GD:FNET_W + (g + 1) * FNET_GD] = -np.sin(ang)
    k2 = np.arange(FFT_N2)[:, None]
    n2 = np.arange(FFT_N2)[None, :]
    ga = np.zeros((FFT_N1, 2 * FFT_N2, 2 * FFT_N2))
    for n1 in range(FFT_N1):
        th = 2.0 * np.pi * ((k2 * (n1 + FFT_N1 * n2)) % SEQ) / SEQ
        cs, sn = np.cos(th), np.sin(th)
        ga[n1] = np.block([[cs, sn], [-sn, cs]])
    k1 = np.arange(FFT_N1)
    ph = 2.0 * np.pi * np.outer(k1, k1) / FFT_N1
    csb = np.concatenate([np.cos(ph), np.sin(ph)], axis=1)
    p = np.arange(CTX)
    pc = 2.0 * np.pi * (np.outer(p, p) % CTX) / CTX
    csc = np.concatenate([np.cos(pc), np.sin(pc)], axis=1)
    as_bf16 = lambda a: jnp.asarray(a, dtype=F32).astype(BF16)
    return as_bf16(d64), as_bf16(ga), as_bf16(csb), as_bf16(csc)


def _mod_table(mod_l):
    t = mod_l.reshape(2, 6, D)
    return jnp.concatenate([t, jnp.zeros((2, 2, D), F32)], axis=1)


def kernel(x, c, ctx, c_ctx, ada_w, ada_b, norm_g, final_norm_g, mixab_w_in, mixab_w_out, na_rpb,
           ret_w_in, ret_w_out, ret_decay, router_w, router_b, moe_w1, moe_w3, moe_w2):
    d64, ga, csb, csc = _dft_constants()
    triu = jnp.asarray(np.triu(np.ones((TM, TM)), 1), dtype=BF16)
    rw = jnp.pad(router_w, ((0, 0), (0, LANES - N_EXPERTS)))
    rb = jnp.broadcast_to(router_b[:, None], (N_EXPERTS, TM))
    w1 = moe_w1.reshape(DEPTH * N_EXPERTS, D, D_EXPERT)
    w3 = moe_w3.reshape(DEPTH * N_EXPERTS, D, D_EXPERT)
    w2 = moe_w2.reshape(DEPTH * N_EXPERTS, D_EXPERT, D)

    mod = _mod_vectors(c, c_ctx, ada_w, ada_b)
    modv0, modv1 = _mod_table(mod[0]), _mod_table(mod[1])

    qkv, br, bi = _inproj0(x[0], ctx[0], modv0, norm_g[0, 0].reshape(1, D), mixab_w_in[0].astype(BF16), d64)
    a_lat = _fft_latent(br, bi, ga, csb)
    ao_ctx = _ctx_mixer(qkv, br, bi, csc)
    o_lat = _na_attention(qkv, _na_bias_rows(na_rpb[0]))
    h1, xrow, meta, cnt = _outproj0(a_lat, o_lat, ao_ctx, x[0], ctx[0], modv0, mixab_w_out[0].astype(BF16),
                                    norm_g[0, 1].reshape(1, D), rw, rb, triu)
    n_tiles0 = NT + N_REAL_BUCKETS
    pos0, fill0, plan0, n_used0 = _route_plan(meta, cnt, n_tiles0, 0)
    ys0 = _moe(plan0, n_used0, _dispatch(pos0, fill0, xrow, n_tiles0), w1, w3, w2, n_tiles0)

    h2, q, k, v, gg = _inproj1(pos0, h1, ys0, modv0, modv1, norm_g[1, 0].reshape(1, D),
                               ret_w_in[0].astype(BF16), _rope_tables())
    o_f, o_b = _ret_scan(ret_decay[0].astype(F32), q, k, v)
    h3, xrow1, meta1, cnt1 = _outproj1(o_f, o_b, gg, h2, modv1, ret_w_out[0].astype(BF16),
                                       norm_g[1, 1].reshape(1, D), rw, rb, triu)
    n_tiles1 = NT_LAT + N_REAL_BUCKETS
    pos1, fill1, plan1, n_used1 = _route_plan(meta1, cnt1, n_tiles1, 1)
    ys1 = _moe(plan1, n_used1, _dispatch(pos1, fill1, xrow1, n_tiles1), w1, w3, w2, n_tiles1)
    out = _final(pos1, h3, ys1, modv1, final_norm_g.reshape(1, D))
    return out[None]
```

```python
import functools
import math

import numpy as np
import jax
import jax.numpy as jnp
from jax import lax
from jax.experimental import pallas as pl
from jax.experimental.pallas import tpu as pltpu

F32 = jnp.float32
BF16 = jnp.bfloat16

D = 1024
SEQ = 16384
CTX = 256
T = SEQ + CTX
DEPTH = 2
GRID_W = 64
ROWS = SEQ // GRID_W
RMS_EPS = 1e-6
ROPE_BASE = 10000.0

FNET_W = 256
FNET_GD = 64
NA_HEADS = 12
NA_HD = 64
NA_W = NA_HEADS * NA_HD
NA_KH = 8
NA_KW = 16
AB_IN = FNET_W + 3 * NA_W
QKV_W = 3 * NA_W

RET_HEADS = 4
RET_DK = 256
RET_DV = 512
RET_QK = RET_HEADS * RET_DK
RET_V = RET_HEADS * RET_DV
RET_CHUNK = 256

N_EXPERTS = 16
EPG = 4
D_EXPERT = 512

LANES = 128
TM = 256
NT = T // TM
NT_LAT = SEQ // TM
TMM = TM
N_BUCKET_IDS = 64
N_REAL_BUCKETS = 24
XROW_W = D + LANES
FFT_N1 = 128
FFT_N2 = 128
NEG_BIG = -1e30
SUBLANES = 8

VMEM_LIMIT = 56 * 1024 * 1024


def _params(sem, vmem=VMEM_LIMIT):
    return pltpu.CompilerParams(dimension_semantics=sem, vmem_limit_bytes=vmem)


def _dot(a, b):
    return jnp.dot(a, b, preferred_element_type=F32)


def _dot_nt(a, b):
    return lax.dot_general(a, b, (((1,), (1,)), ((), ())), preferred_element_type=F32)


def _dot_tn(a, b):
    return lax.dot_general(a, b, (((0,), (0,)), ((), ())), preferred_element_type=F32)


def _rms_mod(x, g, sh, sc):
    ms = jnp.mean(x * x, axis=-1, keepdims=True)
    y = x * lax.rsqrt(ms + RMS_EPS) * g
    return y * (1.0 + sc) + sh


def _resident(shape):
    nd = len(shape)
    return pl.BlockSpec(shape, lambda *_: (0,) * nd, pipeline_mode=pl.Buffered(1))


MOD_TN = 768


def _mod_kernel(cs_ref, w_ref, b_ref, o_ref):
    cs = cs_ref[...]
    s = cs * jax.nn.sigmoid(cs)
    w = w_ref[0]
    r0 = jnp.sum(s[:, 0:1] * w, axis=0, keepdims=True)
    r1 = jnp.sum(s[:, 1:2] * w, axis=0, keepdims=True)
    o_ref[0] = jnp.concatenate([r0, r1], axis=0) + b_ref[0]


def _mod_vectors(c, c_ctx, ada_w, ada_b):
    cs = jnp.stack([c[0], c_ctx], axis=1)
    n = 6 * D
    return pl.pallas_call(
        _mod_kernel,
        grid=(DEPTH, n // MOD_TN),
        in_specs=[
            pl.BlockSpec((D, 2), lambda l, j: (0, 0)),
            pl.BlockSpec((1, D, MOD_TN), lambda l, j: (l, 0, j)),
            pl.BlockSpec((1, 1, MOD_TN), lambda l, j: (l, 0, j)),
        ],
        out_specs=pl.BlockSpec((1, 2, MOD_TN), lambda l, j: (l, 0, j)),
        out_shape=jax.ShapeDtypeStruct((DEPTH, 2, n), F32),
        compiler_params=_params(("arbitrary", "arbitrary")),
        name="mod_vectors",
    )(cs, ada_w, ada_b.reshape(DEPTH, 1, n))


def _mod_spec():
    return pl.BlockSpec((1, 8, D), lambda i, *_: (i // NT_LAT, 0, 0))


def _stream_tile(x_ref, ctx_ref):
    return jnp.where(pl.program_id(0) >= NT_LAT, ctx_ref[...], x_ref[...])


def _stream_specs():
    return [pl.BlockSpec((TM, D), lambda i: (jnp.minimum(i, NT_LAT - 1), 0)),
            pl.BlockSpec((CTX, D), lambda i: (0, 0))]


def _inproj0_kernel(x_ref, ctx_ref, mod_ref, g_ref, w_ref, d64_ref, qkv_ref, br_ref, bi_ref):
    h = _stream_tile(x_ref, ctx_ref)
    xm = _rms_mod(h, g_ref[...], mod_ref[0, 0:1, :], mod_ref[0, 1:2, :]).astype(BF16)
    p = _dot(xm, w_ref[...])
    qkv_ref[...] = p[:, FNET_W:].astype(BF16)
    b = _dot(p[:, :FNET_W].astype(BF16), d64_ref[...])
    br_ref[...] = b[:, :FNET_W].astype(BF16)
    bi_ref[...] = b[:, FNET_W:].astype(BF16)


def _inproj0(x, ctx, modv, g, w_in, d64):
    return pl.pallas_call(
        _inproj0_kernel,
        grid=(NT,),
        in_specs=_stream_specs() + [
            _mod_spec(),
            _resident((1, D)),
            _resident((D, AB_IN)),
            _resident((FNET_W, 2 * FNET_W)),
        ],
        out_specs=[
            pl.BlockSpec((TM, QKV_W), lambda i: (i, 0)),
            pl.BlockSpec((TM, FNET_W), lambda i: (i, 0)),
            pl.BlockSpec((TM, FNET_W), lambda i: (i, 0)),
        ],
        out_shape=[
            jax.ShapeDtypeStruct((T, QKV_W), BF16),
            jax.ShapeDtypeStruct((T, FNET_W), BF16),
            jax.ShapeDtypeStruct((T, FNET_W), BF16),
        ],
        compiler_params=_params(("arbitrary",)),
        name="inproj0",
    )(x, ctx, modv, g, w_in, d64)


FFT_A_NB = 8
FFT_B_TN = 4096


def _fft_a_kernel(br_ref, bi_ref, ga_ref, v_ref):
    for u in range(FFT_A_NB):
        cols = slice(u * FNET_W, (u + 1) * FNET_W)
        z = _dot(ga_ref[u, :, 0:FFT_N2], br_ref[:, cols]) + _dot(ga_ref[u, :, FFT_N2:], bi_ref[:, cols])
        v_ref[0, u] = z[:FFT_N2].astype(BF16)
        v_ref[1, u] = z[FFT_N2:].astype(BF16)


def _fft_b_kernel(cs_ref, v_ref, o_ref):
    o_ref[...] = _dot(cs_ref[...], v_ref[...]) * (1.0 / math.sqrt(SEQ * FNET_GD))


def _fft_latent(br, bi, ga, csb):
    width = FFT_N1 * FNET_W
    br2 = br.reshape(T // FFT_N1, width)
    bi2 = bi.reshape(T // FFT_N1, width)
    v = pl.pallas_call(
        _fft_a_kernel,
        grid=(FFT_N1 // FFT_A_NB,),
        in_specs=[
            pl.BlockSpec((FFT_N2, FFT_A_NB * FNET_W), lambda j: (0, j)),
            pl.BlockSpec((FFT_N2, FFT_A_NB * FNET_W), lambda j: (0, j)),
            pl.BlockSpec((FFT_A_NB, 2 * FFT_N2, 2 * FFT_N2), lambda j: (j, 0, 0)),
        ],
        out_specs=pl.BlockSpec((2, FFT_A_NB, FFT_N2, FNET_W), lambda j: (0, j, 0, 0)),
        out_shape=jax.ShapeDtypeStruct((2, FFT_N1, FFT_N2, FNET_W), BF16),
        compiler_params=_params(("arbitrary",)),
        name="fft_stage_a",
    )(br2, bi2, ga)
    v2 = v.reshape(2 * FFT_N1, FFT_N2 * FNET_W)
    x = pl.pallas_call(
        _fft_b_kernel,
        grid=(FFT_N2 * FNET_W // FFT_B_TN,),
        in_specs=[
            _resident((FFT_N1, 2 * FFT_N1)),
            pl.BlockSpec((2 * FFT_N1, FFT_B_TN), lambda j: (0, j)),
        ],
        out_specs=pl.BlockSpec((FFT_N1, FFT_B_TN), lambda j: (0, j)),
        out_shape=jax.ShapeDtypeStruct((FFT_N1, FFT_N2 * FNET_W), F32),
        compiler_params=_params(("arbitrary",)),
        name="fft_stage_b",
    )(csb, v2)
    return x.reshape(SEQ, FNET_W)


def _head_pair_masks():
    lane = lax.broadcasted_iota(jnp.int32, (1, LANES), 1)
    return lane < NA_HD


def _ctx_kernel(qkv_ref, br_ref, bi_ref, cs_ref, o_ref):
    a = _dot(cs_ref[:, 0:CTX], br_ref[...]) + _dot(cs_ref[:, CTX:], bi_ref[...])
    o_ref[:, 0:FNET_W] = (a * (1.0 / math.sqrt(CTX * FNET_GD))).astype(BF16)
    m0 = _head_pair_masks()
    for hp in range(NA_HEADS // 2):
        q = qkv_ref[:, hp * LANES:(hp + 1) * LANES]
        k = qkv_ref[:, NA_W + hp * LANES:NA_W + (hp + 1) * LANES]
        v = qkv_ref[:, 2 * NA_W + hp * LANES:2 * NA_W + (hp + 1) * LANES]
        outs = []
        for a_ in range(2):
            qa = jnp.where(m0 if a_ == 0 else jnp.logical_not(m0), q, jnp.zeros_like(q))
            s = _dot_nt(qa, k) * (NA_HD ** -0.5)
            p = jnp.exp(s - jnp.max(s, axis=-1, keepdims=True))
            l = jnp.sum(p, axis=-1, keepdims=True)
            outs.append(_dot(p.astype(BF16), v) / l)
        o_ref[:, FNET_W + hp * LANES:FNET_W + (hp + 1) * LANES] = jnp.where(m0, outs[0], outs[1]).astype(BF16)


def _ctx_mixer(qkv, br, bi, csc):
    return pl.pallas_call(
        _ctx_kernel,
        grid=(1,),
        in_specs=[
            pl.BlockSpec((CTX, QKV_W), lambda i: (SEQ // CTX, 0)),
            pl.BlockSpec((CTX, FNET_W), lambda i: (SEQ // CTX, 0)),
            pl.BlockSpec((CTX, FNET_W), lambda i: (SEQ // CTX, 0)),
            pl.BlockSpec((CTX, 2 * CTX), lambda i: (0, 0)),
        ],
        out_specs=pl.BlockSpec((CTX, D), lambda i: (0, 0)),
        out_shape=jax.ShapeDtypeStruct((CTX, D), BF16),
        compiler_params=_params(("arbitrary",)),
        name="ctx_mixer",
    )(qkv, br, bi, csc)


NA_RB = 16
NA_WIN = NA_KH * GRID_W


NA_NK = NA_WIN + CTX
NA_SUB = 16


def _na_kernel(q_ref, k_ref, v_ref, kc_ref, vc_ref, bias_ref, o_ref, s_scr, p_scr, l_scr):
    b = pl.program_id(1)
    m0 = _head_pair_masks()
    nm0 = jnp.logical_not(m0)
    scale = NA_HD ** -0.5

    def window(i):
        r = b * NA_RB + i
        rs = jnp.clip(r - NA_KH // 2, 0, ROWS - NA_KH)
        return pl.multiple_of(rs * GRID_W, GRID_W), rs - r + (NA_KH - 1)

    def scores(i):
        start, _ = window(i)
        q = q_ref[i * GRID_W:(i + 1) * GRID_W, :] * scale
        qs = jnp.concatenate([jnp.where(m0, q, jnp.zeros_like(q)), jnp.where(nm0, q, jnp.zeros_like(q))], axis=0)
        s_scr[i, :, 0:NA_WIN] = _dot_nt(qs, k_ref[pl.ds(start, NA_WIN), :])
        s_scr[i, :, NA_WIN:NA_NK] = _dot_nt(qs, kc_ref[...])

    def softmax(i):
        _, e = window(i)
        for g in range(2 * GRID_W // NA_SUB):
            rows = slice(g * NA_SUB, (g + 1) * NA_SUB)
            a, c0 = divmod(g * NA_SUB, GRID_W)
            s1 = s_scr[i, rows, 0:NA_WIN] + bias_ref[a, e, c0:c0 + NA_SUB, :]
            s2 = s_scr[i, rows, NA_WIN:NA_NK]
            mx = jnp.maximum(jnp.max(s1, axis=-1, keepdims=True), jnp.max(s2, axis=-1, keepdims=True))
            p1 = jnp.exp(s1 - mx)
            p2 = jnp.exp(s2 - mx)
            l = jnp.sum(p1, axis=-1, keepdims=True) + jnp.sum(p2, axis=-1, keepdims=True)
            p_scr[i, rows, 0:NA_WIN] = p1.astype(BF16)
            p_scr[i, rows, NA_WIN:NA_NK] = p2.astype(BF16)
            l_scr[i, rows, :] = jnp.broadcast_to(l, (NA_SUB, LANES))

    def values(i):
        start, _ = window(i)
        o = _dot(p_scr[i, :, 0:NA_WIN], v_ref[pl.ds(start, NA_WIN), :]) + _dot(p_scr[i, :, NA_WIN:NA_NK], vc_ref[...])
        o = o / l_scr[i]
        o_ref[i * GRID_W:(i + 1) * GRID_W, :] = jnp.where(m0, o[0:GRID_W], o[GRID_W:]).astype(BF16)

    for i in range(NA_RB):
        scores(i)
    for i in range(NA_RB):
        softmax(i)
        values(i)


def _na_attention(qkv, bias):
    nq = NA_RB * GRID_W
    nhp = NA_HEADS // 2
    return pl.pallas_call(
        _na_kernel,
        grid=(nhp, ROWS // NA_RB),
        in_specs=[
            pl.BlockSpec((nq, LANES), lambda hp, b: (b, hp)),
            pl.BlockSpec((SEQ, LANES), lambda hp, b: (0, nhp + hp)),
            pl.BlockSpec((SEQ, LANES), lambda hp, b: (0, 2 * nhp + hp)),
            pl.BlockSpec((CTX, LANES), lambda hp, b: (SEQ // CTX, nhp + hp)),
            pl.BlockSpec((CTX, LANES), lambda hp, b: (SEQ // CTX, 2 * nhp + hp)),
            pl.BlockSpec((2, NA_KH, GRID_W, NA_WIN), lambda hp, b: (hp, 0, 0, 0)),
        ],
        out_specs=pl.BlockSpec((nq, LANES), lambda hp, b: (b, hp)),
        out_shape=jax.ShapeDtypeStruct((SEQ, NA_W), BF16),
        scratch_shapes=[
            pltpu.VMEM((NA_RB, 2 * GRID_W, NA_NK), F32),
            pltpu.VMEM((NA_RB, 2 * GRID_W, NA_NK), BF16),
            pltpu.VMEM((NA_RB, 2 * GRID_W, LANES), F32),
        ],
        compiler_params=_params(("arbitrary", "arbitrary")),
        name="na_attention",
    )(qkv, qkv, qkv, qkv, qkv, bias)


def _na_bias_rows(rpb):
    c = np.arange(GRID_W)[:, None]
    kc = np.arange(GRID_W)[None, :]
    ws = np.clip(c - NA_KW // 2, 0, GRID_W - NA_KW)
    col_ok = (kc >= ws) & (kc < ws + NA_KW)
    dc = np.clip(kc - c + NA_KW - 1, 0, 2 * NA_KW - 2)
    pick = (dc[None] == np.arange(2 * NA_KW - 1)[:, None, None]).astype(np.float32)
    full = jnp.einsum("hrd,dck->hrck", rpb.astype(F32), jnp.asarray(pick), precision=lax.Precision.HIGHEST)
    full = jnp.where(jnp.asarray(col_ok)[None, None], full, NEG_BIG)
    rows = [jnp.transpose(full[:, e:e + NA_KH], (0, 2, 1, 3)).reshape(NA_HEADS, GRID_W, NA_WIN)
            for e in range(NA_KH)]
    return jnp.stack(rows, axis=1)


def _route_stage(xm2, valid, rw_ref, rb_ref, triu_ref, run_ref, xrow_ref, meta_ref, cnt_ref):
    xrow_ref[:, 0:D] = xm2
    xh = xm2.astype(BF16)
    xl = (xm2 - xh.astype(F32)).astype(BF16)
    w = rw_ref[...]
    wh = w.astype(BF16)
    wl = (w - wh.astype(F32)).astype(BF16)
    logits = _dot(xh, wh) + (_dot(xl, wh) + _dot(xh, wl))
    scores = jax.nn.sigmoid(logits.T[0:N_EXPERTS, :])
    biased = scores + rb_ref[...]
    s = [scores[e:e + 1, :] for e in range(N_EXPERTS)]
    b = [biased[e:e + 1, :] for e in range(N_EXPERTS)]
    n_groups = N_EXPERTS // EPG
    top2, group_score = [], []
    for g in range(n_groups):
        m = b[EPG * g:EPG * (g + 1)]
        total = None
        for k in range(EPG):
            rank = None
            for j in range(EPG):
                if j != k:
                    beats = jnp.where((m[j] >= m[k]) if j < k else (m[j] > m[k]), 1.0, 0.0)
                    rank = beats if rank is None else rank + beats
            top2.append(rank < 2.0)
            term = jnp.where(top2[-1], m[k], 0.0)
            total = term if total is None else total + term
        group_score.append(total)
    best, g_sel = group_score[0], jnp.zeros((1, TM), jnp.int32)
    for g in range(1, n_groups):
        better = group_score[g] > best
        g_sel = jnp.where(better, g, g_sel)
        best = jnp.where(better, group_score[g], best)
    sel = [top2[e] & (g_sel == e // EPG) for e in range(N_EXPERTS)]
    w_sum = None
    for e in range(N_EXPERTS):
        term = jnp.where(sel[e], s[e], 0.0)
        w_sum = term if w_sum is None else w_sum + term
    zero = jnp.zeros((1, TM), F32)
    w_lo, w_hi, bid = zero, zero, zero
    for g in range(n_groups):
        seen = None
        for k in range(EPG):
            e = EPG * g + k
            if seen is None:
                w_lo = w_lo + jnp.where(sel[e], s[e], 0.0)
                seen = sel[e]
            else:
                w_lo = w_lo + jnp.where(sel[e] & jnp.logical_not(seen), s[e], 0.0)
                w_hi = w_hi + jnp.where(sel[e] & seen, s[e], 0.0)
                seen = seen | sel[e]
            bid = bid + jnp.where(sel[e], float((1 << k) + 8 * g), 0.0)
    w_lo = w_lo / w_sum
    w_hi = w_hi / w_sum
    bucket = lax.broadcasted_iota(jnp.int32, (N_BUCKET_IDS, TM), 0).astype(F32)
    onehot = bucket == bid
    ones = jnp.where(onehot & valid, 1.0, 0.0)
    earlier = _dot(ones.astype(BF16), triu_ref[...])
    run = run_ref[...]
    run_t = jnp.concatenate([run] * (TM // LANES), axis=1)
    rank_row = jnp.sum(jnp.where(onehot, earlier + run_t, 0.0), axis=0, keepdims=True)
    new_run = run + jnp.sum(ones, axis=1, keepdims=True)
    run_ref[...] = new_run
    cnt_ref[...] = new_run
    meta_rows = jnp.concatenate([w_lo, w_hi, bid, rank_row, jnp.zeros((SUBLANES - 4, TM), F32)], axis=0)
    meta_ref[...] = meta_rows
    padded = jnp.concatenate([meta_rows, jnp.zeros((LANES - SUBLANES, TM), F32)], axis=0)
    xrow_ref[:, D:XROW_W] = padded.T


def _epilogue_specs(ntiles):
    in_specs = [
        _resident((1, D)),
        _resident((D, LANES)),
        _resident((N_EXPERTS, TM)),
        _resident((TM, TM)),
    ]
    out_specs = [
        pl.BlockSpec((TM, D), lambda i: (jnp.minimum(i, ntiles - 1), 0)),
        pl.BlockSpec((TM, XROW_W), lambda i: (jnp.maximum(i - 1, 0), 0)),
        pl.BlockSpec((SUBLANES, TM), lambda i: (0, jnp.maximum(i - 1, 0))),
        pl.BlockSpec((N_BUCKET_IDS, LANES), lambda i: (0, 0)),
    ]
    n = ntiles * TM
    out_shape = [
        jax.ShapeDtypeStruct((n, D), F32),
        jax.ShapeDtypeStruct((n, XROW_W), F32),
        jax.ShapeDtypeStruct((SUBLANES, n), F32),
        jax.ShapeDtypeStruct((N_BUCKET_IDS, LANES), F32),
    ]
    return in_specs, out_specs, out_shape


_EPILOGUE_SCRATCH = [pltpu.VMEM((N_BUCKET_IDS, LANES), F32), pltpu.VMEM((2, TM, D), F32)]


def _two_stage(residual_tile, mod_ref, g2_ref, rw_ref, rb_ref, triu_ref,
               h1_ref, xrow_ref, meta_ref, cnt_ref, run_ref, stash):
    i = pl.program_id(0)

    @pl.when(i == 0)
    def _():
        run_ref[...] = jnp.zeros_like(run_ref)
        stash[1] = jnp.zeros((TM, D), F32)

    _route_stage(stash[1 - i % 2], i >= 1, rw_ref, rb_ref, triu_ref, run_ref, xrow_ref, meta_ref, cnt_ref)
    h1 = residual_tile()
    h1_ref[...] = h1
    stash[i % 2] = _rms_mod(h1, g2_ref[...], mod_ref[0, 3:4, :], mod_ref[0, 4:5, :])


def _outproj0_kernel(a_ref, o_ref, aoc_ref, x_ref, ctx_ref, mod_ref, w_ref, *rest):
    def residual_tile():
        is_ctx = pl.program_id(0) >= NT_LAT
        a = jnp.where(is_ctx, aoc_ref[:, 0:FNET_W], a_ref[...].astype(BF16))
        o = jnp.where(is_ctx, aoc_ref[:, FNET_W:], o_ref[...])
        y = _dot(a, w_ref[0:FNET_W, :]) + _dot(o, w_ref[FNET_W:, :])
        return _stream_tile(x_ref, ctx_ref) + mod_ref[0, 2:3, :] * y

    _two_stage(residual_tile, mod_ref, *rest)


def _outproj0(a_lat, o_lat, ao_ctx, x, ctx, modv, w_out, g2, rw, rb, triu):
    e_in, e_out, e_shape = _epilogue_specs(NT)
    lat = lambda w: pl.BlockSpec((TM, w), lambda i: (jnp.minimum(i, NT_LAT - 1), 0))
    return pl.pallas_call(
        _outproj0_kernel,
        grid=(NT + 1,),
        in_specs=[lat(FNET_W), lat(NA_W), pl.BlockSpec((CTX, D), lambda i: (0, 0))] + _stream_specs() + [
            _mod_spec(),
            _resident((D, D)),
        ] + e_in,
        out_specs=e_out,
        out_shape=e_shape,
        scratch_shapes=_EPILOGUE_SCRATCH,
        compiler_params=_params(("arbitrary",)),
        name="outproj0",
    )(a_lat, o_lat, ao_ctx, x, ctx, modv, w_out, g2, rw, rb, triu)


def _bucket_tables():
    lo = np.zeros(N_BUCKET_IDS, np.int32)
    hi = np.zeros(N_BUCKET_IDS, np.int32)
    for g in range(N_EXPERTS // EPG):
        for a in range(EPG):
            for b in range(a + 1, EPG):
                i = 16 * g + (1 << a) + (1 << b)
                lo[i] = EPG * g + a
                hi[i] = EPG * g + b
    return lo, hi


def _slot_kernel(meta_ref, starts_ref, o_ref):
    n = meta_ref.shape[1]
    bucket = lax.broadcasted_iota(jnp.int32, (N_BUCKET_IDS, n), 0).astype(F32)
    first = jnp.sum(jnp.where(bucket == meta_ref[2:3, :], starts_ref[:, 0:1], 0.0), axis=0, keepdims=True)
    o_ref[...] = jnp.broadcast_to((first + meta_ref[3:4, :]).astype(jnp.int32), (SUBLANES, n))


def _token_slots(meta, first_row):
    n = meta.shape[1]
    out = pl.pallas_call(
        _slot_kernel,
        grid=(1,),
        in_specs=[pl.BlockSpec((SUBLANES, n), lambda i: (0, 0)),
                  pl.BlockSpec((N_BUCKET_IDS, LANES), lambda i: (0, 0))],
        out_specs=pl.BlockSpec((SUBLANES, n), lambda i: (0, 0)),
        out_shape=jax.ShapeDtypeStruct((SUBLANES, n), jnp.int32),
        compiler_params=_params(("arbitrary",)),
        name="token_slots",
    )(meta, first_row)
    return out[0]


def _route_plan(meta, cnt, n_tiles, layer):
    counts = cnt[:, 0].astype(jnp.int32)
    nt_b = (counts + TMM - 1) // TMM
    ends = jnp.cumsum(nt_b)
    starts = ends - nt_b
    first_row = jnp.broadcast_to((starts * TMM).astype(F32)[:, None], (N_BUCKET_IDS, LANES))
    pos = _token_slots(meta, first_row)
    n_used = ends[-1]
    tile = jnp.minimum(jnp.arange(n_tiles, dtype=jnp.int32), n_used - 1)
    tile_b = jnp.sum((ends[None, :] <= tile[:, None]).astype(jnp.int32), axis=1)
    lo, hi = (jnp.asarray(t) + layer * N_EXPERTS for t in _bucket_tables())
    ids = jnp.arange(n_tiles, dtype=jnp.int32)
    first = (tile_b != jnp.concatenate([jnp.full((1,), -1, jnp.int32), tile_b[:-1]])) & (ids < n_used)
    wslot = (jnp.cumsum(first.astype(jnp.int32)) - 1) % 2
    next_tile = ends[tile_b]
    has_next = next_tile < n_used
    next_b = tile_b[jnp.minimum(next_tile, n_tiles - 1)]
    plan = jnp.stack([lo[tile_b], hi[tile_b], first.astype(jnp.int32), wslot,
                      jnp.where(has_next, lo[next_b], -1), jnp.where(has_next, hi[next_b], -1)])
    fill = ((ids >= n_used) | (ids == next_tile - 1)).astype(jnp.int32)
    return pos, fill, plan.astype(jnp.int32), n_used.reshape(1).astype(jnp.int32)


def _dispatch_kernel(pos_ref, fill_ref, x_ref, xs_ref, sem, zeros, zsem):
    base = pl.program_id(0) * TM
    tile_rows = TMM // SUBLANES

    @pl.when(pl.program_id(0) == 0)
    def _():
        zeros[...] = jnp.zeros_like(zeros)

        def fill(t):
            return pltpu.make_async_copy(zeros, xs_ref.at[pl.ds(t * tile_rows, tile_rows)], zsem)

        def start(t, carry):
            @pl.when(fill_ref[t] == 1)
            def _():
                fill(t).start()
            return carry

        def wait(t, carry):
            @pl.when(fill_ref[t] == 1)
            def _():
                fill(t).wait()
            return carry

        lax.fori_loop(0, fill_ref.shape[0], start, 0)
        lax.fori_loop(0, fill_ref.shape[0], wait, 0)

    for j in range(TM // SUBLANES):
        for u in range(SUBLANES):
            p = pos_ref[base + j * SUBLANES + u]
            pltpu.make_async_copy(x_ref.at[j, pl.ds(u, 1), :],
                                  xs_ref.at[p >> 3, pl.ds(p & (SUBLANES - 1), 1), :], sem).start()
    pltpu.make_async_copy(x_ref, xs_ref.at[pl.ds(0, TM // SUBLANES)], sem).wait()


def _dispatch(pos, fill, xrow, n_tiles):
    ntok = xrow.shape[0]
    xs = pl.pallas_call(
        _dispatch_kernel,
        grid_spec=pltpu.PrefetchScalarGridSpec(
            num_scalar_prefetch=2,
            grid=(ntok // TM,),
            in_specs=[pl.BlockSpec((TM // SUBLANES, SUBLANES, XROW_W), lambda i, pos, fill: (i, 0, 0))],
            out_specs=pl.BlockSpec(memory_space=pl.ANY),
            scratch_shapes=[
                pltpu.SemaphoreType.DMA(()),
                pltpu.VMEM((TMM // SUBLANES, SUBLANES, XROW_W), F32),
                pltpu.SemaphoreType.DMA(()),
            ],
        ),
        out_shape=jax.ShapeDtypeStruct((n_tiles * TMM // SUBLANES, SUBLANES, XROW_W), F32),
        compiler_params=_params(("arbitrary",)),
        name="moe_dispatch",
    )(pos, fill, _by_sublane_tile(xrow))
    return xs.reshape(n_tiles * TMM, XROW_W)


def _moe_kernel(plan_ref, nu_ref, xs_ref, w1_hbm, w3_hbm, w2_hbm, ys_ref, wb1, wb3, wb2, sem):
    j = pl.program_id(0)
    slot = plan_ref[3, j]

    def weight_copies(e_lo, e_hi, s):
        out = []
        for k, e in enumerate((e_lo, e_hi)):
            out.append(pltpu.make_async_copy(w1_hbm.at[e], wb1.at[s, k], sem.at[s, 3 * k]))
            out.append(pltpu.make_async_copy(w3_hbm.at[e], wb3.at[s, k], sem.at[s, 3 * k + 1]))
            out.append(pltpu.make_async_copy(w2_hbm.at[e], wb2.at[s, k], sem.at[s, 3 * k + 2]))
        return out

    @pl.when(j == 0)
    def _():
        for c in weight_copies(plan_ref[0, 0], plan_ref[1, 0], 0):
            c.start()

    @pl.when(plan_ref[2, j] == 1)
    def _():
        for c in weight_copies(plan_ref[0, j], plan_ref[1, j], slot):
            c.wait()

        @pl.when(plan_ref[4, j] >= 0)
        def _():
            for c in weight_copies(plan_ref[4, j], plan_ref[5, j], 1 - slot):
                c.start()

    @pl.when(j < nu_ref[0])
    def _():
        x = xs_ref[:, 0:D].astype(BF16)
        g_lo = xs_ref[:, D:D + 1]
        g_hi = xs_ref[:, D + 1:D + 2]

        def expert(k):
            a = _dot(x, wb1[slot, k].astype(BF16))
            hid = (a * jax.nn.sigmoid(a)) * _dot(x, wb3[slot, k].astype(BF16))
            return _dot(hid.astype(BF16), wb2[slot, k].astype(BF16))

        ys_ref[...] = g_lo * expert(0) + g_hi * expert(1)

    @pl.when(j >= nu_ref[0])
    def _():
        ys_ref[...] = jnp.zeros_like(ys_ref)


def _moe(plan, n_used, xs, w1, w3, w2, n_tiles):
    hbm = pl.BlockSpec(memory_space=pl.ANY)
    return pl.pallas_call(
        _moe_kernel,
        grid_spec=pltpu.PrefetchScalarGridSpec(
            num_scalar_prefetch=2,
            grid=(n_tiles,),
            in_specs=[pl.BlockSpec((TMM, XROW_W), lambda j, plan, nu: (j, 0)), hbm, hbm, hbm],
            out_specs=pl.BlockSpec((TMM, D), lambda j, plan, nu: (j, 0)),
            scratch_shapes=[
                pltpu.VMEM((2, 2, D, D_EXPERT), F32),
                pltpu.VMEM((2, 2, D, D_EXPERT), F32),
                pltpu.VMEM((2, 2, D_EXPERT, D), F32),
                pltpu.SemaphoreType.DMA((2, 6)),
            ],
        ),
        out_shape=jax.ShapeDtypeStruct((n_tiles * TMM, D), F32),
        compiler_params=_params(("arbitrary",)),
        name="moe_experts",
    )(plan, n_used, xs, w1, w3, w2)


def _gather_issue(pos_ref, ys_hbm, buf, sem, tile, slot, inline, part=(0, TM // SUBLANES)):
    base = tile * TM

    def issue(j, carry):
        for u in range(SUBLANES):
            p = pos_ref[base + j * SUBLANES + u]
            pltpu.make_async_copy(ys_hbm.at[p >> 3, pl.ds(p & (SUBLANES - 1), 1), :],
                                  buf.at[slot, j, pl.ds(u, 1), :], sem.at[slot]).start()
        return carry

    if inline:
        for j in range(*part):
            issue(j, 0)
    else:
        lax.fori_loop(0, TM // SUBLANES, issue, 0)


def _gather_wait(ys_hbm, buf, sem, slot):
    pltpu.make_async_copy(ys_hbm.at[pl.ds(0, TM // SUBLANES)], buf.at[slot], sem.at[slot]).wait()


def _gathered_rows(pos_ref, ys_hbm, buf, sem, request_next):
    i = pl.program_id(0)
    n_slots = buf.shape[0]
    slot = i % n_slots

    @pl.when(i == 0)
    def _():
        for t in range(1 if request_next else n_slots - 1):
            _gather_issue(pos_ref, ys_hbm, buf, sem, t, t, inline=False)

    if request_next:
        @pl.when(i + 1 < pl.num_programs(0))
        def _():
            _gather_issue(pos_ref, ys_hbm, buf, sem, i + 1, 1 - slot, inline=True)

    _gather_wait(ys_hbm, buf, sem, slot)
    return buf[slot].reshape(TM, D)


def _gather_request_ahead(pos_ref, ys_hbm, buf, sem):
    i = pl.program_id(0)
    n = pl.num_programs(0)
    ahead = buf.shape[0] - 1
    tile = i + ahead
    _gather_issue(pos_ref, ys_hbm, buf, sem, jnp.where(tile < n, tile, tile - n), tile % buf.shape[0], inline=True)


def _gather_drain(ys_hbm, buf, sem):
    i = pl.program_id(0)

    @pl.when(i + 1 == pl.num_programs(0))
    def _():
        for k in range(1, buf.shape[0]):
            _gather_wait(ys_hbm, buf, sem, (i + k) % buf.shape[0])


_GATHER_SCRATCH = [pltpu.VMEM((2, TM // SUBLANES, SUBLANES, D), F32), pltpu.SemaphoreType.DMA((2,))]


def _by_sublane_tile(a):
    return a.reshape(a.shape[0] // SUBLANES, SUBLANES, a.shape[1])


def _rope_store(x, tabs, out_ref):
    cr, sr, cc, sc = tabs
    for ch in range(RET_QK // LANES):
        xc = x[:, ch * LANES:(ch + 1) * LANES]
        xr = pltpu.roll(xc, LANES // 2, axis=1)
        y = xc * cr + xr * sr if ch % 2 == 0 else xc * cc + xr * sc
        out_ref[:, ch * LANES:(ch + 1) * LANES] = y.astype(BF16)


def _inproj1_kernel(pos_ref, h_ref, ys_hbm, mod0_ref, mod1_ref, g_ref, w_ref, rt_ref, ct_ref,
                    h2_ref, q_ref, k_ref, v_ref, gg_ref, buf, sem, xm_scr):
    f = _gathered_rows(pos_ref, ys_hbm, buf, sem, request_next=False)
    h2 = h_ref[...] + mod0_ref[0, 5:6, :] * f
    h2_ref[...] = h2
    xm_scr[...] = _rms_mod(h2, g_ref[...], mod1_ref[0, 0:1, :], mod1_ref[0, 1:2, :]).astype(BF16)
    _gather_request_ahead(pos_ref, ys_hbm, buf, sem)
    xm = xm_scr[...]

    off = (pl.program_id(0) % (SUBLANES // ROPE_TILE_ROWS)) * ROPE_TILE_ROWS

    def row_table(which):
        return jnp.concatenate([jnp.broadcast_to(rt_ref[which, pl.ds(off + j, 1), :], (GRID_W, LANES))
                                for j in range(ROPE_TILE_ROWS)], axis=0)

    tabs = (row_table(0), row_table(1), ct_ref[0, 0], ct_ref[0, 1])
    _rope_store(_dot(xm, w_ref[:, 0:RET_QK]), tabs, q_ref)
    _rope_store(_dot(xm, w_ref[:, RET_QK:2 * RET_QK]) * (RET_DK ** -0.5), tabs, k_ref)
    v_ref[...] = _dot(xm, w_ref[:, 2 * RET_QK:2 * RET_QK + RET_V]).astype(BF16)
    gg_ref[...] = _dot(xm, w_ref[:, 2 * RET_QK + RET_V:]).astype(BF16)
    _gather_drain(ys_hbm, buf, sem)


def _inproj1(pos, h1, ys, modv0, modv1, g, w_in, tabs):
    tok = lambda w: pl.BlockSpec((TM, w), lambda i, pos: (i, 0))
    res = lambda shape: pl.BlockSpec(shape, lambda i, pos: (0,) * len(shape), pipeline_mode=pl.Buffered(1))
    mod = pl.BlockSpec((1, 8, D), lambda i, pos: (i // NT_LAT, 0, 0))
    row_tab = pl.BlockSpec((2, SUBLANES, LANES), lambda i, pos: (0, i // (SUBLANES // ROPE_TILE_ROWS), 0))
    col_tab = pl.BlockSpec((1, 2, TM, LANES), lambda i, pos: (i // NT_LAT, 0, 0, 0))
    return pl.pallas_call(
        _inproj1_kernel,
        grid_spec=pltpu.PrefetchScalarGridSpec(
            num_scalar_prefetch=1,
            grid=(NT,),
            in_specs=[tok(D), pl.BlockSpec(memory_space=pl.ANY), mod, mod, res((1, D)), res((D, 2 * RET_QK + 2 * RET_V)),
                      row_tab, col_tab],
            out_specs=[tok(D), tok(RET_QK), tok(RET_QK), tok(RET_V), tok(RET_V)],
            scratch_shapes=[pltpu.VMEM((3, TM // SUBLANES, SUBLANES, D), F32), pltpu.SemaphoreType.DMA((3,)),
                            pltpu.VMEM((TM, D), BF16)],
        ),
        out_shape=[
            jax.ShapeDtypeStruct((T, D), F32),
            jax.ShapeDtypeStruct((T, RET_QK), BF16),
            jax.ShapeDtypeStruct((T, RET_QK), BF16),
            jax.ShapeDtypeStruct((T, RET_V), BF16),
            jax.ShapeDtypeStruct((T, RET_V), BF16),
        ],
        compiler_params=_params(("arbitrary",)),
        name="inproj1",
    )(pos, h1, _by_sublane_tile(ys), modv0, modv1, g, w_in, *tabs)


ROPE_TILE_ROWS = TM // GRID_W


def _rope_tables():
    n = LANES // 2
    inv_freq = ROPE_BASE ** (-np.arange(n, dtype=np.float64) / n)

    def lines(p):
        ang = p[:, None] * inv_freq[None, :]
        return np.stack([np.concatenate([np.cos(ang), np.cos(ang)], axis=1),
                         np.concatenate([-np.sin(ang), np.sin(ang)], axis=1)])

    ident = np.stack([np.ones((TM, LANES)), np.zeros((TM, LANES))])
    row_tab = np.concatenate([lines(np.arange(ROWS, dtype=np.float64)), ident[:, :SUBLANES]], axis=1)
    col_lat = lines(np.tile(np.arange(GRID_W, dtype=np.float64), ROPE_TILE_ROWS))
    col_tab = np.stack([col_lat, ident])
    return jnp.asarray(row_tab, dtype=F32), jnp.asarray(col_tab, dtype=F32)


RET_NCHUNK = T // RET_CHUNK
RET_CTX_CHUNKS = CTX // RET_CHUNK
RET_LAT_CHUNKS = SEQ // RET_CHUNK
RET_NHD = 2 * RET_HEADS


def _ret_kernel(dec_ref, qf_ref, kf_ref, vf_ref, qb_ref, kb_ref, vb_ref, of_ref, ob_ref,
                state, dm, qd, kd, cd):
    C = RET_CHUNK

    @pl.when(pl.program_id(0) == 0)
    def _():
        state[...] = jnp.zeros_like(state)
        n = lax.broadcasted_iota(jnp.int32, (C, C), 0).astype(F32)
        m = lax.broadcasted_iota(jnp.int32, (C, C), 1).astype(F32)
        for d in range(2):
            for h in range(RET_HEADS):
                idx = d * RET_HEADS + h
                lg = jnp.log1p(-jnp.exp(jnp.full((C, C), dec_ref[d, h], F32)))
                diff = n - m if d == 0 else m - n
                low = diff >= 0.0
                dm[idx] = jnp.where(low, jnp.exp(jnp.where(low, diff, 0.0) * lg), 0.0)
                qpow = n + 1.0 if d == 0 else C - n
                kpow = (C - 1.0) - n if d == 0 else n
                qcol = jnp.exp(qpow * lg)
                kcol = jnp.exp(kpow * lg)
                for c0 in range(0, RET_DV, C):
                    qd[idx, :, c0:c0 + C] = qcol
                for c0 in range(0, RET_DK, C):
                    kd[idx, :, c0:c0 + C] = kcol
                for c0 in range(0, RET_DV, LANES):
                    cd[idx, :, c0:c0 + LANES] = jnp.exp(C * lg[0:8, 0:LANES])

    for d, (q_ref, k_ref, v_ref, o_ref) in enumerate(((qf_ref, kf_ref, vf_ref, of_ref),
                                                       (qb_ref, kb_ref, vb_ref, ob_ref))):
        for h in range(RET_HEADS):
            idx = d * RET_HEADS + h
            q = q_ref[:, h * RET_DK:(h + 1) * RET_DK]
            k = k_ref[:, h * RET_DK:(h + 1) * RET_DK]
            v = v_ref[:, h * RET_DV:(h + 1) * RET_DV]
            st = state[idx]
            s = _dot_nt(q, k) * dm[idx]
            o = _dot(s.astype(BF16), v) + qd[idx] * _dot(q, st.astype(BF16))
            o_ref[:, h * RET_DV:(h + 1) * RET_DV] = o.astype(BF16)
            kdk = (k.astype(F32) * kd[idx]).astype(BF16)
            cdv = cd[idx, 0:1, :]
            state[idx] = st * cdv + _dot(kdk.T, v)


def _ret_scan(decay, q, k, v):
    def fwd(s):
        return jnp.where(s < RET_CTX_CHUNKS, RET_LAT_CHUNKS + s, s - RET_CTX_CHUNKS)

    def bwd(s):
        return RET_NCHUNK - 1 - s

    def spec(w, f):
        return pl.BlockSpec((RET_CHUNK, w), lambda s: (f(s), 0))

    return pl.pallas_call(
        _ret_kernel,
        grid=(RET_NCHUNK,),
        in_specs=[
            pl.BlockSpec(memory_space=pltpu.SMEM),
            spec(RET_QK, fwd), spec(RET_QK, fwd), spec(RET_V, fwd),
            spec(RET_QK, bwd), spec(RET_QK, bwd), spec(RET_V, bwd),
        ],
        out_specs=[spec(RET_V, fwd), spec(RET_V, bwd)],
        out_shape=[jax.ShapeDtypeStruct((T, RET_V), BF16), jax.ShapeDtypeStruct((T, RET_V), BF16)],
        scratch_shapes=[
            pltpu.VMEM((RET_NHD, RET_DK, RET_DV), F32),
            pltpu.VMEM((RET_NHD, RET_CHUNK, RET_CHUNK), F32),
            pltpu.VMEM((RET_NHD, RET_CHUNK, RET_DV), F32),
            pltpu.VMEM((RET_NHD, RET_CHUNK, RET_DK), F32),
            pltpu.VMEM((RET_NHD, 8, RET_DV), F32),
        ],
        compiler_params=_params(("arbitrary",)),
        name="retention_scan",
    )(decay, q, k, v, q, k, v)


def _outproj1_kernel(of_ref, ob_ref, gg_ref, h_ref, mod_ref, w_ref, *rest):
    def residual_tile():
        y = jnp.zeros((TM, D), F32)
        for hh in range(RET_HEADS):
            cols = slice(hh * RET_DV, (hh + 1) * RET_DV)
            o = of_ref[:, cols].astype(F32) + ob_ref[:, cols].astype(F32)
            on = o * lax.rsqrt(jnp.mean(o * o, axis=-1, keepdims=True) + RMS_EPS)
            g = gg_ref[:, cols].astype(F32)
            z = (g * jax.nn.sigmoid(g)) * on
            y = y + _dot(z.astype(BF16), w_ref[cols, :])
        return h_ref[...] + mod_ref[0, 2:3, :] * y

    _two_stage(residual_tile, mod_ref, *rest)


def _outproj1(o_f, o_b, gg, h, modv, w_out, g2, rw, rb, triu):
    e_in, e_out, e_shape = _epilogue_specs(NT_LAT)
    tok = lambda w: pl.BlockSpec((TM, w), lambda i: (jnp.minimum(i, NT_LAT - 1), 0))
    latent_mod = pl.BlockSpec((1, 8, D), lambda i: (0, 0, 0))
    return pl.pallas_call(
        _outproj1_kernel,
        grid=(NT_LAT + 1,),
        in_specs=[tok(RET_V), tok(RET_V), tok(RET_V), tok(D), latent_mod, _resident((RET_V, D))] + e_in,
        out_specs=e_out,
        out_shape=e_shape,
        scratch_shapes=_EPILOGUE_SCRATCH,
        compiler_params=_params(("arbitrary",)),
        name="outproj1",
    )(o_f, o_b, gg, h, modv, w_out, g2, rw, rb, triu)


def _final_kernel(pos_ref, h_ref, ys_hbm, mod_ref, g_ref, o_ref, buf, sem):
    f = _gathered_rows(pos_ref, ys_hbm, buf, sem, request_next=True)
    h = h_ref[...] + mod_ref[0, 5:6, :] * f
    o_ref[...] = h * lax.rsqrt(jnp.mean(h * h, axis=-1, keepdims=True) + RMS_EPS) * g_ref[...]


def _final(pos, h, ys, modv, g):
    tok = pl.BlockSpec((TM, D), lambda i, pos: (i, 0))
    return pl.pallas_call(
        _final_kernel,
        grid_spec=pltpu.PrefetchScalarGridSpec(
            num_scalar_prefetch=1,
            grid=(NT_LAT,),
            in_specs=[tok, pl.BlockSpec(memory_space=pl.ANY),
                      pl.BlockSpec((1, 8, D), lambda i, pos: (0, 0, 0)),
                      pl.BlockSpec((1, D), lambda i, pos: (0, 0))],
            out_specs=tok,
            scratch_shapes=_GATHER_SCRATCH,
        ),
        out_shape=jax.ShapeDtypeStruct((SEQ, D), F32),
        compiler_params=_params(("arbitrary",)),
        name="final_norm",
    )(pos, h, _by_sublane_tile(ys), modv, g)


def _dft_constants():
    c = np.arange(FNET_GD)
    ang = 2.0 * np.pi * np.outer(c, c) / FNET_GD
    d64 = np.zeros((FNET_W, 2 * FNET_W))
    for g in range(FNET_W // FNET_GD):
        sl = slice(g * FNET_GD, (g + 1) * FNET_GD)
        d64[sl, sl] = np.cos(ang)
        d64[sl, FNET_W + g * FNET_GD:FNET_W + (g + 1) * FNET_GD] = -np.sin(ang)
    k2 = np.arange(FFT_N2)[:, None]
    n2 = np.arange(FFT_N2)[None, :]
    ga = np.zeros((FFT_N1, 2 * FFT_N2, 2 * FFT_N2))
    for n1 in range(FFT_N1):
        th = 2.0 * np.pi * ((k2 * (n1 + FFT_N1 * n2)) % SEQ) / SEQ
        cs, sn = np.cos(th), np.sin(th)
        ga[n1] = np.block([[cs, sn], [-sn, cs]])
    k1 = np.arange(FFT_N1)
    ph = 2.0 * np.pi * np.outer(k1, k1) / FFT_N1
    csb = np.concatenate([np.cos(ph), np.sin(ph)], axis=1)
    p = np.arange(CTX)
    pc = 2.0 * np.pi * (np.outer(p, p) % CTX) / CTX
    csc = np.concatenate([np.cos(pc), np.sin(pc)], axis=1)
    as_bf16 = lambda a: jnp.asarray(a, dtype=F32).astype(BF16)
    return as_bf16(d64), as_bf16(ga), as_bf16(csb), as_bf16(csc)


def _mod_table(mod_l):
    t = mod_l.reshape(2, 6, D)
    return jnp.concatenate([t, jnp.zeros((2, 2, D), F32)], axis=1)


def kernel(x, c, ctx, c_ctx, ada_w, ada_b, norm_g, final_norm_g, mixab_w_in, mixab_w_out, na_rpb,
           ret_w_in, ret_w_out, ret_decay, router_w, router_b, moe_w1, moe_w3, moe_w2):
    d64, ga, csb, csc = _dft_constants()
    triu = jnp.asarray(np.triu(np.ones((TM, TM)), 1), dtype=BF16)
    rw = jnp.pad(router_w, ((0, 0), (0, LANES - N_EXPERTS)))
    rb = jnp.broadcast_to(router_b[:, None], (N_EXPERTS, TM))
    w1 = moe_w1.reshape(DEPTH * N_EXPERTS, D, D_EXPERT)
    w3 = moe_w3.reshape(DEPTH * N_EXPERTS, D, D_EXPERT)
    w2 = moe_w2.reshape(DEPTH * N_EXPERTS, D_EXPERT, D)

    mod = _mod_vectors(c, c_ctx, ada_w, ada_b)
    modv0, modv1 = _mod_table(mod[0]), _mod_table(mod[1])

    qkv, br, bi = _inproj0(x[0], ctx[0], modv0, norm_g[0, 0].reshape(1, D), mixab_w_in[0].astype(BF16), d64)
    a_lat = _fft_latent(br, bi, ga, csb)
    ao_ctx = _ctx_mixer(qkv, br, bi, csc)
    o_lat = _na_attention(qkv, _na_bias_rows(na_rpb[0]))
    h1, xrow, meta, cnt = _outproj0(a_lat, o_lat, ao_ctx, x[0], ctx[0], modv0, mixab_w_out[0].astype(BF16),
                                    norm_g[0, 1].reshape(1, D), rw, rb, triu)
    n_tiles0 = NT + N_REAL_BUCKETS
    pos0, fill0, plan0, n_used0 = _route_plan(meta, cnt, n_tiles0, 0)
    ys0 = _moe(plan0, n_used0, _dispatch(pos0, fill0, xrow, n_tiles0), w1, w3, w2, n_tiles0)

    h2, q, k, v, gg = _inproj1(pos0, h1, ys0, modv0, modv1, norm_g[1, 0].reshape(1, D),
                               ret_w_in[0].astype(BF16), _rope_tables())
    o_f, o_b = _ret_scan(ret_decay[0].astype(F32), q, k, v)
    h3, xrow1, meta1, cnt1 = _outproj1(o_f, o_b, gg, h2, modv1, ret_w_out[0].astype(BF16),
                                       norm_g[1, 1].reshape(1, D), rw, rb, triu)
    n_tiles1 = NT_LAT + N_REAL_BUCKETS
    pos1, fill1, plan1, n_used1 = _route_plan(meta1, cnt1, n_tiles1, 1)
    ys1 = _moe(plan1, n_used1, _dispatch(pos1, fill1, xrow1, n_tiles1), w1, w3, w2, n_tiles1)
    out = _final(pos1, h3, ys1, modv1, final_norm_g.reshape(1, D))
    return out[None]
```

```python
import functools
import math

import numpy as np
import jax
import jax.numpy as jnp
from jax import lax
from jax.experimental import pallas as pl
from jax.experimental.pallas import tpu as pltpu

F32 = jnp.float32
BF16 = jnp.bfloat16

D = 1024
SEQ = 16384
CTX = 256
T = SEQ + CTX
DEPTH = 2
GRID_W = 64
ROWS = SEQ // GRID_W
RMS_EPS = 1e-6
ROPE_BASE = 10000.0

FNET_W = 256
FNET_GD = 64
NA_HEADS = 12
NA_HD = 64
NA_W = NA_HEADS * NA_HD
NA_KH = 8
NA_KW = 16
AB_IN = FNET_W + 3 * NA_W
QKV_W = 3 * NA_W

RET_HEADS = 4
RET_DK = 256
RET_DV = 512
RET_QK = RET_HEADS * RET_DK
RET_V = RET_HEADS * RET_DV
RET_CHUNK = 256

N_EXPERTS = 16
EPG = 4
D_EXPERT = 512

LANES = 128
TM = 256
NT = T // TM
NT_LAT = SEQ // TM
TMM = TM
N_BUCKET_IDS = 64
N_REAL_BUCKETS = 24
XROW_W = D + LANES
FFT_N1 = 128
FFT_N2 = 128
NEG_BIG = -1e30
SUBLANES = 8

VMEM_LIMIT = 56 * 1024 * 1024


def _params(sem, vmem=VMEM_LIMIT):
    return pltpu.CompilerParams(dimension_semantics=sem, vmem_limit_bytes=vmem)


def _dot(a, b):
    return jnp.dot(a, b, preferred_element_type=F32)


def _dot_nt(a, b):
    return lax.dot_general(a, b, (((1,), (1,)), ((), ())), preferred_element_type=F32)


def _dot_tn(a, b):
    return lax.dot_general(a, b, (((0,), (0,)), ((), ())), preferred_element_type=F32)


def _rms_mod(x, g, sh, sc):
    ms = jnp.mean(x * x, axis=-1, keepdims=True)
    y = x * lax.rsqrt(ms + RMS_EPS) * g
    return y * (1.0 + sc) + sh


def _resident(shape):
    nd = len(shape)
    return pl.BlockSpec(shape, lambda *_: (0,) * nd, pipeline_mode=pl.Buffered(1))


MOD_TN = 768


def _mod_kernel(cs_ref, w_ref, b_ref, o_ref):
    cs = cs_ref[...]
    s = cs * jax.nn.sigmoid(cs)
    w = w_ref[0]
    r0 = jnp.sum(s[:, 0:1] * w, axis=0, keepdims=True)
    r1 = jnp.sum(s[:, 1:2] * w, axis=0, keepdims=True)
    o_ref[0] = jnp.concatenate([r0, r1], axis=0) + b_ref[0]


def _mod_vectors(c, c_ctx, ada_w, ada_b):
    cs = jnp.stack([c[0], c_ctx], axis=1)
    n = 6 * D
    return pl.pallas_call(
        _mod_kernel,
        grid=(DEPTH, n // MOD_TN),
        in_specs=[
            pl.BlockSpec((D, 2), lambda l, j: (0, 0)),
            pl.BlockSpec((1, D, MOD_TN), lambda l, j: (l, 0, j)),
            pl.BlockSpec((1, 1, MOD_TN), lambda l, j: (l, 0, j)),
        ],
        out_specs=pl.BlockSpec((1, 2, MOD_TN), lambda l, j: (l, 0, j)),
        out_shape=jax.ShapeDtypeStruct((DEPTH, 2, n), F32),
        compiler_params=_params(("arbitrary", "arbitrary")),
        name="mod_vectors",
    )(cs, ada_w, ada_b.reshape(DEPTH, 1, n))


def _mod_spec():
    return pl.BlockSpec((1, 8, D), lambda i, *_: (i // NT_LAT, 0, 0))


def _stream_tile(x_ref, ctx_ref):
    return jnp.where(pl.program_id(0) >= NT_LAT, ctx_ref[...], x_ref[...])


def _stream_specs():
    return [pl.BlockSpec((TM, D), lambda i: (jnp.minimum(i, NT_LAT - 1), 0)),
            pl.BlockSpec((CTX, D), lambda i: (0, 0))]


def _inproj0_kernel(x_ref, ctx_ref, mod_ref, g_ref, w_ref, d64_ref, qkv_ref, br_ref, bi_ref):
    h = _stream_tile(x_ref, ctx_ref)
    xm = _rms_mod(h, g_ref[...], mod_ref[0, 0:1, :], mod_ref[0, 1:2, :]).astype(BF16)
    p = _dot(xm, w_ref[...])
    qkv_ref[...] = p[:, FNET_W:].astype(BF16)
    b = _dot(p[:, :FNET_W].astype(BF16), d64_ref[...])
    br_ref[...] = b[:, :FNET_W].astype(BF16)
    bi_ref[...] = b[:, FNET_W:].astype(BF16)


def _inproj0(x, ctx, modv, g, w_in, d64):
    return pl.pallas_call(
        _inproj0_kernel,
        grid=(NT,),
        in_specs=_stream_specs() + [
            _mod_spec(),
            _resident((1, D)),
            _resident((D, AB_IN)),
            _resident((FNET_W, 2 * FNET_W)),
        ],
        out_specs=[
            pl.BlockSpec((TM, QKV_W), lambda i: (i, 0)),
            pl.BlockSpec((TM, FNET_W), lambda i: (i, 0)),
            pl.BlockSpec((TM, FNET_W), lambda i: (i, 0)),
        ],
        out_shape=[
            jax.ShapeDtypeStruct((T, QKV_W), BF16),
            jax.ShapeDtypeStruct((T, FNET_W), BF16),
            jax.ShapeDtypeStruct((T, FNET_W), BF16),
        ],
        compiler_params=_params(("arbitrary",)),
        name="inproj0",
    )(x, ctx, modv, g, w_in, d64)


FFT_A_NB = 8
FFT_B_TN = 4096


def _fft_a_kernel(br_ref, bi_ref, ga_ref, v_ref):
    for u in range(FFT_A_NB):
        cols = slice(u * FNET_W, (u + 1) * FNET_W)
        z = _dot(ga_ref[u, :, 0:FFT_N2], br_ref[:, cols]) + _dot(ga_ref[u, :, FFT_N2:], bi_ref[:, cols])
        v_ref[0, u] = z[:FFT_N2].astype(BF16)
        v_ref[1, u] = z[FFT_N2:].astype(BF16)


def _fft_b_kernel(cs_ref, v_ref, o_ref):
    o_ref[...] = _dot(cs_ref[...], v_ref[...]) * (1.0 / math.sqrt(SEQ * FNET_GD))


def _fft_latent(br, bi, ga, csb):
    width = FFT_N1 * FNET_W
    br2 = br.reshape(T // FFT_N1, width)
    bi2 = bi.reshape(T // FFT_N1, width)
    v = pl.pallas_call(
        _fft_a_kernel,
        grid=(FFT_N1 // FFT_A_NB,),
        in_specs=[
            pl.BlockSpec((FFT_N2, FFT_A_NB * FNET_W), lambda j: (0, j)),
            pl.BlockSpec((FFT_N2, FFT_A_NB * FNET_W), lambda j: (0, j)),
            pl.BlockSpec((FFT_A_NB, 2 * FFT_N2, 2 * FFT_N2), lambda j: (j, 0, 0)),
        ],
        out_specs=pl.BlockSpec((2, FFT_A_NB, FFT_N2, FNET_W), lambda j: (0, j, 0, 0)),
        out_shape=jax.ShapeDtypeStruct((2, FFT_N1, FFT_N2, FNET_W), BF16),
        compiler_params=_params(("arbitrary",)),
        name="fft_stage_a",
    )(br2, bi2, ga)
    v2 = v.reshape(2 * FFT_N1, FFT_N2 * FNET_W)
    x = pl.pallas_call(
        _fft_b_kernel,
        grid=(FFT_N2 * FNET_W // FFT_B_TN,),
        in_specs=[
            _resident((FFT_N1, 2 * FFT_N1)),
            pl.BlockSpec((2 * FFT_N1, FFT_B_TN), lambda j: (0, j)),
        ],
        out_specs=pl.BlockSpec((FFT_N1, FFT_B_TN), lambda j: (0, j)),
        out_shape=jax.ShapeDtypeStruct((FFT_N1, FFT_N2 * FNET_W), F32),
        compiler_params=_params(("arbitrary",)),
        name="fft_stage_b",
    )(csb, v2)
    return x.reshape(SEQ, FNET_W)


def _head_pair_masks():
    lane = lax.broadcasted_iota(jnp.int32, (1, LANES), 1)
    return lane < NA_HD


def _ctx_kernel(qkv_ref, br_ref, bi_ref, cs_ref, o_ref):
    a = _dot(cs_ref[:, 0:CTX], br_ref[...]) + _dot(cs_ref[:, CTX:], bi_ref[...])
    o_ref[:, 0:FNET_W] = (a * (1.0 / math.sqrt(CTX * FNET_GD))).astype(BF16)
    m0 = _head_pair_masks()
    for hp in range(NA_HEADS // 2):
        q = qkv_ref[:, hp * LANES:(hp + 1) * LANES]
        k = qkv_ref[:, NA_W + hp * LANES:NA_W + (hp + 1) * LANES]
        v = qkv_ref[:, 2 * NA_W + hp * LANES:2 * NA_W + (hp + 1) * LANES]
        outs = []
        for a_ in range(2):
            qa = jnp.where(m0 if a_ == 0 else jnp.logical_not(m0), q, jnp.zeros_like(q))
            s = _dot_nt(qa, k) * (NA_HD ** -0.5)
            p = jnp.exp(s - jnp.max(s, axis=-1, keepdims=True))
            l = jnp.sum(p, axis=-1, keepdims=True)
            outs.append(_dot(p.astype(BF16), v) / l)
        o_ref[:, FNET_W + hp * LANES:FNET_W + (hp + 1) * LANES] = jnp.where(m0, outs[0], outs[1]).astype(BF16)


def _ctx_mixer(qkv, br, bi, csc):
    return pl.pallas_call(
        _ctx_kernel,
        grid=(1,),
        in_specs=[
            pl.BlockSpec((CTX, QKV_W), lambda i: (SEQ // CTX, 0)),
            pl.BlockSpec((CTX, FNET_W), lambda i: (SEQ // CTX, 0)),
            pl.BlockSpec((CTX, FNET_W), lambda i: (SEQ // CTX, 0)),
            pl.BlockSpec((CTX, 2 * CTX), lambda i: (0, 0)),
        ],
        out_specs=pl.BlockSpec((CTX, D), lambda i: (0, 0)),
        out_shape=jax.ShapeDtypeStruct((CTX, D), BF16),
        compiler_params=_params(("arbitrary",)),
        name="ctx_mixer",
    )(qkv, br, bi, csc)


NA_RB = 16
NA_WIN = NA_KH * GRID_W


NA_NK = NA_WIN + CTX
NA_SUB = 16


def _na_kernel(q_ref, k_ref, v_ref, kc_ref, vc_ref, bias_ref, o_ref, s_scr, p_scr, l_scr):
    b = pl.program_id(1)
    m0 = _head_pair_masks()
    nm0 = jnp.logical_not(m0)
    scale = NA_HD ** -0.5

    def window(i):
        r = b * NA_RB + i
        rs = jnp.clip(r - NA_KH // 2, 0, ROWS - NA_KH)
        return pl.multiple_of(rs * GRID_W, GRID_W), rs - r + (NA_KH - 1)

    def scores(i):
        start, _ = window(i)
        q = q_ref[i * GRID_W:(i + 1) * GRID_W, :] * scale
        qs = jnp.concatenate([jnp.where(m0, q, jnp.zeros_like(q)), jnp.where(nm0, q, jnp.zeros_like(q))], axis=0)
        s_scr[i, :, 0:NA_WIN] = _dot_nt(qs, k_ref[pl.ds(start, NA_WIN), :])
        s_scr[i, :, NA_WIN:NA_NK] = _dot_nt(qs, kc_ref[...])

    def softmax(i):
        _, e = window(i)
        for g in range(2 * GRID_W // NA_SUB):
            rows = slice(g * NA_SUB, (g + 1) * NA_SUB)
            a, c0 = divmod(g * NA_SUB, GRID_W)
            s1 = s_scr[i, rows, 0:NA_WIN] + bias_ref[a, e, c0:c0 + NA_SUB, :]
            s2 = s_scr[i, rows, NA_WIN:NA_NK]
            mx = jnp.maximum(jnp.max(s1, axis=-1, keepdims=True), jnp.max(s2, axis=-1, keepdims=True))
            p1 = jnp.exp(s1 - mx)
            p2 = jnp.exp(s2 - mx)
            l = jnp.sum(p1, axis=-1, keepdims=True) + jnp.sum(p2, axis=-1, keepdims=True)
            p_scr[i, rows, 0:NA_WIN] = p1.astype(BF16)
            p_scr[i, rows, NA_WIN:NA_NK] = p2.astype(BF16)
            l_scr[i, rows, :] = jnp.broadcast_to(l, (NA_SUB, LANES))

    def values(i):
        start, _ = window(i)
        o = _dot(p_scr[i, :, 0:NA_WIN], v_ref[pl.ds(start, NA_WIN), :]) + _dot(p_scr[i, :, NA_WIN:NA_NK], vc_ref[...])
        o = o / l_scr[i]
        o_ref[i * GRID_W:(i + 1) * GRID_W, :] = jnp.where(m0, o[0:GRID_W], o[GRID_W:]).astype(BF16)

    for i in range(NA_RB):
        scores(i)
    for i in range(NA_RB):
        softmax(i)
        values(i)


def _na_attention(qkv, bias):
    nq = NA_RB * GRID_W
    nhp = NA_HEADS // 2
    return pl.pallas_call(
        _na_kernel,
        grid=(nhp, ROWS // NA_RB),
        in_specs=[
            pl.BlockSpec((nq, LANES), lambda hp, b: (b, hp)),
            pl.BlockSpec((SEQ, LANES), lambda hp, b: (0, nhp + hp)),
            pl.BlockSpec((SEQ, LANES), lambda hp, b: (0, 2 * nhp + hp)),
            pl.BlockSpec((CTX, LANES), lambda hp, b: (SEQ // CTX, nhp + hp)),
            pl.BlockSpec((CTX, LANES), lambda hp, b: (SEQ // CTX, 2 * nhp + hp)),
            pl.BlockSpec((2, NA_KH, GRID_W, NA_WIN), lambda hp, b: (hp, 0, 0, 0)),
        ],
        out_specs=pl.BlockSpec((nq, LANES), lambda hp, b: (b, hp)),
        out_shape=jax.ShapeDtypeStruct((SEQ, NA_W), BF16),
        scratch_shapes=[
            pltpu.VMEM((NA_RB, 2 * GRID_W, NA_NK), F32),
            pltpu.VMEM((NA_RB, 2 * GRID_W, NA_NK), BF16),
            pltpu.VMEM((NA_RB, 2 * GRID_W, LANES), F32),
        ],
        compiler_params=_params(("arbitrary", "arbitrary")),
        name="na_attention",
    )(qkv, qkv, qkv, qkv, qkv, bias)


def _na_bias_rows(rpb):
    c = np.arange(GRID_W)[:, None]
    kc = np.arange(GRID_W)[None, :]
    ws = np.clip(c - NA_KW // 2, 0, GRID_W - NA_KW)
    col_ok = (kc >= ws) & (kc < ws + NA_KW)
    dc = np.clip(kc - c + NA_KW - 1, 0, 2 * NA_KW - 2)
    pick = (dc[None] == np.arange(2 * NA_KW - 1)[:, None, None]).astype(np.float32)
    full = jnp.einsum("hrd,dck->hrck", rpb.astype(F32), jnp.asarray(pick), precision=lax.Precision.HIGHEST)
    full = jnp.where(jnp.asarray(col_ok)[None, None], full, NEG_BIG)
    rows = [jnp.transpose(full[:, e:e + NA_KH], (0, 2, 1, 3)).reshape(NA_HEADS, GRID_W, NA_WIN)
            for e in range(NA_KH)]
    return jnp.stack(rows, axis=1)


def _route_stage(xm2, valid, rw_ref, rb_ref, triu_ref, run_ref, xrow_ref, meta_ref, cnt_ref):
    xrow_ref[:, 0:D] = xm2
    xh = xm2.astype(BF16)
    xl = (xm2 - xh.astype(F32)).astype(BF16)
    w = rw_ref[...]
    wh = w.astype(BF16)
    wl = (w - wh.astype(F32)).astype(BF16)
    logits = _dot(xh, wh) + (_dot(xl, wh) + _dot(xh, wl))
    scores = jax.nn.sigmoid(logits.T[0:N_EXPERTS, :])
    biased = scores + rb_ref[...]
    s = [scores[e:e + 1, :] for e in range(N_EXPERTS)]
    b = [biased[e:e + 1, :] for e in range(N_EXPERTS)]
    n_groups = N_EXPERTS // EPG
    top2, group_score = [], []
    for g in range(n_groups):
        m = b[EPG * g:EPG * (g + 1)]
        total = None
        for k in range(EPG):
            rank = None
            for j in range(EPG):
                if j != k:
                    beats = jnp.where((m[j] >= m[k]) if j < k else (m[j] > m[k]), 1.0, 0.0)
                    rank = beats if rank is None else rank + beats
            top2.append(rank < 2.0)
            term = jnp.where(top2[-1], m[k], 0.0)
            total = term if total is None else total + term
        group_score.append(total)
    best, g_sel = group_score[0], jnp.zeros((1, TM), jnp.int32)
    for g in range(1, n_groups):
        better = group_score[g] > best
        g_sel = jnp.where(better, g, g_sel)
        best = jnp.where(better, group_score[g], best)
    sel = [top2[e] & (g_sel == e // EPG) for e in range(N_EXPERTS)]
    w_sum = None
    for e in range(N_EXPERTS):
        term = jnp.where(sel[e], s[e], 0.0)
        w_sum = term if w_sum is None else w_sum + term
    zero = jnp.zeros((1, TM), F32)
    w_lo, w_hi, bid = zero, zero, zero
    for g in range(n_groups):
        seen = None
        for k in range(EPG):
            e = EPG * g + k
            if seen is None:
                w_lo = w_lo + jnp.where(sel[e], s[e], 0.0)
                seen = sel[e]
            else:
                w_lo = w_lo + jnp.where(sel[e] & jnp.logical_not(seen), s[e], 0.0)
                w_hi = w_hi + jnp.where(sel[e] & seen, s[e], 0.0)
                seen = seen | sel[e]
            bid = bid + jnp.where(sel[e], float((1 << k) + 8 * g), 0.0)
    w_lo = w_lo / w_sum
    w_hi = w_hi / w_sum
    bucket = lax.broadcasted_iota(jnp.int32, (N_BUCKET_IDS, TM), 0).astype(F32)
    onehot = bucket == bid
    ones = jnp.where(onehot & valid, 1.0, 0.0)
    earlier = _dot(ones.astype(BF16), triu_ref[...])
    run = run_ref[...]
    run_t = jnp.concatenate([run] * (TM // LANES), axis=1)
    rank_row = jnp.sum(jnp.where(onehot, earlier + run_t, 0.0), axis=0, keepdims=True)
    new_run = run + jnp.sum(ones, axis=1, keepdims=True)
    run_ref[...] = new_run
    cnt_ref[...] = new_run
    meta_rows = jnp.concatenate([w_lo, w_hi, bid, rank_row, jnp.zeros((SUBLANES - 4, TM), F32)], axis=0)
    meta_ref[...] = meta_rows
    padded = jnp.concatenate([meta_rows, jnp.zeros((LANES - SUBLANES, TM), F32)], axis=0)
    xrow_ref[:, D:XROW_W] = padded.T


def _epilogue_specs(ntiles):
    in_specs = [
        _resident((1, D)),
        _resident((D, LANES)),
        _resident((N_EXPERTS, TM)),
        _resident((TM, TM)),
    ]
    out_specs = [
        pl.BlockSpec((TM, D), lambda i: (jnp.minimum(i, ntiles - 1), 0)),
        pl.BlockSpec((TM, XROW_W), lambda i: (jnp.maximum(i - 1, 0), 0)),
        pl.BlockSpec((SUBLANES, TM), lambda i: (0, jnp.maximum(i - 1, 0))),
        pl.BlockSpec((N_BUCKET_IDS, LANES), lambda i: (0, 0)),
    ]
    n = ntiles * TM
    out_shape = [
        jax.ShapeDtypeStruct((n, D), F32),
        jax.ShapeDtypeStruct((n, XROW_W), F32),
        jax.ShapeDtypeStruct((SUBLANES, n), F32),
        jax.ShapeDtypeStruct((N_BUCKET_IDS, LANES), F32),
    ]
    return in_specs, out_specs, out_shape


_EPILOGUE_SCRATCH = [pltpu.VMEM((N_BUCKET_IDS, LANES), F32), pltpu.VMEM((2, TM, D), F32)]


def _two_stage(residual_tile, mod_ref, g2_ref, rw_ref, rb_ref, triu_ref,
               h1_ref, xrow_ref, meta_ref, cnt_ref, run_ref, stash):
    i = pl.program_id(0)

    @pl.when(i == 0)
    def _():
        run_ref[...] = jnp.zeros_like(run_ref)
        stash[1] = jnp.zeros((TM, D), F32)

    _route_stage(stash[1 - i % 2], i >= 1, rw_ref, rb_ref, triu_ref, run_ref, xrow_ref, meta_ref, cnt_ref)
    h1 = residual_tile()
    h1_ref[...] = h1
    stash[i % 2] = _rms_mod(h1, g2_ref[...], mod_ref[0, 3:4, :], mod_ref[0, 4:5, :])


def _outproj0_kernel(a_ref, o_ref, aoc_ref, x_ref, ctx_ref, mod_ref, w_ref, *rest):
    def residual_tile():
        is_ctx = pl.program_id(0) >= NT_LAT
        a = jnp.where(is_ctx, aoc_ref[:, 0:FNET_W], a_ref[...].astype(BF16))
        o = jnp.where(is_ctx, aoc_ref[:, FNET_W:], o_ref[...])
        y = _dot(a, w_ref[0:FNET_W, :]) + _dot(o, w_ref[FNET_W:, :])
        return _stream_tile(x_ref, ctx_ref) + mod_ref[0, 2:3, :] * y

    _two_stage(residual_tile, mod_ref, *rest)


def _outproj0(a_lat, o_lat, ao_ctx, x, ctx, modv, w_out, g2, rw, rb, triu):
    e_in, e_out, e_shape = _epilogue_specs(NT)
    lat = lambda w: pl.BlockSpec((TM, w), lambda i: (jnp.minimum(i, NT_LAT - 1), 0))
    return pl.pallas_call(
        _outproj0_kernel,
        grid=(NT + 1,),
        in_specs=[lat(FNET_W), lat(NA_W), pl.BlockSpec((CTX, D), lambda i: (0, 0))] + _stream_specs() + [
            _mod_spec(),
            _resident((D, D)),
        ] + e_in,
        out_specs=e_out,
        out_shape=e_shape,
        scratch_shapes=_EPILOGUE_SCRATCH,
        compiler_params=_params(("arbitrary",)),
        name="outproj0",
    )(a_lat, o_lat, ao_ctx, x, ctx, modv, w_out, g2, rw, rb, triu)


def _bucket_tables():
    lo = np.zeros(N_BUCKET_IDS, np.int32)
    hi = np.zeros(N_BUCKET_IDS, np.int32)
    for g in range(N_EXPERTS // EPG):
        for a in range(EPG):
            for b in range(a + 1, EPG):
                i = 16 * g + (1 << a) + (1 << b)
                lo[i] = EPG * g + a
                hi[i] = EPG * g + b
    return lo, hi


def _slot_kernel(meta_ref, starts_ref, o_ref):
    n = meta_ref.shape[1]
    bucket = lax.broadcasted_iota(jnp.int32, (N_BUCKET_IDS, n), 0).astype(F32)
    first = jnp.sum(jnp.where(bucket == meta_ref[2:3, :], starts_ref[:, 0:1], 0.0), axis=0, keepdims=True)
    o_ref[...] = jnp.broadcast_to((first + meta_ref[3:4, :]).astype(jnp.int32), (SUBLANES, n))


def _token_slots(meta, first_row):
    n = meta.shape[1]
    out = pl.pallas_call(
        _slot_kernel,
        grid=(1,),
        in_specs=[pl.BlockSpec((SUBLANES, n), lambda i: (0, 0)),
                  pl.BlockSpec((N_BUCKET_IDS, LANES), lambda i: (0, 0))],
        out_specs=pl.BlockSpec((SUBLANES, n), lambda i: (0, 0)),
        out_shape=jax.ShapeDtypeStruct((SUBLANES, n), jnp.int32),
        compiler_params=_params(("arbitrary",)),
        name="token_slots",
    )(meta, first_row)
    return out[0]


def _route_plan(meta, cnt, n_tiles, layer):
    counts = cnt[:, 0].astype(jnp.int32)
    nt_b = (counts + TMM - 1) // TMM
    ends = jnp.cumsum(nt_b)
    starts = ends - nt_b
    first_row = jnp.broadcast_to((starts * TMM).astype(F32)[:, None], (N_BUCKET_IDS, LANES))
    pos = _token_slots(meta, first_row)
    n_used = ends[-1]
    tile = jnp.minimum(jnp.arange(n_tiles, dtype=jnp.int32), n_used - 1)
    tile_b = jnp.sum((ends[None, :] <= tile[:, None]).astype(jnp.int32), axis=1)
    lo, hi = (jnp.asarray(t) + layer * N_EXPERTS for t in _bucket_tables())
    ids = jnp.arange(n_tiles, dtype=jnp.int32)
    first = (tile_b != jnp.concatenate([jnp.full((1,), -1, jnp.int32), tile_b[:-1]])) & (ids < n_used)
    wslot = (jnp.cumsum(first.astype(jnp.int32)) - 1) % 2
    next_tile = ends[tile_b]
    has_next = next_tile < n_used
    next_b = tile_b[jnp.minimum(next_tile, n_tiles - 1)]
    plan = jnp.stack([lo[tile_b], hi[tile_b], first.astype(jnp.int32), wslot,
                      jnp.where(has_next, lo[next_b], -1), jnp.where(has_next, hi[next_b], -1)])
    fill = ((ids >= n_used) | (ids == next_tile - 1)).astype(jnp.int32)
    return pos, fill, plan.astype(jnp.int32), n_used.reshape(1).astype(jnp.int32)


def _dispatch_kernel(pos_ref, fill_ref, x_ref, xs_ref, sem, zeros, zsem):
    base = pl.program_id(0) * TM
    tile_rows = TMM // SUBLANES

    @pl.when(pl.program_id(0) == 0)
    def _():
        zeros[...] = jnp.zeros_like(zeros)

        def fill(t):
            return pltpu.make_async_copy(zeros, xs_ref.at[pl.ds(t * tile_rows, tile_rows)], zsem)

        def start(t, carry):
            @pl.when(fill_ref[t] == 1)
            def _():
                fill(t).start()
            return carry

        def wait(t, carry):
            @pl.when(fill_ref[t] == 1)
            def _():
                fill(t).wait()
            return carry

        lax.fori_loop(0, fill_ref.shape[0], start, 0)
        lax.fori_loop(0, fill_ref.shape[0], wait, 0)

    for j in range(TM // SUBLANES):
        for u in range(SUBLANES):
            p = pos_ref[base + j * SUBLANES + u]
            pltpu.make_async_copy(x_ref.at[j, pl.ds(u, 1), :],
                                  xs_ref.at[p >> 3, pl.ds(p & (SUBLANES - 1), 1), :], sem).start()
    pltpu.make_async_copy(x_ref, xs_ref.at[pl.ds(0, TM // SUBLANES)], sem).wait()


def _dispatch(pos, fill, xrow, n_tiles):
    ntok = xrow.shape[0]
    xs = pl.pallas_call(
        _dispatch_kernel,
        grid_spec=pltpu.PrefetchScalarGridSpec(
            num_scalar_prefetch=2,
            grid=(ntok // TM,),
            in_specs=[pl.BlockSpec((TM // SUBLANES, SUBLANES, XROW_W), lambda i, pos, fill: (i, 0, 0))],
            out_specs=pl.BlockSpec(memory_space=pl.ANY),
            scratch_shapes=[
                pltpu.SemaphoreType.DMA(()),
                pltpu.VMEM((TMM // SUBLANES, SUBLANES, XROW_W), F32),
                pltpu.SemaphoreType.DMA(()),
            ],
        ),
        out_shape=jax.ShapeDtypeStruct((n_tiles * TMM // SUBLANES, SUBLANES, XROW_W), F32),
        compiler_params=_params(("arbitrary",)),
        name="moe_dispatch",
    )(pos, fill, _by_sublane_tile(xrow))
    return xs.reshape(n_tiles * TMM, XROW_W)


def _moe_kernel(plan_ref, nu_ref, xs_ref, w1_hbm, w3_hbm, w2_hbm, ys_ref, wb1, wb3, wb2, sem):
    j = pl.program_id(0)
    slot = plan_ref[3, j]

    def weight_copies(e_lo, e_hi, s):
        out = []
        for k, e in enumerate((e_lo, e_hi)):
            out.append(pltpu.make_async_copy(w1_hbm.at[e], wb1.at[s, k], sem.at[s, 3 * k]))
            out.append(pltpu.make_async_copy(w3_hbm.at[e], wb3.at[s, k], sem.at[s, 3 * k + 1]))
            out.append(pltpu.make_async_copy(w2_hbm.at[e], wb2.at[s, k], sem.at[s, 3 * k + 2]))
        return out

    @pl.when(j == 0)
    def _():
        for c in weight_copies(plan_ref[0, 0], plan_ref[1, 0], 0):
            c.start()

    @pl.when(plan_ref[2, j] == 1)
    def _():
        for c in weight_copies(plan_ref[0, j], plan_ref[1, j], slot):
            c.wait()

        @pl.when(plan_ref[4, j] >= 0)
        def _():
            for c in weight_copies(plan_ref[4, j], plan_ref[5, j], 1 - slot):
                c.start()

    @pl.when(j < nu_ref[0])
    def _():
        x = xs_ref[:, 0:D].astype(BF16)
        g_lo = xs_ref[:, D:D + 1]
        g_hi = xs_ref[:, D + 1:D + 2]

        def expert(k):
            a = _dot(x, wb1[slot, k].astype(BF16))
            hid = (a * jax.nn.sigmoid(a)) * _dot(x, wb3[slot, k].astype(BF16))
            return _dot(hid.astype(BF16), wb2[slot, k].astype(BF16))

        ys_ref[...] = g_lo * expert(0) + g_hi * expert(1)

    @pl.when(j >= nu_ref[0])
    def _():
        ys_ref[...] = jnp.zeros_like(ys_ref)


def _moe(plan, n_used, xs, w1, w3, w2, n_tiles):
    hbm = pl.BlockSpec(memory_space=pl.ANY)
    return pl.pallas_call(
        _moe_kernel,
        grid_spec=pltpu.PrefetchScalarGridSpec(
            num_scalar_prefetch=2,
            grid=(n_tiles,),
            in_specs=[pl.BlockSpec((TMM, XROW_W), lambda j, plan, nu: (j, 0)), hbm, hbm, hbm],
            out_specs=pl.BlockSpec((TMM, D), lambda j, plan, nu: (j, 0)),
            scratch_shapes=[
                pltpu.VMEM((2, 2, D, D_EXPERT), F32),
                pltpu.VMEM((2, 2, D, D_EXPERT), F32),
                pltpu.VMEM((2, 2, D_EXPERT, D), F32),
                pltpu.SemaphoreType.DMA((2, 6)),
            ],
        ),
        out_shape=jax.ShapeDtypeStruct((n_tiles * TMM, D), F32),
        compiler_params=_params(("arbitrary",)),
        name="moe_experts",
    )(plan, n_used, xs, w1, w3, w2)


def _gather_issue(pos_ref, ys_hbm, buf, sem, tile, slot, inline):
    base = tile * TM

    def issue(j, carry):
        for u in range(SUBLANES):
            p = pos_ref[base + j * SUBLANES + u]
            pltpu.make_async_copy(ys_hbm.at[p >> 3, pl.ds(p & (SUBLANES - 1), 1), :],
                                  buf.at[slot, j, pl.ds(u, 1), :], sem.at[slot]).start()
        return carry

    if inline:
        for j in range(TM // SUBLANES):
            issue(j, 0)
    else:
        lax.fori_loop(0, TM // SUBLANES, issue, 0)


def _gather_wait(ys_hbm, buf, sem, slot):
    pltpu.make_async_copy(ys_hbm.at[pl.ds(0, TM // SUBLANES)], buf.at[slot], sem.at[slot]).wait()


def _gathered_rows(pos_ref, ys_hbm, buf, sem, request_next):
    i = pl.program_id(0)
    n_slots = buf.shape[0]
    slot = i % n_slots

    @pl.when(i == 0)
    def _():
        for t in range(1 if request_next else n_slots - 1):
            _gather_issue(pos_ref, ys_hbm, buf, sem, t, t, inline=False)

    if request_next:
        @pl.when(i + 1 < pl.num_programs(0))
        def _():
            _gather_issue(pos_ref, ys_hbm, buf, sem, i + 1, 1 - slot, inline=True)

    _gather_wait(ys_hbm, buf, sem, slot)
    return buf[slot].reshape(TM, D)


def _gather_request_ahead(pos_ref, ys_hbm, buf, sem):
    i = pl.program_id(0)
    n = pl.num_programs(0)
    ahead = buf.shape[0] - 1
    tile = i + ahead
    _gather_issue(pos_ref, ys_hbm, buf, sem, jnp.where(tile < n, tile, tile - n), tile % buf.shape[0], inline=True)


def _gather_drain(ys_hbm, buf, sem):
    i = pl.program_id(0)

    @pl.when(i + 1 == pl.num_programs(0))
    def _():
        for k in range(1, buf.shape[0]):
            _gather_wait(ys_hbm, buf, sem, (i + k) % buf.shape[0])


_GATHER_SCRATCH = [pltpu.VMEM((2, TM // SUBLANES, SUBLANES, D), F32), pltpu.SemaphoreType.DMA((2,))]


def _by_sublane_tile(a):
    return a.reshape(a.shape[0] // SUBLANES, SUBLANES, a.shape[1])


def _rope_store(x, tabs, out_ref):
    cr, sr, cc, sc = tabs
    for ch in range(RET_QK // LANES):
        xc = x[:, ch * LANES:(ch + 1) * LANES]
        xr = pltpu.roll(xc, LANES // 2, axis=1)
        y = xc * cr + xr * sr if ch % 2 == 0 else xc * cc + xr * sc
        out_ref[:, ch * LANES:(ch + 1) * LANES] = y.astype(BF16)


def _inproj1_kernel(pos_ref, h_ref, ys_hbm, mod0_ref, mod1_ref, g_ref, w_ref, rt_ref, ct_ref,
                    h2_ref, q_ref, k_ref, v_ref, gg_ref, buf, sem, xm_scr):
    f = _gathered_rows(pos_ref, ys_hbm, buf, sem, request_next=False)
    h2 = h_ref[...] + mod0_ref[0, 5:6, :] * f
    h2_ref[...] = h2
    xm_scr[...] = _rms_mod(h2, g_ref[...], mod1_ref[0, 0:1, :], mod1_ref[0, 1:2, :]).astype(BF16)
    _gather_request_ahead(pos_ref, ys_hbm, buf, sem)
    xm = xm_scr[...]

    off = (pl.program_id(0) % (SUBLANES // ROPE_TILE_ROWS)) * ROPE_TILE_ROWS

    def row_table(which):
        return jnp.concatenate([jnp.broadcast_to(rt_ref[which, pl.ds(off + j, 1), :], (GRID_W, LANES))
                                for j in range(ROPE_TILE_ROWS)], axis=0)

    tabs = (row_table(0), row_table(1), ct_ref[0, 0], ct_ref[0, 1])
    _rope_store(_dot(xm, w_ref[:, 0:RET_QK]), tabs, q_ref)
    _rope_store(_dot(xm, w_ref[:, RET_QK:2 * RET_QK]) * (RET_DK ** -0.5), tabs, k_ref)
    v_ref[...] = _dot(xm, w_ref[:, 2 * RET_QK:2 * RET_QK + RET_V]).astype(BF16)
    gg_ref[...] = _dot(xm, w_ref[:, 2 * RET_QK + RET_V:]).astype(BF16)
    _gather_drain(ys_hbm, buf, sem)


def _inproj1(pos, h1, ys, modv0, modv1, g, w_in, tabs):
    tok = lambda w: pl.BlockSpec((TM, w), lambda i, pos: (i, 0))
    res = lambda shape: pl.BlockSpec(shape, lambda i, pos: (0,) * len(shape), pipeline_mode=pl.Buffered(1))
    mod = pl.BlockSpec((1, 8, D), lambda i, pos: (i // NT_LAT, 0, 0))
    row_tab = pl.BlockSpec((2, SUBLANES, LANES), lambda i, pos: (0, i // (SUBLANES // ROPE_TILE_ROWS), 0))
    col_tab = pl.BlockSpec((1, 2, TM, LANES), lambda i, pos: (i // NT_LAT, 0, 0, 0))
    return pl.pallas_call(
        _inproj1_kernel,
        grid_spec=pltpu.PrefetchScalarGridSpec(
            num_scalar_prefetch=1,
            grid=(NT,),
            in_specs=[tok(D), pl.BlockSpec(memory_space=pl.ANY), mod, mod, res((1, D)), res((D, 2 * RET_QK + 2 * RET_V)),
                      row_tab, col_tab],
            out_specs=[tok(D), tok(RET_QK), tok(RET_QK), tok(RET_V), tok(RET_V)],
            scratch_shapes=[pltpu.VMEM((3, TM // SUBLANES, SUBLANES, D), F32), pltpu.SemaphoreType.DMA((3,)),
                            pltpu.VMEM((TM, D), BF16)],
        ),
        out_shape=[
            jax.ShapeDtypeStruct((T, D), F32),
            jax.ShapeDtypeStruct((T, RET_QK), BF16),
            jax.ShapeDtypeStruct((T, RET_QK), BF16),
            jax.ShapeDtypeStruct((T, RET_V), BF16),
            jax.ShapeDtypeStruct((T, RET_V), BF16),
        ],
        compiler_params=_params(("arbitrary",)),
        name="inproj1",
    )(pos, h1, _by_sublane_tile(ys), modv0, modv1, g, w_in, *tabs)


ROPE_TILE_ROWS = TM // GRID_W


def _rope_tables():
    n = LANES // 2
    inv_freq = ROPE_BASE ** (-np.arange(n, dtype=np.float64) / n)

    def lines(p):
        ang = p[:, None] * inv_freq[None, :]
        return np.stack([np.concatenate([np.cos(ang), np.cos(ang)], axis=1),
                         np.concatenate([-np.sin(ang), np.sin(ang)], axis=1)])

    ident = np.stack([np.ones((TM, LANES)), np.zeros((TM, LANES))])
    row_tab = np.concatenate([lines(np.arange(ROWS, dtype=np.float64)), ident[:, :SUBLANES]], axis=1)
    col_lat = lines(np.tile(np.arange(GRID_W, dtype=np.float64), ROPE_TILE_ROWS))
    col_tab = np.stack([col_lat, ident])
    return jnp.asarray(row_tab, dtype=F32), jnp.asarray(col_tab, dtype=F32)


RET_NCHUNK = T // RET_CHUNK
RET_CTX_CHUNKS = CTX // RET_CHUNK
RET_LAT_CHUNKS = SEQ // RET_CHUNK
RET_NHD = 2 * RET_HEADS


def _ret_kernel(dec_ref, qf_ref, kf_ref, vf_ref, qb_ref, kb_ref, vb_ref, of_ref, ob_ref,
                state, dm, qd, kd, cd):
    C = RET_CHUNK

    @pl.when(pl.program_id(0) == 0)
    def _():
        state[...] = jnp.zeros_like(state)
        n = lax.broadcasted_iota(jnp.int32, (C, C), 0).astype(F32)
        m = lax.broadcasted_iota(jnp.int32, (C, C), 1).astype(F32)
        for d in range(2):
            for h in range(RET_HEADS):
                idx = d * RET_HEADS + h
                lg = jnp.log1p(-jnp.exp(jnp.full((C, C), dec_ref[d, h], F32)))
                diff = n - m if d == 0 else m - n
                low = diff >= 0.0
                dm[idx] = jnp.where(low, jnp.exp(jnp.where(low, diff, 0.0) * lg), 0.0)
                qpow = n + 1.0 if d == 0 else C - n
                kpow = (C - 1.0) - n if d == 0 else n
                qcol = jnp.exp(qpow * lg)
                kcol = jnp.exp(kpow * lg)
                for c0 in range(0, RET_DV, C):
                    qd[idx, :, c0:c0 + C] = qcol
                for c0 in range(0, RET_DK, C):
                    kd[idx, :, c0:c0 + C] = kcol
                for c0 in range(0, RET_DV, LANES):
                    cd[idx, :, c0:c0 + LANES] = jnp.exp(C * lg[0:8, 0:LANES])

    for d, (q_ref, k_ref, v_ref, o_ref) in enumerate(((qf_ref, kf_ref, vf_ref, of_ref),
                                                       (qb_ref, kb_ref, vb_ref, ob_ref))):
        for h in range(RET_HEADS):
            idx = d * RET_HEADS + h
            q = q_ref[:, h * RET_DK:(h + 1) * RET_DK]
            k = k_ref[:, h * RET_DK:(h + 1) * RET_DK]
            v = v_ref[:, h * RET_DV:(h + 1) * RET_DV]
            st = state[idx]
            s = _dot_nt(q, k) * dm[idx]
            o = _dot(s.astype(BF16), v) + qd[idx] * _dot(q, st.astype(BF16))
            o_ref[:, h * RET_DV:(h + 1) * RET_DV] = o.astype(BF16)
            kdk = (k.astype(F32) * kd[idx]).astype(BF16)
            cdv = cd[idx, 0:1, :]
            state[idx] = st * cdv + _dot(kdk.T, v)


def _ret_scan(decay, q, k, v):
    def fwd(s):
        return jnp.where(s < RET_CTX_CHUNKS, RET_LAT_CHUNKS + s, s - RET_CTX_CHUNKS)

    def bwd(s):
        return RET_NCHUNK - 1 - s

    def spec(w, f):
        return pl.BlockSpec((RET_CHUNK, w), lambda s: (f(s), 0))

    return pl.pallas_call(
        _ret_kernel,
        grid=(RET_NCHUNK,),
        in_specs=[
            pl.BlockSpec(memory_space=pltpu.SMEM),
            spec(RET_QK, fwd), spec(RET_QK, fwd), spec(RET_V, fwd),
            spec(RET_QK, bwd), spec(RET_QK, bwd), spec(RET_V, bwd),
        ],
        out_specs=[spec(RET_V, fwd), spec(RET_V, bwd)],
        out_shape=[jax.ShapeDtypeStruct((T, RET_V), BF16), jax.ShapeDtypeStruct((T, RET_V), BF16)],
        scratch_shapes=[
            pltpu.VMEM((RET_NHD, RET_DK, RET_DV), F32),
            pltpu.VMEM((RET_NHD, RET_CHUNK, RET_CHUNK), F32),
            pltpu.VMEM((RET_NHD, RET_CHUNK, RET_DV), F32),
            pltpu.VMEM((RET_NHD, RET_CHUNK, RET_DK), F32),
            pltpu.VMEM((RET_NHD, 8, RET_DV), F32),
        ],
        compiler_params=_params(("arbitrary",)),
        name="retention_scan",
    )(decay, q, k, v, q, k, v)


def _outproj1_kernel(of_ref, ob_ref, gg_ref, h_ref, mod_ref, w_ref, *rest):
    def residual_tile():
        y = jnp.zeros((TM, D), F32)
        for hh in range(RET_HEADS):
            cols = slice(hh * RET_DV, (hh + 1) * RET_DV)
            o = of_ref[:, cols].astype(F32) + ob_ref[:, cols].astype(F32)
            on = o * lax.rsqrt(jnp.mean(o * o, axis=-1, keepdims=True) + RMS_EPS)
            g = gg_ref[:, cols].astype(F32)
            z = (g * jax.nn.sigmoid(g)) * on
            y = y + _dot(z.astype(BF16), w_ref[cols, :])
        return h_ref[...] + mod_ref[0, 2:3, :] * y

    _two_stage(residual_tile, mod_ref, *rest)


def _outproj1(o_f, o_b, gg, h, modv, w_out, g2, rw, rb, triu):
    e_in, e_out, e_shape = _epilogue_specs(NT_LAT)
    tok = lambda w: pl.BlockSpec((TM, w), lambda i: (jnp.minimum(i, NT_LAT - 1), 0))
    latent_mod = pl.BlockSpec((1, 8, D), lambda i: (0, 0, 0))
    return pl.pallas_call(
        _outproj1_kernel,
        grid=(NT_LAT + 1,),
        in_specs=[tok(RET_V), tok(RET_V), tok(RET_V), tok(D), latent_mod, _resident((RET_V, D))] + e_in,
        out_specs=e_out,
        out_shape=e_shape,
        scratch_shapes=_EPILOGUE_SCRATCH,
        compiler_params=_params(("arbitrary",)),
        name="outproj1",
    )(o_f, o_b, gg, h, modv, w_out, g2, rw, rb, triu)


def _final_kernel(pos_ref, h_ref, ys_hbm, mod_ref, g_ref, o_ref, buf, sem):
    f = _gathered_rows(pos_ref, ys_hbm, buf, sem, request_next=True)
    h = h_ref[...] + mod_ref[0, 5:6, :] * f
    o_ref[...] = h * lax.rsqrt(jnp.mean(h * h, axis=-1, keepdims=True) + RMS_EPS) * g_ref[...]


def _final(pos, h, ys, modv, g):
    tok = pl.BlockSpec((TM, D), lambda i, pos: (i, 0))
    return pl.pallas_call(
        _final_kernel,
        grid_spec=pltpu.PrefetchScalarGridSpec(
            num_scalar_prefetch=1,
            grid=(NT_LAT,),
            in_specs=[tok, pl.BlockSpec(memory_space=pl.ANY),
                      pl.BlockSpec((1, 8, D), lambda i, pos: (0, 0, 0)),
                      pl.BlockSpec((1, D), lambda i, pos: (0, 0))],
            out_specs=tok,
            scratch_shapes=_GATHER_SCRATCH,
        ),
        out_shape=jax.ShapeDtypeStruct((SEQ, D), F32),
        compiler_params=_params(("arbitrary",)),
        name="final_norm",
    )(pos, h, _by_sublane_tile(ys), modv, g)


def _dft_constants():
    c = np.arange(FNET_GD)
    ang = 2.0 * np.pi * np.outer(c, c) / FNET_GD
    d64 = np.zeros((FNET_W, 2 * FNET_W))
    for g in range(FNET_W // FNET_GD):
        sl = slice(g * FNET_GD, (g + 1) * FNET_GD)
        d64[sl, sl] = np.cos(ang)
        d64[sl, FNET_W + g * FNET_GD:FNET_W + (g + 1) * FNET_GD] = -np.sin(ang)
    k2 = np.arange(FFT_N2)[:, None]
    n2 = np.arange(FFT_N2)[None, :]
    ga = np.zeros((FFT_N1, 2 * FFT_N2, 2 * FFT_N2))
    for n1 in range(FFT_N1):
        th = 2.0 * np.pi * ((k2 * (n1 + FFT_N1 * n2)) % SEQ) / SEQ
        cs, sn = np.cos(th), np.sin(th)
        ga[n1] = np.block([[cs, sn], [-sn, cs]])
    k1 = np.arange(FFT_N1)
    ph = 2.0 * np.pi * np.outer(k1, k1) / FFT_N1
    csb = np.concatenate([np.cos(ph), np.sin(ph)], axis=1)
    p = np.arange(CTX)
    pc = 2.0 * np.pi * (np.outer(p, p) % CTX) / CTX
    csc = np.concatenate([np.cos(pc), np.sin(pc)], axis=1)
    as_bf16 = lambda a: jnp.asarray(a, dtype=F32).astype(BF16)
    return as_bf16(d64), as_bf16(ga), as_bf16(csb), as_bf16(csc)


def _mod_table(mod_l):
    t = mod_l.reshape(2, 6, D)
    return jnp.concatenate([t, jnp.zeros((2, 2, D), F32)], axis=1)


def kernel(x, c, ctx, c_ctx, ada_w, ada_b, norm_g, final_norm_g, mixab_w_in, mixab_w_out, na_rpb,
           ret_w_in, ret_w_out, ret_decay, router_w, router_b, moe_w1, moe_w3, moe_w2):
    d64, ga, csb, csc = _dft_constants()
    triu = jnp.asarray(np.triu(np.ones((TM, TM)), 1), dtype=BF16)
    rw = jnp.pad(router_w, ((0, 0), (0, LANES - N_EXPERTS)))
    rb = jnp.broadcast_to(router_b[:, None], (N_EXPERTS, TM))
    w1 = moe_w1.reshape(DEPTH * N_EXPERTS, D, D_EXPERT)
    w3 = moe_w3.reshape(DEPTH * N_EXPERTS, D, D_EXPERT)
    w2 = moe_w2.reshape(DEPTH * N_EXPERTS, D_EXPERT, D)

    mod = _mod_vectors(c, c_ctx, ada_w, ada_b)
    modv0, modv1 = _mod_table(mod[0]), _mod_table(mod[1])

    qkv, br, bi = _inproj0(x[0], ctx[0], modv0, norm_g[0, 0].reshape(1, D), mixab_w_in[0].astype(BF16), d64)
    a_lat = _fft_latent(br, bi, ga, csb)
    ao_ctx = _ctx_mixer(qkv, br, bi, csc)
    o_lat = _na_attention(qkv, _na_bias_rows(na_rpb[0]))
    h1, xrow, meta, cnt = _outproj0(a_lat, o_lat, ao_ctx, x[0], ctx[0], modv0, mixab_w_out[0].astype(BF16),
                                    norm_g[0, 1].reshape(1, D), rw, rb, triu)
    n_tiles0 = NT + N_REAL_BUCKETS
    pos0, fill0, plan0, n_used0 = _route_plan(meta, cnt, n_tiles0, 0)
    ys0 = _moe(plan0, n_used0, _dispatch(pos0, fill0, xrow, n_tiles0), w1, w3, w2, n_tiles0)

    h2, q, k, v, gg = _inproj1(pos0, h1, ys0, modv0, modv1, norm_g[1, 0].reshape(1, D),
                               ret_w_in[0].astype(BF16), _rope_tables())
    o_f, o_b = _ret_scan(ret_decay[0].astype(F32), q, k, v)
    h3, xrow1, meta1, cnt1 = _outproj1(o_f, o_b, gg, h2, modv1, ret_w_out[0].astype(BF16),
                                       norm_g[1, 1].reshape(1, D), rw, rb, triu)
    n_tiles1 = NT_LAT + N_REAL_BUCKETS
    pos1, fill1, plan1, n_used1 = _route_plan(meta1, cnt1, n_tiles1, 1)
    ys1 = _moe(plan1, n_used1, _dispatch(pos1, fill1, xrow1, n_tiles1), w1, w3, w2, n_tiles1)
    out = _final(pos1, h3, ys1, modv1, final_norm_g.reshape(1, D))
    return out[None]
```

```python
import functools
import math

import numpy as np
import jax
import jax.numpy as jnp
from jax import lax
from jax.experimental import pallas as pl
from jax.experimental.pallas import tpu as pltpu

F32 = jnp.float32
BF16 = jnp.bfloat16

D = 1024
SEQ = 16384
CTX = 256
T = SEQ + CTX
DEPTH = 2
GRID_W = 64
ROWS = SEQ // GRID_W
RMS_EPS = 1e-6
ROPE_BASE = 10000.0

FNET_W = 256
FNET_GD = 64
NA_HEADS = 12
NA_HD = 64
NA_W = NA_HEADS * NA_HD
NA_KH = 8
NA_KW = 16
AB_IN = FNET_W + 3 * NA_W
QKV_W = 3 * NA_W

RET_HEADS = 4
RET_DK = 256
RET_DV = 512
RET_QK = RET_HEADS * RET_DK
RET_V = RET_HEADS * RET_DV
RET_CHUNK = 256

N_EXPERTS = 16
EPG = 4
D_EXPERT = 512

LANES = 128
TM = 256
NT = T // TM
NT_LAT = SEQ // TM
TMM = TM
N_BUCKET_IDS = 64
N_REAL_BUCKETS = 24
XROW_W = D + LANES
FFT_N1 = 128
FFT_N2 = 128
NEG_BIG = -1e30
SUBLANES = 8

VMEM_LIMIT = 56 * 1024 * 1024


def _params(sem, vmem=VMEM_LIMIT, **kw):
    return pltpu.CompilerParams(dimension_semantics=sem, vmem_limit_bytes=vmem, **kw)


def _dot(a, b):
    return jnp.dot(a, b, preferred_element_type=F32)


def _dot_nt(a, b):
    return lax.dot_general(a, b, (((1,), (1,)), ((), ())), preferred_element_type=F32)


def _dot_tn(a, b):
    return lax.dot_general(a, b, (((0,), (0,)), ((), ())), preferred_element_type=F32)


def _rms_mod(x, g, sh, sc):
    ms = jnp.mean(x * x, axis=-1, keepdims=True)
    y = x * lax.rsqrt(ms + RMS_EPS) * g
    return y * (1.0 + sc) + sh


def _resident(shape):
    nd = len(shape)
    return pl.BlockSpec(shape, lambda *_: (0,) * nd, pipeline_mode=pl.Buffered(1))


MOD_TN = 768


def _mod_kernel(cs_ref, w_ref, b_ref, o_ref):
    cs = cs_ref[...]
    s = cs * jax.nn.sigmoid(cs)
    w = w_ref[0]
    r0 = jnp.sum(s[:, 0:1] * w, axis=0, keepdims=True)
    r1 = jnp.sum(s[:, 1:2] * w, axis=0, keepdims=True)
    o_ref[0] = jnp.concatenate([r0, r1], axis=0) + b_ref[0]


def _mod_vectors(c, c_ctx, ada_w, ada_b):
    cs = jnp.stack([c[0], c_ctx], axis=1)
    n = 6 * D
    return pl.pallas_call(
        _mod_kernel,
        grid=(DEPTH, n // MOD_TN),
        in_specs=[
            pl.BlockSpec((D, 2), lambda l, j: (0, 0)),
            pl.BlockSpec((1, D, MOD_TN), lambda l, j: (l, 0, j)),
            pl.BlockSpec((1, 1, MOD_TN), lambda l, j: (l, 0, j)),
        ],
        out_specs=pl.BlockSpec((1, 2, MOD_TN), lambda l, j: (l, 0, j)),
        out_shape=jax.ShapeDtypeStruct((DEPTH, 2, n), F32),
        compiler_params=_params(("arbitrary", "arbitrary")),
        name="mod_vectors",
    )(cs, ada_w, ada_b.reshape(DEPTH, 1, n))


def _mod_spec():
    return pl.BlockSpec((1, 8, D), lambda i, *_: (i // NT_LAT, 0, 0))


def _stream_tile(x_ref, ctx_ref):
    return jnp.where(pl.program_id(0) >= NT_LAT, ctx_ref[...], x_ref[...])


def _stream_specs():
    return [pl.BlockSpec((TM, D), lambda i: (jnp.minimum(i, NT_LAT - 1), 0)),
            pl.BlockSpec((CTX, D), lambda i: (0, 0))]


def _inproj0_kernel(x_ref, ctx_ref, mod_ref, g_ref, w_ref, d64_ref, qkv_ref, br_ref, bi_ref):
    h = _stream_tile(x_ref, ctx_ref)
    xm = _rms_mod(h, g_ref[...], mod_ref[0, 0:1, :], mod_ref[0, 1:2, :]).astype(BF16)
    p = _dot(xm, w_ref[...])
    qkv_ref[...] = p[:, FNET_W:].astype(BF16)
    b = _dot(p[:, :FNET_W].astype(BF16), d64_ref[...])
    br_ref[...] = b[:, :FNET_W].astype(BF16)
    bi_ref[...] = b[:, FNET_W:].astype(BF16)


def _inproj0(x, ctx, modv, g, w_in, d64):
    return pl.pallas_call(
        _inproj0_kernel,
        grid=(NT,),
        in_specs=_stream_specs() + [
            _mod_spec(),
            _resident((1, D)),
            _resident((D, AB_IN)),
            _resident((FNET_W, 2 * FNET_W)),
        ],
        out_specs=[
            pl.BlockSpec((TM, QKV_W), lambda i: (i, 0)),
            pl.BlockSpec((TM, FNET_W), lambda i: (i, 0)),
            pl.BlockSpec((TM, FNET_W), lambda i: (i, 0)),
        ],
        out_shape=[
            jax.ShapeDtypeStruct((T, QKV_W), BF16),
            jax.ShapeDtypeStruct((T, FNET_W), BF16),
            jax.ShapeDtypeStruct((T, FNET_W), BF16),
        ],
        compiler_params=_params(("arbitrary",)),
        name="inproj0",
    )(x, ctx, modv, g, w_in, d64)


FFT_A_NB = 8
FFT_B_TN = 4096


def _fft_a_kernel(br_ref, bi_ref, ga_ref, v_ref):
    for u in range(FFT_A_NB):
        cols = slice(u * FNET_W, (u + 1) * FNET_W)
        z = _dot(ga_ref[u, :, 0:FFT_N2], br_ref[:, cols]) + _dot(ga_ref[u, :, FFT_N2:], bi_ref[:, cols])
        v_ref[0, u] = z[:FFT_N2].astype(BF16)
        v_ref[1, u] = z[FFT_N2:].astype(BF16)


def _fft_b_kernel(cs_ref, v_ref, o_ref):
    o_ref[...] = _dot(cs_ref[...], v_ref[...]) * (1.0 / math.sqrt(SEQ * FNET_GD))


def _fft_latent(br, bi, ga, csb):
    width = FFT_N1 * FNET_W
    br2 = br.reshape(T // FFT_N1, width)
    bi2 = bi.reshape(T // FFT_N1, width)
    v = pl.pallas_call(
        _fft_a_kernel,
        grid=(FFT_N1 // FFT_A_NB,),
        in_specs=[
            pl.BlockSpec((FFT_N2, FFT_A_NB * FNET_W), lambda j: (0, j)),
            pl.BlockSpec((FFT_N2, FFT_A_NB * FNET_W), lambda j: (0, j)),
            pl.BlockSpec((FFT_A_NB, 2 * FFT_N2, 2 * FFT_N2), lambda j: (j, 0, 0)),
        ],
        out_specs=pl.BlockSpec((2, FFT_A_NB, FFT_N2, FNET_W), lambda j: (0, j, 0, 0)),
        out_shape=jax.ShapeDtypeStruct((2, FFT_N1, FFT_N2, FNET_W), BF16),
        compiler_params=_params(("arbitrary",)),
        name="fft_stage_a",
    )(br2, bi2, ga)
    v2 = v.reshape(2 * FFT_N1, FFT_N2 * FNET_W)
    x = pl.pallas_call(
        _fft_b_kernel,
        grid=(FFT_N2 * FNET_W // FFT_B_TN,),
        in_specs=[
            _resident((FFT_N1, 2 * FFT_N1)),
            pl.BlockSpec((2 * FFT_N1, FFT_B_TN), lambda j: (0, j)),
        ],
        out_specs=pl.BlockSpec((FFT_N1, FFT_B_TN), lambda j: (0, j)),
        out_shape=jax.ShapeDtypeStruct((FFT_N1, FFT_N2 * FNET_W), F32),
        compiler_params=_params(("arbitrary",)),
        name="fft_stage_b",
    )(csb, v2)
    return x.reshape(SEQ, FNET_W)


def _head_pair_masks():
    lane = lax.broadcasted_iota(jnp.int32, (1, LANES), 1)
    return lane < NA_HD


def _ctx_kernel(qkv_ref, br_ref, bi_ref, cs_ref, o_ref):
    a = _dot(cs_ref[:, 0:CTX], br_ref[...]) + _dot(cs_ref[:, CTX:], bi_ref[...])
    o_ref[:, 0:FNET_W] = (a * (1.0 / math.sqrt(CTX * FNET_GD))).astype(BF16)
    m0 = _head_pair_masks()
    for hp in range(NA_HEADS // 2):
        q = qkv_ref[:, hp * LANES:(hp + 1) * LANES]
        k = qkv_ref[:, NA_W + hp * LANES:NA_W + (hp + 1) * LANES]
        v = qkv_ref[:, 2 * NA_W + hp * LANES:2 * NA_W + (hp + 1) * LANES]
        outs = []
        for a_ in range(2):
            qa = jnp.where(m0 if a_ == 0 else jnp.logical_not(m0), q, jnp.zeros_like(q))
            s = _dot_nt(qa, k) * (NA_HD ** -0.5)
            p = jnp.exp(s - jnp.max(s, axis=-1, keepdims=True))
            l = jnp.sum(p, axis=-1, keepdims=True)
            outs.append(_dot(p.astype(BF16), v) / l)
        o_ref[:, FNET_W + hp * LANES:FNET_W + (hp + 1) * LANES] = jnp.where(m0, outs[0], outs[1]).astype(BF16)


def _ctx_mixer(qkv, br, bi, csc):
    return pl.pallas_call(
        _ctx_kernel,
        grid=(1,),
        in_specs=[
            pl.BlockSpec((CTX, QKV_W), lambda i: (SEQ // CTX, 0)),
            pl.BlockSpec((CTX, FNET_W), lambda i: (SEQ // CTX, 0)),
            pl.BlockSpec((CTX, FNET_W), lambda i: (SEQ // CTX, 0)),
            pl.BlockSpec((CTX, 2 * CTX), lambda i: (0, 0)),
        ],
        out_specs=pl.BlockSpec((CTX, D), lambda i: (0, 0)),
        out_shape=jax.ShapeDtypeStruct((CTX, D), BF16),
        compiler_params=_params(("arbitrary",)),
        name="ctx_mixer",
    )(qkv, br, bi, csc)


NA_RB = 16
NA_WIN = NA_KH * GRID_W


NA_NK = NA_WIN + CTX
NA_SUB = 16


def _na_kernel(q_ref, k_ref, v_ref, kc_ref, vc_ref, bias_ref, o_ref, s_scr, p_scr, l_scr):
    b = pl.program_id(1)
    m0 = _head_pair_masks()
    nm0 = jnp.logical_not(m0)
    scale = NA_HD ** -0.5

    def window(i):
        r = b * NA_RB + i
        rs = jnp.clip(r - NA_KH // 2, 0, ROWS - NA_KH)
        return pl.multiple_of(rs * GRID_W, GRID_W), rs - r + (NA_KH - 1)

    def scores(i):
        start, _ = window(i)
        q = q_ref[i * GRID_W:(i + 1) * GRID_W, :] * scale
        qs = jnp.concatenate([jnp.where(m0, q, jnp.zeros_like(q)), jnp.where(nm0, q, jnp.zeros_like(q))], axis=0)
        s_scr[i, :, 0:NA_WIN] = _dot_nt(qs, k_ref[pl.ds(start, NA_WIN), :])
        s_scr[i, :, NA_WIN:NA_NK] = _dot_nt(qs, kc_ref[...])

    def softmax(i):
        _, e = window(i)
        for g in range(2 * GRID_W // NA_SUB):
            rows = slice(g * NA_SUB, (g + 1) * NA_SUB)
            a, c0 = divmod(g * NA_SUB, GRID_W)
            s1 = s_scr[i, rows, 0:NA_WIN] + bias_ref[a, e, c0:c0 + NA_SUB, :]
            s2 = s_scr[i, rows, NA_WIN:NA_NK]
            mx = jnp.maximum(jnp.max(s1, axis=-1, keepdims=True), jnp.max(s2, axis=-1, keepdims=True))
            p1 = jnp.exp(s1 - mx)
            p2 = jnp.exp(s2 - mx)
            l = jnp.sum(p1, axis=-1, keepdims=True) + jnp.sum(p2, axis=-1, keepdims=True)
            p_scr[i, rows, 0:NA_WIN] = p1.astype(BF16)
            p_scr[i, rows, NA_WIN:NA_NK] = p2.astype(BF16)
            l_scr[i, rows, :] = jnp.broadcast_to(l, (NA_SUB, LANES))

    def values(i):
        start, _ = window(i)
        o = _dot(p_scr[i, :, 0:NA_WIN], v_ref[pl.ds(start, NA_WIN), :]) + _dot(p_scr[i, :, NA_WIN:NA_NK], vc_ref[...])
        o = o / l_scr[i]
        o_ref[i * GRID_W:(i + 1) * GRID_W, :] = jnp.where(m0, o[0:GRID_W], o[GRID_W:]).astype(BF16)

    for i in range(NA_RB):
        scores(i)
    for i in range(NA_RB):
        softmax(i)
        values(i)


def _na_attention(qkv, bias):
    nq = NA_RB * GRID_W
    nhp = NA_HEADS // 2
    return pl.pallas_call(
        _na_kernel,
        grid=(nhp, ROWS // NA_RB),
        in_specs=[
            pl.BlockSpec((nq, LANES), lambda hp, b: (b, hp)),
            pl.BlockSpec((SEQ, LANES), lambda hp, b: (0, nhp + hp)),
            pl.BlockSpec((SEQ, LANES), lambda hp, b: (0, 2 * nhp + hp)),
            pl.BlockSpec((CTX, LANES), lambda hp, b: (SEQ // CTX, nhp + hp)),
            pl.BlockSpec((CTX, LANES), lambda hp, b: (SEQ // CTX, 2 * nhp + hp)),
            pl.BlockSpec((2, NA_KH, GRID_W, NA_WIN), lambda hp, b: (hp, 0, 0, 0)),
        ],
        out_specs=pl.BlockSpec((nq, LANES), lambda hp, b: (b, hp)),
        out_shape=jax.ShapeDtypeStruct((SEQ, NA_W), BF16),
        scratch_shapes=[
            pltpu.VMEM((NA_RB, 2 * GRID_W, NA_NK), F32),
            pltpu.VMEM((NA_RB, 2 * GRID_W, NA_NK), BF16),
            pltpu.VMEM((NA_RB, 2 * GRID_W, LANES), F32),
        ],
        compiler_params=_params(("arbitrary", "arbitrary")),
        name="na_attention",
    )(qkv, qkv, qkv, qkv, qkv, bias)


def _na_bias_rows(rpb):
    c = np.arange(GRID_W)[:, None]
    kc = np.arange(GRID_W)[None, :]
    ws = np.clip(c - NA_KW // 2, 0, GRID_W - NA_KW)
    col_ok = (kc >= ws) & (kc < ws + NA_KW)
    dc = np.clip(kc - c + NA_KW - 1, 0, 2 * NA_KW - 2)
    pick = (dc[None] == np.arange(2 * NA_KW - 1)[:, None, None]).astype(np.float32)
    full = jnp.einsum("hrd,dck->hrck", rpb.astype(F32), jnp.asarray(pick), precision=lax.Precision.HIGHEST)
    full = jnp.where(jnp.asarray(col_ok)[None, None], full, NEG_BIG)
    rows = [jnp.transpose(full[:, e:e + NA_KH], (0, 2, 1, 3)).reshape(NA_HEADS, GRID_W, NA_WIN)
            for e in range(NA_KH)]
    return jnp.stack(rows, axis=1)


def _route_stage(xm2, valid, rw_ref, rb_ref, triu_ref, run_ref, xrow_ref, meta_ref, cnt_ref):
    xrow_ref[:, 0:D] = xm2
    xh = xm2.astype(BF16)
    xl = (xm2 - xh.astype(F32)).astype(BF16)
    w = rw_ref[...]
    wh = w.astype(BF16)
    wl = (w - wh.astype(F32)).astype(BF16)
    logits = _dot(xh, wh) + (_dot(xl, wh) + _dot(xh, wl))
    scores = jax.nn.sigmoid(logits.T[0:N_EXPERTS, :])
    biased = scores + rb_ref[...]
    s = [scores[e:e + 1, :] for e in range(N_EXPERTS)]
    b = [biased[e:e + 1, :] for e in range(N_EXPERTS)]
    n_groups = N_EXPERTS // EPG
    top2, group_score = [], []
    for g in range(n_groups):
        m = b[EPG * g:EPG * (g + 1)]
        total = None
        for k in range(EPG):
            rank = None
            for j in range(EPG):
                if j != k:
                    beats = jnp.where((m[j] >= m[k]) if j < k else (m[j] > m[k]), 1.0, 0.0)
                    rank = beats if rank is None else rank + beats
            top2.append(rank < 2.0)
            term = jnp.where(top2[-1], m[k], 0.0)
            total = term if total is None else total + term
        group_score.append(total)
    best, g_sel = group_score[0], jnp.zeros((1, TM), jnp.int32)
    for g in range(1, n_groups):
        better = group_score[g] > best
        g_sel = jnp.where(better, g, g_sel)
        best = jnp.where(better, group_score[g], best)
    sel = [top2[e] & (g_sel == e // EPG) for e in range(N_EXPERTS)]
    w_sum = None
    for e in range(N_EXPERTS):
        term = jnp.where(sel[e], s[e], 0.0)
        w_sum = term if w_sum is None else w_sum + term
    zero = jnp.zeros((1, TM), F32)
    w_lo, w_hi, bid = zero, zero, zero
    for g in range(n_groups):
        seen = None
        for k in range(EPG):
            e = EPG * g + k
            if seen is None:
                w_lo = w_lo + jnp.where(sel[e], s[e], 0.0)
                seen = sel[e]
            else:
                w_lo = w_lo + jnp.where(sel[e] & jnp.logical_not(seen), s[e], 0.0)
                w_hi = w_hi + jnp.where(sel[e] & seen, s[e], 0.0)
                seen = seen | sel[e]
            bid = bid + jnp.where(sel[e], float((1 << k) + 8 * g), 0.0)
    w_lo = w_lo / w_sum
    w_hi = w_hi / w_sum
    bucket = lax.broadcasted_iota(jnp.int32, (N_BUCKET_IDS, TM), 0).astype(F32)
    onehot = bucket == bid
    ones = jnp.where(onehot & valid, 1.0, 0.0)
    earlier = _dot(ones.astype(BF16), triu_ref[...])
    run = run_ref[...]
    run_t = jnp.concatenate([run] * (TM // LANES), axis=1)
    rank_row = jnp.sum(jnp.where(onehot, earlier + run_t, 0.0), axis=0, keepdims=True)
    new_run = run + jnp.sum(ones, axis=1, keepdims=True)
    run_ref[...] = new_run
    cnt_ref[...] = new_run
    meta_rows = jnp.concatenate([w_lo, w_hi, bid, rank_row, jnp.zeros((SUBLANES - 4, TM), F32)], axis=0)
    meta_ref[...] = meta_rows
    padded = jnp.concatenate([meta_rows, jnp.zeros((LANES - SUBLANES, TM), F32)], axis=0)
    xrow_ref[:, D:XROW_W] = padded.T


def _epilogue_specs(ntiles):
    in_specs = [
        _resident((1, D)),
        _resident((D, LANES)),
        _resident((N_EXPERTS, TM)),
        _resident((TM, TM)),
    ]
    out_specs = [
        pl.BlockSpec((TM, D), lambda i: (jnp.minimum(i, ntiles - 1), 0)),
        pl.BlockSpec((TM, XROW_W), lambda i: (jnp.maximum(i - 1, 0), 0)),
        pl.BlockSpec((SUBLANES, TM), lambda i: (0, jnp.maximum(i - 1, 0))),
        pl.BlockSpec((N_BUCKET_IDS, LANES), lambda i: (0, 0)),
    ]
    n = ntiles * TM
    out_shape = [
        jax.ShapeDtypeStruct((n, D), F32),
        jax.ShapeDtypeStruct((n, XROW_W), F32),
        jax.ShapeDtypeStruct((SUBLANES, n), F32),
        jax.ShapeDtypeStruct((N_BUCKET_IDS, LANES), F32),
    ]
    return in_specs, out_specs, out_shape


_EPILOGUE_SCRATCH = [pltpu.VMEM((N_BUCKET_IDS, LANES), F32), pltpu.VMEM((2, TM, D), F32)]


def _two_stage(residual_tile, mod_ref, g2_ref, rw_ref, rb_ref, triu_ref,
               h1_ref, xrow_ref, meta_ref, cnt_ref, run_ref, stash):
    i = pl.program_id(0)

    @pl.when(i == 0)
    def _():
        run_ref[...] = jnp.zeros_like(run_ref)
        stash[1] = jnp.zeros((TM, D), F32)

    _route_stage(stash[1 - i % 2], i >= 1, rw_ref, rb_ref, triu_ref, run_ref, xrow_ref, meta_ref, cnt_ref)
    h1 = residual_tile()
    h1_ref[...] = h1
    stash[i % 2] = _rms_mod(h1, g2_ref[...], mod_ref[0, 3:4, :], mod_ref[0, 4:5, :])


def _outproj0_kernel(a_ref, o_ref, aoc_ref, x_ref, ctx_ref, mod_ref, w_ref, *rest):
    def residual_tile():
        is_ctx = pl.program_id(0) >= NT_LAT
        a = jnp.where(is_ctx, aoc_ref[:, 0:FNET_W], a_ref[...].astype(BF16))
        o = jnp.where(is_ctx, aoc_ref[:, FNET_W:], o_ref[...])
        y = _dot(a, w_ref[0:FNET_W, :]) + _dot(o, w_ref[FNET_W:, :])
        return _stream_tile(x_ref, ctx_ref) + mod_ref[0, 2:3, :] * y

    _two_stage(residual_tile, mod_ref, *rest)


def _outproj0(a_lat, o_lat, ao_ctx, x, ctx, modv, w_out, g2, rw, rb, triu):
    e_in, e_out, e_shape = _epilogue_specs(NT)
    lat = lambda w: pl.BlockSpec((TM, w), lambda i: (jnp.minimum(i, NT_LAT - 1), 0))
    return pl.pallas_call(
        _outproj0_kernel,
        grid=(NT + 1,),
        in_specs=[lat(FNET_W), lat(NA_W), pl.BlockSpec((CTX, D), lambda i: (0, 0))] + _stream_specs() + [
            _mod_spec(),
            _resident((D, D)),
        ] + e_in,
        out_specs=e_out,
        out_shape=e_shape,
        scratch_shapes=_EPILOGUE_SCRATCH,
        compiler_params=_params(("arbitrary",)),
        name="outproj0",
    )(a_lat, o_lat, ao_ctx, x, ctx, modv, w_out, g2, rw, rb, triu)


def _bucket_tables():
    lo = np.zeros(N_BUCKET_IDS, np.int32)
    hi = np.zeros(N_BUCKET_IDS, np.int32)
    for g in range(N_EXPERTS // EPG):
        for a in range(EPG):
            for b in range(a + 1, EPG):
                i = 16 * g + (1 << a) + (1 << b)
                lo[i] = EPG * g + a
                hi[i] = EPG * g + b
    return lo, hi


def _slot_kernel(meta_ref, starts_ref, o_ref):
    n = meta_ref.shape[1]
    bucket = lax.broadcasted_iota(jnp.int32, (N_BUCKET_IDS, n), 0).astype(F32)
    first = jnp.sum(jnp.where(bucket == meta_ref[2:3, :], starts_ref[:, 0:1], 0.0), axis=0, keepdims=True)
    o_ref[...] = jnp.broadcast_to((first + meta_ref[3:4, :]).astype(jnp.int32), (SUBLANES, n))


def _token_slots(meta, first_row):
    n = meta.shape[1]
    out = pl.pallas_call(
        _slot_kernel,
        grid=(1,),
        in_specs=[pl.BlockSpec((SUBLANES, n), lambda i: (0, 0)),
                  pl.BlockSpec((N_BUCKET_IDS, LANES), lambda i: (0, 0))],
        out_specs=pl.BlockSpec((SUBLANES, n), lambda i: (0, 0)),
        out_shape=jax.ShapeDtypeStruct((SUBLANES, n), jnp.int32),
        compiler_params=_params(("arbitrary",)),
        name="token_slots",
    )(meta, first_row)
    return out[0]


def _route_plan(meta, cnt, n_tiles, layer):
    counts = cnt[:, 0].astype(jnp.int32)
    nt_b = (counts + TMM - 1) // TMM
    ends = jnp.cumsum(nt_b)
    starts = ends - nt_b
    first_row = jnp.broadcast_to((starts * TMM).astype(F32)[:, None], (N_BUCKET_IDS, LANES))
    pos = _token_slots(meta, first_row)
    n_used = ends[-1]
    tile = jnp.minimum(jnp.arange(n_tiles, dtype=jnp.int32), n_used - 1)
    tile_b = jnp.sum((ends[None, :] <= tile[:, None]).astype(jnp.int32), axis=1)
    lo, hi = (jnp.asarray(t) + layer * N_EXPERTS for t in _bucket_tables())
    ids = jnp.arange(n_tiles, dtype=jnp.int32)
    first = (tile_b != jnp.concatenate([jnp.full((1,), -1, jnp.int32), tile_b[:-1]])) & (ids < n_used)
    wslot = (jnp.cumsum(first.astype(jnp.int32)) - 1) % 2
    next_tile = ends[tile_b]
    has_next = next_tile < n_used
    next_b = tile_b[jnp.minimum(next_tile, n_tiles - 1)]
    plan = jnp.stack([lo[tile_b], hi[tile_b], first.astype(jnp.int32), wslot,
                      jnp.where(has_next, lo[next_b], -1), jnp.where(has_next, hi[next_b], -1)])
    fill = ((ids >= n_used) | (ids == next_tile - 1)).astype(jnp.int32)
    return pos, fill, plan.astype(jnp.int32), n_used.reshape(1).astype(jnp.int32)


def _dispatch_kernel(pos_ref, fill_ref, x_ref, xs_ref, sem, zeros, zsem):
    base = pl.program_id(0) * TM
    tile_rows = TMM // SUBLANES

    @pl.when(pl.program_id(0) == 0)
    def _():
        zeros[...] = jnp.zeros_like(zeros)

        def fill(t):
            return pltpu.make_async_copy(zeros, xs_ref.at[pl.ds(t * tile_rows, tile_rows)], zsem)

        def start(t, carry):
            @pl.when(fill_ref[t] == 1)
            def _():
                fill(t).start()
            return carry

        def wait(t, carry):
            @pl.when(fill_ref[t] == 1)
            def _():
                fill(t).wait()
            return carry

        lax.fori_loop(0, fill_ref.shape[0], start, 0)
        lax.fori_loop(0, fill_ref.shape[0], wait, 0)

    for j in range(TM // SUBLANES):
        for u in range(SUBLANES):
            p = pos_ref[base + j * SUBLANES + u]
            pltpu.make_async_copy(x_ref.at[j, pl.ds(u, 1), :],
                                  xs_ref.at[p >> 3, pl.ds(p & (SUBLANES - 1), 1), :], sem).start()
    pltpu.make_async_copy(x_ref, xs_ref.at[pl.ds(0, TM // SUBLANES)], sem).wait()


def _dispatch(pos, fill, xrow, n_tiles):
    ntok = xrow.shape[0]
    xs = pl.pallas_call(
        _dispatch_kernel,
        grid_spec=pltpu.PrefetchScalarGridSpec(
            num_scalar_prefetch=2,
            grid=(ntok // TM,),
            in_specs=[pl.BlockSpec((TM // SUBLANES, SUBLANES, XROW_W), lambda i, pos, fill: (i, 0, 0))],
            out_specs=pl.BlockSpec(memory_space=pl.ANY),
            scratch_shapes=[
                pltpu.SemaphoreType.DMA(()),
                pltpu.VMEM((TMM // SUBLANES, SUBLANES, XROW_W), F32),
                pltpu.SemaphoreType.DMA(()),
            ],
        ),
        out_shape=jax.ShapeDtypeStruct((n_tiles * TMM // SUBLANES, SUBLANES, XROW_W), F32),
        compiler_params=_params(("arbitrary",)),
        name="moe_dispatch",
    )(pos, fill, _by_sublane_tile(xrow))
    return xs.reshape(n_tiles * TMM, XROW_W)


def _moe_kernel(plan_ref, nu_ref, xs_ref, w1_hbm, w3_hbm, w2_hbm, ys_ref, wb1, wb3, wb2, sem):
    j = pl.program_id(0)
    slot = plan_ref[3, j]

    def weight_copies(e_lo, e_hi, s):
        out = []
        for k, e in enumerate((e_lo, e_hi)):
            out.append(pltpu.make_async_copy(w1_hbm.at[e], wb1.at[s, k], sem.at[s, 3 * k]))
            out.append(pltpu.make_async_copy(w3_hbm.at[e], wb3.at[s, k], sem.at[s, 3 * k + 1]))
            out.append(pltpu.make_async_copy(w2_hbm.at[e], wb2.at[s, k], sem.at[s, 3 * k + 2]))
        return out

    @pl.when(j == 0)
    def _():
        for c in weight_copies(plan_ref[0, 0], plan_ref[1, 0], 0):
            c.start()

    @pl.when(plan_ref[2, j] == 1)
    def _():
        for c in weight_copies(plan_ref[0, j], plan_ref[1, j], slot):
            c.wait()

        @pl.when(plan_ref[4, j] >= 0)
        def _():
            for c in weight_copies(plan_ref[4, j], plan_ref[5, j], 1 - slot):
                c.start()

    @pl.when(j < nu_ref[0])
    def _():
        x = xs_ref[:, 0:D].astype(BF16)
        g_lo = xs_ref[:, D:D + 1]
        g_hi = xs_ref[:, D + 1:D + 2]

        def expert(k):
            a = _dot(x, wb1[slot, k].astype(BF16))
            hid = (a * jax.nn.sigmoid(a)) * _dot(x, wb3[slot, k].astype(BF16))
            return _dot(hid.astype(BF16), wb2[slot, k].astype(BF16))

        ys_ref[...] = g_lo * expert(0) + g_hi * expert(1)

    @pl.when(j >= nu_ref[0])
    def _():
        ys_ref[...] = jnp.zeros_like(ys_ref)


def _moe(plan, n_used, xs, w1, w3, w2, n_tiles):
    hbm = pl.BlockSpec(memory_space=pl.ANY)
    return pl.pallas_call(
        _moe_kernel,
        grid_spec=pltpu.PrefetchScalarGridSpec(
            num_scalar_prefetch=2,
            grid=(n_tiles,),
            in_specs=[pl.BlockSpec((TMM, XROW_W), lambda j, plan, nu: (j, 0)), hbm, hbm, hbm],
            out_specs=pl.BlockSpec((TMM, D), lambda j, plan, nu: (j, 0)),
            scratch_shapes=[
                pltpu.VMEM((2, 2, D, D_EXPERT), F32),
                pltpu.VMEM((2, 2, D, D_EXPERT), F32),
                pltpu.VMEM((2, 2, D_EXPERT, D), F32),
                pltpu.SemaphoreType.DMA((2, 6)),
            ],
        ),
        out_shape=jax.ShapeDtypeStruct((n_tiles * TMM, D), F32),
        compiler_params=_params(("arbitrary",)),
        name="moe_experts",
    )(plan, n_used, xs, w1, w3, w2)


def _gather_issue(pos_ref, ys_hbm, buf, sem, tile, slot, inline):
    base = tile * TM

    def issue(j, carry):
        for u in range(SUBLANES):
            p = pos_ref[base + j * SUBLANES + u]
            pltpu.make_async_copy(ys_hbm.at[p >> 3, pl.ds(p & (SUBLANES - 1), 1), :],
                                  buf.at[slot, j, pl.ds(u, 1), :], sem.at[slot]).start()
        return carry

    if inline:
        for j in range(TM // SUBLANES):
            issue(j, 0)
    else:
        lax.fori_loop(0, TM // SUBLANES, issue, 0)


def _gather_wait(ys_hbm, buf, sem, slot):
    pltpu.make_async_copy(ys_hbm.at[pl.ds(0, TM // SUBLANES)], buf.at[slot], sem.at[slot]).wait()


def _gathered_rows(pos_ref, ys_hbm, buf, sem, request_next):
    i = pl.program_id(0)
    n_slots = buf.shape[0]
    slot = i % n_slots

    @pl.when(i == 0)
    def _():
        for t in range(1 if request_next else n_slots - 1):
            _gather_issue(pos_ref, ys_hbm, buf, sem, t, t, inline=False)

    if request_next:
        @pl.when(i + 1 < pl.num_programs(0))
        def _():
            _gather_issue(pos_ref, ys_hbm, buf, sem, i + 1, 1 - slot, inline=True)

    _gather_wait(ys_hbm, buf, sem, slot)
    return buf[slot].reshape(TM, D)


def _gather_request_ahead(pos_ref, ys_hbm, buf, sem):
    i = pl.program_id(0)
    n = pl.num_programs(0)
    ahead = buf.shape[0] - 1
    tile = i + ahead
    _gather_issue(pos_ref, ys_hbm, buf, sem, jnp.where(tile < n, tile, tile - n), tile % buf.shape[0], inline=True)


def _gather_drain(ys_hbm, buf, sem):
    i = pl.program_id(0)

    @pl.when(i + 1 == pl.num_programs(0))
    def _():
        for k in range(1, buf.shape[0]):
            _gather_wait(ys_hbm, buf, sem, (i + k) % buf.shape[0])


_GATHER_SCRATCH = [pltpu.VMEM((2, TM // SUBLANES, SUBLANES, D), F32), pltpu.SemaphoreType.DMA((2,))]


def _by_sublane_tile(a):
    return a.reshape(a.shape[0] // SUBLANES, SUBLANES, a.shape[1])


def _rope_store(x, tabs, out_ref):
    cr, sr, cc, sc = tabs
    for ch in range(RET_QK // LANES):
        xc = x[:, ch * LANES:(ch + 1) * LANES]
        xr = pltpu.roll(xc, LANES // 2, axis=1)
        y = xc * cr + xr * sr if ch % 2 == 0 else xc * cc + xr * sc
        out_ref[:, ch * LANES:(ch + 1) * LANES] = y.astype(BF16)


def _inproj1_kernel(pos_ref, h_ref, ys_hbm, mod0_ref, mod1_ref, g_ref, w_ref, rt_ref, ct_ref,
                    h2_ref, q_ref, k_ref, v_ref, gg_ref, buf, sem, xm_scr):
    f = _gathered_rows(pos_ref, ys_hbm, buf, sem, request_next=False)
    h2 = h_ref[...] + mod0_ref[0, 5:6, :] * f
    h2_ref[...] = h2
    xm_scr[...] = _rms_mod(h2, g_ref[...], mod1_ref[0, 0:1, :], mod1_ref[0, 1:2, :]).astype(BF16)
    _gather_request_ahead(pos_ref, ys_hbm, buf, sem)
    xm = xm_scr[...]

    off = (pl.program_id(0) % (SUBLANES // ROPE_TILE_ROWS)) * ROPE_TILE_ROWS

    def row_table(which):
        return jnp.concatenate([jnp.broadcast_to(rt_ref[which, pl.ds(off + j, 1), :], (GRID_W, LANES))
                                for j in range(ROPE_TILE_ROWS)], axis=0)

    tabs = (row_table(0), row_table(1), ct_ref[0, 0], ct_ref[0, 1])
    _rope_store(_dot(xm, w_ref[:, 0:RET_QK]), tabs, q_ref)
    _rope_store(_dot(xm, w_ref[:, RET_QK:2 * RET_QK]) * (RET_DK ** -0.5), tabs, k_ref)
    v_ref[...] = _dot(xm, w_ref[:, 2 * RET_QK:2 * RET_QK + RET_V]).astype(BF16)
    gg_ref[...] = _dot(xm, w_ref[:, 2 * RET_QK + RET_V:]).astype(BF16)
    _gather_drain(ys_hbm, buf, sem)


def _inproj1(pos, h1, ys, modv0, modv1, g, w_in, tabs):
    tok = lambda w: pl.BlockSpec((TM, w), lambda i, pos: (i, 0))
    res = lambda shape: pl.BlockSpec(shape, lambda i, pos: (0,) * len(shape), pipeline_mode=pl.Buffered(1))
    mod = pl.BlockSpec((1, 8, D), lambda i, pos: (i // NT_LAT, 0, 0))
    row_tab = pl.BlockSpec((2, SUBLANES, LANES), lambda i, pos: (0, i // (SUBLANES // ROPE_TILE_ROWS), 0))
    col_tab = pl.BlockSpec((1, 2, TM, LANES), lambda i, pos: (i // NT_LAT, 0, 0, 0))
    return pl.pallas_call(
        _inproj1_kernel,
        grid_spec=pltpu.PrefetchScalarGridSpec(
            num_scalar_prefetch=1,
            grid=(NT,),
            in_specs=[tok(D), pl.BlockSpec(memory_space=pl.ANY), mod, mod, res((1, D)), res((D, 2 * RET_QK + 2 * RET_V)),
                      row_tab, col_tab],
            out_specs=[tok(D), tok(RET_QK), tok(RET_QK), tok(RET_V), tok(RET_V)],
            scratch_shapes=[pltpu.VMEM((3, TM // SUBLANES, SUBLANES, D), F32), pltpu.SemaphoreType.DMA((3,)),
                            pltpu.VMEM((TM, D), BF16)],
        ),
        out_shape=[
            jax.ShapeDtypeStruct((T, D), F32),
            jax.ShapeDtypeStruct((T, RET_QK), BF16),
            jax.ShapeDtypeStruct((T, RET_QK), BF16),
            jax.ShapeDtypeStruct((T, RET_V), BF16),
            jax.ShapeDtypeStruct((T, RET_V), BF16),
        ],
        compiler_params=_params(("arbitrary",)),
        name="inproj1",
    )(pos, h1, _by_sublane_tile(ys), modv0, modv1, g, w_in, *tabs)


ROPE_TILE_ROWS = TM // GRID_W


def _rope_tables():
    n = LANES // 2
    inv_freq = ROPE_BASE ** (-np.arange(n, dtype=np.float64) / n)

    def lines(p):
        ang = p[:, None] * inv_freq[None, :]
        return np.stack([np.concatenate([np.cos(ang), np.cos(ang)], axis=1),
                         np.concatenate([-np.sin(ang), np.sin(ang)], axis=1)])

    ident = np.stack([np.ones((TM, LANES)), np.zeros((TM, LANES))])
    row_tab = np.concatenate([lines(np.arange(ROWS, dtype=np.float64)), ident[:, :SUBLANES]], axis=1)
    col_lat = lines(np.tile(np.arange(GRID_W, dtype=np.float64), ROPE_TILE_ROWS))
    col_tab = np.stack([col_lat, ident])
    return jnp.asarray(row_tab, dtype=F32), jnp.asarray(col_tab, dtype=F32)


RET_NCHUNK = T // RET_CHUNK
RET_CTX_CHUNKS = CTX // RET_CHUNK
RET_LAT_CHUNKS = SEQ // RET_CHUNK
RET_NHD = 2 * RET_HEADS


def _ret_kernel(dec_ref, qf_ref, kf_ref, vf_ref, qb_ref, kb_ref, vb_ref, of_ref, ob_ref,
                state, dm, qd, kd, cd):
    C = RET_CHUNK

    @pl.when((pl.program_id(0) == 0) & (pl.program_id(1) == 0))
    def _():
        state[...] = jnp.zeros_like(state)
        n = lax.broadcasted_iota(jnp.int32, (C, C), 0).astype(F32)
        m = lax.broadcasted_iota(jnp.int32, (C, C), 1).astype(F32)
        for d in range(2):
            for h in range(RET_HEADS):
                idx = d * RET_HEADS + h
                lg = jnp.log1p(-jnp.exp(jnp.full((C, C), dec_ref[d, h], F32)))
                diff = n - m if d == 0 else m - n
                low = diff >= 0.0
                dm[idx] = jnp.where(low, jnp.exp(jnp.where(low, diff, 0.0) * lg), 0.0)
                qpow = n + 1.0 if d == 0 else C - n
                kpow = (C - 1.0) - n if d == 0 else n
                qcol = jnp.exp(qpow * lg)
                kcol = jnp.exp(kpow * lg)
                for c0 in range(0, RET_DV, C):
                    qd[idx, :, c0:c0 + C] = qcol
                for c0 in range(0, RET_DK, C):
                    kd[idx, :, c0:c0 + C] = kcol
                for c0 in range(0, RET_DV, LANES):
                    cd[idx, :, c0:c0 + LANES] = jnp.exp(C * lg[0:8, 0:LANES])

    def direction(d, q_ref, k_ref, v_ref, o_ref):
        for h in range(RET_HEADS):
            idx = d * RET_HEADS + h
            q = q_ref[:, h * RET_DK:(h + 1) * RET_DK]
            k = k_ref[:, h * RET_DK:(h + 1) * RET_DK]
            v = v_ref[:, h * RET_DV:(h + 1) * RET_DV]
            st = state[idx]
            s = _dot_nt(q, k) * dm[idx]
            o = _dot(s.astype(BF16), v) + qd[idx] * _dot(q, st.astype(BF16))
            o_ref[:, h * RET_DV:(h + 1) * RET_DV] = o.astype(BF16)
            kdk = (k.astype(F32) * kd[idx]).astype(BF16)
            cdv = cd[idx, 0:1, :]
            state[idx] = st * cdv + _dot(kdk.T, v)

    @pl.when(pl.program_id(1) == 0)
    def _():
        direction(0, qf_ref, kf_ref, vf_ref, of_ref)

    @pl.when(pl.program_id(1) == 1)
    def _():
        direction(1, qb_ref, kb_ref, vb_ref, ob_ref)


def _ret_scan(decay, q, k, v):
    def fwd(s):
        return jnp.where(s < RET_CTX_CHUNKS, RET_LAT_CHUNKS + s, s - RET_CTX_CHUNKS)

    def bwd(s):
        return RET_NCHUNK - 1 - s

    def spec(w, f):
        return pl.BlockSpec((RET_CHUNK, w), lambda s, d: (f(s), 0))

    return pl.pallas_call(
        _ret_kernel,
        grid=(RET_NCHUNK, 2),
        in_specs=[
            pl.BlockSpec(memory_space=pltpu.SMEM),
            spec(RET_QK, fwd), spec(RET_QK, fwd), spec(RET_V, fwd),
            spec(RET_QK, bwd), spec(RET_QK, bwd), spec(RET_V, bwd),
        ],
        out_specs=[spec(RET_V, fwd), spec(RET_V, bwd)],
        out_shape=[jax.ShapeDtypeStruct((T, RET_V), BF16), jax.ShapeDtypeStruct((T, RET_V), BF16)],
        scratch_shapes=[
            pltpu.VMEM((RET_NHD, RET_DK, RET_DV), F32),
            pltpu.VMEM((RET_NHD, RET_CHUNK, RET_CHUNK), F32),
            pltpu.VMEM((RET_NHD, RET_CHUNK, RET_DV), F32),
            pltpu.VMEM((RET_NHD, RET_CHUNK, RET_DK), F32),
            pltpu.VMEM((RET_NHD, 8, RET_DV), F32),
        ],
        compiler_params=_params(("arbitrary", "arbitrary")),
        name="retention_scan",
    )(decay, q, k, v, q, k, v)


def _outproj1_kernel(of_ref, ob_ref, gg_ref, h_ref, mod_ref, w_ref, *rest):
    def residual_tile():
        y = jnp.zeros((TM, D), F32)
        for hh in range(RET_HEADS):
            cols = slice(hh * RET_DV, (hh + 1) * RET_DV)
            o = of_ref[:, cols].astype(F32) + ob_ref[:, cols].astype(F32)
            on = o * lax.rsqrt(jnp.mean(o * o, axis=-1, keepdims=True) + RMS_EPS)
            g = gg_ref[:, cols].astype(F32)
            z = (g * jax.nn.sigmoid(g)) * on
            y = y + _dot(z.astype(BF16), w_ref[cols, :])
        return h_ref[...] + mod_ref[0, 2:3, :] * y

    _two_stage(residual_tile, mod_ref, *rest)


def _outproj1(o_f, o_b, gg, h, modv, w_out, g2, rw, rb, triu):
    e_in, e_out, e_shape = _epilogue_specs(NT_LAT)
    tok = lambda w: pl.BlockSpec((TM, w), lambda i: (jnp.minimum(i, NT_LAT - 1), 0))
    latent_mod = pl.BlockSpec((1, 8, D), lambda i: (0, 0, 0))
    return pl.pallas_call(
        _outproj1_kernel,
        grid=(NT_LAT + 1,),
        in_specs=[tok(RET_V), tok(RET_V), tok(RET_V), tok(D), latent_mod, _resident((RET_V, D))] + e_in,
        out_specs=e_out,
        out_shape=e_shape,
        scratch_shapes=_EPILOGUE_SCRATCH,
        compiler_params=_params(("arbitrary",)),
        name="outproj1",
    )(o_f, o_b, gg, h, modv, w_out, g2, rw, rb, triu)


def _final_kernel(pos_ref, h_ref, ys_hbm, mod_ref, g_ref, o_ref, buf, sem):
    f = _gathered_rows(pos_ref, ys_hbm, buf, sem, request_next=True)
    h = h_ref[...] + mod_ref[0, 5:6, :] * f
    o_ref[...] = h * lax.rsqrt(jnp.mean(h * h, axis=-1, keepdims=True) + RMS_EPS) * g_ref[...]


def _final(pos, h, ys, modv, g):
    tok = pl.BlockSpec((TM, D), lambda i, pos: (i, 0))
    return pl.pallas_call(
        _final_kernel,
        grid_spec=pltpu.PrefetchScalarGridSpec(
            num_scalar_prefetch=1,
            grid=(NT_LAT,),
            in_specs=[tok, pl.BlockSpec(memory_space=pl.ANY),
                      pl.BlockSpec((1, 8, D), lambda i, pos: (0, 0, 0)),
                      pl.BlockSpec((1, D), lambda i, pos: (0, 0))],
            out_specs=tok,
            scratch_shapes=_GATHER_SCRATCH,
        ),
        out_shape=jax.ShapeDtypeStruct((SEQ, D), F32),
        compiler_params=_params(("arbitrary",)),
        name="final_norm",
    )(pos, h, _by_sublane_tile(ys), modv, g)


def _dft_constants():
    c = np.arange(FNET_GD)
    ang = 2.0 * np.pi * np.outer(c, c) / FNET_GD
    d64 = np.zeros((FNET_W, 2 * FNET_W))
    for g in range(FNET_W // FNET_GD):
        sl = slice(g * FNET_GD, (g + 1) * FNET_GD)
        d64[sl, sl] = np.cos(ang)
        d64[sl, FNET_W + g * FNET_GD:FNET_W + (g + 1) * FNET_GD] = -np.sin(ang)
    k2 = np.arange(FFT_N2)[:, None]
    n2 = np.arange(FFT_N2)[None, :]
    ga = np.zeros((FFT_N1, 2 * FFT_N2, 2 * FFT_N2))
    for n1 in range(FFT_N1):
        th = 2.0 * np.pi * ((k2 * (n1 + FFT_N1 * n2)) % SEQ) / SEQ
        cs, sn = np.cos(th), np.sin(th)
        ga[n1] = np.block([[cs, sn], [-sn, cs]])
    k1 = np.arange(FFT_N1)
    ph = 2.0 * np.pi * np.outer(k1, k1) / FFT_N1
    csb = np.concatenate([np.cos(ph), np.sin(ph)], axis=1)
    p = np.arange(CTX)
    pc = 2.0 * np.pi * (np.outer(p, p) % CTX) / CTX
    csc = np.concatenate([np.cos(pc), np.sin(pc)], axis=1)
    as_bf16 = lambda a: jnp.asarray(a, dtype=F32).astype(BF16)
    return as_bf16(d64), as_bf16(ga), as_bf16(csb), as_bf16(csc)


def _mod_table(mod_l):
    t = mod_l.reshape(2, 6, D)
    return jnp.concatenate([t, jnp.zeros((2, 2, D), F32)], axis=1)


def kernel(x, c, ctx, c_ctx, ada_w, ada_b, norm_g, final_norm_g, mixab_w_in, mixab_w_out, na_rpb,
           ret_w_in, ret_w_out, ret_decay, router_w, router_b, moe_w1, moe_w3, moe_w2):
    d64, ga, csb, csc = _dft_constants()
    triu = jnp.asarray(np.triu(np.ones((TM, TM)), 1), dtype=BF16)
    rw = jnp.pad(router_w, ((0, 0), (0, LANES - N_EXPERTS)))
    rb = jnp.broadcast_to(router_b[:, None], (N_EXPERTS, TM))
    w1 = moe_w1.reshape(DEPTH * N_EXPERTS, D, D_EXPERT)
    w3 = moe_w3.reshape(DEPTH * N_EXPERTS, D, D_EXPERT)
    w2 = moe_w2.reshape(DEPTH * N_EXPERTS, D_EXPERT, D)

    mod = _mod_vectors(c, c_ctx, ada_w, ada_b)
    modv0, modv1 = _mod_table(mod[0]), _mod_table(mod[1])

    qkv, br, bi = _inproj0(x[0], ctx[0], modv0, norm_g[0, 0].reshape(1, D), mixab_w_in[0].astype(BF16), d64)
    a_lat = _fft_latent(br, bi, ga, csb)
    ao_ctx = _ctx_mixer(qkv, br, bi, csc)
    o_lat = _na_attention(qkv, _na_bias_rows(na_rpb[0]))
    h1, xrow, meta, cnt = _outproj0(a_lat, o_lat, ao_ctx, x[0], ctx[0], modv0, mixab_w_out[0].astype(BF16),
                                    norm_g[0, 1].reshape(1, D), rw, rb, triu)
    n_tiles0 = NT + N_REAL_BUCKETS
    pos0, fill0, plan0, n_used0 = _route_plan(meta, cnt, n_tiles0, 0)
    ys0 = _moe(plan0, n_used0, _dispatch(pos0, fill0, xrow, n_tiles0), w1, w3, w2, n_tiles0)

    h2, q, k, v, gg = _inproj1(pos0, h1, ys0, modv0, modv1, norm_g[1, 0].reshape(1, D),
                               ret_w_in[0].astype(BF16), _rope_tables())
    o_f, o_b = _ret_scan(ret_decay[0].astype(F32), q, k, v)
    h3, xrow1, meta1, cnt1 = _outproj1(o_f, o_b, gg, h2, modv1, ret_w_out[0].astype(BF16),
                                       norm_g[1, 1].reshape(1, D), rw, rb, triu)
    n_tiles1 = NT_LAT + N_REAL_BUCKETS
    pos1, fill1, plan1, n_used1 = _route_plan(meta1, cnt1, n_tiles1, 1)
    ys1 = _moe(plan1, n_used1, _dispatch(pos1, fill1, xrow1, n_tiles1), w1, w3, w2, n_tiles1)
    out = _final(pos1, h3, ys1, modv1, final_norm_g.reshape(1, D))
    return out[None]
```
